```python
import math
import jax, jax.numpy as jnp
from jax import lax
import numpy as np

D_MODEL = 1024
BATCH = 8
SEQ = 2048
DEPTH = 4
DEC_BATCH = 128
DEC_SEQ = 1
PAST_LEN = 2048
PAGE_SIZE = 128

N_MIXERS = 4
N_LAYERS_A = (DEPTH + 3) // 4
N_LAYERS_B = (DEPTH + 2) // 4
N_LAYERS_C = (DEPTH + 1) // 4
N_LAYERS_D = DEPTH // 4

RMS_EPS = 1e-6
LN_EPS = 1e-5
GM_CHUNK = 128
GM_WIDTH = D_MODEL
GM_GROUPS = 8
GM_GROUP_DIM = GM_WIDTH // GM_GROUPS
MOBA_HEADS = 16
MOBA_HEAD_DIM = D_MODEL // MOBA_HEADS
MOBA_BLOCK = 256
MOBA_TOPK = 3
MOBA_QBLOCK = 64
REL_BUCKETS = 32
REL_MAX_DIST = 128
POOL_WINDOWS = (2, 4, 8, 16)
POOL_GROUPS = len(POOL_WINDOWS)
POOL_GROUP_DIM = D_MODEL // POOL_GROUPS
POOL_CTX = max(POOL_WINDOWS) - 1
RWKV_HEAD_DIM = 64
RWKV_HEADS = D_MODEL // RWKV_HEAD_DIM
RWKV_LORA_W = 64
RWKV_LORA_A = 64
RWKV_LORA_G = 128
RWKV_LNX_EPS = 64e-5
FFN_HIDDEN = 2816
FFN_CONV = 3

kernel_name = 'hybrid_gmlp_moba_pool_rwkv7_step'


def rmsnorm(x, g):
    xf = x.astype(jnp.float32)
    y = xf * lax.rsqrt(jnp.mean(xf * xf, axis=-1, keepdims=True) + RMS_EPS)
    return (y * g.astype(jnp.float32)).astype(x.dtype)


def layernorm(x, g, b, eps):
    xf = x.astype(jnp.float32)
    mu = jnp.mean(xf, axis=-1, keepdims=True)
    var = jnp.mean(jnp.square(xf - mu), axis=-1, keepdims=True)
    return ((xf - mu) * lax.rsqrt(var + eps) * g + b).astype(x.dtype)


def chunk_gmlp(h, w_in, ln_g, ln_b, w_s, b_s, w_out):
    n, t, _ = h.shape
    z = jax.nn.gelu(h @ w_in)
    u, v = jnp.split(z, 2, axis=-1)
    v = layernorm(v, ln_g, ln_b, LN_EPS)
    n_chunks = -(-t // GM_CHUNK)
    vp = jnp.pad(v, ((0, 0), (0, n_chunks * GM_CHUNK - t), (0, 0)))
    vc = vp.reshape(n, n_chunks, GM_CHUNK, GM_GROUPS, GM_GROUP_DIM)
    w_causal = w_s * jnp.tril(jnp.ones((GM_CHUNK, GM_CHUNK), w_s.dtype))
    s = jnp.einsum('gts,ncsgd->nctgd', w_causal, vc) + b_s.T[None, None, :, :, None]
    s = s.reshape(n, n_chunks * GM_CHUNK, GM_WIDTH)[:, :t]
    return (u * s) @ w_out, v


def t5_bucket(rel):
    n = jnp.maximum(rel, 0)
    exact = REL_BUCKETS // 2
    nf = jnp.maximum(n, 1).astype(jnp.float32)
    large = exact + (jnp.log(nf / exact) / math.log(REL_MAX_DIST / exact)
                     * (REL_BUCKETS - exact)).astype(jnp.int32)
    return jnp.where(n < exact, n, jnp.minimum(large, REL_BUCKETS - 1))


def pad_to_blocks(x):
    pad = (-x.shape[-3]) % MOBA_BLOCK
    return jnp.pad(x, [(0, 0)] * (x.ndim - 3) + [(0, pad), (0, 0), (0, 0)])


def moba_core(q, q_pos, k, v, rel_bias):
    f32 = jnp.float32
    nb = k.shape[0] // MOBA_BLOCK
    top = min(MOBA_TOPK, nb)
    kb = k.reshape(nb, MOBA_BLOCK, MOBA_HEADS, MOBA_HEAD_DIM)
    vb = v.reshape(nb, MOBA_BLOCK, MOBA_HEADS, MOBA_HEAD_DIM)
    k_mean = jnp.mean(kb.astype(f32), axis=1)
    own = q_pos // MOBA_BLOCK
    gate = jnp.einsum('qhd,nhd->qhn', q.astype(f32), k_mean)
    past = jnp.arange(nb)[None, None, :] < own[:, None, None]
    gate = jnp.where(past, gate, -jnp.inf)
    _, sel = lax.top_k(gate, top)
    sel_ok = sel < own[:, None, None]
    own_b = jnp.broadcast_to(own[:, None, None], sel.shape[:2] + (1,)).astype(sel.dtype)
    blocks = jnp.concatenate([sel, own_b], axis=-1)
    ok = jnp.concatenate([sel_ok, jnp.ones(sel.shape[:2] + (1,), bool)], axis=-1)
    h_idx = jnp.arange(MOBA_HEADS)[None, :, None]
    k_sel = kb.transpose(2, 0, 1, 3)[h_idx, blocks]
    v_sel = vb.transpose(2, 0, 1, 3)[h_idx, blocks]
    key_pos = blocks[..., None] * MOBA_BLOCK + jnp.arange(MOBA_BLOCK, dtype=blocks.dtype)
    rel = q_pos[:, None, None, None] - key_pos
    bias = rel_bias[t5_bucket(rel), h_idx[..., None]].astype(f32)
    logits = jnp.einsum('qhd,qhjsd->qhjs', q, k_sel).astype(f32) * (MOBA_HEAD_DIM ** -0.5) + bias
    logits = jnp.where(ok[..., None] & (rel >= 0), logits, -jnp.inf)
    p = jax.nn.softmax(logits, axis=(-2, -1))
    return jnp.einsum('qhjs,qhjsd->qhd', p.astype(v.dtype), v_sel)


def moba_qkv(h, w_qkv):
    n, t, _ = h.shape
    z = (h @ w_qkv).reshape(n, t, 3, MOBA_HEADS, MOBA_HEAD_DIM)
    return z[:, :, 0], z[:, :, 1], z[:, :, 2]


def moba_prompt(q, k, v, rel_bias):
    b, t = q.shape[:2]
    nq = t // MOBA_QBLOCK
    kp, vp = pad_to_blocks(k), pad_to_blocks(v)
    qc = q.reshape(b * nq, MOBA_QBLOCK, MOBA_HEADS, MOBA_HEAD_DIM)
    ids = jnp.arange(b * nq, dtype=jnp.int32)
    seq_idx = ids // nq
    pos = (ids % nq)[:, None] * MOBA_QBLOCK + jnp.arange(MOBA_QBLOCK, dtype=jnp.int32)[None, :]
    out = lax.map(lambda a: moba_core(a[0], a[2], kp[a[1]], vp[a[1]], rel_bias), (qc, seq_idx, pos))
    return out.reshape(b, t, MOBA_HEADS, MOBA_HEAD_DIM)


def moba_sample(q, k_new, v_new, cache_k, cache_v, layer, page_table, rel_bias):
    t = q.shape[1]
    pos = PAST_LEN + jnp.arange(t, dtype=jnp.int32)

    def one(a):
        pt, qs, ks, vs = a
        k_past = cache_k[layer, pt].reshape(-1, MOBA_HEADS, MOBA_HEAD_DIM)
        v_past = cache_v[layer, pt].reshape(-1, MOBA_HEADS, MOBA_HEAD_DIM)
        k_all = pad_to_blocks(jnp.concatenate([k_past, ks.astype(k_past.dtype)], axis=0))
        v_all = pad_to_blocks(jnp.concatenate([v_past, vs.astype(v_past.dtype)], axis=0))
        return moba_core(qs, pos, k_all, v_all, rel_bias)

    return lax.map(one, (page_table, q, k_new, v_new))


def pool_mixer(h, prev, start, w, scale):
    f32 = jnp.float32
    n, t, _ = h.shape
    xp = jnp.concatenate([prev.astype(h.dtype), h], axis=1)
    cs = jnp.cumsum(jnp.pad(xp.astype(f32), ((0, 0), (1, 0), (0, 0))), axis=1)
    n_avail = start + jnp.arange(t, dtype=jnp.int32) + 1
    hf = h.astype(f32)
    outs = []
    for g, win in enumerate(POOL_WINDOWS):
        lo, hi = g * POOL_GROUP_DIM, (g + 1) * POOL_GROUP_DIM
        wsum = (cs[:, POOL_CTX + 1:POOL_CTX + 1 + t, lo:hi]
                - cs[:, POOL_CTX + 1 - win:POOL_CTX + 1 - win + t, lo:hi])
        cnt = jnp.minimum(n_avail, win).astype(f32)[None, :, None]
        outs.append((wsum / cnt - hf[..., lo:hi]) @ w[g])
    y = jnp.concatenate(outs, axis=-1) * scale
    return y.astype(h.dtype), xp[:, -POOL_CTX:]


def rwkv7_mixer(h, shift_prev, wkv_prev, mu, w_r, w_k, w_v, w_o, w0, w1, w2,
                a0, a1, a2, g1, g2, k_k, k_a, r_k, lnx_g, lnx_b):
    f32 = jnp.float32
    n, t, d = h.shape
    hf = h.astype(f32)
    xx = jnp.concatenate([shift_prev[:, None].astype(f32), hf[:, :-1]], axis=1) - hf
    mu = mu.astype(f32)
    xr, xw, xk, xv, xa, xg = [hf + xx * mu[m] for m in range(6)]
    r = xr @ w_r
    k = xk @ w_k
    v = xv @ w_v
    w_log = -jax.nn.softplus(-(w0 + jnp.tanh(xw @ w1) @ w2)) - 0.5
    decay = jnp.exp(-jnp.exp(w_log))
    a = jax.nn.sigmoid(a0 + (xa @ a1) @ a2)
    g = jax.nn.sigmoid(xg @ g1) @ g2

    def heads(z):
        return z.reshape(n, t, RWKV_HEADS, RWKV_HEAD_DIM)

    kk = heads(k * k_k)
    kk = kk / jnp.maximum(jnp.linalg.norm(kk, axis=-1, keepdims=True), 1e-12)
    k = k * (1.0 + (a - 1.0) * k_a)
    r, k, v, decay, a = heads(r), heads(k), heads(v), heads(decay), heads(a)

    def step(S, inp):
        r_t, w_t, k_t, v_t, kk_t, a_t = inp
        sa = jnp.einsum('nhij,nhj->nhi', S, -kk_t)
        S = (S * w_t[:, :, None, :] + sa[..., None] * (kk_t * a_t)[:, :, None, :]
             + v_t[..., None] * k_t[:, :, None, :])
        return S, jnp.einsum('nhij,nhj->nhi', S, r_t)

    xs = tuple(jnp.swapaxes(z, 0, 1) for z in (r, decay, k, v, kk, a))
    S, o = lax.scan(step, wkv_prev.astype(f32), xs)
    o = jnp.swapaxes(o, 0, 1)
    o = layernorm(o, lnx_g.reshape(RWKV_HEADS, RWKV_HEAD_DIM),
                  lnx_b.reshape(RWKV_HEADS, RWKV_HEAD_DIM), RWKV_LNX_EPS)
    o = o + jnp.sum(r * k * r_k, axis=-1, keepdims=True) * v
    y = (o.reshape(n, t, d) * g) @ w_o
    return y.astype(h.dtype), h[:, -1], S


def conv_ffn(h, prev, w_in, conv_w, conv_b, w_out):
    t = h.shape[1]
    gate, up = jnp.split(h @ w_in, 2, axis=-1)
    gp = jnp.concatenate([prev.astype(gate.dtype), gate], axis=1)
    c = sum((conv_w[j] * gp[:, j:j + t] for j in range(FFN_CONV)), conv_b)
    return (jax.nn.gelu(c) * up) @ w_out, gp[:, -(FFN_CONV - 1):]


def _keys(key):
    i = 0
    while True:
        yield jax.random.fold_in(key, i)
        i += 1


def setup_inputs(seed: int = 0) -> dict:
    key = jax.random.key(seed)
    ks = _keys(key)
    f32 = jnp.float32

    def nrm(shape, scale=1.0):
        return jax.random.normal(next(ks), shape, f32) * scale

    def unif(shape, lo, hi):
        return jax.random.uniform(next(ks), shape, f32, lo, hi)

    D = D_MODEL
    H, Dh = MOBA_HEADS, MOBA_HEAD_DIM
    RH, RN = RWKV_HEADS, RWKV_HEAD_DIM
    n_pages = PAST_LEN // PAGE_SIZE
    n_pool = (DEC_BATCH * n_pages * 5) // 4
    inp = {}
    inp['x_prompt'] = nrm((BATCH, SEQ, D))
    inp['x_sample'] = nrm((DEC_BATCH, DEC_SEQ, D))
    inp['cache_moba_k'] = nrm((N_LAYERS_B, n_pool, PAGE_SIZE, H, Dh))
    inp['cache_moba_v'] = nrm((N_LAYERS_B, n_pool, PAGE_SIZE, H, Dh))
    inp['state_pool'] = nrm((N_LAYERS_C, DEC_BATCH, POOL_CTX, D))
    inp['state_rwkv_wkv'] = nrm((N_LAYERS_D, DEC_BATCH, RH, RN, RN), 0.3)
    inp['state_rwkv_shift'] = nrm((N_LAYERS_D, DEC_BATCH, D))
    inp['state_ffn_conv'] = nrm((DEPTH, DEC_BATCH, FFN_CONV - 1, FFN_HIDDEN))
    perm = jax.random.permutation(next(ks), n_pool)[:DEC_BATCH * n_pages]
    inp['page_table'] = perm.reshape(DEC_BATCH, n_pages).astype(jnp.int32)
    inp['norm_mix_g'] = 1.0 + nrm((DEPTH, D), 0.05)
    inp['norm_ffn_g'] = 1.0 + nrm((DEPTH, D), 0.05)
    inp['norm_final_g'] = 1.0 + nrm((D,), 0.05)
    inp['rel_bias'] = nrm((REL_BUCKETS, H), 0.3)
    inp['gm_w_in'] = nrm((N_LAYERS_A, D, 2 * GM_WIDTH), D ** -0.5)
    inp['gm_ln_g'] = 1.0 + nrm((N_LAYERS_A, GM_WIDTH), 0.05)
    inp['gm_ln_b'] = nrm((N_LAYERS_A, GM_WIDTH), 0.02)
    inp['gm_w_s'] = nrm((N_LAYERS_A, GM_GROUPS, GM_CHUNK, GM_CHUNK), 0.5 * GM_CHUNK ** -0.5)
    inp['gm_b_s'] = 1.0 + nrm((N_LAYERS_A, GM_GROUPS, GM_CHUNK), 0.05)
    inp['gm_w_out'] = nrm((N_LAYERS_A, GM_WIDTH, D), GM_WIDTH ** -0.5)
    inp['moba_w_qkv'] = nrm((N_LAYERS_B, D, 3 * D), D ** -0.5)
    inp['moba_w_o'] = nrm((N_LAYERS_B, D, D), D ** -0.5)
    inp['pool_w'] = nrm((N_LAYERS_C, POOL_GROUPS, POOL_GROUP_DIM, POOL_GROUP_DIM), POOL_GROUP_DIM ** -0.5)
    inp['pool_scale'] = 0.5 + nrm((N_LAYERS_C, D), 0.05)
    inp['rwkv_mu'] = unif((N_LAYERS_D, 6, D), 0.0, 1.0)
    inp['rwkv_w_r'] = nrm((N_LAYERS_D, D, D), D ** -0.5)
    inp['rwkv_w_k'] = nrm((N_LAYERS_D, D, D), D ** -0.5)
    inp['rwkv_w_v'] = nrm((N_LAYERS_D, D, D), D ** -0.5)
    inp['rwkv_w_o'] = nrm((N_LAYERS_D, D, D), D ** -0.5)
    inp['rwkv_w0'] = unif((N_LAYERS_D, D), -4.0, 0.0)
    inp['rwkv_w1'] = nrm((N_LAYERS_D, D, RWKV_LORA_W), D ** -0.5)
    inp['rwkv_w2'] = nrm((N_LAYERS_D, RWKV_LORA_W, D), 0.5 * RWKV_LORA_W ** -0.5)
    inp['rwkv_a0'] = nrm((N_LAYERS_D, D), 0.1)
    inp['rwkv_a1'] = nrm((N_LAYERS_D, D, RWKV_LORA_A), D ** -0.5)
    inp['rwkv_a2'] = nrm((N_LAYERS_D, RWKV_LORA_A, D), RWKV_LORA_A ** -0.5)
    inp['rwkv_g1'] = nrm((N_LAYERS_D, D, RWKV_LORA_G), D ** -0.5)
    inp['rwkv_g2'] = nrm((N_LAYERS_D, RWKV_LORA_G, D), RWKV_LORA_G ** -0.5)
    inp['rwkv_k_k'] = 0.85 + nrm((N_LAYERS_D, D), 0.05)
    inp['rwkv_k_a'] = 1.0 + nrm((N_LAYERS_D, D), 0.05)
    inp['rwkv_r_k'] = nrm((N_LAYERS_D, RH, RN), 0.1)
    inp['rwkv_lnx_g'] = 1.0 + nrm((N_LAYERS_D, D), 0.05)
    inp['rwkv_lnx_b'] = nrm((N_LAYERS_D, D), 0.02)
    inp['ffn_w_in'] = nrm((DEPTH, D, 2 * FFN_HIDDEN), D ** -0.5)
    inp['ffn_conv_w'] = nrm((DEPTH, FFN_CONV, FFN_HIDDEN), FFN_CONV ** -0.5)
    inp['ffn_conv_b'] = nrm((DEPTH, FFN_HIDDEN), 0.02)
    inp['ffn_w_out'] = nrm((DEPTH, FFN_HIDDEN, D), FFN_HIDDEN ** -0.5)
    return inp


def reference(x_prompt, x_sample, cache_moba_k, cache_moba_v, state_pool, state_rwkv_wkv,
              state_rwkv_shift, state_ffn_conv, page_table, norm_mix_g, norm_ffn_g, norm_final_g,
              rel_bias, gm_w_in, gm_ln_g, gm_ln_b, gm_w_s, gm_b_s, gm_w_out, moba_w_qkv, moba_w_o,
              pool_w, pool_scale, rwkv_mu, rwkv_w_r, rwkv_w_k, rwkv_w_v, rwkv_w_o, rwkv_w0, rwkv_w1,
              rwkv_w2, rwkv_a0, rwkv_a1, rwkv_a2, rwkv_g1, rwkv_g2, rwkv_k_k, rwkv_k_a, rwkv_r_k,
              rwkv_lnx_g, rwkv_lnx_b, ffn_w_in, ffn_conv_w, ffn_conv_b, ffn_w_out):
    bp, tp, D = x_prompt.shape
    bs, ts, _ = x_sample.shape
    yp, ys = x_prompt, x_sample
    gm_v_s = []
    k_p, v_p, k_s, v_s = [], [], [], []
    pool_p, pool_s = [], []
    wkv_p, wkv_s, sh_p, sh_s = [], [], [], []
    conv_p, conv_s = [], []
    for i in range(DEPTH):
        j = i // N_MIXERS
        kind = i % N_MIXERS
        hp = rmsnorm(yp, norm_mix_g[i])
        hs = rmsnorm(ys, norm_mix_g[i])
        if kind == 0:
            gm = (gm_w_in[j], gm_ln_g[j], gm_ln_b[j], gm_w_s[j], gm_b_s[j], gm_w_out[j])
            mp, _ = chunk_gmlp(hp, *gm)
            ms, vrow = chunk_gmlp(hs, *gm)
            gm_v_s.append(vrow)
        elif kind == 1:
            qp, kp, vp = moba_qkv(hp, moba_w_qkv[j])
            qs, kss, vss = moba_qkv(hs, moba_w_qkv[j])
            op = moba_prompt(qp, kp, vp, rel_bias)
            os_ = moba_sample(qs, kss, vss, cache_moba_k, cache_moba_v, j, page_table, rel_bias)
            mp = op.reshape(bp, tp, D) @ moba_w_o[j]
            ms = os_.reshape(bs, ts, D) @ moba_w_o[j]
            k_p.append(kp)
            v_p.append(vp)
            k_s.append(kss)
            v_s.append(vss)
        elif kind == 2:
            mp, st_p = pool_mixer(hp, jnp.zeros((bp, POOL_CTX, D), hp.dtype), 0, pool_w[j], pool_scale[j])
            ms, st_s = pool_mixer(hs, state_pool[j], PAST_LEN, pool_w[j], pool_scale[j])
            pool_p.append(st_p)
            pool_s.append(st_s)
        else:
            rw = (rwkv_mu[j], rwkv_w_r[j], rwkv_w_k[j], rwkv_w_v[j], rwkv_w_o[j], rwkv_w0[j],
                  rwkv_w1[j], rwkv_w2[j], rwkv_a0[j], rwkv_a1[j], rwkv_a2[j], rwkv_g1[j], rwkv_g2[j],
                  rwkv_k_k[j], rwkv_k_a[j], rwkv_r_k[j], rwkv_lnx_g[j], rwkv_lnx_b[j])
            mp, shp, Sp = rwkv7_mixer(hp, jnp.zeros((bp, D), hp.dtype),
                                      jnp.zeros((bp, RWKV_HEADS, RWKV_HEAD_DIM, RWKV_HEAD_DIM), jnp.float32), *rw)
            ms, shs, Ss = rwkv7_mixer(hs, state_rwkv_shift[j], state_rwkv_wkv[j], *rw)
            wkv_p.append(Sp)
            wkv_s.append(Ss)
            sh_p.append(shp)
            sh_s.append(shs)
        yp = yp + mp
        ys = ys + ms
        ff = (ffn_w_in[i], ffn_conv_w[i], ffn_conv_b[i], ffn_w_out[i])
        fp, cp = conv_ffn(rmsnorm(yp, norm_ffn_g[i]), jnp.zeros((bp, FFN_CONV - 1, FFN_HIDDEN), yp.dtype), *ff)
        fs, cs = conv_ffn(rmsnorm(ys, norm_ffn_g[i]), state_ffn_conv[i], *ff)
        conv_p.append(cp)
        conv_s.append(cs)
        yp = yp + fp
        ys = ys + fs
    y_prompt = rmsnorm(yp, norm_final_g)
    y_sample = rmsnorm(ys, norm_final_g)
    gm_v_sample = jnp.stack(gm_v_s)
    moba_k_prompt = jnp.stack(k_p)
    moba_v_prompt = jnp.stack(v_p)
    moba_k_sample = jnp.stack(k_s)
    moba_v_sample = jnp.stack(v_s)
    pool_prompt = jnp.stack(pool_p)
    pool_sample = jnp.stack(pool_s)
    wkv_prompt = jnp.stack(wkv_p)
    wkv_sample = jnp.stack(wkv_s)
    shift_prompt = jnp.stack(sh_p)
    shift_sample = jnp.stack(sh_s)
    conv_prompt = jnp.stack(conv_p)
    conv_sample = jnp.stack(conv_s)
    return (y_prompt, y_sample, gm_v_sample, moba_k_prompt, moba_v_prompt, moba_k_sample,
            moba_v_sample, pool_prompt, pool_sample, wkv_prompt, wkv_sample, shift_prompt,
            shift_sample, conv_prompt, conv_sample)
```

```python
import functools
import math

import jax
import jax.numpy as jnp
from jax import lax
from jax.experimental import pallas as pl
from jax.experimental.pallas import tpu as pltpu

F32 = jnp.float32
BF16 = jnp.bfloat16
I32 = jnp.int32

LANES = 128
SUBLANES = 8
VMEM_LIMIT_BYTES = 56 * 2**20

RMS_EPS = 1e-6
GM_LN_EPS = 1e-5
GM_CHUNK = 128
GM_GROUPS = 8
MOBA_HEADS = 16
MOBA_BLOCK = 256
MOBA_TOPK = 3
REL_BUCKETS = 32
REL_MAX_DIST = 128
PAGE_SIZE = 128
POOL_WINDOWS = (2, 4, 8, 16)
POOL_CTX = max(POOL_WINDOWS) - 1
RWKV_HEAD_DIM = 64
RWKV_LNX_EPS = 64e-5
FFN_CONV = 3
NEG_INF = float("-inf")

assert MOBA_BLOCK >= REL_MAX_DIST


def _cparams(*sem):
    return pltpu.CompilerParams(dimension_semantics=sem, vmem_limit_bytes=VMEM_LIMIT_BYTES)


def _whole(shape):
    nd = len(shape)
    return pl.BlockSpec(shape, lambda *_: (0,) * nd, pipeline_mode=pl.Buffered(1))


def _rows(tm, width):
    return pl.BlockSpec((tm, width), lambda i: (i, 0))


def _rms(x, g):
    return x * lax.rsqrt(jnp.mean(x * x, axis=-1, keepdims=True) + RMS_EPS) * g


def _bdot(a, b):
    return jnp.dot(a.astype(BF16), b.astype(BF16), preferred_element_type=F32)


def _nt_dot(a, b):
    return lax.dot_general(a.astype(BF16), b.astype(BF16), (((1,), (1,)), ((), ())),
                           preferred_element_type=F32)


def _split(x):
    hi = x.astype(BF16)
    lo = (x - hi.astype(F32)).astype(BF16)
    return hi, lo


def _split_dot(x, m):
    hi, lo = _split(x)
    return (jnp.dot(hi, m, preferred_element_type=F32)
            + jnp.dot(lo, m, preferred_element_type=F32))


def _same_head(n, head_dim):
    shift = int(math.log2(head_dim))
    r = lax.broadcasted_iota(I32, (n, n), 0) >> shift
    c = lax.broadcasted_iota(I32, (n, n), 1) >> shift
    return jnp.where(r == c, 1.0, 0.0).astype(BF16)


def _seg_sum(x, head_dim):
    g = _same_head(LANES, head_dim)
    parts = [_split_dot(x[:, i:i + LANES], g) for i in range(0, x.shape[1], LANES)]
    return jnp.concatenate(parts, axis=1)


def _gelu(x):
    return 0.5 * x * (1.0 + jnp.tanh(0.7978845608028654 * (x + 0.044715 * x * x * x)))


def _sigmoid(x):
    return 1.0 / (1.0 + jnp.exp(-x))


def _softplus(x):
    return jnp.maximum(x, 0.0) + jnp.log(1.0 + jnp.exp(-jnp.abs(x)))


def _top_mask(gate, idx, axis, n_valid):
    cur = gate
    sel = jnp.zeros(gate.shape, F32)
    for _ in range(MOBA_TOPK):
        m = jnp.max(cur, axis=axis, keepdims=True)
        first = jnp.min(jnp.where(cur == m, idx, n_valid), axis=axis, keepdims=True)
        pick = (idx == first) & (m > NEG_INF)
        sel = jnp.where(pick, 1.0, sel)
        cur = jnp.where(pick, NEG_INF, cur)
    return sel


def _gmlp_kernel(x_ref, ng_ref, win_ref, lng_ref, lnb_ref, sa_ref, sb_ref, wout_ref, y_ref, aux_ref,
                 *, sample):
    x = x_ref[...]
    tm, width = x.shape[0], wout_ref.shape[0]
    h = _rms(x, ng_ref[...])
    z = _gelu(_bdot(h, win_ref[...]))
    u, v = z[:, :width], z[:, width:]
    mu = jnp.mean(v, axis=-1, keepdims=True)
    d = v - mu
    var = jnp.mean(d * d, axis=-1, keepdims=True)
    v = d * lax.rsqrt(var + GM_LN_EPS) * lng_ref[...] + lnb_ref[...]
    if sample:
        aux_ref[...] = v
        s = v * sa_ref[...] + sb_ref[...]
    else:
        gd = width // GM_GROUPS
        causal = (lax.broadcasted_iota(I32, (GM_CHUNK, GM_CHUNK), 0)
                  >= lax.broadcasted_iota(I32, (GM_CHUNK, GM_CHUNK), 1))
        for g in range(GM_GROUPS):
            wg = jnp.where(causal, sa_ref[g], 0.0).astype(BF16)
            cols = slice(g * gd, (g + 1) * gd)
            for c in range(tm // GM_CHUNK):
                rows = slice(c * GM_CHUNK, (c + 1) * GM_CHUNK)
                aux_ref[rows, cols] = (jnp.dot(wg, v[rows, cols].astype(BF16), preferred_element_type=F32)
                                       + sb_ref[:, cols])
        s = aux_ref[...]
    y_ref[...] = x + _bdot(u * s, wout_ref[...])


def _gmlp(x, ng, win, lng, lnb, sa, sb, wout, *, tm, sample):
    n, d = x.shape
    width = wout.shape[0]
    outs = [jax.ShapeDtypeStruct((n, d), F32)]
    out_specs = [_rows(tm, d)]
    scratch = []
    if sample:
        outs.append(jax.ShapeDtypeStruct((n, width), F32))
        out_specs.append(_rows(tm, width))
    else:
        scratch.append(pltpu.VMEM((tm, width), F32))
    return pl.pallas_call(
        functools.partial(_gmlp_kernel, sample=sample),
        out_shape=outs, grid=(n // tm,),
        in_specs=[_rows(tm, d), _whole(ng.shape), _whole(win.shape), _whole(lng.shape), _whole(lnb.shape),
                  _whole(sa.shape), _whole(sb.shape), _whole(wout.shape)],
        out_specs=out_specs, scratch_shapes=scratch,
        compiler_params=_cparams("parallel"), name="gmlp_sample" if sample else "gmlp_prompt",
    )(x, ng, win, lng, lnb, sa, sb, wout)


def _norm_linear_kernel(x_ref, ng_ref, w_ref, *o_refs):
    z = _bdot(_rms(x_ref[...], ng_ref[...]), w_ref[...])
    wd = z.shape[1] // len(o_refs)
    for i, o_ref in enumerate(o_refs):
        o_ref[...] = z[:, i * wd:(i + 1) * wd]


def _norm_linear(x, ng, w, n_out, *, tm, name):
    n, d = x.shape
    wd = w.shape[1] // n_out
    return pl.pallas_call(
        _norm_linear_kernel,
        out_shape=[jax.ShapeDtypeStruct((n, wd), F32)] * n_out, grid=(n // tm,),
        in_specs=[_rows(tm, d), _whole(ng.shape), _whole(w.shape)],
        out_specs=[_rows(tm, wd)] * n_out,
        compiler_params=_cparams("parallel"), name=name,
    )(x, ng, w)


def _linear_res_kernel(a_ref, w_ref, y_ref, o_ref):
    o_ref[...] = y_ref[...] + _bdot(a_ref[...], w_ref[...])


def _linear_res(a, w, y, *, tm, name):
    n, d = y.shape
    return pl.pallas_call(
        _linear_res_kernel, out_shape=jax.ShapeDtypeStruct((n, d), F32), grid=(n // tm,),
        in_specs=[_rows(tm, a.shape[1]), _whole(w.shape), _rows(tm, d)], out_specs=_rows(tm, d),
        compiler_params=_cparams("parallel"), name=name,
    )(a, w, y)


def _t5_bucket_table(rel):
    n = jnp.maximum(rel, 0)
    exact = REL_BUCKETS // 2
    nf = jnp.maximum(n, 1).astype(F32)
    large = exact + (jnp.log(nf / exact) / math.log(REL_MAX_DIST / exact)
                     * (REL_BUCKETS - exact)).astype(I32)
    return jnp.where(n < exact, n, jnp.minimum(large, REL_BUCKETS - 1)).astype(I32)


def _bias_tiles_kernel(rb_ref, bkt_ref, o_ref):
    h = pl.program_id(0)
    bkt = bkt_ref[...]
    acc = jnp.zeros(bkt.shape, F32)
    for b in range(REL_BUCKETS):
        acc = jnp.where(bkt == b, rb_ref[b, h], acc)
    o_ref[0] = acc


def _bias_tiles(rel_bias, bkt):
    heads = rel_bias.shape[1]
    return pl.pallas_call(
        _bias_tiles_kernel, out_shape=jax.ShapeDtypeStruct((heads,) + bkt.shape, F32), grid=(heads,),
        in_specs=[pl.BlockSpec(memory_space=pltpu.SMEM), _whole(bkt.shape)],
        out_specs=pl.BlockSpec((1,) + bkt.shape, lambda h: (h, 0, 0, 0)),
        compiler_params=_cparams("parallel"), name="moba_bias_tiles",
    )(rel_bias, bkt)


def _bias_rows_kernel(rb_ref, bkt_ref, o_ref):
    bkt = bkt_ref[...]
    acc = jnp.zeros(bkt.shape, F32)
    for b in range(REL_BUCKETS):
        acc = jnp.where(bkt == b, rb_ref[b:b + 1, :], acc)
    o_ref[...] = acc


def _bias_rows(rb_pad, bkt):
    return pl.pallas_call(
        _bias_rows_kernel, out_shape=jax.ShapeDtypeStruct(bkt.shape, F32), grid=(1,),
        in_specs=[_whole(rb_pad.shape), _whole(bkt.shape)], out_specs=_whole(bkt.shape),
        compiler_params=_cparams("arbitrary"), name="moba_bias_rows",
    )(rb_pad, bkt)


def _moba_prompt_kernel(rb_ref, q_ref, k_ref, v_ref, bias_ref, o_ref, kmean_scr, *, head_dim):
    pair, qb = pl.program_id(1), pl.program_id(2)
    blk = MOBA_BLOCK
    n_blocks = k_ref.shape[0] // blk
    heads_per_tile = LANES // head_dim
    shift = int(math.log2(head_dim))
    scale = head_dim ** -0.5

    @pl.when(qb == 0)
    def _():
        kmean_scr[...] = jnp.zeros(kmean_scr.shape, F32)
        for n in range(n_blocks):
            kmean_scr[n:n + 1, :] = jnp.mean(k_ref[n * blk:(n + 1) * blk, :], axis=0, keepdims=True)

    q = q_ref[...]
    lane_head = lax.broadcasted_iota(I32, (blk, LANES), 1) >> shift
    block_id = lax.broadcasted_iota(I32, (blk, LANES), 1)
    causal = (lax.broadcasted_iota(I32, (blk, blk), 1) <= lax.broadcasted_iota(I32, (blk, blk), 0))
    km_hi, km_lo = _split(kmean_scr[...])
    own = pl.multiple_of(qb * blk, blk)
    k_own = k_ref[pl.ds(own, blk), :]
    v_own = v_ref[pl.ds(own, blk), :]
    out = jnp.zeros((blk, LANES), F32)
    for hh in range(heads_per_tile):
        in_head = lane_head == hh
        qh = jnp.where(in_head, q, 0.0)
        q_hi, q_lo = _split(qh)
        nt = lambda a, b: lax.dot_general(a, b, (((1,), (1,)), ((), ())), preferred_element_type=F32)
        gate = nt(q_hi, km_hi) + nt(q_hi, km_lo) + nt(q_lo, km_hi)
        sel = _top_mask(jnp.where(block_id < qb, gate, NEG_INF), block_id, -1, LANES)
        far_bias = rb_ref[REL_BUCKETS - 1, pair * heads_per_tile + hh]

        s = _nt_dot(q_hi, k_own) * scale + bias_ref[hh, 0]
        s = jnp.where(causal, s, NEG_INF)
        m0 = jnp.max(s, axis=-1, keepdims=True)
        p0 = jnp.exp(s - m0)
        carry0 = (m0, jnp.sum(p0, axis=-1, keepdims=True), _bdot(p0, v_own))

        def past_block(n, carry, q_hi=q_hi, sel=sel, far_bias=far_bias, hh=hh):
            m, l, acc = carry
            off = pl.multiple_of(n * blk, blk)
            s = _nt_dot(q_hi, k_ref[pl.ds(off, blk), :]) * scale
            bias = jnp.where(qb - n >= 2, far_bias, bias_ref[hh, 1])
            chosen = jnp.sum(jnp.where(block_id == n, sel, 0.0), axis=-1, keepdims=True)
            s = jnp.where(chosen > 0.0, s + bias, NEG_INF)
            m_new = jnp.maximum(m, jnp.max(s, axis=-1, keepdims=True))
            alpha = jnp.exp(m - m_new)
            p = jnp.exp(s - m_new)
            l = alpha * l + jnp.sum(p, axis=-1, keepdims=True)
            acc = alpha * acc + _bdot(p, v_ref[pl.ds(off, blk), :])
            return m_new, l, acc

        _, l, acc = lax.fori_loop(0, qb, past_block, carry0)
        out = jnp.where(in_head, acc / l, out)
    o_ref[...] = out


def _moba_prompt(rel_bias, q, k, v, bias_tiles, *, batch, seq, head_dim):
    n, d = q.shape
    blk = MOBA_BLOCK
    nq = seq // blk
    hpt = LANES // head_dim
    return pl.pallas_call(
        functools.partial(_moba_prompt_kernel, head_dim=head_dim),
        out_shape=jax.ShapeDtypeStruct((n, d), F32), grid=(batch, d // LANES, nq),
        in_specs=[pl.BlockSpec(memory_space=pltpu.SMEM),
                  pl.BlockSpec((blk, LANES), lambda b, p, i: (b * nq + i, p)),
                  pl.BlockSpec((seq, LANES), lambda b, p, i: (b, p)),
                  pl.BlockSpec((seq, LANES), lambda b, p, i: (b, p)),
                  pl.BlockSpec((hpt, 2, blk, blk), lambda b, p, i: (p, 0, 0, 0))],
        out_specs=pl.BlockSpec((blk, LANES), lambda b, p, i: (b * nq + i, p)),
        scratch_shapes=[pltpu.VMEM((LANES, LANES), F32)],
        compiler_params=_cparams("parallel", "parallel", "arbitrary"), name="moba_prompt_attn",
    )(rel_bias, q, k, v, bias_tiles)


def _moba_sample_kernel(pt_ref, q_ref, kn_ref, vn_ref, ka_ref, kb_ref, va_ref, vb_ref, bias_ref, rb_ref,
                        g_ref, gt_ref, o_ref, m_scr, l_scr, gs_scr, acc_scr, *, head_dim):
    del pt_ref
    n = pl.program_id(1)
    n_blocks = pl.num_programs(1)
    scale = head_dim ** -0.5
    q = q_ref[0]
    g, gt = g_ref[...], gt_ref[...]
    logits, gsum = [], None
    for k_page in (ka_ref, kb_ref):
        prod = k_page[0] * q
        part = jnp.sum(prod, axis=0, keepdims=True)
        gsum = part if gsum is None else gsum + part
        logits.append(jnp.dot(prod.astype(BF16), g, preferred_element_type=F32))
    s = jnp.concatenate(logits, axis=0) * scale + bias_ref[n]
    m = jnp.max(s, axis=0, keepdims=True)
    p = jnp.exp(s - m)
    p_wide = jnp.dot(p.astype(BF16), gt, preferred_element_type=F32)
    acc = (jnp.sum(p_wide[:PAGE_SIZE] * va_ref[0], axis=0, keepdims=True)
           + jnp.sum(p_wide[PAGE_SIZE:] * vb_ref[0], axis=0, keepdims=True))
    m_scr[pl.ds(n, 1), :] = m
    l_scr[pl.ds(n, 1), :] = jnp.sum(p, axis=0, keepdims=True)
    gs_scr[pl.ds(n, 1), :] = gsum
    acc_scr[pl.ds(n, 1), :] = acc

    @pl.when(n == n_blocks - 1)
    def _():
        nb = m_scr.shape[0]
        gate = _split_dot(gs_scr[...], g) * (1.0 / MOBA_BLOCK)
        sel = _top_mask(gate, lax.broadcasted_iota(I32, gate.shape, 0), 0, nb)
        rows = lambda x: jnp.broadcast_to(x, (SUBLANES,) + x.shape[1:])
        s_new = jnp.dot(rows(kn_ref[0] * q).astype(BF16), g, preferred_element_type=F32)[0:1]
        s_new = s_new * scale + rb_ref[0:1, :]
        m_all = m_scr[...]
        m_tot = jnp.maximum(jnp.max(jnp.where(sel > 0.0, m_all, NEG_INF), axis=0, keepdims=True), s_new)
        w_blk = jnp.where(sel > 0.0, jnp.exp(m_all - m_tot), 0.0)
        w_new = jnp.exp(s_new - m_tot)
        den = jnp.sum(w_blk * l_scr[...], axis=0, keepdims=True) + w_new
        num = (jnp.sum(_split_dot(w_blk, gt) * acc_scr[...], axis=0, keepdims=True)
               + _split_dot(rows(w_new), gt)[0:1] * vn_ref[0])
        o_ref[0] = num / _split_dot(rows(den), gt)[0:1]


def _moba_sample(page_table, q, k_new, v_new, cache_k, cache_v, bias_rows, rb_pad, *, head_dim):
    nb, d = q.shape
    n_pages = page_table.shape[1]
    per_block = MOBA_BLOCK // PAGE_SIZE
    assert per_block == 2
    n_blocks = n_pages // per_block
    heads = d // head_dim
    col_head = jnp.arange(d, dtype=I32) // head_dim
    g = (col_head[:, None] == jnp.arange(LANES, dtype=I32)[None, :]).astype(BF16)
    row = lambda x: x.reshape(nb, 1, d)
    vec = pl.BlockSpec((1, 1, d), lambda s, n, pt: (s, 0, 0))
    page = lambda j: pl.BlockSpec((1, PAGE_SIZE, d), lambda s, n, pt: (pt[s, per_block * n + j], 0, 0))
    const = lambda shape: pl.BlockSpec(shape, lambda s, n, pt: (0,) * len(shape))
    del heads
    out = pl.pallas_call(
        functools.partial(_moba_sample_kernel, head_dim=head_dim),
        out_shape=jax.ShapeDtypeStruct((nb, 1, d), F32),
        grid_spec=pltpu.PrefetchScalarGridSpec(
            num_scalar_prefetch=1, grid=(nb, n_blocks),
            in_specs=[vec, vec, vec, page(0), page(1), page(0), page(1),
                      const((n_blocks, MOBA_BLOCK, LANES)), const(rb_pad.shape),
                      const(g.shape), const((LANES, d))],
            out_specs=vec,
            scratch_shapes=[pltpu.VMEM((n_blocks, LANES), F32), pltpu.VMEM((n_blocks, LANES), F32),
                            pltpu.VMEM((n_blocks, d), F32), pltpu.VMEM((n_blocks, d), F32)]),
        compiler_params=_cparams("parallel", "arbitrary"), name="moba_sample_attn",
    )(page_table, row(q), row(k_new), row(v_new), cache_k, cache_k, cache_v, cache_v,
      bias_rows.reshape(n_blocks, MOBA_BLOCK, LANES), rb_pad, g, g.T)
    return out.reshape(nb, d)


POOL_PAD = 32


def _pool_prompt_kernel(y_ref, ng_ref, w_ref, sc_ref, yout_ref, st_ref, b0, b1, b2, b3, b4, *, tps):
    i = pl.program_id(0)
    y = y_ref[...]
    tm, d = y.shape
    gd = d // len(POOL_WINDOWS)
    h = _rms(y, ng_ref[...])
    half = POOL_PAD // 2

    @pl.when(i == 0)
    def _():
        for b in (b0, b1, b2, b3, b4):
            b[0:half, :] = jnp.zeros((half, b.shape[1]), F32)

    @pl.when(i % tps == 0)
    def _():
        b0[half:POOL_PAD, :] = jnp.zeros((half, d), F32)

    b0[POOL_PAD:, :] = h
    stages = (b0, b1, b2, b3, b4)
    for k in range(1, len(stages)):
        src, dst = stages[k - 1], stages[k]
        back = 2 ** (k - 1)
        off = src.shape[1] - dst.shape[1]
        dst[SUBLANES:, :] = (src[SUBLANES:, off:] + src[SUBLANES - back:tm + POOL_PAD - back, off:])
    pos = (i % tps) * tm + lax.broadcasted_iota(I32, (tm, 1), 0) + 1
    for g, win in enumerate(POOL_WINDOWS):
        cols = slice(g * gd, (g + 1) * gd)
        wsum = stages[g + 1][POOL_PAD:, 0:gd]
        cnt = jnp.minimum(pos, win).astype(F32)
        mixed = _bdot(wsum / cnt - h[:, cols], w_ref[g])
        yout_ref[:, cols] = y[:, cols] + mixed * sc_ref[:, cols]
    st_ref[0] = b0[tm + POOL_PAD - POOL_CTX:, :]
    b0[half:POOL_PAD, :] = b0[tm + half:, :]


def _pool_prompt(y, ng, w, sc, *, tm, seq):
    n, d = y.shape
    gd = d // len(POOL_WINDOWS)
    tps = seq // tm
    rows = tm + POOL_PAD
    return pl.pallas_call(
        functools.partial(_pool_prompt_kernel, tps=tps),
        out_shape=[jax.ShapeDtypeStruct((n, d), F32), jax.ShapeDtypeStruct((n // seq, POOL_CTX, d), F32)],
        grid=(n // tm,),
        in_specs=[_rows(tm, d), _whole(ng.shape), _whole(w.shape), _whole(sc.shape)],
        out_specs=[_rows(tm, d), pl.BlockSpec((1, POOL_CTX, d), lambda i: (i // tps, 0, 0))],
        scratch_shapes=[pltpu.VMEM((rows, d - k * gd), F32) for k in (0, 0, 1, 2, 3)],
        compiler_params=_cparams("arbitrary"), name="pool_prompt",
    )(y, ng, w, sc)


def _pool_sample_kernel(y_ref, ng_ref, prev_ref, w_ref, sc_ref, yout_ref, h_ref):
    y = y_ref[...]
    d = y.shape[1]
    gd = d // len(POOL_WINDOWS)
    h = _rms(y, ng_ref[...])
    h_ref[...] = h
    for g, win in enumerate(POOL_WINDOWS):
        cols = slice(g * gd, (g + 1) * gd)
        wsum = h[:, cols]
        for back in range(1, win):
            wsum = wsum + prev_ref[POOL_CTX - back, :, cols]
        mixed = _bdot(wsum / float(win) - h[:, cols], w_ref[g])
        yout_ref[:, cols] = y[:, cols] + mixed * sc_ref[:, cols]


def _pool_sample(y, ng, prev_t, w, sc):
    n, d = y.shape
    return pl.pallas_call(
        _pool_sample_kernel, out_shape=[jax.ShapeDtypeStruct((n, d), F32)] * 2, grid=(1,),
        in_specs=[_whole(y.shape), _whole(ng.shape), _whole(prev_t.shape), _whole(w.shape), _whole(sc.shape)],
        out_specs=[_whole(y.shape)] * 2,
        compiler_params=_cparams("arbitrary"), name="pool_sample",
    )(y, ng, prev_t, w, sc)


def _rwkv_proj_kernel(y_ref, ng_ref, mu_ref, wr_ref, wk_ref, wv_ref, w1_ref, w2_ref, a1_ref, a2_ref,
                      g1_ref, g2_ref, vec_ref, *rest, tps, sample):
    if sample:
        prev_ref, r_o, w_o, k_o, v_o, kk_o, kka_o, g_o, sh_o = rest
    else:
        r_o, w_o, k_o, v_o, kk_o, kka_o, g_o, sh_o, hs_scr = rest
    y = y_ref[...]
    tm = y.shape[0]
    h = _rms(y, ng_ref[...])
    if sample:
        h_prev = prev_ref[...]
        sh_o[...] = h
    else:
        i = pl.program_id(0)

        @pl.when(i % tps == 0)
        def _():
            hs_scr[0:SUBLANES, :] = jnp.zeros((SUBLANES, y.shape[1]), F32)

        hs_scr[SUBLANES:, :] = h
        h_prev = hs_scr[SUBLANES - 1:tm + SUBLANES - 1, :]
        hs_scr[0:SUBLANES, :] = h[tm - SUBLANES:, :]
        sh_o[0] = h[tm - 1:tm, :]
    xx = h_prev - h
    mix = lambda m: h + xx * mu_ref[m:m + 1, :]
    r = _bdot(mix(0), wr_ref[...])
    k = _bdot(mix(2), wk_ref[...])
    v = _bdot(mix(3), wv_ref[...])
    w_log = -_softplus(-(vec_ref[0:1, :] + _bdot(jnp.tanh(_bdot(mix(1), w1_ref[...])), w2_ref[...]))) - 0.5
    a = _sigmoid(vec_ref[1:2, :] + _bdot(_bdot(mix(4), a1_ref[...]), a2_ref[...]))
    g_o[...] = _bdot(_sigmoid(_bdot(mix(5), g1_ref[...])), g2_ref[...])
    kk = k * vec_ref[2:3, :]
    kk = kk / jnp.maximum(jnp.sqrt(_seg_sum(kk * kk, RWKV_HEAD_DIM)), 1e-12)
    r_o[...] = r
    w_o[...] = jnp.exp(-jnp.exp(w_log))
    k_o[...] = k * (1.0 + (a - 1.0) * vec_ref[3:4, :])
    v_o[...] = v
    kk_o[...] = kk
    kka_o[...] = kk * a


def _rwkv_proj(y, ng, mu, mats, vec, prev, *, tm, seq, sample):
    n, d = y.shape
    tps = max(seq // tm, 1)
    ins = [y, ng, mu, *mats, vec]
    in_specs = [_rows(tm, d), _whole(ng.shape), _whole(mu.shape), *[_whole(m.shape) for m in mats],
                _whole(vec.shape)]
    outs = [jax.ShapeDtypeStruct((n, d), F32)] * 7
    out_specs = [_rows(tm, d)] * 7
    scratch = []
    if sample:
        ins.append(prev)
        in_specs.append(_rows(tm, d))
        outs.append(jax.ShapeDtypeStruct((n, d), F32))
        out_specs.append(_rows(tm, d))
    else:
        outs.append(jax.ShapeDtypeStruct((n // seq, 1, d), F32))
        out_specs.append(pl.BlockSpec((1, 1, d), lambda i: (i // tps, 0, 0)))
        scratch.append(pltpu.VMEM((tm + SUBLANES, d), F32))
    return pl.pallas_call(
        functools.partial(_rwkv_proj_kernel, tps=tps, sample=sample),
        out_shape=outs, grid=(n // tm,), in_specs=in_specs, out_specs=out_specs, scratch_shapes=scratch,
        compiler_params=_cparams("arbitrary"), name="rwkv_proj_sample" if sample else "rwkv_proj_prompt",
    )(*ins)


def _rwkv_scan_kernel(r_ref, w_ref, k_ref, v_ref, kk_ref, kka_ref, s0_ref, o_ref, st_ref, s_scr):
    c = pl.program_id(2)
    bb, tt, _ = r_ref.shape
    hd = RWKV_HEAD_DIM

    @pl.when(c == 0)
    def _():
        for b in range(bb):
            s_scr[b] = jnp.concatenate([s0_ref[b, 0], s0_ref[b, 1]], axis=1)

    ones = _same_head(LANES, hd)
    diag = jnp.where(lax.broadcasted_iota(I32, (hd, LANES), 0)
                     == (lax.broadcasted_iota(I32, (hd, LANES), 1) & (hd - 1)), 1.0, 0.0)

    def step(t, carry):
        for b in range(bb):
            row = lambda ref: jnp.broadcast_to(ref[b, pl.ds(t, 1), :], (hd, LANES))
            s = s_scr[b]
            s_kk = jnp.dot((s * row(kk_ref)).astype(BF16), ones, preferred_element_type=F32)
            v_col = _split_dot(diag * row(v_ref), ones)
            s = s * row(w_ref) - s_kk * row(kka_ref) + v_col * row(k_ref)
            s_scr[b] = s
            s_r = jnp.dot((s * row(r_ref)).astype(BF16), ones, preferred_element_type=F32)
            o_ref[b, pl.ds(t, 1), :] = jnp.sum(s_r * diag, axis=0, keepdims=True)
        return carry

    lax.fori_loop(0, tt, step, 0)

    @pl.when(c == pl.num_programs(2) - 1)
    def _():
        for b in range(bb):
            s = s_scr[b]
            st_ref[b, 0] = s[:, :hd]
            st_ref[b, 1] = s[:, hd:]


def _rwkv_scan(r, w, k, v, kk, kka, s0, *, bb, tt):
    batch, time, d = r.shape
    heads = d // RWKV_HEAD_DIM
    seq_spec = pl.BlockSpec((bb, tt, LANES), lambda p, b, c: (b, c, p))
    st_spec = pl.BlockSpec((bb, 2, RWKV_HEAD_DIM, RWKV_HEAD_DIM), lambda p, b, c: (b, p, 0, 0))
    return pl.pallas_call(
        _rwkv_scan_kernel,
        out_shape=[jax.ShapeDtypeStruct((batch, time, d), F32),
                   jax.ShapeDtypeStruct((batch, heads, RWKV_HEAD_DIM, RWKV_HEAD_DIM), F32)],
        grid=(d // LANES, batch // bb, time // tt),
        in_specs=[seq_spec] * 6 + [st_spec], out_specs=[seq_spec, st_spec],
        scratch_shapes=[pltpu.VMEM((bb, RWKV_HEAD_DIM, LANES), F32)],
        compiler_params=_cparams("parallel", "parallel", "arbitrary"), name="rwkv_scan",
    )(r, w, k, v, kk, kka, s0)


def _rwkv_out_kernel(o_ref, r_ref, k_ref, v_ref, g_ref, y_ref, vec_ref, wo_ref, yout_ref):
    hd = RWKV_HEAD_DIM
    o = o_ref[...]
    mu = _seg_sum(o, hd) * (1.0 / hd)
    dlt = o - mu
    var = _seg_sum(dlt * dlt, hd) * (1.0 / hd)
    o = dlt * lax.rsqrt(var + RWKV_LNX_EPS) * vec_ref[1:2, :] + vec_ref[2:3, :]
    o = o + _seg_sum(r_ref[...] * k_ref[...] * vec_ref[0:1, :], hd) * v_ref[...]
    yout_ref[...] = y_ref[...] + _bdot(o * g_ref[...], wo_ref[...])


def _rwkv_out(o, r, k, v, g, y, vec, wo, *, tm, name):
    n, d = y.shape
    return pl.pallas_call(
        _rwkv_out_kernel, out_shape=jax.ShapeDtypeStruct((n, d), F32), grid=(n // tm,),
        in_specs=[_rows(tm, d)] * 6 + [_whole(vec.shape), _whole(wo.shape)], out_specs=_rows(tm, d),
        compiler_params=_cparams("parallel"), name=name,
    )(o, r, k, v, g, y, vec, wo)


FFN_CHUNK = 256


def _ffn_act(gate, g1, g2, up, cw, cb):
    return _gelu(cw[0:1] * g2 + cw[1:2] * g1 + cw[2:3] * gate + cb) * up


def _ffn_prompt_kernel(y_ref, ng_ref, win_ref, cw_ref, cb_ref, wout_ref, *rest, tps, final):
    if final:
        fg_ref, yout_ref, st_ref, yfin_ref, h_scr, gs_scr, act_scr, carry_scr = rest
    else:
        yout_ref, st_ref, h_scr, gs_scr, act_scr, carry_scr = rest
    i = pl.program_id(0)
    y = y_ref[...]
    tm = y.shape[0]
    hid = wout_ref.shape[0]
    tf = FFN_CHUNK
    h_scr[...] = _rms(y, ng_ref[...]).astype(BF16)

    @pl.when(i % tps == 0)
    def _():
        carry_scr[...] = jnp.zeros(carry_scr.shape, F32)

    for j in range(hid // tf):
        cols = slice(j * tf, (j + 1) * tf)
        hb = h_scr[...]
        gs_scr[0:SUBLANES, :] = carry_scr[:, cols]
        gs_scr[SUBLANES:, :] = jnp.dot(hb, win_ref[:, cols], preferred_element_type=F32)
        up = jnp.dot(hb, win_ref[:, hid + j * tf:hid + (j + 1) * tf], preferred_element_type=F32)
        gate = gs_scr[SUBLANES:, :]
        carry_scr[:, cols] = gate[tm - SUBLANES:, :]
        st_ref[0, :, cols] = gate[tm - (FFN_CONV - 1):, :]
        act = _ffn_act(gate, gs_scr[SUBLANES - 1:tm + SUBLANES - 1, :], gs_scr[SUBLANES - 2:tm + SUBLANES - 2, :],
                       up, cw_ref[:, cols], cb_ref[:, cols])
        act_scr[:, cols] = act.astype(BF16)
    y_new = y + jnp.dot(act_scr[...], wout_ref[...], preferred_element_type=F32)
    yout_ref[...] = y_new
    if final:
        yfin_ref[...] = _rms(y_new, fg_ref[...])


def _ffn_prompt(y, ng, win, cw, cb, wout, fg, *, tm, seq):
    n, d = y.shape
    hid = wout.shape[0]
    tps = seq // tm
    final = fg is not None
    ins = [y, ng, win, cw, cb, wout]
    in_specs = [_rows(tm, d)] + [_whole(a.shape) for a in ins[1:]]
    outs = [jax.ShapeDtypeStruct((n, d), F32), jax.ShapeDtypeStruct((n // seq, FFN_CONV - 1, hid), F32)]
    out_specs = [_rows(tm, d), pl.BlockSpec((1, FFN_CONV - 1, hid), lambda i: (i // tps, 0, 0))]
    if final:
        ins.append(fg)
        in_specs.append(_whole(fg.shape))
        outs.append(jax.ShapeDtypeStruct((n, d), F32))
        out_specs.append(_rows(tm, d))
    return pl.pallas_call(
        functools.partial(_ffn_prompt_kernel, tps=tps, final=final),
        out_shape=outs, grid=(n // tm,), in_specs=in_specs, out_specs=out_specs,
        scratch_shapes=[pltpu.VMEM((tm, d), BF16), pltpu.VMEM((tm + SUBLANES, FFN_CHUNK), F32),
                        pltpu.VMEM((tm, hid), BF16), pltpu.VMEM((SUBLANES, hid), F32)],
        compiler_params=_cparams("arbitrary"), name="ffn_prompt",
    )(*ins)


def _ffn_sample_kernel(y_ref, ng_ref, wg_ref, wu_ref, cw_ref, cb_ref, p2_ref, p1_ref, wout_ref, *rest, final):
    if final:
        fg_ref, yout_ref, gate_ref, yfin_ref, h_scr, acc_scr = rest
    else:
        yout_ref, gate_ref, h_scr, acc_scr = rest
    j = pl.program_id(0)

    @pl.when(j == 0)
    def _():
        h_scr[...] = _rms(y_ref[...], ng_ref[...]).astype(BF16)
        acc_scr[...] = jnp.zeros(acc_scr.shape, F32)

    hb = h_scr[...]
    gate = jnp.dot(hb, wg_ref[...], preferred_element_type=F32)
    up = jnp.dot(hb, wu_ref[...], preferred_element_type=F32)
    gate_ref[...] = gate
    act = _ffn_act(gate, p1_ref[...], p2_ref[...], up, cw_ref[...], cb_ref[...])
    acc_scr[...] += _bdot(act, wout_ref[...])

    @pl.when(j == pl.num_programs(0) - 1)
    def _():
        y_new = y_ref[...] + acc_scr[...]
        yout_ref[...] = y_new
        if final:
            yfin_ref[...] = _rms(y_new, fg_ref[...])


def _ffn_sample(y, ng, win, cw, cb, p2, p1, wout, fg):
    n, d = y.shape
    hid = wout.shape[0]
    tf = FFN_CHUNK
    nf = hid // tf
    final = fg is not None
    keep = lambda shape: pl.BlockSpec(shape, lambda j: (0,) * len(shape))
    chunk = lambda rows: pl.BlockSpec((rows, tf), lambda j: (0, j))
    ins = [y, ng, win, win, cw, cb, p2, p1, wout]
    in_specs = [keep((n, d)), keep(ng.shape), chunk(d), pl.BlockSpec((d, tf), lambda j: (0, nf + j)),
                chunk(FFN_CONV), chunk(1), chunk(n), chunk(n), pl.BlockSpec((tf, d), lambda j: (j, 0))]
    outs = [jax.ShapeDtypeStruct((n, d), F32), jax.ShapeDtypeStruct((n, hid), F32)]
    out_specs = [keep((n, d)), chunk(n)]
    if final:
        ins.append(fg)
        in_specs.append(keep(fg.shape))
        outs.append(jax.ShapeDtypeStruct((n, d), F32))
        out_specs.append(keep((n, d)))
    return pl.pallas_call(
        functools.partial(_ffn_sample_kernel, final=final),
        out_shape=outs, grid=(nf,), in_specs=in_specs, out_specs=out_specs,
        scratch_shapes=[pltpu.VMEM((n, d), BF16), pltpu.VMEM((n, d), F32)],
        compiler_params=_cparams("arbitrary"), name="ffn_sample",
    )(*ins)


def kernel(x_prompt, x_sample, cache_moba_k, cache_moba_v, state_pool, state_rwkv_wkv, state_rwkv_shift, state_ffn_conv, page_table, norm_mix_g, norm_ffn_g, norm_final_g, rel_bias, gm_w_in, gm_ln_g, gm_ln_b, gm_w_s, gm_b_s, gm_w_out, moba_w_qkv, moba_w_o, pool_w, pool_scale, rwkv_mu, rwkv_w_r, rwkv_w_k, rwkv_w_v, rwkv_w_o, rwkv_w0, rwkv_w1, rwkv_w2, rwkv_a0, rwkv_a1, rwkv_a2, rwkv_g1, rwkv_g2, rwkv_k_k, rwkv_k_a, rwkv_r_k, rwkv_lnx_g, rwkv_lnx_b, ffn_w_in, ffn_conv_w, ffn_conv_b, ffn_w_out):
    bp, seq, d = x_prompt.shape
    bs = x_sample.shape[0]
    depth = norm_mix_g.shape[0]
    assert x_sample.shape[1] == 1 and depth == 4
    past_len = page_table.shape[1] * PAGE_SIZE
    assert seq % MOBA_BLOCK == 0 and past_len % MOBA_BLOCK == 0 and past_len % GM_CHUNK == 0
    row = lambda vct: vct.reshape(1, -1)
    bf = lambda m: m.astype(BF16)
    yp = x_prompt.reshape(bp * seq, d)
    ys = x_sample.reshape(bs, d)
    tm = 512
    conv_p, conv_s = [], []

    def ffn(i, yp, ys):
        last = i == depth - 1
        win, wout = bf(ffn_w_in[i]), bf(ffn_w_out[i])
        cw, cb, ng = ffn_conv_w[i], row(ffn_conv_b[i]), row(norm_ffn_g[i])
        fg = row(norm_final_g) if last else None
        res_p = _ffn_prompt(yp, ng, win, cw, cb, wout, fg, tm=tm, seq=seq)
        st = state_ffn_conv[i]
        res_s = _ffn_sample(ys, ng, win, cw, cb, st[:, 0], st[:, 1], wout, fg)
        conv_p.append(res_p[1])
        conv_s.append(jnp.stack([st[:, 1], res_s[1]], axis=1))
        if last:
            return res_p[2], res_s[2]
        return res_p[0], res_s[0]

    ng = row(norm_mix_g[0])
    width = gm_w_out.shape[1]
    gd = width // GM_GROUPS
    gm_in, gm_out = bf(gm_w_in[0]), bf(gm_w_out[0])
    lng, lnb = row(gm_ln_g[0]), row(gm_ln_b[0])
    sb_prompt = jnp.repeat(gm_b_s[0].T, gd, axis=1)
    (yp,) = _gmlp(yp, ng, gm_in, lng, lnb, gm_w_s[0], sb_prompt, gm_out, tm=256, sample=False)
    sa_first = row(jnp.repeat(gm_w_s[0][:, 0, 0], gd))
    sb_first = row(jnp.repeat(gm_b_s[0][:, 0], gd))
    ys, gm_v = _gmlp(ys, ng, gm_in, lng, lnb, sa_first, sb_first, gm_out, tm=bs, sample=True)
    gm_v_sample = gm_v.reshape(1, bs, 1, width)
    yp, ys = ffn(0, yp, ys)

    ng = row(norm_mix_g[1])
    w_qkv, w_o = bf(moba_w_qkv[0]), bf(moba_w_o[0])
    heads = MOBA_HEADS
    hd = d // heads
    qp, kp, vp = _norm_linear(yp, ng, w_qkv, 3, tm=tm, name="moba_qkv_prompt")
    qs, ks, vs = _norm_linear(ys, ng, w_qkv, 3, tm=bs, name="moba_qkv_sample")
    blk = MOBA_BLOCK
    qi = jnp.arange(blk, dtype=I32)[:, None]
    ki = jnp.arange(blk, dtype=I32)[None, :]
    bkt_tiles = _t5_bucket_table(jnp.stack([qi - ki, blk + qi - ki]))
    bias_tiles = _bias_tiles(rel_bias, bkt_tiles)
    op = _moba_prompt(rel_bias, qp, kp, vp, bias_tiles, batch=bp, seq=seq, head_dim=hd)
    bkt_rows = _t5_bucket_table(past_len - jnp.arange(past_len, dtype=I32))
    rb_pad = jnp.pad(rel_bias, ((0, 0), (0, LANES - heads)))
    bias_rows = _bias_rows(rb_pad, jnp.broadcast_to(bkt_rows[:, None], (past_len, LANES)))
    n_pool = cache_moba_k.shape[1]
    os_ = _moba_sample(page_table, qs, ks, vs, cache_moba_k[0].reshape(n_pool, PAGE_SIZE, d),
                       cache_moba_v[0].reshape(n_pool, PAGE_SIZE, d), bias_rows, rb_pad, head_dim=hd)
    yp = _linear_res(op, w_o, yp, tm=tm, name="moba_out_prompt")
    ys = _linear_res(os_, w_o, ys, tm=bs, name="moba_out_sample")
    moba_k_prompt = kp.reshape(1, bp, seq, heads, hd)
    moba_v_prompt = vp.reshape(1, bp, seq, heads, hd)
    moba_k_sample = ks.reshape(1, bs, 1, heads, hd)
    moba_v_sample = vs.reshape(1, bs, 1, heads, hd)
    yp, ys = ffn(1, yp, ys)

    ng = row(norm_mix_g[2])
    pw, psc = bf(pool_w[0]), row(pool_scale[0])
    yp, pool_p = _pool_prompt(yp, ng, pw, psc, tm=tm, seq=seq)
    ys, hs = _pool_sample(ys, ng, jnp.swapaxes(state_pool[0], 0, 1), pw, psc)
    pool_prompt = pool_p[None]
    pool_sample = jnp.concatenate([state_pool[0][:, 1:], hs[:, None]], axis=1)[None]
    yp, ys = ffn(2, yp, ys)

    ng = row(norm_mix_g[3])
    mats = [bf(m[0]) for m in (rwkv_w_r, rwkv_w_k, rwkv_w_v, rwkv_w1, rwkv_w2, rwkv_a1, rwkv_a2,
                               rwkv_g1, rwkv_g2)]
    vec_in = jnp.stack([rwkv_w0[0], rwkv_a0[0], rwkv_k_k[0], rwkv_k_a[0]])
    vec_out = jnp.stack([rwkv_r_k[0].reshape(-1), rwkv_lnx_g[0], rwkv_lnx_b[0]])
    rh = d // RWKV_HEAD_DIM
    *seqs_p, gp, shp = _rwkv_proj(yp, ng, rwkv_mu[0], mats, vec_in, None, tm=256, seq=seq, sample=False)
    *seqs_s, gs, shs = _rwkv_proj(ys, ng, rwkv_mu[0], mats, vec_in, state_rwkv_shift[0], tm=bs, seq=1,
                                  sample=True)
    zero_state = jnp.zeros((bp, rh, RWKV_HEAD_DIM, RWKV_HEAD_DIM), F32)
    o_p, wkv_p = _rwkv_scan(*[a.reshape(bp, seq, d) for a in seqs_p], zero_state, bb=bp, tt=256)
    o_s, wkv_s = _rwkv_scan(*[a.reshape(bs, 1, d) for a in seqs_s], state_rwkv_wkv[0], bb=8, tt=1)
    w_o = bf(rwkv_w_o[0])
    r_p, _, k_p, v_p = seqs_p[:4]
    r_s, _, k_s, v_s = seqs_s[:4]
    yp = _rwkv_out(o_p.reshape(bp * seq, d), r_p, k_p, v_p, gp, yp, vec_out, w_o, tm=tm, name="rwkv_out_prompt")
    ys = _rwkv_out(o_s.reshape(bs, d), r_s, k_s, v_s, gs, ys, vec_out, w_o, tm=bs, name="rwkv_out_sample")
    yp, ys = ffn(3, yp, ys)

    return (yp.reshape(bp, seq, d), ys.reshape(bs, 1, d), gm_v_sample, moba_k_prompt, moba_v_prompt,
            moba_k_sample, moba_v_sample, pool_prompt, pool_sample, wkv_p[None], wkv_s[None],
            shp.reshape(1, bp, d), shs[None], jnp.stack(conv_p), jnp.stack(conv_s))
```

```python
import functools
import math

import jax
import jax.numpy as jnp
from jax import lax
from jax.experimental import pallas as pl
from jax.experimental.pallas import tpu as pltpu

F32 = jnp.float32
BF16 = jnp.bfloat16
I32 = jnp.int32

LANES = 128
SUBLANES = 8
VMEM_LIMIT_BYTES = 56 * 2**20

RMS_EPS = 1e-6
GM_LN_EPS = 1e-5
GM_CHUNK = 128
GM_GROUPS = 8
MOBA_HEADS = 16
MOBA_BLOCK = 256
MOBA_TOPK = 3
REL_BUCKETS = 32
REL_MAX_DIST = 128
PAGE_SIZE = 128
POOL_WINDOWS = (2, 4, 8, 16)
POOL_CTX = max(POOL_WINDOWS) - 1
RWKV_HEAD_DIM = 64
RWKV_LNX_EPS = 64e-5
FFN_CONV = 3
NEG_INF = float("-inf")

assert MOBA_BLOCK >= REL_MAX_DIST


def _cparams(*sem):
    return pltpu.CompilerParams(dimension_semantics=sem, vmem_limit_bytes=VMEM_LIMIT_BYTES)


def _whole(shape):
    nd = len(shape)
    return pl.BlockSpec(shape, lambda *_: (0,) * nd, pipeline_mode=pl.Buffered(1))


def _rows(tm, width):
    return pl.BlockSpec((tm, width), lambda i: (i, 0))


def _rms(x, g):
    return x * lax.rsqrt(jnp.mean(x * x, axis=-1, keepdims=True) + RMS_EPS) * g


def _bdot(a, b):
    return jnp.dot(a.astype(BF16), b.astype(BF16), preferred_element_type=F32)


def _nt_dot(a, b):
    return lax.dot_general(a.astype(BF16), b.astype(BF16), (((1,), (1,)), ((), ())),
                           preferred_element_type=F32)


def _split(x):
    hi = x.astype(BF16)
    lo = (x - hi.astype(F32)).astype(BF16)
    return hi, lo


def _split_dot(x, m):
    hi, lo = _split(x)
    return (jnp.dot(hi, m, preferred_element_type=F32)
            + jnp.dot(lo, m, preferred_element_type=F32))


def _same_head(n, head_dim):
    shift = int(math.log2(head_dim))
    r = lax.broadcasted_iota(I32, (n, n), 0) >> shift
    c = lax.broadcasted_iota(I32, (n, n), 1) >> shift
    return jnp.where(r == c, 1.0, 0.0).astype(BF16)


def _seg_sum(x, head_dim):
    g = _same_head(LANES, head_dim)
    parts = [_split_dot(x[:, i:i + LANES], g) for i in range(0, x.shape[1], LANES)]
    return jnp.concatenate(parts, axis=1)


def _gelu(x):
    return 0.5 * x * (1.0 + jnp.tanh(0.7978845608028654 * (x + 0.044715 * x * x * x)))


def _sigmoid(x):
    return 1.0 / (1.0 + jnp.exp(-x))


def _softplus(x):
    return jnp.maximum(x, 0.0) + jnp.log(1.0 + jnp.exp(-jnp.abs(x)))


def _top_mask(gate, idx, axis, n_valid):
    cur = gate
    sel = jnp.zeros(gate.shape, F32)
    for _ in range(MOBA_TOPK):
        m = jnp.max(cur, axis=axis, keepdims=True)
        first = jnp.min(jnp.where(cur == m, idx, n_valid), axis=axis, keepdims=True)
        pick = (idx == first) & (m > NEG_INF)
        sel = jnp.where(pick, 1.0, sel)
        cur = jnp.where(pick, NEG_INF, cur)
    return sel


def _gmlp_kernel(x_ref, ng_ref, win_ref, lng_ref, lnb_ref, sa_ref, sb_ref, wout_ref, y_ref, aux_ref,
                 *, sample):
    x = x_ref[...]
    tm, width = x.shape[0], wout_ref.shape[0]
    h = _rms(x, ng_ref[...])
    z = _gelu(_bdot(h, win_ref[...]))
    u, v = z[:, :width], z[:, width:]
    mu = jnp.mean(v, axis=-1, keepdims=True)
    d = v - mu
    var = jnp.mean(d * d, axis=-1, keepdims=True)
    v = d * lax.rsqrt(var + GM_LN_EPS) * lng_ref[...] + lnb_ref[...]
    if sample:
        aux_ref[...] = v
        s = v * sa_ref[...] + sb_ref[...]
    else:
        gd = width // GM_GROUPS
        causal = (lax.broadcasted_iota(I32, (GM_CHUNK, GM_CHUNK), 0)
                  >= lax.broadcasted_iota(I32, (GM_CHUNK, GM_CHUNK), 1))
        for g in range(GM_GROUPS):
            wg = jnp.where(causal, sa_ref[g], 0.0).astype(BF16)
            cols = slice(g * gd, (g + 1) * gd)
            for c in range(tm // GM_CHUNK):
                rows = slice(c * GM_CHUNK, (c + 1) * GM_CHUNK)
                aux_ref[rows, cols] = (jnp.dot(wg, v[rows, cols].astype(BF16), preferred_element_type=F32)
                                       + sb_ref[:, cols])
        s = aux_ref[...]
    y_ref[...] = x + _bdot(u * s, wout_ref[...])


def _gmlp(x, ng, win, lng, lnb, sa, sb, wout, *, tm, sample):
    n, d = x.shape
    width = wout.shape[0]
    outs = [jax.ShapeDtypeStruct((n, d), F32)]
    out_specs = [_rows(tm, d)]
    scratch = []
    if sample:
        outs.append(jax.ShapeDtypeStruct((n, width), F32))
        out_specs.append(_rows(tm, width))
    else:
        scratch.append(pltpu.VMEM((tm, width), F32))
    return pl.pallas_call(
        functools.partial(_gmlp_kernel, sample=sample),
        out_shape=outs, grid=(n // tm,),
        in_specs=[_rows(tm, d), _whole(ng.shape), _whole(win.shape), _whole(lng.shape), _whole(lnb.shape),
                  _whole(sa.shape), _whole(sb.shape), _whole(wout.shape)],
        out_specs=out_specs, scratch_shapes=scratch,
        compiler_params=_cparams("parallel"), name="gmlp_sample" if sample else "gmlp_prompt",
    )(x, ng, win, lng, lnb, sa, sb, wout)


def _norm_linear_kernel(x_ref, ng_ref, w_ref, *o_refs):
    z = _bdot(_rms(x_ref[...], ng_ref[...]), w_ref[...])
    wd = z.shape[1] // len(o_refs)
    for i, o_ref in enumerate(o_refs):
        o_ref[...] = z[:, i * wd:(i + 1) * wd]


def _norm_linear(x, ng, w, n_out, *, tm, name):
    n, d = x.shape
    wd = w.shape[1] // n_out
    return pl.pallas_call(
        _norm_linear_kernel,
        out_shape=[jax.ShapeDtypeStruct((n, wd), F32)] * n_out, grid=(n // tm,),
        in_specs=[_rows(tm, d), _whole(ng.shape), _whole(w.shape)],
        out_specs=[_rows(tm, wd)] * n_out,
        compiler_params=_cparams("parallel"), name=name,
    )(x, ng, w)


def _linear_res_kernel(a_ref, w_ref, y_ref, o_ref):
    o_ref[...] = y_ref[...] + _bdot(a_ref[...], w_ref[...])


def _linear_res(a, w, y, *, tm, name):
    n, d = y.shape
    return pl.pallas_call(
        _linear_res_kernel, out_shape=jax.ShapeDtypeStruct((n, d), F32), grid=(n // tm,),
        in_specs=[_rows(tm, a.shape[1]), _whole(w.shape), _rows(tm, d)], out_specs=_rows(tm, d),
        compiler_params=_cparams("parallel"), name=name,
    )(a, w, y)


def _t5_bucket_table(rel):
    n = jnp.maximum(rel, 0)
    exact = REL_BUCKETS // 2
    nf = jnp.maximum(n, 1).astype(F32)
    large = exact + (jnp.log(nf / exact) / math.log(REL_MAX_DIST / exact)
                     * (REL_BUCKETS - exact)).astype(I32)
    return jnp.where(n < exact, n, jnp.minimum(large, REL_BUCKETS - 1)).astype(I32)


def _bias_tiles_kernel(rb_ref, bkt_ref, o_ref):
    h = pl.program_id(0)
    bkt = bkt_ref[...]
    acc = jnp.zeros(bkt.shape, F32)
    for b in range(REL_BUCKETS):
        acc = jnp.where(bkt == b, rb_ref[b, h], acc)
    o_ref[0] = acc


def _bias_tiles(rel_bias, bkt):
    heads = rel_bias.shape[1]
    return pl.pallas_call(
        _bias_tiles_kernel, out_shape=jax.ShapeDtypeStruct((heads,) + bkt.shape, F32), grid=(heads,),
        in_specs=[pl.BlockSpec(memory_space=pltpu.SMEM), _whole(bkt.shape)],
        out_specs=pl.BlockSpec((1,) + bkt.shape, lambda h: (h, 0, 0, 0)),
        compiler_params=_cparams("parallel"), name="moba_bias_tiles",
    )(rel_bias, bkt)


def _bias_rows_kernel(rb_ref, bkt_ref, o_ref):
    bkt = bkt_ref[...]
    acc = jnp.zeros(bkt.shape, F32)
    for b in range(REL_BUCKETS):
        acc = jnp.where(bkt == b, rb_ref[b:b + 1, :], acc)
    o_ref[...] = acc


def _bias_rows(rb_pad, bkt):
    return pl.pallas_call(
        _bias_rows_kernel, out_shape=jax.ShapeDtypeStruct(bkt.shape, F32), grid=(1,),
        in_specs=[_whole(rb_pad.shape), _whole(bkt.shape)], out_specs=_whole(bkt.shape),
        compiler_params=_cparams("arbitrary"), name="moba_bias_rows",
    )(rb_pad, bkt)


def _moba_prompt_kernel(rb_ref, q_ref, k_ref, v_ref, bias_ref, o_ref, kb_scr, vh_scr, kmean_scr, *, head_dim):
    pair, qb = pl.program_id(1), pl.program_id(2)
    blk = MOBA_BLOCK
    seq = k_ref.shape[0]
    n_blocks = seq // blk
    assert LANES == 2 * head_dim
    shift = int(math.log2(head_dim))
    scale = head_dim ** -0.5

    @pl.when(qb == 0)
    def _():
        k, v = k_ref[...], v_ref[...]
        second = (lax.broadcasted_iota(I32, (seq, LANES), 1) >> shift) == 1
        kb_scr[...] = k.astype(BF16)
        vh_scr[0] = jnp.where(second, 0.0, v).astype(BF16)
        vh_scr[1] = jnp.where(second, v, 0.0).astype(BF16)
        kmean_scr[...] = jnp.zeros(kmean_scr.shape, F32)
        for n in range(n_blocks):
            kmean_scr[n:n + 1, :] = jnp.mean(k[n * blk:(n + 1) * blk, :], axis=0, keepdims=True)

    q = q_ref[...]
    second = (lax.broadcasted_iota(I32, (blk, LANES), 1) >> shift) == 1
    q2 = jnp.concatenate([jnp.where(second, 0.0, q), jnp.where(second, q, 0.0)], axis=0)
    q_hi, q_lo = _split(q2)
    km_hi, km_lo = _split(kmean_scr[...])
    nt = lambda a, b: lax.dot_general(a, b, (((1,), (1,)), ((), ())), preferred_element_type=F32)
    gate = nt(q_hi, km_hi) + nt(q_hi, km_lo) + nt(q_lo, km_hi)
    block_id = lax.broadcasted_iota(I32, gate.shape, 1)
    sel = _top_mask(jnp.where(block_id < qb, gate, NEG_INF), block_id, -1, LANES)
    causal = (lax.broadcasted_iota(I32, (blk, blk), 1) <= lax.broadcasted_iota(I32, (blk, blk), 0))
    far_bias = [rb_ref[REL_BUCKETS - 1, pair * 2 + hh] for hh in range(2)]

    def attend(own):
        keys = (own + 1) * blk
        s = nt(q_hi, kb_scr[0:keys, :]) * scale
        head_rows = []
        for hh in range(2):
            rows = slice(hh * blk, (hh + 1) * blk)
            tiles = []
            for n in range(own + 1):
                t = s[rows, n * blk:(n + 1) * blk]
                if n == own:
                    t = jnp.where(causal, t + bias_ref[hh, 0], NEG_INF)
                else:
                    bias = bias_ref[hh, 1] if n == own - 1 else far_bias[hh]
                    t = jnp.where(sel[rows, n:n + 1] > 0.0, t + bias, NEG_INF)
                tiles.append(t)
            head_rows.append(jnp.concatenate(tiles, axis=1))
        s = jnp.concatenate(head_rows, axis=0)
        p = jnp.exp(s - jnp.max(s, axis=-1, keepdims=True))
        l = jnp.sum(p, axis=-1, keepdims=True)
        p_pair = jnp.concatenate([p[:blk], p[blk:]], axis=1).astype(BF16)
        v_pair = jnp.concatenate([vh_scr[0, 0:keys, :], vh_scr[1, 0:keys, :]], axis=0)
        o = jnp.dot(p_pair, v_pair, preferred_element_type=F32)
        o_ref[...] = o / jnp.where(second, l[blk:], l[:blk])

    for own in range(n_blocks):
        pl.when(qb == own)(functools.partial(attend, own))


def _moba_prompt(rel_bias, q, k, v, bias_tiles, *, batch, seq, head_dim):
    n, d = q.shape
    blk = MOBA_BLOCK
    nq = seq // blk
    hpt = LANES // head_dim
    return pl.pallas_call(
        functools.partial(_moba_prompt_kernel, head_dim=head_dim),
        out_shape=jax.ShapeDtypeStruct((n, d), F32), grid=(batch, d // LANES, nq),
        in_specs=[pl.BlockSpec(memory_space=pltpu.SMEM),
                  pl.BlockSpec((blk, LANES), lambda b, p, i: (b * nq + i, p)),
                  pl.BlockSpec((seq, LANES), lambda b, p, i: (b, p)),
                  pl.BlockSpec((seq, LANES), lambda b, p, i: (b, p)),
                  pl.BlockSpec((hpt, 2, blk, blk), lambda b, p, i: (p, 0, 0, 0))],
        out_specs=pl.BlockSpec((blk, LANES), lambda b, p, i: (b * nq + i, p)),
        scratch_shapes=[pltpu.VMEM((seq, LANES), BF16), pltpu.VMEM((hpt, seq, LANES), BF16),
                        pltpu.VMEM((LANES, LANES), F32)],
        compiler_params=_cparams("parallel", "parallel", "arbitrary"), name="moba_prompt_attn",
    )(rel_bias, q, k, v, bias_tiles)


def _moba_sample_kernel(pt_ref, q_ref, kn_ref, vn_ref, ka_ref, kb_ref, va_ref, vb_ref, bias_ref, rb_ref,
                        g_ref, gt_ref, o_ref, m_scr, l_scr, gs_scr, acc_scr, *, head_dim):
    del pt_ref
    n = pl.program_id(1)
    n_blocks = pl.num_programs(1)
    scale = head_dim ** -0.5
    q = q_ref[0]
    g, gt = g_ref[...], gt_ref[...]
    heads = bias_ref.shape[1]
    head_lanes = gt[0:heads, :].astype(F32)
    k_blk = jnp.concatenate([ka_ref[0], kb_ref[0]], axis=0)
    v_blk = jnp.concatenate([va_ref[0], vb_ref[0]], axis=0)
    s = _nt_dot(head_lanes * q, k_blk) * scale + bias_ref[n]
    m = jnp.max(s, axis=-1, keepdims=True)
    p = jnp.exp(s - m)
    l = jnp.sum(p, axis=-1, keepdims=True)
    acc = jnp.sum(_bdot(p, v_blk) * head_lanes, axis=0, keepdims=True)
    to_lane = jnp.where(lax.broadcasted_iota(I32, (heads, LANES), 0)
                        == lax.broadcasted_iota(I32, (heads, LANES), 1), 1.0, 0.0)
    m_scr[pl.ds(n, 1), :] = jnp.sum(m * to_lane, axis=0, keepdims=True)
    l_scr[pl.ds(n, 1), :] = jnp.sum(l * to_lane, axis=0, keepdims=True)
    gs_scr[pl.ds(n, 1), :] = jnp.sum(k_blk, axis=0, keepdims=True) * q
    acc_scr[pl.ds(n, 1), :] = acc

    @pl.when(n == n_blocks - 1)
    def _():
        nb = m_scr.shape[0]
        gate = _split_dot(gs_scr[...], g) * (1.0 / MOBA_BLOCK)
        sel = _top_mask(gate, lax.broadcasted_iota(I32, gate.shape, 0), 0, nb)
        rows = lambda x: jnp.broadcast_to(x, (SUBLANES,) + x.shape[1:])
        s_new = jnp.dot(rows(kn_ref[0] * q).astype(BF16), g, preferred_element_type=F32)[0:1]
        s_new = s_new * scale + rb_ref[0:1, :]
        m_all = m_scr[...]
        m_tot = jnp.maximum(jnp.max(jnp.where(sel > 0.0, m_all, NEG_INF), axis=0, keepdims=True), s_new)
        w_blk = jnp.where(sel > 0.0, jnp.exp(m_all - m_tot), 0.0)
        w_new = jnp.exp(s_new - m_tot)
        den = jnp.sum(w_blk * l_scr[...], axis=0, keepdims=True) + w_new
        num = (jnp.sum(_split_dot(w_blk, gt) * acc_scr[...], axis=0, keepdims=True)
               + _split_dot(rows(w_new), gt)[0:1] * vn_ref[0])
        o_ref[0] = num / _split_dot(rows(den), gt)[0:1]


def _moba_sample(page_table, q, k_new, v_new, cache_k, cache_v, bias_rows, rb_pad, *, head_dim):
    nb, d = q.shape
    n_pages = page_table.shape[1]
    per_block = MOBA_BLOCK // PAGE_SIZE
    assert per_block == 2
    n_blocks = n_pages // per_block
    heads = d // head_dim
    col_head = jnp.arange(d, dtype=I32) // head_dim
    g = (col_head[:, None] == jnp.arange(LANES, dtype=I32)[None, :]).astype(BF16)
    row = lambda x: x.reshape(nb, 1, d)
    vec = pl.BlockSpec((1, 1, d), lambda s, n, pt: (s, 0, 0))
    page = lambda j: pl.BlockSpec((1, PAGE_SIZE, d), lambda s, n, pt: (pt[s, per_block * n + j], 0, 0))
    const = lambda shape: pl.BlockSpec(shape, lambda s, n, pt: (0,) * len(shape))
    bias_t = bias_rows[:, :heads].reshape(n_blocks, MOBA_BLOCK, heads).transpose(0, 2, 1)
    out = pl.pallas_call(
        functools.partial(_moba_sample_kernel, head_dim=head_dim),
        out_shape=jax.ShapeDtypeStruct((nb, 1, d), F32),
        grid_spec=pltpu.PrefetchScalarGridSpec(
            num_scalar_prefetch=1, grid=(nb, n_blocks),
            in_specs=[vec, vec, vec, page(0), page(1), page(0), page(1),
                      const(bias_t.shape), const(rb_pad.shape),
                      const(g.shape), const((LANES, d))],
            out_specs=vec,
            scratch_shapes=[pltpu.VMEM((n_blocks, LANES), F32), pltpu.VMEM((n_blocks, LANES), F32),
                            pltpu.VMEM((n_blocks, d), F32), pltpu.VMEM((n_blocks, d), F32)]),
        compiler_params=_cparams("parallel", "arbitrary"), name="moba_sample_attn",
    )(page_table, row(q), row(k_new), row(v_new), cache_k, cache_k, cache_v, cache_v,
      bias_t, rb_pad, g, g.T)
    return out.reshape(nb, d)


POOL_PAD = 32


def _pool_prompt_kernel(y_ref, ng_ref, w_ref, sc_ref, yout_ref, st_ref, b0, b1, b2, b3, b4, *, tps):
    i = pl.program_id(0)
    y = y_ref[...]
    tm, d = y.shape
    gd = d // len(POOL_WINDOWS)
    h = _rms(y, ng_ref[...])
    half = POOL_PAD // 2

    @pl.when(i == 0)
    def _():
        for b in (b0, b1, b2, b3, b4):
            b[0:half, :] = jnp.zeros((half, b.shape[1]), F32)

    @pl.when(i % tps == 0)
    def _():
        b0[half:POOL_PAD, :] = jnp.zeros((half, d), F32)

    b0[POOL_PAD:, :] = h
    stages = (b0, b1, b2, b3, b4)
    for k in range(1, len(stages)):
        src, dst = stages[k - 1], stages[k]
        back = 2 ** (k - 1)
        off = src.shape[1] - dst.shape[1]
        dst[SUBLANES:, :] = (src[SUBLANES:, off:] + src[SUBLANES - back:tm + POOL_PAD - back, off:])
    pos = (i % tps) * tm + lax.broadcasted_iota(I32, (tm, 1), 0) + 1
    for g, win in enumerate(POOL_WINDOWS):
        cols = slice(g * gd, (g + 1) * gd)
        wsum = stages[g + 1][POOL_PAD:, 0:gd]
        cnt = jnp.minimum(pos, win).astype(F32)
        mixed = _bdot(wsum / cnt - h[:, cols], w_ref[g])
        yout_ref[:, cols] = y[:, cols] + mixed * sc_ref[:, cols]
    st_ref[0] = b0[tm + POOL_PAD - POOL_CTX:, :]
    b0[half:POOL_PAD, :] = b0[tm + half:, :]


def _pool_prompt(y, ng, w, sc, *, tm, seq):
    n, d = y.shape
    gd = d // len(POOL_WINDOWS)
    tps = seq // tm
    rows = tm + POOL_PAD
    return pl.pallas_call(
        functools.partial(_pool_prompt_kernel, tps=tps),
        out_shape=[jax.ShapeDtypeStruct((n, d), F32), jax.ShapeDtypeStruct((n // seq, POOL_CTX, d), F32)],
        grid=(n // tm,),
        in_specs=[_rows(tm, d), _whole(ng.shape), _whole(w.shape), _whole(sc.shape)],
        out_specs=[_rows(tm, d), pl.BlockSpec((1, POOL_CTX, d), lambda i: (i // tps, 0, 0))],
        scratch_shapes=[pltpu.VMEM((rows, d - k * gd), F32) for k in (0, 0, 1, 2, 3)],
        compiler_params=_cparams("arbitrary"), name="pool_prompt",
    )(y, ng, w, sc)


def _pool_sample_kernel(y_ref, ng_ref, prev_ref, w_ref, sc_ref, yout_ref, h_ref):
    y = y_ref[...]
    d = y.shape[1]
    gd = d // len(POOL_WINDOWS)
    h = _rms(y, ng_ref[...])
    h_ref[...] = h
    for g, win in enumerate(POOL_WINDOWS):
        cols = slice(g * gd, (g + 1) * gd)
        wsum = h[:, cols]
        for back in range(1, win):
            wsum = wsum + prev_ref[POOL_CTX - back, :, cols]
        mixed = _bdot(wsum / float(win) - h[:, cols], w_ref[g])
        yout_ref[:, cols] = y[:, cols] + mixed * sc_ref[:, cols]


def _pool_sample(y, ng, prev_t, w, sc):
    n, d = y.shape
    return pl.pallas_call(
        _pool_sample_kernel, out_shape=[jax.ShapeDtypeStruct((n, d), F32)] * 2, grid=(1,),
        in_specs=[_whole(y.shape), _whole(ng.shape), _whole(prev_t.shape), _whole(w.shape), _whole(sc.shape)],
        out_specs=[_whole(y.shape)] * 2,
        compiler_params=_cparams("arbitrary"), name="pool_sample",
    )(y, ng, prev_t, w, sc)


def _rwkv_proj_kernel(y_ref, ng_ref, mu_ref, wr_ref, wk_ref, wv_ref, w1_ref, w2_ref, a1_ref, a2_ref,
                      g1_ref, g2_ref, vec_ref, *rest, tps, sample):
    if sample:
        prev_ref, r_o, w_o, k_o, v_o, kk_o, kka_o, g_o, sh_o = rest
    else:
        r_o, w_o, k_o, v_o, kk_o, kka_o, g_o, sh_o, hs_scr = rest
    y = y_ref[...]
    tm = y.shape[0]
    h = _rms(y, ng_ref[...])
    if sample:
        h_prev = prev_ref[...]
        sh_o[...] = h
    else:
        i = pl.program_id(0)

        @pl.when(i % tps == 0)
        def _():
            hs_scr[0:SUBLANES, :] = jnp.zeros((SUBLANES, y.shape[1]), F32)

        hs_scr[SUBLANES:, :] = h
        h_prev = hs_scr[SUBLANES - 1:tm + SUBLANES - 1, :]
        hs_scr[0:SUBLANES, :] = h[tm - SUBLANES:, :]
        sh_o[0] = h[tm - 1:tm, :]
    xx = h_prev - h
    mix = lambda m: h + xx * mu_ref[m:m + 1, :]
    r = _bdot(mix(0), wr_ref[...])
    k = _bdot(mix(2), wk_ref[...])
    v = _bdot(mix(3), wv_ref[...])
    w_log = -_softplus(-(vec_ref[0:1, :] + _bdot(jnp.tanh(_bdot(mix(1), w1_ref[...])), w2_ref[...]))) - 0.5
    a = _sigmoid(vec_ref[1:2, :] + _bdot(_bdot(mix(4), a1_ref[...]), a2_ref[...]))
    g_o[...] = _bdot(_sigmoid(_bdot(mix(5), g1_ref[...])), g2_ref[...])
    kk = k * vec_ref[2:3, :]
    kk = kk / jnp.maximum(jnp.sqrt(_seg_sum(kk * kk, RWKV_HEAD_DIM)), 1e-12)
    r_o[...] = r
    w_o[...] = -jnp.exp(w_log)
    k_o[...] = k * (1.0 + (a - 1.0) * vec_ref[3:4, :])
    v_o[...] = v
    kk_o[...] = kk
    kka_o[...] = kk * a


def _rwkv_proj(y, ng, mu, mats, vec, prev, *, tm, seq, sample):
    n, d = y.shape
    tps = max(seq // tm, 1)
    ins = [y, ng, mu, *mats, vec]
    in_specs = [_rows(tm, d), _whole(ng.shape), _whole(mu.shape), *[_whole(m.shape) for m in mats],
                _whole(vec.shape)]
    outs = [jax.ShapeDtypeStruct((n, d), F32)] * 7
    out_specs = [_rows(tm, d)] * 7
    scratch = []
    if sample:
        ins.append(prev)
        in_specs.append(_rows(tm, d))
        outs.append(jax.ShapeDtypeStruct((n, d), F32))
        out_specs.append(_rows(tm, d))
    else:
        outs.append(jax.ShapeDtypeStruct((n // seq, 1, d), F32))
        out_specs.append(pl.BlockSpec((1, 1, d), lambda i: (i // tps, 0, 0)))
        scratch.append(pltpu.VMEM((tm + SUBLANES, d), F32))
    return pl.pallas_call(
        functools.partial(_rwkv_proj_kernel, tps=tps, sample=sample),
        out_shape=outs, grid=(n // tm,), in_specs=in_specs, out_specs=out_specs, scratch_shapes=scratch,
        compiler_params=_cparams("arbitrary"), name="rwkv_proj_sample" if sample else "rwkv_proj_prompt",
    )(*ins)


RWKV_CHUNK = 128


def _rwkv_chunk_kernel(r_ref, lw_ref, k_ref, v_ref, kk_ref, kka_ref, o_ref, st_ref, t_scr):
    c = pl.program_id(1)
    ch, d = r_ref.shape
    hd = RWKV_HEAD_DIM
    shift = int(math.log2(hd))
    n_pairs = d // LANES

    @pl.when(c == 0)
    def _():
        t_scr[...] = jnp.zeros(t_scr.shape, F32)

    ri = lax.broadcasted_iota(I32, (ch, ch), 0)
    ci = lax.broadcasted_iota(I32, (ch, ch), 1)
    incl = ci <= ri
    strict = ci < ri
    incl2 = jnp.concatenate([incl, incl], axis=1)
    strict2 = jnp.concatenate([strict, strict], axis=1)
    same_head = (ri >> shift) == (ci >> shift)
    eye = ri == ci
    head1 = (lax.broadcasted_iota(I32, (ch, LANES), 1) >> shift) == 1
    head1_wide = jnp.concatenate([head1, head1], axis=1)

    def by_head(x, mask):
        return jnp.concatenate([jnp.where(mask, 0.0, x), jnp.where(mask, x, 0.0)], axis=0).astype(BF16)

    lw = lw_ref[...]
    tri = jnp.where(incl, 1.0, 0.0).astype(BF16)
    p1 = lw.astype(BF16)
    rem = lw - p1.astype(F32)
    p2 = rem.astype(BF16)
    p3 = (rem - p2.astype(F32)).astype(BF16)
    cl = (jnp.dot(tri, p1, preferred_element_type=F32) + jnp.dot(tri, p2, preferred_element_type=F32)
          + jnp.dot(tri, p3, preferred_element_type=F32))
    mid = cl[ch // 2 - 1:ch // 2, :]
    last = cl[ch - 1:ch, :]
    e_mid = jnp.exp(mid)
    e_neg = jnp.exp(mid - cl)
    e_tail = jnp.exp(last - cl)
    p_last = jnp.exp(last)
    kk, kka, k = kk_ref[...], kka_ref[...], k_ref[...]
    a_s = -kk * jnp.exp(cl - lw - mid)
    r_s = r_ref[...] * jnp.exp(cl - mid)
    b_s = kka * e_neg
    k_s = k * e_neg
    b_t = kka * e_tail
    k_t = k * e_tail
    v = v_ref[...]
    zero = jnp.zeros((ch, ch), BF16)

    for p in range(n_pairs):
        cols = slice(p * LANES, (p + 1) * LANES)
        a_p, r_p, v_p = a_s[:, cols], r_s[:, cols], v[:, cols]
        lhs = jnp.concatenate([jnp.where(head1, 0.0, a_p), jnp.where(head1, 0.0, r_p),
                               jnp.where(head1, a_p, 0.0), jnp.where(head1, r_p, 0.0)], axis=0).astype(BF16)
        rhs = jnp.concatenate([b_s[:, cols].T, k_s[:, cols].T], axis=1).astype(BF16)
        gram = jnp.dot(lhs, rhs, preferred_element_type=F32)
        ab0, rb0 = jnp.where(strict2, gram[0:ch], 0.0), jnp.where(incl2, gram[ch:2 * ch], 0.0)
        ab1, rb1 = jnp.where(strict2, gram[2 * ch:3 * ch], 0.0), jnp.where(incl2, gram[3 * ch:], 0.0)
        v_heads = by_head(v_p, head1)
        l_ak = jnp.concatenate([ab0[:, ch:], ab1[:, ch:]], axis=1).astype(BF16)
        x = jnp.concatenate([a_p, jnp.dot(l_ak, v_heads, preferred_element_type=F32)], axis=1)
        pw = jnp.concatenate([ab0[:, :ch], ab1[:, :ch]], axis=1).astype(BF16)
        n_sq = int(math.log2(ch))
        for it in range(n_sq):
            x = x + jnp.dot(pw, by_head(x, head1_wide), preferred_element_type=F32)
            if it < n_sq - 1:
                blockdiag = jnp.concatenate([jnp.concatenate([pw[:, :ch], zero], axis=1),
                                             jnp.concatenate([zero, pw[:, ch:]], axis=1)], axis=0)
                pw = jnp.dot(pw, blockdiag, preferred_element_type=F32).astype(BF16)
        x = jnp.concatenate([x[:, :ch] * e_mid[:, cols], x[:, ch:]], axis=1)
        m_rb = jnp.concatenate([rb0[:, :ch], rb1[:, :ch]], axis=1).astype(BF16)
        m_rk = jnp.concatenate([rb0[:, ch:], rb1[:, ch:]], axis=1).astype(BF16)
        qo = jnp.dot(m_rb, by_head(x, head1_wide), preferred_element_type=F32)
        q = r_p * e_mid[:, cols] + qo[:, :ch]
        o_intra = qo[:, ch:] + jnp.dot(m_rk, v_heads, preferred_element_type=F32)
        gh = jnp.dot(b_t[:, cols].T.astype(BF16), x.astype(BF16), preferred_element_type=F32)
        g = jnp.where(same_head, gh[:, :ch], 0.0) + jnp.where(eye, p_last[:, cols], 0.0)
        h = jnp.where(same_head, gh[:, ch:] + _bdot(k_t[:, cols].T, v_p), 0.0)
        t_old = t_scr[p].astype(BF16)
        o_ref[:, cols] = jnp.dot(q.astype(BF16), t_old, preferred_element_type=F32) + o_intra
        t_scr[p] = jnp.dot(g.astype(BF16), t_old, preferred_element_type=F32) + h

    @pl.when(c == pl.num_programs(1) - 1)
    def _():
        for p in range(n_pairs):
            s_pair = t_scr[p].T
            st_ref[0, p] = s_pair[:hd, :] + s_pair[hd:, :]


def _rwkv_chunked(r, lw, k, v, kk, kka, *, batch, seq):
    n, d = r.shape
    ch = RWKV_CHUNK
    assert ch == LANES and seq % ch == 0
    n_chunks = seq // ch
    n_pairs = d // LANES
    blk = pl.BlockSpec((ch, d), lambda b, c: (b * n_chunks + c, 0))
    return pl.pallas_call(
        _rwkv_chunk_kernel,
        out_shape=[jax.ShapeDtypeStruct((n, d), F32),
                   jax.ShapeDtypeStruct((batch, n_pairs, RWKV_HEAD_DIM, LANES), F32)],
        grid=(batch, n_chunks), in_specs=[blk] * 6,
        out_specs=[blk, pl.BlockSpec((1, n_pairs, RWKV_HEAD_DIM, LANES), lambda b, c: (b, 0, 0, 0))],
        scratch_shapes=[pltpu.VMEM((n_pairs, LANES, LANES), F32)],
        compiler_params=_cparams("parallel", "arbitrary"), name="rwkv_chunked",
    )(r, lw, k, v, kk, kka)


def _rwkv_scan_kernel(r_ref, lw_ref, k_ref, v_ref, kk_ref, kka_ref, s0_ref, o_ref, st_ref, s_scr):
    c = pl.program_id(2)
    bb, tt, _ = r_ref.shape
    hd = RWKV_HEAD_DIM

    @pl.when(c == 0)
    def _():
        for b in range(bb):
            s_scr[b] = jnp.concatenate([s0_ref[b, 0], s0_ref[b, 1]], axis=1)

    ones = _same_head(LANES, hd)
    diag = jnp.where(lax.broadcasted_iota(I32, (hd, LANES), 0)
                     == (lax.broadcasted_iota(I32, (hd, LANES), 1) & (hd - 1)), 1.0, 0.0)

    def step(t, carry):
        for b in range(bb):
            row = lambda ref: jnp.broadcast_to(ref[b, pl.ds(t, 1), :], (hd, LANES))
            s = s_scr[b]
            s_kk = jnp.dot((s * row(kk_ref)).astype(BF16), ones, preferred_element_type=F32)
            v_col = _split_dot(diag * row(v_ref), ones)
            s = s * jnp.exp(row(lw_ref)) - s_kk * row(kka_ref) + v_col * row(k_ref)
            s_scr[b] = s
            s_r = jnp.dot((s * row(r_ref)).astype(BF16), ones, preferred_element_type=F32)
            o_ref[b, pl.ds(t, 1), :] = jnp.sum(s_r * diag, axis=0, keepdims=True)
        return carry

    lax.fori_loop(0, tt, step, 0)

    @pl.when(c == pl.num_programs(2) - 1)
    def _():
        for b in range(bb):
            s = s_scr[b]
            st_ref[b, 0] = s[:, :hd]
            st_ref[b, 1] = s[:, hd:]


def _rwkv_scan(r, w, k, v, kk, kka, s0, *, bb, tt):
    batch, time, d = r.shape
    heads = d // RWKV_HEAD_DIM
    seq_spec = pl.BlockSpec((bb, tt, LANES), lambda p, b, c: (b, c, p))
    st_spec = pl.BlockSpec((bb, 2, RWKV_HEAD_DIM, RWKV_HEAD_DIM), lambda p, b, c: (b, p, 0, 0))
    return pl.pallas_call(
        _rwkv_scan_kernel,
        out_shape=[jax.ShapeDtypeStruct((batch, time, d), F32),
                   jax.ShapeDtypeStruct((batch, heads, RWKV_HEAD_DIM, RWKV_HEAD_DIM), F32)],
        grid=(d // LANES, batch // bb, time // tt),
        in_specs=[seq_spec] * 6 + [st_spec], out_specs=[seq_spec, st_spec],
        scratch_shapes=[pltpu.VMEM((bb, RWKV_HEAD_DIM, LANES), F32)],
        compiler_params=_cparams("parallel", "parallel", "arbitrary"), name="rwkv_scan",
    )(r, w, k, v, kk, kka, s0)


def _rwkv_out_kernel(o_ref, r_ref, k_ref, v_ref, g_ref, y_ref, vec_ref, wo_ref, yout_ref):
    hd = RWKV_HEAD_DIM
    o = o_ref[...]
    mu = _seg_sum(o, hd) * (1.0 / hd)
    dlt = o - mu
    var = _seg_sum(dlt * dlt, hd) * (1.0 / hd)
    o = dlt * lax.rsqrt(var + RWKV_LNX_EPS) * vec_ref[1:2, :] + vec_ref[2:3, :]
    o = o + _seg_sum(r_ref[...] * k_ref[...] * vec_ref[0:1, :], hd) * v_ref[...]
    yout_ref[...] = y_ref[...] + _bdot(o * g_ref[...], wo_ref[...])


def _rwkv_out(o, r, k, v, g, y, vec, wo, *, tm, name):
    n, d = y.shape
    return pl.pallas_call(
        _rwkv_out_kernel, out_shape=jax.ShapeDtypeStruct((n, d), F32), grid=(n // tm,),
        in_specs=[_rows(tm, d)] * 6 + [_whole(vec.shape), _whole(wo.shape)], out_specs=_rows(tm, d),
        compiler_params=_cparams("parallel"), name=name,
    )(o, r, k, v, g, y, vec, wo)


FFN_CHUNK = 256


def _ffn_act(gate, g1, g2, up, cw, cb):
    return _gelu(cw[0:1] * g2 + cw[1:2] * g1 + cw[2:3] * gate + cb) * up


def _ffn_prompt_kernel(y_ref, ng_ref, win_ref, cw_ref, cb_ref, wout_ref, *rest, tps, final):
    if final:
        fg_ref, yout_ref, st_ref, yfin_ref, h_scr, gs_scr, act_scr, carry_scr = rest
    else:
        yout_ref, st_ref, h_scr, gs_scr, act_scr, carry_scr = rest
    i = pl.program_id(0)
    y = y_ref[...]
    tm = y.shape[0]
    hid = wout_ref.shape[0]
    tf = FFN_CHUNK
    h_scr[...] = _rms(y, ng_ref[...]).astype(BF16)

    @pl.when(i % tps == 0)
    def _():
        carry_scr[...] = jnp.zeros(carry_scr.shape, F32)

    for j in range(hid // tf):
        cols = slice(j * tf, (j + 1) * tf)
        hb = h_scr[...]
        gs_scr[0:SUBLANES, :] = carry_scr[:, cols]
        gs_scr[SUBLANES:, :] = jnp.dot(hb, win_ref[:, cols], preferred_element_type=F32)
        up = jnp.dot(hb, win_ref[:, hid + j * tf:hid + (j + 1) * tf], preferred_element_type=F32)
        gate = gs_scr[SUBLANES:, :]
        carry_scr[:, cols] = gate[tm - SUBLANES:, :]
        st_ref[0, :, cols] = gate[tm - (FFN_CONV - 1):, :]
        act = _ffn_act(gate, gs_scr[SUBLANES - 1:tm + SUBLANES - 1, :], gs_scr[SUBLANES - 2:tm + SUBLANES - 2, :],
                       up, cw_ref[:, cols], cb_ref[:, cols])
        act_scr[:, cols] = act.astype(BF16)
    y_new = y + jnp.dot(act_scr[...], wout_ref[...], preferred_element_type=F32)
    yout_ref[...] = y_new
    if final:
        yfin_ref[...] = _rms(y_new, fg_ref[...])


def _ffn_prompt(y, ng, win, cw, cb, wout, fg, *, tm, seq):
    n, d = y.shape
    hid = wout.shape[0]
    tps = seq // tm
    final = fg is not None
    ins = [y, ng, win, cw, cb, wout]
    in_specs = [_rows(tm, d)] + [_whole(a.shape) for a in ins[1:]]
    outs = [jax.ShapeDtypeStruct((n, d), F32), jax.ShapeDtypeStruct((n // seq, FFN_CONV - 1, hid), F32)]
    out_specs = [_rows(tm, d), pl.BlockSpec((1, FFN_CONV - 1, hid), lambda i: (i // tps, 0, 0))]
    if final:
        ins.append(fg)
        in_specs.append(_whole(fg.shape))
        outs.append(jax.ShapeDtypeStruct((n, d), F32))
        out_specs.append(_rows(tm, d))
    return pl.pallas_call(
        functools.partial(_ffn_prompt_kernel, tps=tps, final=final),
        out_shape=outs, grid=(n // tm,), in_specs=in_specs, out_specs=out_specs,
        scratch_shapes=[pltpu.VMEM((tm, d), BF16), pltpu.VMEM((tm + SUBLANES, FFN_CHUNK), F32),
                        pltpu.VMEM((tm, hid), BF16), pltpu.VMEM((SUBLANES, hid), F32)],
        compiler_params=_cparams("arbitrary"), name="ffn_prompt",
    )(*ins)


def _ffn_sample_kernel(y_ref, ng_ref, wg_ref, wu_ref, cw_ref, cb_ref, p2_ref, p1_ref, wout_ref, *rest, final):
    if final:
        fg_ref, yout_ref, gate_ref, yfin_ref, h_scr, acc_scr = rest
    else:
        yout_ref, gate_ref, h_scr, acc_scr = rest
    j = pl.program_id(0)

    @pl.when(j == 0)
    def _():
        h_scr[...] = _rms(y_ref[...], ng_ref[...]).astype(BF16)
        acc_scr[...] = jnp.zeros(acc_scr.shape, F32)

    hb = h_scr[...]
    gate = jnp.dot(hb, wg_ref[...], preferred_element_type=F32)
    up = jnp.dot(hb, wu_ref[...], preferred_element_type=F32)
    gate_ref[...] = gate
    act = _ffn_act(gate, p1_ref[...], p2_ref[...], up, cw_ref[...], cb_ref[...])
    acc_scr[...] += _bdot(act, wout_ref[...])

    @pl.when(j == pl.num_programs(0) - 1)
    def _():
        y_new = y_ref[...] + acc_scr[...]
        yout_ref[...] = y_new
        if final:
            yfin_ref[...] = _rms(y_new, fg_ref[...])


def _ffn_sample(y, ng, win, cw, cb, p2, p1, wout, fg):
    n, d = y.shape
    hid = wout.shape[0]
    tf = FFN_CHUNK
    nf = hid // tf
    final = fg is not None
    keep = lambda shape: pl.BlockSpec(shape, lambda j: (0,) * len(shape))
    chunk = lambda rows: pl.BlockSpec((rows, tf), lambda j: (0, j))
    ins = [y, ng, win, win, cw, cb, p2, p1, wout]
    in_specs = [keep((n, d)), keep(ng.shape), chunk(d), pl.BlockSpec((d, tf), lambda j: (0, nf + j)),
                chunk(FFN_CONV), chunk(1), chunk(n), chunk(n), pl.BlockSpec((tf, d), lambda j: (j, 0))]
    outs = [jax.ShapeDtypeStruct((n, d), F32), jax.ShapeDtypeStruct((n, hid), F32)]
    out_specs = [keep((n, d)), chunk(n)]
    if final:
        ins.append(fg)
        in_specs.append(keep(fg.shape))
        outs.append(jax.ShapeDtypeStruct((n, d), F32))
        out_specs.append(keep((n, d)))
    return pl.pallas_call(
        functools.partial(_ffn_sample_kernel, final=final),
        out_shape=outs, grid=(nf,), in_specs=in_specs, out_specs=out_specs,
        scratch_shapes=[pltpu.VMEM((n, d), BF16), pltpu.VMEM((n, d), F32)],
        compiler_params=_cparams("arbitrary"), name="ffn_sample",
    )(*ins)


def kernel(x_prompt, x_sample, cache_moba_k, cache_moba_v, state_pool, state_rwkv_wkv, state_rwkv_shift, state_ffn_conv, page_table, norm_mix_g, norm_ffn_g, norm_final_g, rel_bias, gm_w_in, gm_ln_g, gm_ln_b, gm_w_s, gm_b_s, gm_w_out, moba_w_qkv, moba_w_o, pool_w, pool_scale, rwkv_mu, rwkv_w_r, rwkv_w_k, rwkv_w_v, rwkv_w_o, rwkv_w0, rwkv_w1, rwkv_w2, rwkv_a0, rwkv_a1, rwkv_a2, rwkv_g1, rwkv_g2, rwkv_k_k, rwkv_k_a, rwkv_r_k, rwkv_lnx_g, rwkv_lnx_b, ffn_w_in, ffn_conv_w, ffn_conv_b, ffn_w_out):
    bp, seq, d = x_prompt.shape
    bs = x_sample.shape[0]
    depth = norm_mix_g.shape[0]
    assert x_sample.shape[1] == 1 and depth == 4
    past_len = page_table.shape[1] * PAGE_SIZE
    assert seq % MOBA_BLOCK == 0 and past_len % MOBA_BLOCK == 0 and past_len % GM_CHUNK == 0
    row = lambda vct: vct.reshape(1, -1)
    bf = lambda m: m.astype(BF16)
    yp = x_prompt.reshape(bp * seq, d)
    ys = x_sample.reshape(bs, d)
    tm = 512
    conv_p, conv_s = [], []

    def ffn(i, yp, ys):
        last = i == depth - 1
        win, wout = bf(ffn_w_in[i]), bf(ffn_w_out[i])
        cw, cb, ng = ffn_conv_w[i], row(ffn_conv_b[i]), row(norm_ffn_g[i])
        fg = row(norm_final_g) if last else None
        res_p = _ffn_prompt(yp, ng, win, cw, cb, wout, fg, tm=tm, seq=seq)
        st = state_ffn_conv[i]
        res_s = _ffn_sample(ys, ng, win, cw, cb, st[:, 0], st[:, 1], wout, fg)
        conv_p.append(res_p[1])
        conv_s.append(jnp.stack([st[:, 1], res_s[1]], axis=1))
        if last:
            return res_p[2], res_s[2]
        return res_p[0], res_s[0]

    ng = row(norm_mix_g[0])
    width = gm_w_out.shape[1]
    gd = width // GM_GROUPS
    gm_in, gm_out = bf(gm_w_in[0]), bf(gm_w_out[0])
    lng, lnb = row(gm_ln_g[0]), row(gm_ln_b[0])
    sb_prompt = jnp.repeat(gm_b_s[0].T, gd, axis=1)
    (yp,) = _gmlp(yp, ng, gm_in, lng, lnb, gm_w_s[0], sb_prompt, gm_out, tm=256, sample=False)
    sa_first = row(jnp.repeat(gm_w_s[0][:, 0, 0], gd))
    sb_first = row(jnp.repeat(gm_b_s[0][:, 0], gd))
    ys, gm_v = _gmlp(ys, ng, gm_in, lng, lnb, sa_first, sb_first, gm_out, tm=bs, sample=True)
    gm_v_sample = gm_v.reshape(1, bs, 1, width)
    yp, ys = ffn(0, yp, ys)

    ng = row(norm_mix_g[1])
    w_qkv, w_o = bf(moba_w_qkv[0]), bf(moba_w_o[0])
    heads = MOBA_HEADS
    hd = d // heads
    qp, kp, vp = _norm_linear(yp, ng, w_qkv, 3, tm=tm, name="moba_qkv_prompt")
    qs, ks, vs = _norm_linear(ys, ng, w_qkv, 3, tm=bs, name="moba_qkv_sample")
    blk = MOBA_BLOCK
    qi = jnp.arange(blk, dtype=I32)[:, None]
    ki = jnp.arange(blk, dtype=I32)[None, :]
    bkt_tiles = _t5_bucket_table(jnp.stack([qi - ki, blk + qi - ki]))
    bias_tiles = _bias_tiles(rel_bias, bkt_tiles)
    op = _moba_prompt(rel_bias, qp, kp, vp, bias_tiles, batch=bp, seq=seq, head_dim=hd)
    bkt_rows = _t5_bucket_table(past_len - jnp.arange(past_len, dtype=I32))
    rb_pad = jnp.pad(rel_bias, ((0, 0), (0, LANES - heads)))
    bias_rows = _bias_rows(rb_pad, jnp.broadcast_to(bkt_rows[:, None], (past_len, LANES)))
    n_pool = cache_moba_k.shape[1]
    os_ = _moba_sample(page_table, qs, ks, vs, cache_moba_k[0].reshape(n_pool, PAGE_SIZE, d),
                       cache_moba_v[0].reshape(n_pool, PAGE_SIZE, d), bias_rows, rb_pad, head_dim=hd)
    yp = _linear_res(op, w_o, yp, tm=tm, name="moba_out_prompt")
    ys = _linear_res(os_, w_o, ys, tm=bs, name="moba_out_sample")
    moba_k_prompt = kp.reshape(1, bp, seq, heads, hd)
    moba_v_prompt = vp.reshape(1, bp, seq, heads, hd)
    moba_k_sample = ks.reshape(1, bs, 1, heads, hd)
    moba_v_sample = vs.reshape(1, bs, 1, heads, hd)
    yp, ys = ffn(1, yp, ys)

    ng = row(norm_mix_g[2])
    pw, psc = bf(pool_w[0]), row(pool_scale[0])
    yp, pool_p = _pool_prompt(yp, ng, pw, psc, tm=tm, seq=seq)
    ys, hs = _pool_sample(ys, ng, jnp.swapaxes(state_pool[0], 0, 1), pw, psc)
    pool_prompt = pool_p[None]
    pool_sample = jnp.concatenate([state_pool[0][:, 1:], hs[:, None]], axis=1)[None]
    yp, ys = ffn(2, yp, ys)

    ng = row(norm_mix_g[3])
    mats = [bf(m[0]) for m in (rwkv_w_r, rwkv_w_k, rwkv_w_v, rwkv_w1, rwkv_w2, rwkv_a1, rwkv_a2,
                               rwkv_g1, rwkv_g2)]
    vec_in = jnp.stack([rwkv_w0[0], rwkv_a0[0], rwkv_k_k[0], rwkv_k_a[0]])
    vec_out = jnp.stack([rwkv_r_k[0].reshape(-1), rwkv_lnx_g[0], rwkv_lnx_b[0]])
    rh = d // RWKV_HEAD_DIM
    *seqs_p, gp, shp = _rwkv_proj(yp, ng, rwkv_mu[0], mats, vec_in, None, tm=256, seq=seq, sample=False)
    *seqs_s, gs, shs = _rwkv_proj(ys, ng, rwkv_mu[0], mats, vec_in, state_rwkv_shift[0], tm=bs, seq=1,
                                  sample=True)
    o_p, st_p = _rwkv_chunked(*seqs_p, batch=bp, seq=seq)
    hd_r = RWKV_HEAD_DIM
    wkv_p = st_p.reshape(bp, rh // 2, hd_r, 2, hd_r).transpose(0, 1, 3, 2, 4).reshape(bp, rh, hd_r, hd_r)
    o_s, wkv_s = _rwkv_scan(*[a.reshape(bs, 1, d) for a in seqs_s], state_rwkv_wkv[0], bb=8, tt=1)
    w_o = bf(rwkv_w_o[0])
    r_p, _, k_p, v_p = seqs_p[:4]
    r_s, _, k_s, v_s = seqs_s[:4]
    yp = _rwkv_out(o_p, r_p, k_p, v_p, gp, yp, vec_out, w_o, tm=tm, name="rwkv_out_prompt")
    ys = _rwkv_out(o_s.reshape(bs, d), r_s, k_s, v_s, gs, ys, vec_out, w_o, tm=bs, name="rwkv_out_sample")
    yp, ys = ffn(3, yp, ys)

    return (yp.reshape(bp, seq, d), ys.reshape(bs, 1, d), gm_v_sample, moba_k_prompt, moba_v_prompt,
            moba_k_sample, moba_v_sample, pool_prompt, pool_sample, wkv_p[None], wkv_s[None],
            shp.reshape(1, bp, d), shs[None], jnp.stack(conv_p), jnp.stack(conv_s))
```

```python
import functools
import math

import jax
import jax.numpy as jnp
from jax import lax
from jax.experimental import pallas as pl
from jax.experimental.pallas import tpu as pltpu

F32 = jnp.float32
BF16 = jnp.bfloat16
I32 = jnp.int32

LANES = 128
SUBLANES = 8
VMEM_LIMIT_BYTES = 56 * 2**20

RMS_EPS = 1e-6
GM_LN_EPS = 1e-5
GM_CHUNK = 128
GM_GROUPS = 8
MOBA_HEADS = 16
MOBA_BLOCK = 256
MOBA_TOPK = 3
REL_BUCKETS = 32
REL_MAX_DIST = 128
PAGE_SIZE = 128
POOL_WINDOWS = (2, 4, 8, 16)
POOL_CTX = max(POOL_WINDOWS) - 1
RWKV_HEAD_DIM = 64
RWKV_LNX_EPS = 64e-5
FFN_CONV = 3
NEG_INF = float("-inf")

assert MOBA_BLOCK >= REL_MAX_DIST


def _cparams(*sem):
    return pltpu.CompilerParams(dimension_semantics=sem, vmem_limit_bytes=VMEM_LIMIT_BYTES)


def _whole(shape):
    nd = len(shape)
    return pl.BlockSpec(shape, lambda *_: (0,) * nd, pipeline_mode=pl.Buffered(1))


def _rows(tm, width):
    return pl.BlockSpec((tm, width), lambda i: (i, 0))


def _rms(x, g):
    return x * lax.rsqrt(jnp.mean(x * x, axis=-1, keepdims=True) + RMS_EPS) * g


def _bdot(a, b):
    return jnp.dot(a.astype(BF16), b.astype(BF16), preferred_element_type=F32)


def _nt_dot(a, b):
    return lax.dot_general(a.astype(BF16), b.astype(BF16), (((1,), (1,)), ((), ())),
                           preferred_element_type=F32)


def _split(x):
    hi = x.astype(BF16)
    lo = (x - hi.astype(F32)).astype(BF16)
    return hi, lo


def _split_dot(x, m):
    hi, lo = _split(x)
    return (jnp.dot(hi, m, preferred_element_type=F32)
            + jnp.dot(lo, m, preferred_element_type=F32))


def _same_head(n, head_dim):
    shift = int(math.log2(head_dim))
    r = lax.broadcasted_iota(I32, (n, n), 0) >> shift
    c = lax.broadcasted_iota(I32, (n, n), 1) >> shift
    return jnp.where(r == c, 1.0, 0.0).astype(BF16)


def _seg_sum(x, head_dim):
    g = _same_head(LANES, head_dim)
    parts = [_split_dot(x[:, i:i + LANES], g) for i in range(0, x.shape[1], LANES)]
    return jnp.concatenate(parts, axis=1)


def _gelu(x):
    return 0.5 * x * (1.0 + jnp.tanh(0.7978845608028654 * (x + 0.044715 * x * x * x)))


def _sigmoid(x):
    return 1.0 / (1.0 + jnp.exp(-x))


def _softplus(x):
    return jnp.maximum(x, 0.0) + jnp.log(1.0 + jnp.exp(-jnp.abs(x)))


def _top_mask(gate, idx, axis, n_valid):
    cur = gate
    sel = jnp.zeros(gate.shape, F32)
    for _ in range(MOBA_TOPK):
        m = jnp.max(cur, axis=axis, keepdims=True)
        first = jnp.min(jnp.where(cur == m, idx, n_valid), axis=axis, keepdims=True)
        pick = (idx == first) & (m > NEG_INF)
        sel = jnp.where(pick, 1.0, sel)
        cur = jnp.where(pick, NEG_INF, cur)
    return sel


def _gmlp_kernel(x_ref, ng_ref, win_ref, lng_ref, lnb_ref, sa_ref, sb_ref, wout_ref, y_ref, aux_ref,
                 *, sample):
    x = x_ref[...]
    tm, width = x.shape[0], wout_ref.shape[0]
    h = _rms(x, ng_ref[...])
    z = _gelu(_bdot(h, win_ref[...]))
    u, v = z[:, :width], z[:, width:]
    mu = jnp.mean(v, axis=-1, keepdims=True)
    d = v - mu
    var = jnp.mean(d * d, axis=-1, keepdims=True)
    v = d * lax.rsqrt(var + GM_LN_EPS) * lng_ref[...] + lnb_ref[...]
    if sample:
        aux_ref[...] = v
        s = v * sa_ref[...] + sb_ref[...]
    else:
        gd = width // GM_GROUPS
        causal = (lax.broadcasted_iota(I32, (GM_CHUNK, GM_CHUNK), 0)
                  >= lax.broadcasted_iota(I32, (GM_CHUNK, GM_CHUNK), 1))
        for g in range(GM_GROUPS):
            wg = jnp.where(causal, sa_ref[g], 0.0).astype(BF16)
            cols = slice(g * gd, (g + 1) * gd)
            for c in range(tm // GM_CHUNK):
                rows = slice(c * GM_CHUNK, (c + 1) * GM_CHUNK)
                aux_ref[rows, cols] = (jnp.dot(wg, v[rows, cols].astype(BF16), preferred_element_type=F32)
                                       + sb_ref[:, cols])
        s = aux_ref[...]
    y_ref[...] = x + _bdot(u * s, wout_ref[...])


def _gmlp(x, ng, win, lng, lnb, sa, sb, wout, *, tm, sample):
    n, d = x.shape
    width = wout.shape[0]
    outs = [jax.ShapeDtypeStruct((n, d), F32)]
    out_specs = [_rows(tm, d)]
    scratch = []
    if sample:
        outs.append(jax.ShapeDtypeStruct((n, width), F32))
        out_specs.append(_rows(tm, width))
    else:
        scratch.append(pltpu.VMEM((tm, width), F32))
    return pl.pallas_call(
        functools.partial(_gmlp_kernel, sample=sample),
        out_shape=outs, grid=(n // tm,),
        in_specs=[_rows(tm, d), _whole(ng.shape), _whole(win.shape), _whole(lng.shape), _whole(lnb.shape),
                  _whole(sa.shape), _whole(sb.shape), _whole(wout.shape)],
        out_specs=out_specs, scratch_shapes=scratch,
        compiler_params=_cparams("parallel"), name="gmlp_sample" if sample else "gmlp_prompt",
    )(x, ng, win, lng, lnb, sa, sb, wout)


def _norm_linear_kernel(x_ref, ng_ref, w_ref, *o_refs):
    z = _bdot(_rms(x_ref[...], ng_ref[...]), w_ref[...])
    wd = z.shape[1] // len(o_refs)
    for i, o_ref in enumerate(o_refs):
        o_ref[...] = z[:, i * wd:(i + 1) * wd]


def _norm_linear(x, ng, w, n_out, *, tm, name):
    n, d = x.shape
    wd = w.shape[1] // n_out
    return pl.pallas_call(
        _norm_linear_kernel,
        out_shape=[jax.ShapeDtypeStruct((n, wd), F32)] * n_out, grid=(n // tm,),
        in_specs=[_rows(tm, d), _whole(ng.shape), _whole(w.shape)],
        out_specs=[_rows(tm, wd)] * n_out,
        compiler_params=_cparams("parallel"), name=name,
    )(x, ng, w)


def _linear_res_kernel(a_ref, w_ref, y_ref, o_ref):
    o_ref[...] = y_ref[...] + _bdot(a_ref[...], w_ref[...])


def _linear_res(a, w, y, *, tm, name):
    n, d = y.shape
    return pl.pallas_call(
        _linear_res_kernel, out_shape=jax.ShapeDtypeStruct((n, d), F32), grid=(n // tm,),
        in_specs=[_rows(tm, a.shape[1]), _whole(w.shape), _rows(tm, d)], out_specs=_rows(tm, d),
        compiler_params=_cparams("parallel"), name=name,
    )(a, w, y)


def _t5_bucket_table(rel):
    n = jnp.maximum(rel, 0)
    exact = REL_BUCKETS // 2
    nf = jnp.maximum(n, 1).astype(F32)
    large = exact + (jnp.log(nf / exact) / math.log(REL_MAX_DIST / exact)
                     * (REL_BUCKETS - exact)).astype(I32)
    return jnp.where(n < exact, n, jnp.minimum(large, REL_BUCKETS - 1)).astype(I32)


def _bias_tiles_kernel(rb_ref, bkt_ref, o_ref):
    h = pl.program_id(0)
    bkt = bkt_ref[...]
    acc = jnp.zeros(bkt.shape, F32)
    for b in range(REL_BUCKETS):
        acc = jnp.where(bkt == b, rb_ref[b, h], acc)
    o_ref[0] = acc


def _bias_tiles(rel_bias, bkt):
    heads = rel_bias.shape[1]
    return pl.pallas_call(
        _bias_tiles_kernel, out_shape=jax.ShapeDtypeStruct((heads,) + bkt.shape, F32), grid=(heads,),
        in_specs=[pl.BlockSpec(memory_space=pltpu.SMEM), _whole(bkt.shape)],
        out_specs=pl.BlockSpec((1,) + bkt.shape, lambda h: (h, 0, 0, 0)),
        compiler_params=_cparams("parallel"), name="moba_bias_tiles",
    )(rel_bias, bkt)


def _bias_rows_kernel(rb_ref, bkt_ref, o_ref):
    bkt = bkt_ref[...]
    acc = jnp.zeros(bkt.shape, F32)
    for b in range(REL_BUCKETS):
        acc = jnp.where(bkt == b, rb_ref[b:b + 1, :], acc)
    o_ref[...] = acc


def _bias_rows(rb_pad, bkt):
    return pl.pallas_call(
        _bias_rows_kernel, out_shape=jax.ShapeDtypeStruct(bkt.shape, F32), grid=(1,),
        in_specs=[_whole(rb_pad.shape), _whole(bkt.shape)], out_specs=_whole(bkt.shape),
        compiler_params=_cparams("arbitrary"), name="moba_bias_rows",
    )(rb_pad, bkt)


def _moba_prompt_kernel(rb_ref, q_ref, k_ref, v_ref, bias_ref, o_ref, kb_scr, vh_scr, kmean_scr, *, head_dim):
    pair, qb = pl.program_id(1), pl.program_id(2)
    blk = MOBA_BLOCK
    seq = k_ref.shape[0]
    n_blocks = seq // blk
    assert LANES == 2 * head_dim
    shift = int(math.log2(head_dim))
    scale = head_dim ** -0.5
    assert shift % 2 == 0 and n_blocks <= kmean_scr.shape[0]

    @pl.when(qb == 0)
    def _():
        k, v = k_ref[...], v_ref[...]
        second = (lax.broadcasted_iota(I32, (seq, LANES), 1) >> shift) == 1
        kb_scr[...] = k.astype(BF16)
        vh_scr[0] = jnp.where(second, 0.0, v).astype(BF16)
        vh_scr[1] = jnp.where(second, v, 0.0).astype(BF16)
        kmean_scr[...] = jnp.zeros(kmean_scr.shape, F32)
        for n in range(n_blocks):
            kmean_scr[n:n + 1, :] = jnp.mean(k[n * blk:(n + 1) * blk, :], axis=0, keepdims=True)

    q = q_ref[...]
    second = (lax.broadcasted_iota(I32, (blk, LANES), 1) >> shift) == 1
    q2 = jnp.concatenate([jnp.where(second, 0.0, q), jnp.where(second, q, 0.0)], axis=0)
    q_hi, q_lo = _split(q2)
    km_hi, km_lo = _split(kmean_scr[...])
    nt = lambda a, b: lax.dot_general(a, b, (((1,), (1,)), ((), ())), preferred_element_type=F32)
    gate = nt(km_hi, q_hi) + nt(km_lo, q_hi) + nt(km_hi, q_lo)
    block_id = lax.broadcasted_iota(I32, gate.shape, 0)
    sel = _top_mask(jnp.where(block_id < qb, gate, NEG_INF), block_id, 0, gate.shape[0]).T
    q_hi = (q2 * scale).astype(BF16)
    causal = (lax.broadcasted_iota(I32, (blk, blk), 1) <= lax.broadcasted_iota(I32, (blk, blk), 0))
    far_bias = [rb_ref[REL_BUCKETS - 1, pair * 2 + hh] for hh in range(2)]

    def attend(own):
        keys = (own + 1) * blk
        s = nt(q_hi, kb_scr[0:keys, :])
        head_rows = []
        for hh in range(2):
            rows = slice(hh * blk, (hh + 1) * blk)
            tiles = []
            for n in range(own + 1):
                t = s[rows, n * blk:(n + 1) * blk]
                if n == own:
                    t = jnp.where(causal, t + bias_ref[hh, 0], NEG_INF)
                else:
                    bias = bias_ref[hh, 1] if n == own - 1 else far_bias[hh]
                    t = jnp.where(sel[rows, n:n + 1] > 0.0, t + bias, NEG_INF)
                tiles.append(t)
            head_rows.append(jnp.concatenate(tiles, axis=1))
        s = jnp.concatenate(head_rows, axis=0)
        p = jnp.exp(s - jnp.max(s, axis=-1, keepdims=True))
        l = jnp.sum(p, axis=-1, keepdims=True)
        p_pair = jnp.concatenate([p[:blk], p[blk:]], axis=1).astype(BF16)
        v_pair = jnp.concatenate([vh_scr[0, 0:keys, :], vh_scr[1, 0:keys, :]], axis=0)
        o = jnp.dot(p_pair, v_pair, preferred_element_type=F32)
        o_ref[...] = o / jnp.where(second, l[blk:], l[:blk])

    for own in range(n_blocks):
        pl.when(qb == own)(functools.partial(attend, own))


def _moba_prompt(rel_bias, q, k, v, bias_tiles, *, batch, seq, head_dim):
    n, d = q.shape
    blk = MOBA_BLOCK
    nq = seq // blk
    hpt = LANES // head_dim
    return pl.pallas_call(
        functools.partial(_moba_prompt_kernel, head_dim=head_dim),
        out_shape=jax.ShapeDtypeStruct((n, d), F32), grid=(batch, d // LANES, nq),
        in_specs=[pl.BlockSpec(memory_space=pltpu.SMEM),
                  pl.BlockSpec((blk, LANES), lambda b, p, i: (b * nq + i, p)),
                  pl.BlockSpec((seq, LANES), lambda b, p, i: (b, p)),
                  pl.BlockSpec((seq, LANES), lambda b, p, i: (b, p)),
                  pl.BlockSpec((hpt, 2, blk, blk), lambda b, p, i: (p, 0, 0, 0))],
        out_specs=pl.BlockSpec((blk, LANES), lambda b, p, i: (b * nq + i, p)),
        scratch_shapes=[pltpu.VMEM((seq, LANES), BF16), pltpu.VMEM((hpt, seq, LANES), BF16),
                        pltpu.VMEM((2 * SUBLANES, LANES), F32)],
        compiler_params=_cparams("parallel", "parallel", "arbitrary"), name="moba_prompt_attn",
    )(rel_bias, q, k, v, bias_tiles)


def _moba_sample_kernel(pt_ref, q_ref, kn_ref, vn_ref, ka_ref, kb_ref, va_ref, vb_ref, bias_ref, rb0_ref,
                        o_ref, m_scr, l_scr, gs_scr, acc_scr):
    del pt_ref
    n = pl.program_id(1)
    n_blocks = pl.num_programs(1)
    heads, hd = q_ref.shape[1], q_ref.shape[2]
    scale = hd ** -0.5
    q = q_ref[0]
    flat = lambda page: page[0, 0].reshape(PAGE_SIZE * heads, hd)
    k2 = jnp.concatenate([flat(ka_ref), flat(kb_ref)], axis=0)
    v2 = jnp.concatenate([flat(va_ref), flat(vb_ref)], axis=0)
    s = _nt_dot(q, k2) * scale + bias_ref[n]
    shape = s.shape
    own = (lax.broadcasted_iota(I32, shape, 1) & (heads - 1)) == lax.broadcasted_iota(I32, shape, 0)
    s = jnp.where(own, s, NEG_INF)
    m = jnp.max(s, axis=-1, keepdims=True)
    p = jnp.exp(s - m)
    m_scr[n] = jnp.broadcast_to(m, m_scr.shape[1:])
    l_scr[n] = jnp.broadcast_to(jnp.sum(p, axis=-1, keepdims=True), l_scr.shape[1:])
    gs_scr[n] = (jnp.sum(ka_ref[0, 0], axis=0) + jnp.sum(kb_ref[0, 0], axis=0)) * q
    acc_scr[n] = _bdot(p, v2)

    @pl.when(n == n_blocks - 1)
    def _():
        nb = m_scr.shape[0]
        wide = m_scr.shape
        gate = jnp.broadcast_to(jnp.sum(gs_scr[...], axis=-1, keepdims=True), wide) * (1.0 / MOBA_BLOCK)
        sel = _top_mask(gate, lax.broadcasted_iota(I32, wide, 0), 0, nb)
        s_new = jnp.sum(kn_ref[0] * q, axis=-1, keepdims=True) * scale + rb0_ref[...]
        m_all = m_scr[...]
        m_tot = jnp.maximum(jnp.max(jnp.where(sel > 0.0, m_all, NEG_INF), axis=0), s_new)
        w_blk = jnp.where(sel > 0.0, jnp.exp(m_all - m_tot[None]), 0.0)
        w_new = jnp.exp(s_new - m_tot)
        den = jnp.sum(w_blk * l_scr[...], axis=0) + w_new
        num = jnp.sum(w_blk[:, :, :hd] * acc_scr[...], axis=0) + w_new[:, :hd] * vn_ref[0]
        o_ref[0] = num / den[:, :hd]


def _moba_sample(page_table, q, k_new, v_new, cache_k, cache_v, layer, bias_rows, rel_bias, *, head_dim):
    nb, d = q.shape
    n_pages = page_table.shape[1]
    per_block = MOBA_BLOCK // PAGE_SIZE
    assert per_block == 2
    n_blocks = n_pages // per_block
    heads = d // head_dim
    assert heads & (heads - 1) == 0
    per_head = lambda x: x.reshape(nb, heads, head_dim)
    vec = pl.BlockSpec((1, heads, head_dim), lambda s, n, pt: (s, 0, 0))
    page = lambda j: pl.BlockSpec((1, 1, PAGE_SIZE, heads, head_dim),
                                  lambda s, n, pt: (layer, pt[s, per_block * n + j], 0, 0, 0))
    const = lambda shape: pl.BlockSpec(shape, lambda s, n, pt: (0,) * len(shape))
    bias = bias_rows[:, :heads].reshape(n_blocks, MOBA_BLOCK, heads).transpose(0, 2, 1)
    bias = jnp.repeat(bias, heads, axis=-1)
    rb0 = jnp.broadcast_to(rel_bias[0][:, None], (heads, LANES))
    out = pl.pallas_call(
        _moba_sample_kernel,
        out_shape=jax.ShapeDtypeStruct((nb, heads, head_dim), F32),
        grid_spec=pltpu.PrefetchScalarGridSpec(
            num_scalar_prefetch=1, grid=(nb, n_blocks),
            in_specs=[vec, vec, vec, page(0), page(1), page(0), page(1), const(bias.shape), const(rb0.shape)],
            out_specs=vec,
            scratch_shapes=[pltpu.VMEM((n_blocks, heads, LANES), F32), pltpu.VMEM((n_blocks, heads, LANES), F32),
                            pltpu.VMEM((n_blocks, heads, head_dim), F32),
                            pltpu.VMEM((n_blocks, heads, head_dim), F32)]),
        compiler_params=_cparams("parallel", "arbitrary"), name="moba_sample_attn",
    )(page_table, per_head(q), per_head(k_new), per_head(v_new), cache_k, cache_k, cache_v, cache_v, bias, rb0)
    return out.reshape(nb, d)


POOL_PAD = 32


def _pool_prompt_kernel(y_ref, ng_ref, w_ref, sc_ref, yout_ref, st_ref, b0, b1, b2, b3, b4, *, tps):
    i = pl.program_id(0)
    y = y_ref[...]
    tm, d = y.shape
    gd = d // len(POOL_WINDOWS)
    h = _rms(y, ng_ref[...])
    half = POOL_PAD // 2

    @pl.when(i == 0)
    def _():
        for b in (b0, b1, b2, b3, b4):
            b[0:half, :] = jnp.zeros((half, b.shape[1]), F32)

    @pl.when(i % tps == 0)
    def _():
        b0[half:POOL_PAD, :] = jnp.zeros((half, d), F32)

    b0[POOL_PAD:, :] = h
    stages = (b0, b1, b2, b3, b4)
    for k in range(1, len(stages)):
        src, dst = stages[k - 1], stages[k]
        back = 2 ** (k - 1)
        off = src.shape[1] - dst.shape[1]
        dst[SUBLANES:, :] = (src[SUBLANES:, off:] + src[SUBLANES - back:tm + POOL_PAD - back, off:])
    pos = (i % tps) * tm + lax.broadcasted_iota(I32, (tm, 1), 0) + 1
    for g, win in enumerate(POOL_WINDOWS):
        cols = slice(g * gd, (g + 1) * gd)
        wsum = stages[g + 1][POOL_PAD:, 0:gd]
        cnt = jnp.minimum(pos, win).astype(F32)
        mixed = _bdot(wsum / cnt - h[:, cols], w_ref[g])
        yout_ref[:, cols] = y[:, cols] + mixed * sc_ref[:, cols]
    st_ref[0] = b0[tm + POOL_PAD - POOL_CTX:, :]
    b0[half:POOL_PAD, :] = b0[tm + half:, :]


def _pool_prompt(y, ng, w, sc, *, tm, seq):
    n, d = y.shape
    gd = d // len(POOL_WINDOWS)
    tps = seq // tm
    rows = tm + POOL_PAD
    return pl.pallas_call(
        functools.partial(_pool_prompt_kernel, tps=tps),
        out_shape=[jax.ShapeDtypeStruct((n, d), F32), jax.ShapeDtypeStruct((n // seq, POOL_CTX, d), F32)],
        grid=(n // tm,),
        in_specs=[_rows(tm, d), _whole(ng.shape), _whole(w.shape), _whole(sc.shape)],
        out_specs=[_rows(tm, d), pl.BlockSpec((1, POOL_CTX, d), lambda i: (i // tps, 0, 0))],
        scratch_shapes=[pltpu.VMEM((rows, d - k * gd), F32) for k in (0, 0, 1, 2, 3)],
        compiler_params=_cparams("arbitrary"), name="pool_prompt",
    )(y, ng, w, sc)


def _pool_sample_kernel(y_ref, ng_ref, prev_ref, w_ref, sc_ref, yout_ref, h_ref):
    y = y_ref[...]
    d = y.shape[1]
    gd = d // len(POOL_WINDOWS)
    h = _rms(y, ng_ref[...])
    h_ref[...] = h
    for g, win in enumerate(POOL_WINDOWS):
        cols = slice(g * gd, (g + 1) * gd)
        wsum = h[:, cols]
        for back in range(1, win):
            wsum = wsum + prev_ref[POOL_CTX - back, :, cols]
        mixed = _bdot(wsum / float(win) - h[:, cols], w_ref[g])
        yout_ref[:, cols] = y[:, cols] + mixed * sc_ref[:, cols]


def _pool_sample(y, ng, prev_t, w, sc):
    n, d = y.shape
    return pl.pallas_call(
        _pool_sample_kernel, out_shape=[jax.ShapeDtypeStruct((n, d), F32)] * 2, grid=(1,),
        in_specs=[_whole(y.shape), _whole(ng.shape), _whole(prev_t.shape), _whole(w.shape), _whole(sc.shape)],
        out_specs=[_whole(y.shape)] * 2,
        compiler_params=_cparams("arbitrary"), name="pool_sample",
    )(y, ng, prev_t, w, sc)


def _rwkv_proj_kernel(y_ref, ng_ref, mu_ref, wr_ref, wk_ref, wv_ref, w1_ref, w2_ref, a1_ref, a2_ref,
                      g1_ref, g2_ref, vec_ref, *rest, tps, sample):
    if sample:
        prev_ref, r_o, w_o, k_o, v_o, kk_o, kka_o, g_o, sh_o = rest
    else:
        r_o, w_o, k_o, v_o, kk_o, kka_o, g_o, sh_o, hs_scr = rest
    y = y_ref[...]
    tm = y.shape[0]
    h = _rms(y, ng_ref[...])
    if sample:
        h_prev = prev_ref[...]
        sh_o[...] = h
    else:
        i = pl.program_id(0)

        @pl.when(i % tps == 0)
        def _():
            hs_scr[0:SUBLANES, :] = jnp.zeros((SUBLANES, y.shape[1]), F32)

        hs_scr[SUBLANES:, :] = h
        h_prev = hs_scr[SUBLANES - 1:tm + SUBLANES - 1, :]
        hs_scr[0:SUBLANES, :] = h[tm - SUBLANES:, :]
        sh_o[0] = h[tm - 1:tm, :]
    xx = h_prev - h
    mix = lambda m: h + xx * mu_ref[m:m + 1, :]
    r = _bdot(mix(0), wr_ref[...])
    k = _bdot(mix(2), wk_ref[...])
    v = _bdot(mix(3), wv_ref[...])
    w_log = -_softplus(-(vec_ref[0:1, :] + _bdot(jnp.tanh(_bdot(mix(1), w1_ref[...])), w2_ref[...]))) - 0.5
    a = _sigmoid(vec_ref[1:2, :] + _bdot(_bdot(mix(4), a1_ref[...]), a2_ref[...]))
    g_o[...] = _bdot(_sigmoid(_bdot(mix(5), g1_ref[...])), g2_ref[...])
    kk = k * vec_ref[2:3, :]
    kk = kk / jnp.maximum(jnp.sqrt(_seg_sum(kk * kk, RWKV_HEAD_DIM)), 1e-12)
    r_o[...] = r
    w_o[...] = -jnp.exp(w_log)
    k_o[...] = k * (1.0 + (a - 1.0) * vec_ref[3:4, :])
    v_o[...] = v
    kk_o[...] = kk
    kka_o[...] = kk * a


def _rwkv_proj(y, ng, mu, mats, vec, prev, *, tm, seq, sample):
    n, d = y.shape
    tps = max(seq // tm, 1)
    ins = [y, ng, mu, *mats, vec]
    in_specs = [_rows(tm, d), _whole(ng.shape), _whole(mu.shape), *[_whole(m.shape) for m in mats],
                _whole(vec.shape)]
    outs = [jax.ShapeDtypeStruct((n, d), F32)] * 7
    out_specs = [_rows(tm, d)] * 7
    scratch = []
    if sample:
        ins.append(prev)
        in_specs.append(_rows(tm, d))
        outs.append(jax.ShapeDtypeStruct((n, d), F32))
        out_specs.append(_rows(tm, d))
    else:
        outs.append(jax.ShapeDtypeStruct((n // seq, 1, d), F32))
        out_specs.append(pl.BlockSpec((1, 1, d), lambda i: (i // tps, 0, 0)))
        scratch.append(pltpu.VMEM((tm + SUBLANES, d), F32))
    return pl.pallas_call(
        functools.partial(_rwkv_proj_kernel, tps=tps, sample=sample),
        out_shape=outs, grid=(n // tm,), in_specs=in_specs, out_specs=out_specs, scratch_shapes=scratch,
        compiler_params=_cparams("arbitrary"), name="rwkv_proj_sample" if sample else "rwkv_proj_prompt",
    )(*ins)


RWKV_CHUNK = 128


def _rwkv_chunk_kernel(r_ref, lw_ref, k_ref, v_ref, kk_ref, kka_ref, o_ref, st_ref, t_scr):
    c = pl.program_id(1)
    ch, d = r_ref.shape
    hd = RWKV_HEAD_DIM
    shift = int(math.log2(hd))
    n_pairs = d // LANES

    @pl.when(c == 0)
    def _():
        t_scr[...] = jnp.zeros(t_scr.shape, F32)

    ri = lax.broadcasted_iota(I32, (ch, ch), 0)
    ci = lax.broadcasted_iota(I32, (ch, ch), 1)
    incl = ci <= ri
    strict = ci < ri
    incl2 = jnp.concatenate([incl, incl], axis=1)
    strict2 = jnp.concatenate([strict, strict], axis=1)
    same_head = (ri >> shift) == (ci >> shift)
    eye = ri == ci
    head1 = (lax.broadcasted_iota(I32, (ch, LANES), 1) >> shift) == 1
    head1_wide = jnp.concatenate([head1, head1], axis=1)

    def by_head(x, mask):
        return jnp.concatenate([jnp.where(mask, 0.0, x), jnp.where(mask, x, 0.0)], axis=0).astype(BF16)

    lw = lw_ref[...]
    tri = jnp.where(incl, 1.0, 0.0).astype(BF16)
    p1 = lw.astype(BF16)
    rem = lw - p1.astype(F32)
    p2 = rem.astype(BF16)
    p3 = (rem - p2.astype(F32)).astype(BF16)
    cl = (jnp.dot(tri, p1, preferred_element_type=F32) + jnp.dot(tri, p2, preferred_element_type=F32)
          + jnp.dot(tri, p3, preferred_element_type=F32))
    mid = cl[ch // 2 - 1:ch // 2, :]
    last = cl[ch - 1:ch, :]
    e_mid = jnp.exp(mid)
    e_neg = jnp.exp(mid - cl)
    e_tail = jnp.exp(last - cl)
    p_last = jnp.exp(last)
    kk, kka, k = kk_ref[...], kka_ref[...], k_ref[...]
    a_s = -kk * jnp.exp(cl - lw - mid)
    r_s = r_ref[...] * jnp.exp(cl - mid)
    b_s = kka * e_neg
    k_s = k * e_neg
    b_t = kka * e_tail
    k_t = k * e_tail
    v = v_ref[...]
    zero = jnp.zeros((ch, ch), BF16)

    pairs = range(n_pairs)
    cols = [slice(p * LANES, (p + 1) * LANES) for p in pairs]
    mm = lambda a, b: jnp.dot(a, b, preferred_element_type=F32)
    lane_cat = lambda a, b: jnp.concatenate([a, b], axis=1)
    a_p = [a_s[:, c] for c in cols]
    r_p = [r_s[:, c] for c in cols]
    v_p = [v[:, c] for c in cols]
    gram = []
    for p in pairs:
        lhs = jnp.concatenate([jnp.where(head1, 0.0, a_p[p]), jnp.where(head1, 0.0, r_p[p]),
                               jnp.where(head1, a_p[p], 0.0), jnp.where(head1, r_p[p], 0.0)], axis=0)
        rhs = lane_cat(b_s[:, cols[p]].T, k_s[:, cols[p]].T)
        gram.append(mm(lhs.astype(BF16), rhs.astype(BF16)))
    ab0 = [jnp.where(strict2, g_[0:ch], 0.0) for g_ in gram]
    rb0 = [jnp.where(incl2, g_[ch:2 * ch], 0.0) for g_ in gram]
    ab1 = [jnp.where(strict2, g_[2 * ch:3 * ch], 0.0) for g_ in gram]
    rb1 = [jnp.where(incl2, g_[3 * ch:], 0.0) for g_ in gram]
    v_heads = [by_head(v_p[p], head1) for p in pairs]
    x = [lane_cat(a_p[p], mm(lane_cat(ab0[p][:, ch:], ab1[p][:, ch:]).astype(BF16), v_heads[p])) for p in pairs]
    pw = [lane_cat(ab0[p][:, :ch], ab1[p][:, :ch]).astype(BF16) for p in pairs]
    n_sq = int(math.log2(ch))
    for it in range(n_sq):
        x = [x[p] + mm(pw[p], by_head(x[p], head1_wide)) for p in pairs]
        if it < n_sq - 1:
            pw = [mm(pw[p], jnp.concatenate([lane_cat(pw[p][:, :ch], zero), lane_cat(zero, pw[p][:, ch:])],
                                            axis=0)).astype(BF16) for p in pairs]
    x = [lane_cat(x[p][:, :ch] * e_mid[:, cols[p]], x[p][:, ch:]) for p in pairs]
    qo = [mm(lane_cat(rb0[p][:, :ch], rb1[p][:, :ch]).astype(BF16), by_head(x[p], head1_wide)) for p in pairs]
    o_intra = [qo[p][:, ch:] + mm(lane_cat(rb0[p][:, ch:], rb1[p][:, ch:]).astype(BF16), v_heads[p])
               for p in pairs]
    q = [r_p[p] * e_mid[:, cols[p]] + qo[p][:, :ch] for p in pairs]
    gh = [mm(b_t[:, cols[p]].T.astype(BF16), x[p].astype(BF16)) for p in pairs]
    g = [jnp.where(same_head, gh[p][:, :ch], 0.0) + jnp.where(eye, p_last[:, cols[p]], 0.0) for p in pairs]
    h = [jnp.where(same_head, gh[p][:, ch:] + _bdot(k_t[:, cols[p]].T, v_p[p]), 0.0) for p in pairs]
    for p in pairs:
        t_old = t_scr[p].astype(BF16)
        o_ref[:, cols[p]] = mm(q[p].astype(BF16), t_old) + o_intra[p]
        t_scr[p] = mm(g[p].astype(BF16), t_old) + h[p]

    @pl.when(c == pl.num_programs(1) - 1)
    def _():
        for p in range(n_pairs):
            s_pair = t_scr[p].T
            st_ref[0, p] = s_pair[:hd, :] + s_pair[hd:, :]


def _rwkv_chunked(r, lw, k, v, kk, kka, *, batch, seq):
    n, d = r.shape
    ch = RWKV_CHUNK
    assert ch == LANES and seq % ch == 0
    n_chunks = seq // ch
    n_pairs = d // LANES
    blk = pl.BlockSpec((ch, d), lambda b, c: (b * n_chunks + c, 0))
    return pl.pallas_call(
        _rwkv_chunk_kernel,
        out_shape=[jax.ShapeDtypeStruct((n, d), F32),
                   jax.ShapeDtypeStruct((batch, n_pairs, RWKV_HEAD_DIM, LANES), F32)],
        grid=(batch, n_chunks), in_specs=[blk] * 6,
        out_specs=[blk, pl.BlockSpec((1, n_pairs, RWKV_HEAD_DIM, LANES), lambda b, c: (b, 0, 0, 0))],
        scratch_shapes=[pltpu.VMEM((n_pairs, LANES, LANES), F32)],
        compiler_params=_cparams("parallel", "arbitrary"), name="rwkv_chunked",
    )(r, lw, k, v, kk, kka)


def _rwkv_scan_kernel(r_ref, lw_ref, k_ref, v_ref, kk_ref, kka_ref, s0_ref, o_ref, st_ref, s_scr):
    c = pl.program_id(2)
    bb, tt, _ = r_ref.shape
    hd = RWKV_HEAD_DIM

    @pl.when(c == 0)
    def _():
        for b in range(bb):
            s_scr[b] = jnp.concatenate([s0_ref[b, 0], s0_ref[b, 1]], axis=1)

    ones = _same_head(LANES, hd)
    diag = jnp.where(lax.broadcasted_iota(I32, (hd, LANES), 0)
                     == (lax.broadcasted_iota(I32, (hd, LANES), 1) & (hd - 1)), 1.0, 0.0)

    def step(t, carry):
        seqs = range(bb)
        rows = lambda ref: [jnp.broadcast_to(ref[b, pl.ds(t, 1), :], (hd, LANES)) for b in seqs]
        mm = lambda x: jnp.dot(x.astype(BF16), ones, preferred_element_type=F32)
        kk, kka, k, r, w = rows(kk_ref), rows(kka_ref), rows(k_ref), rows(r_ref), rows(lw_ref)
        s = [s_scr[b] for b in seqs]
        s_kk = [mm(s[b] * kk[b]) for b in seqs]
        v_col = [_split_dot(diag * v_row, ones) for v_row in rows(v_ref)]
        s = [s[b] * jnp.exp(w[b]) - s_kk[b] * kka[b] + v_col[b] * k[b] for b in seqs]
        s_r = [mm(s[b] * r[b]) for b in seqs]
        for b in seqs:
            s_scr[b] = s[b]
            o_ref[b, pl.ds(t, 1), :] = jnp.sum(s_r[b] * diag, axis=0, keepdims=True)
        return carry

    lax.fori_loop(0, tt, step, 0)

    @pl.when(c == pl.num_programs(2) - 1)
    def _():
        for b in range(bb):
            s = s_scr[b]
            st_ref[b, 0] = s[:, :hd]
            st_ref[b, 1] = s[:, hd:]


def _rwkv_scan(r, w, k, v, kk, kka, s0, *, bb, tt):
    batch, time, d = r.shape
    heads = d // RWKV_HEAD_DIM
    seq_spec = pl.BlockSpec((bb, tt, LANES), lambda p, b, c: (b, c, p))
    st_spec = pl.BlockSpec((bb, 2, RWKV_HEAD_DIM, RWKV_HEAD_DIM), lambda p, b, c: (b, p, 0, 0))
    return pl.pallas_call(
        _rwkv_scan_kernel,
        out_shape=[jax.ShapeDtypeStruct((batch, time, d), F32),
                   jax.ShapeDtypeStruct((batch, heads, RWKV_HEAD_DIM, RWKV_HEAD_DIM), F32)],
        grid=(d // LANES, batch // bb, time // tt),
        in_specs=[seq_spec] * 6 + [st_spec], out_specs=[seq_spec, st_spec],
        scratch_shapes=[pltpu.VMEM((bb, RWKV_HEAD_DIM, LANES), F32)],
        compiler_params=_cparams("parallel", "parallel", "arbitrary"), name="rwkv_scan",
    )(r, w, k, v, kk, kka, s0)


def _rwkv_out_kernel(o_ref, r_ref, k_ref, v_ref, g_ref, y_ref, vec_ref, wo_ref, yout_ref):
    hd = RWKV_HEAD_DIM
    o = o_ref[...]
    mu = _seg_sum(o, hd) * (1.0 / hd)
    dlt = o - mu
    var = _seg_sum(dlt * dlt, hd) * (1.0 / hd)
    o = dlt * lax.rsqrt(var + RWKV_LNX_EPS) * vec_ref[1:2, :] + vec_ref[2:3, :]
    o = o + _seg_sum(r_ref[...] * k_ref[...] * vec_ref[0:1, :], hd) * v_ref[...]
    yout_ref[...] = y_ref[...] + _bdot(o * g_ref[...], wo_ref[...])


def _rwkv_out(o, r, k, v, g, y, vec, wo, *, tm, name):
    n, d = y.shape
    return pl.pallas_call(
        _rwkv_out_kernel, out_shape=jax.ShapeDtypeStruct((n, d), F32), grid=(n // tm,),
        in_specs=[_rows(tm, d)] * 6 + [_whole(vec.shape), _whole(wo.shape)], out_specs=_rows(tm, d),
        compiler_params=_cparams("parallel"), name=name,
    )(o, r, k, v, g, y, vec, wo)


FFN_CHUNK = 256


def _ffn_act(gate, g1, g2, up, cw, cb):
    return _gelu(cw[0:1] * g2 + cw[1:2] * g1 + cw[2:3] * gate + cb) * up


def _ffn_prompt_kernel(y_ref, ng_ref, win_ref, cw_ref, cb_ref, wout_ref, *rest, tps, final):
    if final:
        fg_ref, yout_ref, st_ref, yfin_ref, h_scr, gs_scr, act_scr, carry_scr = rest
    else:
        yout_ref, st_ref, h_scr, gs_scr, act_scr, carry_scr = rest
    i = pl.program_id(0)
    y = y_ref[...]
    tm = y.shape[0]
    hid = wout_ref.shape[0]
    tf = FFN_CHUNK
    h_scr[...] = _rms(y, ng_ref[...]).astype(BF16)

    @pl.when(i % tps == 0)
    def _():
        carry_scr[...] = jnp.zeros(carry_scr.shape, F32)

    for j in range(hid // tf):
        cols = slice(j * tf, (j + 1) * tf)
        hb = h_scr[...]
        gs_scr[0:SUBLANES, :] = carry_scr[:, cols]
        gs_scr[SUBLANES:, :] = jnp.dot(hb, win_ref[:, cols], preferred_element_type=F32)
        up = jnp.dot(hb, win_ref[:, hid + j * tf:hid + (j + 1) * tf], preferred_element_type=F32)
        gate = gs_scr[SUBLANES:, :]
        carry_scr[:, cols] = gate[tm - SUBLANES:, :]
        st_ref[0, :, cols] = gate[tm - (FFN_CONV - 1):, :]
        act = _ffn_act(gate, gs_scr[SUBLANES - 1:tm + SUBLANES - 1, :], gs_scr[SUBLANES - 2:tm + SUBLANES - 2, :],
                       up, cw_ref[:, cols], cb_ref[:, cols])
        act_scr[:, cols] = act.astype(BF16)
    y_new = y + jnp.dot(act_scr[...], wout_ref[...], preferred_element_type=F32)
    yout_ref[...] = y_new
    if final:
        yfin_ref[...] = _rms(y_new, fg_ref[...])


def _ffn_prompt(y, ng, win, cw, cb, wout, fg, *, tm, seq):
    n, d = y.shape
    hid = wout.shape[0]
    tps = seq // tm
    final = fg is not None
    ins = [y, ng, win, cw, cb, wout]
    in_specs = [_rows(tm, d)] + [_whole(a.shape) for a in ins[1:]]
    outs = [jax.ShapeDtypeStruct((n, d), F32), jax.ShapeDtypeStruct((n // seq, FFN_CONV - 1, hid), F32)]
    out_specs = [_rows(tm, d), pl.BlockSpec((1, FFN_CONV - 1, hid), lambda i: (i // tps, 0, 0))]
    if final:
        ins.append(fg)
        in_specs.append(_whole(fg.shape))
        outs.append(jax.ShapeDtypeStruct((n, d), F32))
        out_specs.append(_rows(tm, d))
    return pl.pallas_call(
        functools.partial(_ffn_prompt_kernel, tps=tps, final=final),
        out_shape=outs, grid=(n // tm,), in_specs=in_specs, out_specs=out_specs,
        scratch_shapes=[pltpu.VMEM((tm, d), BF16), pltpu.VMEM((tm + SUBLANES, FFN_CHUNK), F32),
                        pltpu.VMEM((tm, hid), BF16), pltpu.VMEM((SUBLANES, hid), F32)],
        compiler_params=_cparams("arbitrary"), name="ffn_prompt",
    )(*ins)


def _ffn_sample_kernel(y_ref, ng_ref, wg_ref, wu_ref, cw_ref, cb_ref, p2_ref, p1_ref, wout_ref, *rest, final):
    if final:
        fg_ref, yout_ref, gate_ref, yfin_ref, h_scr, acc_scr = rest
    else:
        yout_ref, gate_ref, h_scr, acc_scr = rest
    j = pl.program_id(0)

    @pl.when(j == 0)
    def _():
        h_scr[...] = _rms(y_ref[...], ng_ref[...]).astype(BF16)
        acc_scr[...] = jnp.zeros(acc_scr.shape, F32)

    hb = h_scr[...]
    gate = jnp.dot(hb, wg_ref[...], preferred_element_type=F32)
    up = jnp.dot(hb, wu_ref[...], preferred_element_type=F32)
    gate_ref[...] = gate
    act = _ffn_act(gate, p1_ref[...], p2_ref[...], up, cw_ref[...], cb_ref[...])
    acc_scr[...] += _bdot(act, wout_ref[...])

    @pl.when(j == pl.num_programs(0) - 1)
    def _():
        y_new = y_ref[...] + acc_scr[...]
        yout_ref[...] = y_new
        if final:
            yfin_ref[...] = _rms(y_new, fg_ref[...])


def _ffn_sample(y, ng, win, cw, cb, p2, p1, wout, fg):
    n, d = y.shape
    hid = wout.shape[0]
    tf = FFN_CHUNK
    nf = hid // tf
    final = fg is not None
    keep = lambda shape: pl.BlockSpec(shape, lambda j: (0,) * len(shape))
    chunk = lambda rows: pl.BlockSpec((rows, tf), lambda j: (0, j))
    ins = [y, ng, win, win, cw, cb, p2, p1, wout]
    in_specs = [keep((n, d)), keep(ng.shape), chunk(d), pl.BlockSpec((d, tf), lambda j: (0, nf + j)),
                chunk(FFN_CONV), chunk(1), chunk(n), chunk(n), pl.BlockSpec((tf, d), lambda j: (j, 0))]
    outs = [jax.ShapeDtypeStruct((n, d), F32), jax.ShapeDtypeStruct((n, hid), F32)]
    out_specs = [keep((n, d)), chunk(n)]
    if final:
        ins.append(fg)
        in_specs.append(keep(fg.shape))
        outs.append(jax.ShapeDtypeStruct((n, d), F32))
        out_specs.append(keep((n, d)))
    return pl.pallas_call(
        functools.partial(_ffn_sample_kernel, final=final),
        out_shape=outs, grid=(nf,), in_specs=in_specs, out_specs=out_specs,
        scratch_shapes=[pltpu.VMEM((n, d), BF16), pltpu.VMEM((n, d), F32)],
        compiler_params=_cparams("arbitrary"), name="ffn_sample",
    )(*ins)


def kernel(x_prompt, x_sample, cache_moba_k, cache_moba_v, state_pool, state_rwkv_wkv, state_rwkv_shift, state_ffn_conv, page_table, norm_mix_g, norm_ffn_g, norm_final_g, rel_bias, gm_w_in, gm_ln_g, gm_ln_b, gm_w_s, gm_b_s, gm_w_out, moba_w_qkv, moba_w_o, pool_w, pool_scale, rwkv_mu, rwkv_w_r, rwkv_w_k, rwkv_w_v, rwkv_w_o, rwkv_w0, rwkv_w1, rwkv_w2, rwkv_a0, rwkv_a1, rwkv_a2, rwkv_g1, rwkv_g2, rwkv_k_k, rwkv_k_a, rwkv_r_k, rwkv_lnx_g, rwkv_lnx_b, ffn_w_in, ffn_conv_w, ffn_conv_b, ffn_w_out):
    bp, seq, d = x_prompt.shape
    bs = x_sample.shape[0]
    depth = norm_mix_g.shape[0]
    assert x_sample.shape[1] == 1 and depth == 4
    past_len = page_table.shape[1] * PAGE_SIZE
    assert seq % MOBA_BLOCK == 0 and past_len % MOBA_BLOCK == 0 and past_len % GM_CHUNK == 0
    row = lambda vct: vct.reshape(1, -1)
    bf = lambda m: m.astype(BF16)
    yp = x_prompt.reshape(bp * seq, d)
    ys = x_sample.reshape(bs, d)
    tm = 512
    conv_p, conv_s = [], []

    def ffn(i, yp, ys):
        last = i == depth - 1
        win, wout = bf(ffn_w_in[i]), bf(ffn_w_out[i])
        cw, cb, ng = ffn_conv_w[i], row(ffn_conv_b[i]), row(norm_ffn_g[i])
        fg = row(norm_final_g) if last else None
        res_p = _ffn_prompt(yp, ng, win, cw, cb, wout, fg, tm=tm, seq=seq)
        st = state_ffn_conv[i]
        res_s = _ffn_sample(ys, ng, win, cw, cb, st[:, 0], st[:, 1], wout, fg)
        conv_p.append(res_p[1])
        conv_s.append(jnp.stack([st[:, 1], res_s[1]], axis=1))
        if last:
            return res_p[2], res_s[2]
        return res_p[0], res_s[0]

    ng = row(norm_mix_g[0])
    width = gm_w_out.shape[1]
    gd = width // GM_GROUPS
    gm_in, gm_out = bf(gm_w_in[0]), bf(gm_w_out[0])
    lng, lnb = row(gm_ln_g[0]), row(gm_ln_b[0])
    sb_prompt = jnp.repeat(gm_b_s[0].T, gd, axis=1)
    (yp,) = _gmlp(yp, ng, gm_in, lng, lnb, gm_w_s[0], sb_prompt, gm_out, tm=256, sample=False)
    sa_first = row(jnp.repeat(gm_w_s[0][:, 0, 0], gd))
    sb_first = row(jnp.repeat(gm_b_s[0][:, 0], gd))
    ys, gm_v = _gmlp(ys, ng, gm_in, lng, lnb, sa_first, sb_first, gm_out, tm=bs, sample=True)
    gm_v_sample = gm_v.reshape(1, bs, 1, width)
    yp, ys = ffn(0, yp, ys)

    ng = row(norm_mix_g[1])
    w_qkv, w_o = bf(moba_w_qkv[0]), bf(moba_w_o[0])
    heads = MOBA_HEADS
    hd = d // heads
    qp, kp, vp = _norm_linear(yp, ng, w_qkv, 3, tm=tm, name="moba_qkv_prompt")
    qs, ks, vs = _norm_linear(ys, ng, w_qkv, 3, tm=bs, name="moba_qkv_sample")
    blk = MOBA_BLOCK
    qi = jnp.arange(blk, dtype=I32)[:, None]
    ki = jnp.arange(blk, dtype=I32)[None, :]
    bkt_tiles = _t5_bucket_table(jnp.stack([qi - ki, blk + qi - ki]))
    bias_tiles = _bias_tiles(rel_bias, bkt_tiles)
    op = _moba_prompt(rel_bias, qp, kp, vp, bias_tiles, batch=bp, seq=seq, head_dim=hd)
    bkt_rows = _t5_bucket_table(past_len - jnp.arange(past_len, dtype=I32))
    rb_pad = jnp.pad(rel_bias, ((0, 0), (0, LANES - heads)))
    bias_rows = _bias_rows(rb_pad, jnp.broadcast_to(bkt_rows[:, None], (past_len, LANES)))
    os_ = _moba_sample(page_table, qs, ks, vs, cache_moba_k, cache_moba_v, 0, bias_rows, rel_bias, head_dim=hd)
    yp = _linear_res(op, w_o, yp, tm=tm, name="moba_out_prompt")
    ys = _linear_res(os_, w_o, ys, tm=bs, name="moba_out_sample")
    moba_k_prompt = kp.reshape(1, bp, seq, heads, hd)
    moba_v_prompt = vp.reshape(1, bp, seq, heads, hd)
    moba_k_sample = ks.reshape(1, bs, 1, heads, hd)
    moba_v_sample = vs.reshape(1, bs, 1, heads, hd)
    yp, ys = ffn(1, yp, ys)

    ng = row(norm_mix_g[2])
    pw, psc = bf(pool_w[0]), row(pool_scale[0])
    yp, pool_p = _pool_prompt(yp, ng, pw, psc, tm=tm, seq=seq)
    ys, hs = _pool_sample(ys, ng, jnp.swapaxes(state_pool[0], 0, 1), pw, psc)
    pool_prompt = pool_p[None]
    pool_sample = jnp.concatenate([state_pool[0][:, 1:], hs[:, None]], axis=1)[None]
    yp, ys = ffn(2, yp, ys)

    ng = row(norm_mix_g[3])
    mats = [bf(m[0]) for m in (rwkv_w_r, rwkv_w_k, rwkv_w_v, rwkv_w1, rwkv_w2, rwkv_a1, rwkv_a2,
                               rwkv_g1, rwkv_g2)]
    vec_in = jnp.stack([rwkv_w0[0], rwkv_a0[0], rwkv_k_k[0], rwkv_k_a[0]])
    vec_out = jnp.stack([rwkv_r_k[0].reshape(-1), rwkv_lnx_g[0], rwkv_lnx_b[0]])
    rh = d // RWKV_HEAD_DIM
    *seqs_p, gp, shp = _rwkv_proj(yp, ng, rwkv_mu[0], mats, vec_in, None, tm=256, seq=seq, sample=False)
    *seqs_s, gs, shs = _rwkv_proj(ys, ng, rwkv_mu[0], mats, vec_in, state_rwkv_shift[0], tm=bs, seq=1,
                                  sample=True)
    o_p, st_p = _rwkv_chunked(*seqs_p, batch=bp, seq=seq)
    hd_r = RWKV_HEAD_DIM
    wkv_p = st_p.reshape(bp, rh // 2, hd_r, 2, hd_r).transpose(0, 1, 3, 2, 4).reshape(bp, rh, hd_r, hd_r)
    o_s, wkv_s = _rwkv_scan(*[a.reshape(bs, 1, d) for a in seqs_s], state_rwkv_wkv[0], bb=8, tt=1)
    w_o = bf(rwkv_w_o[0])
    r_p, _, k_p, v_p = seqs_p[:4]
    r_s, _, k_s, v_s = seqs_s[:4]
    yp = _rwkv_out(o_p, r_p, k_p, v_p, gp, yp, vec_out, w_o, tm=tm, name="rwkv_out_prompt")
    ys = _rwkv_out(o_s.reshape(bs, d), r_s, k_s, v_s, gs, ys, vec_out, w_o, tm=bs, name="rwkv_out_sample")
    yp, ys = ffn(3, yp, ys)

    return (yp.reshape(bp, seq, d), ys.reshape(bs, 1, d), gm_v_sample, moba_k_prompt, moba_v_prompt,
            moba_k_sample, moba_v_sample, pool_prompt, pool_sample, wkv_p[None], wkv_s[None],
            shp.reshape(1, bp, d), shs[None], jnp.stack(conv_p), jnp.stack(conv_s))
```

```python
import functools
import math

import jax
import jax.numpy as jnp
from jax import lax
from jax.experimental import pallas as pl
from jax.experimental.pallas import tpu as pltpu

F32 = jnp.float32
BF16 = jnp.bfloat16
I32 = jnp.int32

LANES = 128
SUBLANES = 8
VMEM_LIMIT_BYTES = 56 * 2**20

RMS_EPS = 1e-6
GM_LN_EPS = 1e-5
GM_CHUNK = 128
GM_GROUPS = 8
MOBA_HEADS = 16
MOBA_BLOCK = 256
MOBA_TOPK = 3
REL_BUCKETS = 32
REL_MAX_DIST = 128
PAGE_SIZE = 128
POOL_WINDOWS = (2, 4, 8, 16)
POOL_CTX = max(POOL_WINDOWS) - 1
RWKV_HEAD_DIM = 64
RWKV_LNX_EPS = 64e-5
FFN_CONV = 3
NEG_INF = float("-inf")

assert MOBA_BLOCK >= REL_MAX_DIST


def _cparams(*sem):
    return pltpu.CompilerParams(dimension_semantics=sem, vmem_limit_bytes=VMEM_LIMIT_BYTES)


def _whole(shape):
    nd = len(shape)
    return pl.BlockSpec(shape, lambda *_: (0,) * nd, pipeline_mode=pl.Buffered(1))


def _rows(tm, width):
    return pl.BlockSpec((tm, width), lambda i: (i, 0))


def _rms(x, g):
    return x * lax.rsqrt(jnp.mean(x * x, axis=-1, keepdims=True) + RMS_EPS) * g


def _bdot(a, b):
    return jnp.dot(a.astype(BF16), b.astype(BF16), preferred_element_type=F32)


def _nt_dot(a, b):
    return lax.dot_general(a.astype(BF16), b.astype(BF16), (((1,), (1,)), ((), ())),
                           preferred_element_type=F32)


def _split(x):
    hi = x.astype(BF16)
    lo = (x - hi.astype(F32)).astype(BF16)
    return hi, lo


def _split_dot(x, m):
    hi, lo = _split(x)
    return (jnp.dot(hi, m, preferred_element_type=F32)
            + jnp.dot(lo, m, preferred_element_type=F32))


def _same_head(n, head_dim):
    shift = int(math.log2(head_dim))
    r = lax.broadcasted_iota(I32, (n, n), 0) >> shift
    c = lax.broadcasted_iota(I32, (n, n), 1) >> shift
    return jnp.where(r == c, 1.0, 0.0).astype(BF16)


def _seg_sum(x, head_dim):
    g = _same_head(LANES, head_dim)
    parts = [_split_dot(x[:, i:i + LANES], g) for i in range(0, x.shape[1], LANES)]
    return jnp.concatenate(parts, axis=1)


def _gelu(x):
    return 0.5 * x * (1.0 + jnp.tanh(0.7978845608028654 * (x + 0.044715 * x * x * x)))


def _sigmoid(x):
    return 1.0 / (1.0 + jnp.exp(-x))


def _softplus(x):
    return jnp.maximum(x, 0.0) + jnp.log(1.0 + jnp.exp(-jnp.abs(x)))


def _top_mask(gate, idx, axis, n_valid):
    cur = gate
    sel = jnp.zeros(gate.shape, F32)
    for _ in range(MOBA_TOPK):
        m = jnp.max(cur, axis=axis, keepdims=True)
        first = jnp.min(jnp.where(cur == m, idx, n_valid), axis=axis, keepdims=True)
        pick = (idx == first) & (m > NEG_INF)
        sel = jnp.where(pick, 1.0, sel)
        cur = jnp.where(pick, NEG_INF, cur)
    return sel


def _gmlp_kernel(x_ref, ng_ref, win_ref, lng_ref, lnb_ref, sa_ref, sb_ref, wout_ref, y_ref, aux_ref,
                 *, sample):
    x = x_ref[...]
    tm, width = x.shape[0], wout_ref.shape[0]
    h = _rms(x, ng_ref[...])
    z = _gelu(_bdot(h, win_ref[...]))
    u, v = z[:, :width], z[:, width:]
    mu = jnp.mean(v, axis=-1, keepdims=True)
    d = v - mu
    var = jnp.mean(d * d, axis=-1, keepdims=True)
    v = d * lax.rsqrt(var + GM_LN_EPS) * lng_ref[...] + lnb_ref[...]
    if sample:
        aux_ref[...] = v
        s = v * sa_ref[...] + sb_ref[...]
    else:
        gd = width // GM_GROUPS
        causal = (lax.broadcasted_iota(I32, (GM_CHUNK, GM_CHUNK), 0)
                  >= lax.broadcasted_iota(I32, (GM_CHUNK, GM_CHUNK), 1))
        for g in range(GM_GROUPS):
            wg = jnp.where(causal, sa_ref[g], 0.0).astype(BF16)
            cols = slice(g * gd, (g + 1) * gd)
            for c in range(tm // GM_CHUNK):
                rows = slice(c * GM_CHUNK, (c + 1) * GM_CHUNK)
                aux_ref[rows, cols] = (jnp.dot(wg, v[rows, cols].astype(BF16), preferred_element_type=F32)
                                       + sb_ref[:, cols])
        s = aux_ref[...]
    y_ref[...] = x + _bdot(u * s, wout_ref[...])


def _gmlp(x, ng, win, lng, lnb, sa, sb, wout, *, tm, sample):
    n, d = x.shape
    width = wout.shape[0]
    outs = [jax.ShapeDtypeStruct((n, d), F32)]
    out_specs = [_rows(tm, d)]
    scratch = []
    if sample:
        outs.append(jax.ShapeDtypeStruct((n, width), F32))
        out_specs.append(_rows(tm, width))
    else:
        scratch.append(pltpu.VMEM((tm, width), F32))
    return pl.pallas_call(
        functools.partial(_gmlp_kernel, sample=sample),
        out_shape=outs, grid=(n // tm,),
        in_specs=[_rows(tm, d), _whole(ng.shape), _whole(win.shape), _whole(lng.shape), _whole(lnb.shape),
                  _whole(sa.shape), _whole(sb.shape), _whole(wout.shape)],
        out_specs=out_specs, scratch_shapes=scratch,
        compiler_params=_cparams("parallel"), name="gmlp_sample" if sample else "gmlp_prompt",
    )(x, ng, win, lng, lnb, sa, sb, wout)


def _norm_linear_kernel(x_ref, ng_ref, w_ref, *o_refs):
    z = _bdot(_rms(x_ref[...], ng_ref[...]), w_ref[...])
    wd = z.shape[1] // len(o_refs)
    for i, o_ref in enumerate(o_refs):
        o_ref[...] = z[:, i * wd:(i + 1) * wd]


def _norm_linear(x, ng, w, n_out, *, tm, name):
    n, d = x.shape
    wd = w.shape[1] // n_out
    return pl.pallas_call(
        _norm_linear_kernel,
        out_shape=[jax.ShapeDtypeStruct((n, wd), F32)] * n_out, grid=(n // tm,),
        in_specs=[_rows(tm, d), _whole(ng.shape), _whole(w.shape)],
        out_specs=[_rows(tm, wd)] * n_out,
        compiler_params=_cparams("parallel"), name=name,
    )(x, ng, w)


def _qkv_prompt_kernel(x_ref, ng_ref, wq_ref, wkt_ref, wvt_ref, q_ref, kt_ref, vt_ref):
    h = _rms(x_ref[...], ng_ref[...]).astype(BF16)
    q_ref[...] = jnp.dot(h, wq_ref[...], preferred_element_type=F32)
    kt_ref[0] = _nt_dot(wkt_ref[...], h)
    vt_ref[0] = _nt_dot(wvt_ref[...], h)


def _qkv_prompt(x, ng, wq, wkt, wvt, *, tm, batch, seq):
    n, d = x.shape
    tps = seq // tm
    t_spec = pl.BlockSpec((1, d, tm), lambda i: (i // tps, 0, i % tps))
    return pl.pallas_call(
        _qkv_prompt_kernel,
        out_shape=[jax.ShapeDtypeStruct((n, d), F32)] + [jax.ShapeDtypeStruct((batch, d, seq), F32)] * 2,
        grid=(n // tm,),
        in_specs=[_rows(tm, d), _whole(ng.shape), _whole(wq.shape), _whole(wkt.shape), _whole(wvt.shape)],
        out_specs=[_rows(tm, d), t_spec, t_spec],
        compiler_params=_cparams("parallel"), name="moba_qkv_prompt",
    )(x, ng, wq, wkt, wvt)


def _linear_res_kernel(a_ref, w_ref, y_ref, o_ref):
    o_ref[...] = y_ref[...] + _bdot(a_ref[...], w_ref[...])


def _linear_res(a, w, y, *, tm, name):
    n, d = y.shape
    return pl.pallas_call(
        _linear_res_kernel, out_shape=jax.ShapeDtypeStruct((n, d), F32), grid=(n // tm,),
        in_specs=[_rows(tm, a.shape[1]), _whole(w.shape), _rows(tm, d)], out_specs=_rows(tm, d),
        compiler_params=_cparams("parallel"), name=name,
    )(a, w, y)


def _t5_bucket_table(rel):
    n = jnp.maximum(rel, 0)
    exact = REL_BUCKETS // 2
    nf = jnp.maximum(n, 1).astype(F32)
    large = exact + (jnp.log(nf / exact) / math.log(REL_MAX_DIST / exact)
                     * (REL_BUCKETS - exact)).astype(I32)
    return jnp.where(n < exact, n, jnp.minimum(large, REL_BUCKETS - 1)).astype(I32)


def _bias_tiles_kernel(rb_ref, bkt_ref, o_ref):
    h = pl.program_id(0)
    bkt = bkt_ref[...]
    acc = jnp.zeros(bkt.shape, F32)
    for b in range(REL_BUCKETS):
        acc = jnp.where(bkt == b, rb_ref[b, h], acc)
    o_ref[0] = acc


def _bias_tiles(rel_bias, bkt):
    heads = rel_bias.shape[1]
    return pl.pallas_call(
        _bias_tiles_kernel, out_shape=jax.ShapeDtypeStruct((heads,) + bkt.shape, F32), grid=(heads,),
        in_specs=[pl.BlockSpec(memory_space=pltpu.SMEM), _whole(bkt.shape)],
        out_specs=pl.BlockSpec((1,) + bkt.shape, lambda h: (h, 0, 0, 0)),
        compiler_params=_cparams("parallel"), name="moba_bias_tiles",
    )(rel_bias, bkt)


def _bias_rows_kernel(rb_ref, bkt_ref, o_ref):
    bkt = bkt_ref[...]
    acc = jnp.zeros(bkt.shape, F32)
    for b in range(REL_BUCKETS):
        acc = jnp.where(bkt == b, rb_ref[b:b + 1, :], acc)
    o_ref[...] = acc


def _bias_rows(rb_pad, bkt):
    return pl.pallas_call(
        _bias_rows_kernel, out_shape=jax.ShapeDtypeStruct(bkt.shape, F32), grid=(1,),
        in_specs=[_whole(rb_pad.shape), _whole(bkt.shape)], out_specs=_whole(bkt.shape),
        compiler_params=_cparams("arbitrary"), name="moba_bias_rows",
    )(rb_pad, bkt)


def _moba_prompt_kernel(rb_ref, q_ref, k_ref, v_ref, bias_ref, o_ref, kb_scr, vh_scr, kmean_scr, *, head_dim):
    pair, qb = pl.program_id(1), pl.program_id(2)
    blk = MOBA_BLOCK
    seq = k_ref.shape[2]
    n_blocks = seq // blk
    assert LANES == 2 * head_dim
    shift = int(math.log2(head_dim))
    scale = head_dim ** -0.5
    assert shift % 2 == 0 and n_blocks <= kmean_scr.shape[0]

    @pl.when(qb == 0)
    def _():
        kt, vt = k_ref[0], v_ref[0]
        second = (lax.broadcasted_iota(I32, (LANES, seq), 0) >> shift) == 1
        kb_scr[...] = kt.astype(BF16)
        vh_scr[0] = jnp.where(second, 0.0, vt).astype(BF16)
        vh_scr[1] = jnp.where(second, vt, 0.0).astype(BF16)
        rows = kmean_scr.shape[0]
        in_block = jnp.where((lax.broadcasted_iota(I32, (rows, seq), 1) >> int(math.log2(blk)))
                             == lax.broadcasted_iota(I32, (rows, seq), 0), 1.0, 0.0).astype(BF16)
        k_hi, k_lo = _split(kt)
        nt = lambda a, b: lax.dot_general(a, b, (((1,), (1,)), ((), ())), preferred_element_type=F32)
        kmean_scr[...] = (nt(in_block, k_hi) + nt(in_block, k_lo)) * (1.0 / blk)

    q = q_ref[...]
    second = (lax.broadcasted_iota(I32, (blk, LANES), 1) >> shift) == 1
    q2 = jnp.concatenate([jnp.where(second, 0.0, q), jnp.where(second, q, 0.0)], axis=0)
    q_hi, q_lo = _split(q2)
    km_hi, km_lo = _split(kmean_scr[...])
    nt = lambda a, b: lax.dot_general(a, b, (((1,), (1,)), ((), ())), preferred_element_type=F32)
    gate = nt(km_hi, q_hi) + nt(km_lo, q_hi) + nt(km_hi, q_lo)
    block_id = lax.broadcasted_iota(I32, gate.shape, 0)
    sel = _top_mask(jnp.where(block_id < qb, gate, NEG_INF), block_id, 0, gate.shape[0]).T
    q_hi = (q2 * scale).astype(BF16)
    causal = (lax.broadcasted_iota(I32, (blk, blk), 1) <= lax.broadcasted_iota(I32, (blk, blk), 0))
    far_bias = [rb_ref[REL_BUCKETS - 1, pair * 2 + hh] for hh in range(2)]

    def attend(own):
        keys = (own + 1) * blk
        s = jnp.dot(q_hi, kb_scr[:, 0:keys], preferred_element_type=F32)
        head_rows = []
        for hh in range(2):
            rows = slice(hh * blk, (hh + 1) * blk)
            tiles = []
            for n in range(own + 1):
                t = s[rows, n * blk:(n + 1) * blk]
                if n == own:
                    t = jnp.where(causal, t + bias_ref[hh, 0], NEG_INF)
                else:
                    bias = bias_ref[hh, 1] if n == own - 1 else far_bias[hh]
                    t = jnp.where(sel[rows, n:n + 1] > 0.0, t + bias, NEG_INF)
                tiles.append(t)
            head_rows.append(jnp.concatenate(tiles, axis=1))
        s = jnp.concatenate(head_rows, axis=0)
        p = jnp.exp(s - jnp.max(s, axis=-1, keepdims=True))
        l = jnp.sum(p, axis=-1, keepdims=True)
        p_pair = jnp.concatenate([p[:blk], p[blk:]], axis=1).astype(BF16)
        v_pair = jnp.concatenate([vh_scr[0, :, 0:keys], vh_scr[1, :, 0:keys]], axis=1)
        o = nt(p_pair, v_pair)
        o_ref[...] = o / jnp.where(second, l[blk:], l[:blk])

    for own in range(n_blocks):
        pl.when(qb == own)(functools.partial(attend, own))


def _moba_prompt(rel_bias, q, k, v, bias_tiles, *, batch, seq, head_dim):
    n, d = q.shape
    blk = MOBA_BLOCK
    nq = seq // blk
    hpt = LANES // head_dim
    return pl.pallas_call(
        functools.partial(_moba_prompt_kernel, head_dim=head_dim),
        out_shape=jax.ShapeDtypeStruct((n, d), F32), grid=(batch, d // LANES, nq),
        in_specs=[pl.BlockSpec(memory_space=pltpu.SMEM),
                  pl.BlockSpec((blk, LANES), lambda b, p, i: (b * nq + i, p)),
                  pl.BlockSpec((1, LANES, seq), lambda b, p, i: (b, p, 0)),
                  pl.BlockSpec((1, LANES, seq), lambda b, p, i: (b, p, 0)),
                  pl.BlockSpec((hpt, 2, blk, blk), lambda b, p, i: (p, 0, 0, 0))],
        out_specs=pl.BlockSpec((blk, LANES), lambda b, p, i: (b * nq + i, p)),
        scratch_shapes=[pltpu.VMEM((LANES, seq), BF16), pltpu.VMEM((hpt, LANES, seq), BF16),
                        pltpu.VMEM((2 * SUBLANES, LANES), F32)],
        compiler_params=_cparams("parallel", "parallel", "arbitrary"), name="moba_prompt_attn",
    )(rel_bias, q, k, v, bias_tiles)


def _moba_sample_kernel(pt_ref, q_ref, kn_ref, vn_ref, *rest, n_pages, head_dim):
    del pt_ref
    k_refs, v_refs = rest[:n_pages], rest[n_pages:2 * n_pages]
    bias_ref, rb0_ref, o_ref = rest[2 * n_pages:]
    d = q_ref.shape[2]
    heads = d // head_dim
    shift = int(math.log2(head_dim))
    per_block = MOBA_BLOCK // PAGE_SIZE
    blocks = range(n_pages // per_block)
    scale = head_dim ** -0.5
    head_lanes = jnp.where((lax.broadcasted_iota(I32, (heads, d), 1) >> shift)
                           == lax.broadcasted_iota(I32, (heads, d), 0), 1.0, 0.0)
    q = head_lanes * q_ref[0]
    q_hi, q_lo = _split(q)
    q2 = jnp.concatenate([q_hi, q_lo], axis=0)
    block_of = lambda refs, n: jnp.concatenate(
        [refs[per_block * n + j][0, 0].reshape(d, PAGE_SIZE) for j in range(per_block)], axis=1).astype(BF16)
    raw = [jnp.dot(q2, block_of(k_refs, n), preferred_element_type=F32) for n in blocks]
    raw = [r[:heads] + r[heads:] for r in raw]
    block_id = lax.broadcasted_iota(I32, (heads, LANES), 1)
    gate = jnp.full((heads, LANES), NEG_INF, F32)
    for n in blocks:
        gate = jnp.where(block_id == n, jnp.sum(raw[n], axis=-1, keepdims=True), gate)
    sel = _top_mask(gate, block_id, -1, LANES)
    s_new = jnp.sum(q * kn_ref[0], axis=-1, keepdims=True) * scale + rb0_ref[:, 0:1]
    s = [raw[n] * scale + bias_ref[n] for n in blocks]
    m = s_new
    for n in blocks:
        m = jnp.maximum(m, jnp.where(sel[:, n:n + 1] > 0.0, jnp.max(s[n], axis=-1, keepdims=True), NEG_INF))
    p = [jnp.exp(jnp.where(sel[:, n:n + 1] > 0.0, s[n] - m, NEG_INF)) for n in blocks]
    p_new = jnp.exp(s_new - m)
    den = p_new
    for n in blocks:
        den = den + jnp.sum(p[n], axis=-1, keepdims=True)
    p_all = jnp.concatenate(p, axis=1).astype(BF16)
    v_all = jnp.concatenate([block_of(v_refs, n) for n in blocks], axis=1)
    acc = lax.dot_general(p_all, v_all, (((1,), (1,)), ((), ())), preferred_element_type=F32)
    out = (acc + p_new * vn_ref[0]) * head_lanes / den
    o_ref[0] = jnp.sum(out, axis=0, keepdims=True)


def _moba_sample(page_table, q, k_new, v_new, cache_k, cache_v, layer, bias_rows, rel_bias, *, head_dim):
    nb, d = q.shape
    n_pages = page_table.shape[1]
    n_blocks = n_pages * PAGE_SIZE // MOBA_BLOCK
    heads = d // head_dim
    cache_kt = jnp.transpose(cache_k, (0, 1, 3, 4, 2))
    cache_vt = jnp.transpose(cache_v, (0, 1, 3, 4, 2))
    row = lambda x: x.reshape(nb, 1, d)
    vec = pl.BlockSpec((1, 1, d), lambda s, pt: (s, 0, 0))
    page = lambda j: pl.BlockSpec((1, 1, heads, head_dim, PAGE_SIZE), lambda s, pt: (layer, pt[s, j], 0, 0, 0))
    const = lambda shape: pl.BlockSpec(shape, lambda s, pt: (0,) * len(shape))
    bias = bias_rows[:, :heads].reshape(n_blocks, MOBA_BLOCK, heads).transpose(0, 2, 1)
    rb0 = jnp.broadcast_to(rel_bias[0][:, None], (heads, LANES))
    pages = [page(j) for j in range(n_pages)]
    out = pl.pallas_call(
        functools.partial(_moba_sample_kernel, n_pages=n_pages, head_dim=head_dim),
        out_shape=jax.ShapeDtypeStruct((nb, 1, d), F32),
        grid_spec=pltpu.PrefetchScalarGridSpec(
            num_scalar_prefetch=1, grid=(nb,),
            in_specs=[vec, vec, vec, *pages, *pages, const(bias.shape), const(rb0.shape)],
            out_specs=vec),
        compiler_params=_cparams("parallel"), name="moba_sample_attn",
    )(page_table, row(q), row(k_new), row(v_new), *([cache_kt] * n_pages), *([cache_vt] * n_pages), bias, rb0)
    return out.reshape(nb, d)


POOL_PAD = 32


def _pool_prompt_kernel(y_ref, ng_ref, w_ref, sc_ref, yout_ref, st_ref, b0, b1, b2, b3, b4, *, tps):
    i = pl.program_id(0)
    y = y_ref[...]
    tm, d = y.shape
    gd = d // len(POOL_WINDOWS)
    h = _rms(y, ng_ref[...])
    half = POOL_PAD // 2

    @pl.when(i == 0)
    def _():
        for b in (b0, b1, b2, b3, b4):
            b[0:half, :] = jnp.zeros((half, b.shape[1]), F32)

    @pl.when(i % tps == 0)
    def _():
        b0[half:POOL_PAD, :] = jnp.zeros((half, d), F32)

    b0[POOL_PAD:, :] = h
    stages = (b0, b1, b2, b3, b4)
    for k in range(1, len(stages)):
        src, dst = stages[k - 1], stages[k]
        back = 2 ** (k - 1)
        off = src.shape[1] - dst.shape[1]
        dst[SUBLANES:, :] = (src[SUBLANES:, off:] + src[SUBLANES - back:tm + POOL_PAD - back, off:])
    pos = (i % tps) * tm + lax.broadcasted_iota(I32, (tm, 1), 0) + 1
    for g, win in enumerate(POOL_WINDOWS):
        cols = slice(g * gd, (g + 1) * gd)
        wsum = stages[g + 1][POOL_PAD:, 0:gd]
        cnt = jnp.minimum(pos, win).astype(F32)
        mixed = _bdot(wsum / cnt - h[:, cols], w_ref[g])
        yout_ref[:, cols] = y[:, cols] + mixed * sc_ref[:, cols]
    st_ref[0] = b0[tm + POOL_PAD - POOL_CTX:, :]
    b0[half:POOL_PAD, :] = b0[tm + half:, :]


def _pool_prompt(y, ng, w, sc, *, tm, seq):
    n, d = y.shape
    gd = d // len(POOL_WINDOWS)
    tps = seq // tm
    rows = tm + POOL_PAD
    return pl.pallas_call(
        functools.partial(_pool_prompt_kernel, tps=tps),
        out_shape=[jax.ShapeDtypeStruct((n, d), F32), jax.ShapeDtypeStruct((n // seq, POOL_CTX, d), F32)],
        grid=(n // tm,),
        in_specs=[_rows(tm, d), _whole(ng.shape), _whole(w.shape), _whole(sc.shape)],
        out_specs=[_rows(tm, d), pl.BlockSpec((1, POOL_CTX, d), lambda i: (i // tps, 0, 0))],
        scratch_shapes=[pltpu.VMEM((rows, d - k * gd), F32) for k in (0, 0, 1, 2, 3)],
        compiler_params=_cparams("arbitrary"), name="pool_prompt",
    )(y, ng, w, sc)


def _pool_sample_kernel(y_ref, ng_ref, prev_ref, w_ref, sc_ref, yout_ref, h_ref):
    y = y_ref[...]
    d = y.shape[1]
    gd = d // len(POOL_WINDOWS)
    h = _rms(y, ng_ref[...])
    h_ref[...] = h
    for g, win in enumerate(POOL_WINDOWS):
        cols = slice(g * gd, (g + 1) * gd)
        wsum = h[:, cols]
        for back in range(1, win):
            wsum = wsum + prev_ref[POOL_CTX - back, :, cols]
        mixed = _bdot(wsum / float(win) - h[:, cols], w_ref[g])
        yout_ref[:, cols] = y[:, cols] + mixed * sc_ref[:, cols]


def _pool_sample(y, ng, prev_t, w, sc):
    n, d = y.shape
    return pl.pallas_call(
        _pool_sample_kernel, out_shape=[jax.ShapeDtypeStruct((n, d), F32)] * 2, grid=(1,),
        in_specs=[_whole(y.shape), _whole(ng.shape), _whole(prev_t.shape), _whole(w.shape), _whole(sc.shape)],
        out_specs=[_whole(y.shape)] * 2,
        compiler_params=_cparams("arbitrary"), name="pool_sample",
    )(y, ng, prev_t, w, sc)


def _rwkv_proj_kernel(y_ref, ng_ref, mu_ref, wr_ref, wk_ref, wv_ref, w1_ref, w2_ref, a1_ref, a2_ref,
                      g1_ref, g2_ref, vec_ref, *rest, tps, sample):
    if sample:
        prev_ref, r_o, w_o, k_o, v_o, kk_o, kka_o, g_o, sh_o = rest
    else:
        r_o, w_o, k_o, v_o, kk_o, kka_o, g_o, sh_o, hs_scr = rest
    y = y_ref[...]
    tm = y.shape[0]
    h = _rms(y, ng_ref[...])
    if sample:
        h_prev = prev_ref[...]
        sh_o[...] = h
    else:
        i = pl.program_id(0)

        @pl.when(i % tps == 0)
        def _():
            hs_scr[0:SUBLANES, :] = jnp.zeros((SUBLANES, y.shape[1]), F32)

        hs_scr[SUBLANES:, :] = h
        h_prev = hs_scr[SUBLANES - 1:tm + SUBLANES - 1, :]
        hs_scr[0:SUBLANES, :] = h[tm - SUBLANES:, :]
        sh_o[0] = h[tm - 1:tm, :]
    xx = h_prev - h
    mix = lambda m: h + xx * mu_ref[m:m + 1, :]
    r = _bdot(mix(0), wr_ref[...])
    k = _bdot(mix(2), wk_ref[...])
    v = _bdot(mix(3), wv_ref[...])
    w_log = -_softplus(-(vec_ref[0:1, :] + _bdot(jnp.tanh(_bdot(mix(1), w1_ref[...])), w2_ref[...]))) - 0.5
    a = _sigmoid(vec_ref[1:2, :] + _bdot(_bdot(mix(4), a1_ref[...]), a2_ref[...]))
    g_o[...] = _bdot(_sigmoid(_bdot(mix(5), g1_ref[...])), g2_ref[...])
    kk = k * vec_ref[2:3, :]
    kk = kk / jnp.maximum(jnp.sqrt(_seg_sum(kk * kk, RWKV_HEAD_DIM)), 1e-12)
    r_o[...] = r
    w_o[...] = -jnp.exp(w_log)
    k_o[...] = k * (1.0 + (a - 1.0) * vec_ref[3:4, :])
    v_o[...] = v
    kk_o[...] = kk
    kka_o[...] = kk * a


def _rwkv_proj(y, ng, mu, mats, vec, prev, *, tm, seq, sample):
    n, d = y.shape
    tps = max(seq // tm, 1)
    ins = [y, ng, mu, *mats, vec]
    in_specs = [_rows(tm, d), _whole(ng.shape), _whole(mu.shape), *[_whole(m.shape) for m in mats],
                _whole(vec.shape)]
    outs = [jax.ShapeDtypeStruct((n, d), F32)] * 7
    out_specs = [_rows(tm, d)] * 7
    scratch = []
    if sample:
        ins.append(prev)
        in_specs.append(_rows(tm, d))
        outs.append(jax.ShapeDtypeStruct((n, d), F32))
        out_specs.append(_rows(tm, d))
    else:
        outs.append(jax.ShapeDtypeStruct((n // seq, 1, d), F32))
        out_specs.append(pl.BlockSpec((1, 1, d), lambda i: (i // tps, 0, 0)))
        scratch.append(pltpu.VMEM((tm + SUBLANES, d), F32))
    return pl.pallas_call(
        functools.partial(_rwkv_proj_kernel, tps=tps, sample=sample),
        out_shape=outs, grid=(n // tm,), in_specs=in_specs, out_specs=out_specs, scratch_shapes=scratch,
        compiler_params=_cparams("arbitrary"), name="rwkv_proj_sample" if sample else "rwkv_proj_prompt",
    )(*ins)


RWKV_CHUNK = 128


def _rwkv_chunk_kernel(r_ref, lw_ref, k_ref, v_ref, kk_ref, kka_ref, o_ref, st_ref, t_scr):
    c = pl.program_id(1)
    ch, d = r_ref.shape
    hd = RWKV_HEAD_DIM
    shift = int(math.log2(hd))
    n_pairs = d // LANES

    @pl.when(c == 0)
    def _():
        t_scr[...] = jnp.zeros(t_scr.shape, F32)

    ri = lax.broadcasted_iota(I32, (ch, ch), 0)
    ci = lax.broadcasted_iota(I32, (ch, ch), 1)
    incl = ci <= ri
    strict = ci < ri
    incl2 = jnp.concatenate([incl, incl], axis=1)
    strict2 = jnp.concatenate([strict, strict], axis=1)
    same_head = (ri >> shift) == (ci >> shift)
    eye = ri == ci
    head1 = (lax.broadcasted_iota(I32, (ch, LANES), 1) >> shift) == 1
    head1_wide = jnp.concatenate([head1, head1], axis=1)

    def by_head(x, mask):
        return jnp.concatenate([jnp.where(mask, 0.0, x), jnp.where(mask, x, 0.0)], axis=0).astype(BF16)

    lw = lw_ref[...]
    tri = jnp.where(incl, 1.0, 0.0).astype(BF16)
    p1 = lw.astype(BF16)
    rem = lw - p1.astype(F32)
    p2 = rem.astype(BF16)
    p3 = (rem - p2.astype(F32)).astype(BF16)
    cl = (jnp.dot(tri, p1, preferred_element_type=F32) + jnp.dot(tri, p2, preferred_element_type=F32)
          + jnp.dot(tri, p3, preferred_element_type=F32))
    mid = cl[ch // 2 - 1:ch // 2, :]
    last = cl[ch - 1:ch, :]
    e_mid = jnp.exp(mid)
    e_neg = jnp.exp(mid - cl)
    e_tail = jnp.exp(last - cl)
    p_last = jnp.exp(last)
    kk, kka, k = kk_ref[...], kka_ref[...], k_ref[...]
    a_s = -kk * jnp.exp(cl - lw - mid)
    r_s = r_ref[...] * jnp.exp(cl - mid)
    b_s = kka * e_neg
    k_s = k * e_neg
    b_t = kka * e_tail
    k_t = k * e_tail
    v = v_ref[...]
    zero = jnp.zeros((ch, ch), BF16)

    pairs = range(n_pairs)
    cols = [slice(p * LANES, (p + 1) * LANES) for p in pairs]
    mm = lambda a, b: jnp.dot(a, b, preferred_element_type=F32)
    lane_cat = lambda a, b: jnp.concatenate([a, b], axis=1)
    a_p = [a_s[:, c] for c in cols]
    r_p = [r_s[:, c] for c in cols]
    v_p = [v[:, c] for c in cols]
    gram = []
    for p in pairs:
        lhs = jnp.concatenate([jnp.where(head1, 0.0, a_p[p]), jnp.where(head1, 0.0, r_p[p]),
                               jnp.where(head1, a_p[p], 0.0), jnp.where(head1, r_p[p], 0.0)], axis=0)
        rhs = lane_cat(b_s[:, cols[p]].T, k_s[:, cols[p]].T)
        gram.append(mm(lhs.astype(BF16), rhs.astype(BF16)))
    ab0 = [jnp.where(strict2, g_[0:ch], 0.0) for g_ in gram]
    rb0 = [jnp.where(incl2, g_[ch:2 * ch], 0.0) for g_ in gram]
    ab1 = [jnp.where(strict2, g_[2 * ch:3 * ch], 0.0) for g_ in gram]
    rb1 = [jnp.where(incl2, g_[3 * ch:], 0.0) for g_ in gram]
    v_heads = [by_head(v_p[p], head1) for p in pairs]
    x = [lane_cat(a_p[p], mm(lane_cat(ab0[p][:, ch:], ab1[p][:, ch:]).astype(BF16), v_heads[p])) for p in pairs]
    pw = [lane_cat(ab0[p][:, :ch], ab1[p][:, :ch]).astype(BF16) for p in pairs]
    n_sq = int(math.log2(ch))
    for it in range(n_sq):
        x = [x[p] + mm(pw[p], by_head(x[p], head1_wide)) for p in pairs]
        if it < n_sq - 1:
            pw = [mm(pw[p], jnp.concatenate([lane_cat(pw[p][:, :ch], zero), lane_cat(zero, pw[p][:, ch:])],
                                            axis=0)).astype(BF16) for p in pairs]
    x = [lane_cat(x[p][:, :ch] * e_mid[:, cols[p]], x[p][:, ch:]) for p in pairs]
    qo = [mm(lane_cat(rb0[p][:, :ch], rb1[p][:, :ch]).astype(BF16), by_head(x[p], head1_wide)) for p in pairs]
    o_intra = [qo[p][:, ch:] + mm(lane_cat(rb0[p][:, ch:], rb1[p][:, ch:]).astype(BF16), v_heads[p])
               for p in pairs]
    q = [r_p[p] * e_mid[:, cols[p]] + qo[p][:, :ch] for p in pairs]
    gh = [mm(b_t[:, cols[p]].T.astype(BF16), x[p].astype(BF16)) for p in pairs]
    g = [jnp.where(same_head, gh[p][:, :ch], 0.0) + jnp.where(eye, p_last[:, cols[p]], 0.0) for p in pairs]
    h = [jnp.where(same_head, gh[p][:, ch:] + _bdot(k_t[:, cols[p]].T, v_p[p]), 0.0) for p in pairs]
    for p in pairs:
        t_old = t_scr[p].astype(BF16)
        o_ref[:, cols[p]] = mm(q[p].astype(BF16), t_old) + o_intra[p]
        t_scr[p] = mm(g[p].astype(BF16), t_old) + h[p]

    @pl.when(c == pl.num_programs(1) - 1)
    def _():
        for p in range(n_pairs):
            s_pair = t_scr[p].T
            st_ref[0, p] = s_pair[:hd, :] + s_pair[hd:, :]


def _rwkv_chunked(r, lw, k, v, kk, kka, *, batch, seq):
    n, d = r.shape
    ch = RWKV_CHUNK
    assert ch == LANES and seq % ch == 0
    n_chunks = seq // ch
    n_pairs = d // LANES
    blk = pl.BlockSpec((ch, d), lambda b, c: (b * n_chunks + c, 0))
    return pl.pallas_call(
        _rwkv_chunk_kernel,
        out_shape=[jax.ShapeDtypeStruct((n, d), F32),
                   jax.ShapeDtypeStruct((batch, n_pairs, RWKV_HEAD_DIM, LANES), F32)],
        grid=(batch, n_chunks), in_specs=[blk] * 6,
        out_specs=[blk, pl.BlockSpec((1, n_pairs, RWKV_HEAD_DIM, LANES), lambda b, c: (b, 0, 0, 0))],
        scratch_shapes=[pltpu.VMEM((n_pairs, LANES, LANES), F32)],
        compiler_params=_cparams("parallel", "arbitrary"), name="rwkv_chunked",
    )(r, lw, k, v, kk, kka)


def _rwkv_step_kernel(r_ref, lw_ref, k_ref, v_ref, kk_ref, kka_ref, s0_ref, o_ref, st_ref):
    hd = s0_ref.shape[1]
    w = jnp.exp(lw_ref[...])
    kk, kka, k, r = kk_ref[...], kka_ref[...], k_ref[...], r_ref[...]

    def value_row(i, carry):
        s = s0_ref[0, i]
        s_kk = jnp.sum(s * kk, axis=0, keepdims=True)
        s = s * w - s_kk * kka + v_ref[pl.ds(i, 1), :] * k
        st_ref[0, i] = s
        o_ref[pl.ds(i, 1), :] = jnp.sum(s * r, axis=0, keepdims=True)
        return carry

    lax.fori_loop(0, hd, value_row, 0, unroll=8)


def _rwkv_step(r, lw, k, v, kk, kka, s0):
    d, n = r.shape
    heads, hd = s0.shape[0], s0.shape[1]
    vec = pl.BlockSpec((hd, n), lambda h: (h, 0))
    st = pl.BlockSpec((1, hd, hd, n), lambda h: (h, 0, 0, 0))
    return pl.pallas_call(
        _rwkv_step_kernel,
        out_shape=[jax.ShapeDtypeStruct((d, n), F32), jax.ShapeDtypeStruct(s0.shape, F32)],
        grid=(heads,), in_specs=[vec] * 6 + [st], out_specs=[vec, st],
        compiler_params=_cparams("parallel"), name="rwkv_step",
    )(r, lw, k, v, kk, kka, s0)


def _rwkv_out_kernel(o_ref, r_ref, k_ref, v_ref, g_ref, y_ref, vec_ref, wo_ref, yout_ref):
    hd = RWKV_HEAD_DIM
    o = o_ref[...]
    mu = _seg_sum(o, hd) * (1.0 / hd)
    dlt = o - mu
    var = _seg_sum(dlt * dlt, hd) * (1.0 / hd)
    o = dlt * lax.rsqrt(var + RWKV_LNX_EPS) * vec_ref[1:2, :] + vec_ref[2:3, :]
    o = o + _seg_sum(r_ref[...] * k_ref[...] * vec_ref[0:1, :], hd) * v_ref[...]
    yout_ref[...] = y_ref[...] + _bdot(o * g_ref[...], wo_ref[...])


def _rwkv_out(o, r, k, v, g, y, vec, wo, *, tm, name):
    n, d = y.shape
    return pl.pallas_call(
        _rwkv_out_kernel, out_shape=jax.ShapeDtypeStruct((n, d), F32), grid=(n // tm,),
        in_specs=[_rows(tm, d)] * 6 + [_whole(vec.shape), _whole(wo.shape)], out_specs=_rows(tm, d),
        compiler_params=_cparams("parallel"), name=name,
    )(o, r, k, v, g, y, vec, wo)


FFN_CHUNK = 256


def _ffn_act(gate, g1, g2, up, cw, cb):
    return _gelu(cw[0:1] * g2 + cw[1:2] * g1 + cw[2:3] * gate + cb) * up


def _ffn_prompt_kernel(y_ref, ng_ref, win_ref, cw_ref, cb_ref, wout_ref, *rest, tps, final):
    if final:
        fg_ref, yout_ref, st_ref, yfin_ref, h_scr, gs_scr, act_scr, carry_scr = rest
    else:
        yout_ref, st_ref, h_scr, gs_scr, act_scr, carry_scr = rest
    i = pl.program_id(0)
    y = y_ref[...]
    tm = y.shape[0]
    hid = wout_ref.shape[0]
    tf = FFN_CHUNK
    h_scr[...] = _rms(y, ng_ref[...]).astype(BF16)

    @pl.when(i % tps == 0)
    def _():
        carry_scr[...] = jnp.zeros(carry_scr.shape, F32)

    for j in range(hid // tf):
        cols = slice(j * tf, (j + 1) * tf)
        hb = h_scr[...]
        gs_scr[0:SUBLANES, :] = carry_scr[:, cols]
        gs_scr[SUBLANES:, :] = jnp.dot(hb, win_ref[:, cols], preferred_element_type=F32)
        up = jnp.dot(hb, win_ref[:, hid + j * tf:hid + (j + 1) * tf], preferred_element_type=F32)
        gate = gs_scr[SUBLANES:, :]
        carry_scr[:, cols] = gate[tm - SUBLANES:, :]
        st_ref[0, :, cols] = gate[tm - (FFN_CONV - 1):, :]
        act = _ffn_act(gate, gs_scr[SUBLANES - 1:tm + SUBLANES - 1, :], gs_scr[SUBLANES - 2:tm + SUBLANES - 2, :],
                       up, cw_ref[:, cols], cb_ref[:, cols])
        act_scr[:, cols] = act.astype(BF16)
    y_new = y + jnp.dot(act_scr[...], wout_ref[...], preferred_element_type=F32)
    yout_ref[...] = y_new
    if final:
        yfin_ref[...] = _rms(y_new, fg_ref[...])


def _ffn_prompt(y, ng, win, cw, cb, wout, fg, *, tm, seq):
    n, d = y.shape
    hid = wout.shape[0]
    tps = seq // tm
    final = fg is not None
    ins = [y, ng, win, cw, cb, wout]
    in_specs = [_rows(tm, d)] + [_whole(a.shape) for a in ins[1:]]
    outs = [jax.ShapeDtypeStruct((n, d), F32), jax.ShapeDtypeStruct((n // seq, FFN_CONV - 1, hid), F32)]
    out_specs = [_rows(tm, d), pl.BlockSpec((1, FFN_CONV - 1, hid), lambda i: (i // tps, 0, 0))]
    if final:
        ins.append(fg)
        in_specs.append(_whole(fg.shape))
        outs.append(jax.ShapeDtypeStruct((n, d), F32))
        out_specs.append(_rows(tm, d))
    return pl.pallas_call(
        functools.partial(_ffn_prompt_kernel, tps=tps, final=final),
        out_shape=outs, grid=(n // tm,), in_specs=in_specs, out_specs=out_specs,
        scratch_shapes=[pltpu.VMEM((tm, d), BF16), pltpu.VMEM((tm + SUBLANES, FFN_CHUNK), F32),
                        pltpu.VMEM((tm, hid), BF16), pltpu.VMEM((SUBLANES, hid), F32)],
        compiler_params=_cparams("arbitrary"), name="ffn_prompt",
    )(*ins)


def _ffn_sample_kernel(y_ref, ng_ref, wg_ref, wu_ref, cw_ref, cb_ref, p2_ref, p1_ref, wout_ref, *rest, final):
    if final:
        fg_ref, yout_ref, gate_ref, yfin_ref, h_scr, acc_scr = rest
    else:
        yout_ref, gate_ref, h_scr, acc_scr = rest
    j = pl.program_id(0)

    @pl.when(j == 0)
    def _():
        h_scr[...] = _rms(y_ref[...], ng_ref[...]).astype(BF16)
        acc_scr[...] = jnp.zeros(acc_scr.shape, F32)

    hb = h_scr[...]
    gate = jnp.dot(hb, wg_ref[...], preferred_element_type=F32)
    up = jnp.dot(hb, wu_ref[...], preferred_element_type=F32)
    gate_ref[...] = gate
    act = _ffn_act(gate, p1_ref[...], p2_ref[...], up, cw_ref[...], cb_ref[...])
    acc_scr[...] += _bdot(act, wout_ref[...])

    @pl.when(j == pl.num_programs(0) - 1)
    def _():
        y_new = y_ref[...] + acc_scr[...]
        yout_ref[...] = y_new
        if final:
            yfin_ref[...] = _rms(y_new, fg_ref[...])


def _ffn_sample(y, ng, win, cw, cb, p2, p1, wout, fg):
    n, d = y.shape
    hid = wout.shape[0]
    tf = FFN_CHUNK
    nf = hid // tf
    final = fg is not None
    keep = lambda shape: pl.BlockSpec(shape, lambda j: (0,) * len(shape))
    chunk = lambda rows: pl.BlockSpec((rows, tf), lambda j: (0, j))
    ins = [y, ng, win, win, cw, cb, p2, p1, wout]
    in_specs = [keep((n, d)), keep(ng.shape), chunk(d), pl.BlockSpec((d, tf), lambda j: (0, nf + j)),
                chunk(FFN_CONV), chunk(1), chunk(n), chunk(n), pl.BlockSpec((tf, d), lambda j: (j, 0))]
    outs = [jax.ShapeDtypeStruct((n, d), F32), jax.ShapeDtypeStruct((n, hid), F32)]
    out_specs = [keep((n, d)), chunk(n)]
    if final:
        ins.append(fg)
        in_specs.append(keep(fg.shape))
        outs.append(jax.ShapeDtypeStruct((n, d), F32))
        out_specs.append(keep((n, d)))
    return pl.pallas_call(
        functools.partial(_ffn_sample_kernel, final=final),
        out_shape=outs, grid=(nf,), in_specs=in_specs, out_specs=out_specs,
        scratch_shapes=[pltpu.VMEM((n, d), BF16), pltpu.VMEM((n, d), F32)],
        compiler_params=_cparams("arbitrary"), name="ffn_sample",
    )(*ins)


def kernel(x_prompt, x_sample, cache_moba_k, cache_moba_v, state_pool, state_rwkv_wkv, state_rwkv_shift, state_ffn_conv, page_table, norm_mix_g, norm_ffn_g, norm_final_g, rel_bias, gm_w_in, gm_ln_g, gm_ln_b, gm_w_s, gm_b_s, gm_w_out, moba_w_qkv, moba_w_o, pool_w, pool_scale, rwkv_mu, rwkv_w_r, rwkv_w_k, rwkv_w_v, rwkv_w_o, rwkv_w0, rwkv_w1, rwkv_w2, rwkv_a0, rwkv_a1, rwkv_a2, rwkv_g1, rwkv_g2, rwkv_k_k, rwkv_k_a, rwkv_r_k, rwkv_lnx_g, rwkv_lnx_b, ffn_w_in, ffn_conv_w, ffn_conv_b, ffn_w_out):
    bp, seq, d = x_prompt.shape
    bs = x_sample.shape[0]
    depth = norm_mix_g.shape[0]
    assert x_sample.shape[1] == 1 and depth == 4
    past_len = page_table.shape[1] * PAGE_SIZE
    assert seq % MOBA_BLOCK == 0 and past_len % MOBA_BLOCK == 0 and past_len % GM_CHUNK == 0
    row = lambda vct: vct.reshape(1, -1)
    bf = lambda m: m.astype(BF16)
    yp = x_prompt.reshape(bp * seq, d)
    ys = x_sample.reshape(bs, d)
    tm = 512
    conv_p, conv_s = [], []

    def ffn(i, yp, ys):
        last = i == depth - 1
        win, wout = bf(ffn_w_in[i]), bf(ffn_w_out[i])
        cw, cb, ng = ffn_conv_w[i], row(ffn_conv_b[i]), row(norm_ffn_g[i])
        fg = row(norm_final_g) if last else None
        res_p = _ffn_prompt(yp, ng, win, cw, cb, wout, fg, tm=tm, seq=seq)
        st = state_ffn_conv[i]
        res_s = _ffn_sample(ys, ng, win, cw, cb, st[:, 0], st[:, 1], wout, fg)
        conv_p.append(res_p[1])
        conv_s.append(jnp.stack([st[:, 1], res_s[1]], axis=1))
        if last:
            return res_p[2], res_s[2]
        return res_p[0], res_s[0]

    ng = row(norm_mix_g[0])
    width = gm_w_out.shape[1]
    gd = width // GM_GROUPS
    gm_in, gm_out = bf(gm_w_in[0]), bf(gm_w_out[0])
    lng, lnb = row(gm_ln_g[0]), row(gm_ln_b[0])
    sb_prompt = jnp.repeat(gm_b_s[0].T, gd, axis=1)
    (yp,) = _gmlp(yp, ng, gm_in, lng, lnb, gm_w_s[0], sb_prompt, gm_out, tm=256, sample=False)
    sa_first = row(jnp.repeat(gm_w_s[0][:, 0, 0], gd))
    sb_first = row(jnp.repeat(gm_b_s[0][:, 0], gd))
    ys, gm_v = _gmlp(ys, ng, gm_in, lng, lnb, sa_first, sb_first, gm_out, tm=bs, sample=True)
    gm_v_sample = gm_v.reshape(1, bs, 1, width)
    yp, ys = ffn(0, yp, ys)

    ng = row(norm_mix_g[1])
    w_qkv, w_o = bf(moba_w_qkv[0]), bf(moba_w_o[0])
    heads = MOBA_HEADS
    hd = d // heads
    qp, kp_t, vp_t = _qkv_prompt(yp, ng, w_qkv[:, :d], w_qkv[:, d:2 * d].T, w_qkv[:, 2 * d:].T,
                                 tm=tm, batch=bp, seq=seq)
    qs, ks, vs = _norm_linear(ys, ng, w_qkv, 3, tm=bs, name="moba_qkv_sample")
    blk = MOBA_BLOCK
    qi = jnp.arange(blk, dtype=I32)[:, None]
    ki = jnp.arange(blk, dtype=I32)[None, :]
    bkt_tiles = _t5_bucket_table(jnp.stack([qi - ki, blk + qi - ki]))
    bias_tiles = _bias_tiles(rel_bias, bkt_tiles)
    op = _moba_prompt(rel_bias, qp, kp_t, vp_t, bias_tiles, batch=bp, seq=seq, head_dim=hd)
    bkt_rows = _t5_bucket_table(past_len - jnp.arange(past_len, dtype=I32))
    rb_pad = jnp.pad(rel_bias, ((0, 0), (0, LANES - heads)))
    bias_rows = _bias_rows(rb_pad, jnp.broadcast_to(bkt_rows[:, None], (past_len, LANES)))
    os_ = _moba_sample(page_table, qs, ks, vs, cache_moba_k, cache_moba_v, 0, bias_rows, rel_bias, head_dim=hd)
    yp = _linear_res(op, w_o, yp, tm=tm, name="moba_out_prompt")
    ys = _linear_res(os_, w_o, ys, tm=bs, name="moba_out_sample")
    moba_k_prompt = kp_t.reshape(1, bp, heads, hd, seq).transpose(0, 1, 4, 2, 3)
    moba_v_prompt = vp_t.reshape(1, bp, heads, hd, seq).transpose(0, 1, 4, 2, 3)
    moba_k_sample = ks.reshape(1, bs, 1, heads, hd)
    moba_v_sample = vs.reshape(1, bs, 1, heads, hd)
    yp, ys = ffn(1, yp, ys)

    ng = row(norm_mix_g[2])
    pw, psc = bf(pool_w[0]), row(pool_scale[0])
    yp, pool_p = _pool_prompt(yp, ng, pw, psc, tm=tm, seq=seq)
    ys, hs = _pool_sample(ys, ng, jnp.swapaxes(state_pool[0], 0, 1), pw, psc)
    pool_prompt = pool_p[None]
    pool_sample = jnp.concatenate([state_pool[0][:, 1:], hs[:, None]], axis=1)[None]
    yp, ys = ffn(2, yp, ys)

    ng = row(norm_mix_g[3])
    mats = [bf(m[0]) for m in (rwkv_w_r, rwkv_w_k, rwkv_w_v, rwkv_w1, rwkv_w2, rwkv_a1, rwkv_a2,
                               rwkv_g1, rwkv_g2)]
    vec_in = jnp.stack([rwkv_w0[0], rwkv_a0[0], rwkv_k_k[0], rwkv_k_a[0]])
    vec_out = jnp.stack([rwkv_r_k[0].reshape(-1), rwkv_lnx_g[0], rwkv_lnx_b[0]])
    rh = d // RWKV_HEAD_DIM
    *seqs_p, gp, shp = _rwkv_proj(yp, ng, rwkv_mu[0], mats, vec_in, None, tm=256, seq=seq, sample=False)
    *seqs_s, gs, shs = _rwkv_proj(ys, ng, rwkv_mu[0], mats, vec_in, state_rwkv_shift[0], tm=bs, seq=1,
                                  sample=True)
    o_p, st_p = _rwkv_chunked(*seqs_p, batch=bp, seq=seq)
    hd_r = RWKV_HEAD_DIM
    wkv_p = st_p.reshape(bp, rh // 2, hd_r, 2, hd_r).transpose(0, 1, 3, 2, 4).reshape(bp, rh, hd_r, hd_r)
    o_s_t, wkv_s_t = _rwkv_step(*[a.T for a in seqs_s], jnp.transpose(state_rwkv_wkv[0], (1, 2, 3, 0)))
    o_s, wkv_s = o_s_t.T, jnp.transpose(wkv_s_t, (3, 0, 1, 2))
    w_o = bf(rwkv_w_o[0])
    r_p, _, k_p, v_p = seqs_p[:4]
    r_s, _, k_s, v_s = seqs_s[:4]
    yp = _rwkv_out(o_p, r_p, k_p, v_p, gp, yp, vec_out, w_o, tm=tm, name="rwkv_out_prompt")
    ys = _rwkv_out(o_s.reshape(bs, d), r_s, k_s, v_s, gs, ys, vec_out, w_o, tm=bs, name="rwkv_out_sample")
    yp, ys = ffn(3, yp, ys)

    return (yp.reshape(bp, seq, d), ys.reshape(bs, 1, d), gm_v_sample, moba_k_prompt, moba_v_prompt,
            moba_k_sample, moba_v_sample, pool_prompt, pool_sample, wkv_p[None], wkv_s[None],
            shp.reshape(1, bp, d), shs[None], jnp.stack(conv_p), jnp.stack(conv_s))
```

```python
import functools
import math

import jax
import jax.numpy as jnp
from jax import lax
from jax.experimental import pallas as pl
from jax.experimental.pallas import tpu as pltpu

F32 = jnp.float32
BF16 = jnp.bfloat16
I32 = jnp.int32

LANES = 128
SUBLANES = 8
VMEM_LIMIT_BYTES = 56 * 2**20

RMS_EPS = 1e-6
GM_LN_EPS = 1e-5
GM_CHUNK = 128
GM_GROUPS = 8
MOBA_HEADS = 16
MOBA_BLOCK = 256
MOBA_TOPK = 3
REL_BUCKETS = 32
REL_MAX_DIST = 128
PAGE_SIZE = 128
POOL_WINDOWS = (2, 4, 8, 16)
POOL_CTX = max(POOL_WINDOWS) - 1
RWKV_HEAD_DIM = 64
RWKV_LNX_EPS = 64e-5
FFN_CONV = 3
NEG_INF = float("-inf")

assert MOBA_BLOCK >= REL_MAX_DIST


def _cparams(*sem):
    return pltpu.CompilerParams(dimension_semantics=sem, vmem_limit_bytes=VMEM_LIMIT_BYTES)


def _whole(shape):
    nd = len(shape)
    return pl.BlockSpec(shape, lambda *_: (0,) * nd, pipeline_mode=pl.Buffered(1))


def _rows(tm, width):
    return pl.BlockSpec((tm, width), lambda i: (i, 0))


def _rms(x, g):
    return x * lax.rsqrt(jnp.mean(x * x, axis=-1, keepdims=True) + RMS_EPS) * g


def _bdot(a, b):
    return jnp.dot(a.astype(BF16), b.astype(BF16), preferred_element_type=F32)


def _nt_dot(a, b):
    return lax.dot_general(a.astype(BF16), b.astype(BF16), (((1,), (1,)), ((), ())),
                           preferred_element_type=F32)


def _split(x):
    hi = x.astype(BF16)
    lo = (x - hi.astype(F32)).astype(BF16)
    return hi, lo


def _split_dot(x, m):
    hi, lo = _split(x)
    return (jnp.dot(hi, m, preferred_element_type=F32)
            + jnp.dot(lo, m, preferred_element_type=F32))


def _same_head(n, head_dim):
    shift = int(math.log2(head_dim))
    r = lax.broadcasted_iota(I32, (n, n), 0) >> shift
    c = lax.broadcasted_iota(I32, (n, n), 1) >> shift
    return jnp.where(r == c, 1.0, 0.0).astype(BF16)


def _seg_sum(x, head_dim):
    g = _same_head(LANES, head_dim)
    parts = [_split_dot(x[:, i:i + LANES], g) for i in range(0, x.shape[1], LANES)]
    return jnp.concatenate(parts, axis=1)


def _gelu(x):
    return 0.5 * x * (1.0 + jnp.tanh(0.7978845608028654 * (x + 0.044715 * x * x * x)))


def _sigmoid(x):
    return 1.0 / (1.0 + jnp.exp(-x))


def _softplus(x):
    return jnp.maximum(x, 0.0) + jnp.log(1.0 + jnp.exp(-jnp.abs(x)))


def _top_mask(gate, idx, axis, n_valid):
    cur = gate
    sel = jnp.zeros(gate.shape, F32)
    for _ in range(MOBA_TOPK):
        m = jnp.max(cur, axis=axis, keepdims=True)
        first = jnp.min(jnp.where(cur == m, idx, n_valid), axis=axis, keepdims=True)
        pick = (idx == first) & (m > NEG_INF)
        sel = jnp.where(pick, 1.0, sel)
        cur = jnp.where(pick, NEG_INF, cur)
    return sel


def _gmlp_kernel(x_ref, ng_ref, win_ref, lng_ref, lnb_ref, sa_ref, sb_ref, wout_ref, y_ref, aux_ref,
                 *, sample):
    x = x_ref[...]
    tm, width = x.shape[0], wout_ref.shape[0]
    h = _rms(x, ng_ref[...])
    z = _gelu(_bdot(h, win_ref[...]))
    u, v = z[:, :width], z[:, width:]
    mu = jnp.mean(v, axis=-1, keepdims=True)
    d = v - mu
    var = jnp.mean(d * d, axis=-1, keepdims=True)
    v = d * lax.rsqrt(var + GM_LN_EPS) * lng_ref[...] + lnb_ref[...]
    if sample:
        aux_ref[...] = v
        s = v * sa_ref[...] + sb_ref[...]
    else:
        gd = width // GM_GROUPS
        causal = (lax.broadcasted_iota(I32, (GM_CHUNK, GM_CHUNK), 0)
                  >= lax.broadcasted_iota(I32, (GM_CHUNK, GM_CHUNK), 1))
        for g in range(GM_GROUPS):
            wg = jnp.where(causal, sa_ref[g], 0.0).astype(BF16)
            cols = slice(g * gd, (g + 1) * gd)
            for c in range(tm // GM_CHUNK):
                rows = slice(c * GM_CHUNK, (c + 1) * GM_CHUNK)
                aux_ref[rows, cols] = (jnp.dot(wg, v[rows, cols].astype(BF16), preferred_element_type=F32)
                                       + sb_ref[:, cols])
        s = aux_ref[...]
    y_ref[...] = x + _bdot(u * s, wout_ref[...])


def _gmlp(x, ng, win, lng, lnb, sa, sb, wout, *, tm, sample):
    n, d = x.shape
    width = wout.shape[0]
    outs = [jax.ShapeDtypeStruct((n, d), F32)]
    out_specs = [_rows(tm, d)]
    scratch = []
    if sample:
        outs.append(jax.ShapeDtypeStruct((n, width), F32))
        out_specs.append(_rows(tm, width))
    else:
        scratch.append(pltpu.VMEM((tm, width), F32))
    return pl.pallas_call(
        functools.partial(_gmlp_kernel, sample=sample),
        out_shape=outs, grid=(n // tm,),
        in_specs=[_rows(tm, d), _whole(ng.shape), _whole(win.shape), _whole(lng.shape), _whole(lnb.shape),
                  _whole(sa.shape), _whole(sb.shape), _whole(wout.shape)],
        out_specs=out_specs, scratch_shapes=scratch,
        compiler_params=_cparams("parallel"), name="gmlp_sample" if sample else "gmlp_prompt",
    )(x, ng, win, lng, lnb, sa, sb, wout)


def _norm_linear_kernel(x_ref, ng_ref, w_ref, *o_refs):
    z = _bdot(_rms(x_ref[...], ng_ref[...]), w_ref[...])
    wd = z.shape[1] // len(o_refs)
    for i, o_ref in enumerate(o_refs):
        o_ref[...] = z[:, i * wd:(i + 1) * wd]


def _norm_linear(x, ng, w, n_out, *, tm, name):
    n, d = x.shape
    wd = w.shape[1] // n_out
    return pl.pallas_call(
        _norm_linear_kernel,
        out_shape=[jax.ShapeDtypeStruct((n, wd), F32)] * n_out, grid=(n // tm,),
        in_specs=[_rows(tm, d), _whole(ng.shape), _whole(w.shape)],
        out_specs=[_rows(tm, wd)] * n_out,
        compiler_params=_cparams("parallel"), name=name,
    )(x, ng, w)


def _qkv_prompt_kernel(x_ref, ng_ref, wq_ref, wkt_ref, wvt_ref, q_ref, kt_ref, vt_ref):
    h = _rms(x_ref[...], ng_ref[...]).astype(BF16)
    q_ref[...] = jnp.dot(h, wq_ref[...], preferred_element_type=F32)
    kt_ref[0] = _nt_dot(wkt_ref[...], h)
    vt_ref[0] = _nt_dot(wvt_ref[...], h)


def _qkv_prompt(x, ng, wq, wkt, wvt, *, tm, batch, seq):
    n, d = x.shape
    tps = seq // tm
    t_spec = pl.BlockSpec((1, d, tm), lambda i: (i // tps, 0, i % tps))
    return pl.pallas_call(
        _qkv_prompt_kernel,
        out_shape=[jax.ShapeDtypeStruct((n, d), F32)] + [jax.ShapeDtypeStruct((batch, d, seq), F32)] * 2,
        grid=(n // tm,),
        in_specs=[_rows(tm, d), _whole(ng.shape), _whole(wq.shape), _whole(wkt.shape), _whole(wvt.shape)],
        out_specs=[_rows(tm, d), t_spec, t_spec],
        compiler_params=_cparams("parallel"), name="moba_qkv_prompt",
    )(x, ng, wq, wkt, wvt)


def _linear_res_kernel(a_ref, w_ref, y_ref, o_ref):
    o_ref[...] = y_ref[...] + _bdot(a_ref[...], w_ref[...])


def _linear_res(a, w, y, *, tm, name):
    n, d = y.shape
    return pl.pallas_call(
        _linear_res_kernel, out_shape=jax.ShapeDtypeStruct((n, d), F32), grid=(n // tm,),
        in_specs=[_rows(tm, a.shape[1]), _whole(w.shape), _rows(tm, d)], out_specs=_rows(tm, d),
        compiler_params=_cparams("parallel"), name=name,
    )(a, w, y)


def _t5_bucket_table(rel):
    n = jnp.maximum(rel, 0)
    exact = REL_BUCKETS // 2
    nf = jnp.maximum(n, 1).astype(F32)
    large = exact + (jnp.log(nf / exact) / math.log(REL_MAX_DIST / exact)
                     * (REL_BUCKETS - exact)).astype(I32)
    return jnp.where(n < exact, n, jnp.minimum(large, REL_BUCKETS - 1)).astype(I32)


def _bias_tiles_kernel(rb_ref, bkt_ref, o_ref):
    h = pl.program_id(0)
    bkt = bkt_ref[...]
    acc = jnp.zeros(bkt.shape, F32)
    for b in range(REL_BUCKETS):
        acc = jnp.where(bkt == b, rb_ref[b, h], acc)
    future = (lax.broadcasted_iota(I32, bkt.shape, 2) > lax.broadcasted_iota(I32, bkt.shape, 1))
    own_block = lax.broadcasted_iota(I32, bkt.shape, 0) == 0
    o_ref[0] = jnp.where(future & own_block, NEG_INF, acc)


def _bias_tiles(rel_bias, bkt):
    heads = rel_bias.shape[1]
    return pl.pallas_call(
        _bias_tiles_kernel, out_shape=jax.ShapeDtypeStruct((heads,) + bkt.shape, F32), grid=(heads,),
        in_specs=[pl.BlockSpec(memory_space=pltpu.SMEM), _whole(bkt.shape)],
        out_specs=pl.BlockSpec((1,) + bkt.shape, lambda h: (h, 0, 0, 0)),
        compiler_params=_cparams("parallel"), name="moba_bias_tiles",
    )(rel_bias, bkt)


def _bias_rows_kernel(rb_ref, bkt_ref, o_ref):
    bkt = bkt_ref[...]
    acc = jnp.zeros(bkt.shape, F32)
    for b in range(REL_BUCKETS):
        acc = jnp.where(bkt == b, rb_ref[b:b + 1, :], acc)
    o_ref[...] = acc


def _bias_rows(rb_pad, bkt):
    return pl.pallas_call(
        _bias_rows_kernel, out_shape=jax.ShapeDtypeStruct(bkt.shape, F32), grid=(1,),
        in_specs=[_whole(rb_pad.shape), _whole(bkt.shape)], out_specs=_whole(bkt.shape),
        compiler_params=_cparams("arbitrary"), name="moba_bias_rows",
    )(rb_pad, bkt)


def _moba_prompt_kernel(rb_ref, q_ref, k_ref, v_ref, bias_ref, o_ref, kb_scr, vh_scr, kmean_scr, *, head_dim):
    pair = pl.program_id(1)
    blk = MOBA_BLOCK
    seq = k_ref.shape[2]
    n_blocks = seq // blk
    assert LANES == 2 * head_dim
    shift = int(math.log2(head_dim))
    scale = head_dim ** -0.5
    assert shift % 2 == 0 and n_blocks <= kmean_scr.shape[0]

    def prepare():
        kt, vt = k_ref[0], v_ref[0]
        second = (lax.broadcasted_iota(I32, (LANES, seq), 0) >> shift) == 1
        kb_scr[...] = kt.astype(BF16)
        vh_scr[0] = jnp.where(second, 0.0, vt).astype(BF16)
        vh_scr[1] = jnp.where(second, vt, 0.0).astype(BF16)
        rows = kmean_scr.shape[0]
        in_block = jnp.where((lax.broadcasted_iota(I32, (rows, seq), 1) >> int(math.log2(blk)))
                             == lax.broadcasted_iota(I32, (rows, seq), 0), 1.0, 0.0).astype(BF16)
        k_hi, k_lo = _split(kt)
        nt = lambda a, b: lax.dot_general(a, b, (((1,), (1,)), ((), ())), preferred_element_type=F32)
        kmean_scr[...] = (nt(in_block, k_hi) + nt(in_block, k_lo)) * (1.0 / blk)

    prepare()
    second = (lax.broadcasted_iota(I32, (blk, LANES), 1) >> shift) == 1
    km_hi, km_lo = _split(kmean_scr[...])
    nt = lambda a, b: lax.dot_general(a, b, (((1,), (1,)), ((), ())), preferred_element_type=F32)
    block_id = lax.broadcasted_iota(I32, (kmean_scr.shape[0], 2 * blk), 0)
    far_bias = [rb_ref[REL_BUCKETS - 1, pair * 2 + hh] for hh in range(2)]

    def select(own):
        q = q_ref[own * blk:(own + 1) * blk, :]
        q2 = jnp.concatenate([jnp.where(second, 0.0, q), jnp.where(second, q, 0.0)], axis=0)
        q_hi, q_lo = _split(q2)
        gate = nt(km_hi, q_hi) + nt(km_lo, q_hi) + nt(km_hi, q_lo)
        sel = _top_mask(jnp.where(block_id < own, gate, NEG_INF), block_id, 0, gate.shape[0]).T
        return (q2 * scale).astype(BF16), sel

    def scores(own, q_scaled):
        return jnp.dot(q_scaled, kb_scr[:, 0:(own + 1) * blk], preferred_element_type=F32)

    def mask(own, s, sel):
        head_rows = []
        for hh in range(2):
            rows = slice(hh * blk, (hh + 1) * blk)
            tiles = []
            for n in range(own + 1):
                t = s[rows, n * blk:(n + 1) * blk]
                chosen = sel[rows, n:n + 1] > 0.0
                if n == own:
                    t = t + bias_ref[hh, 0]
                elif n == own - 1:
                    t = jnp.where(chosen, t + bias_ref[hh, 1], NEG_INF)
                else:
                    t = t + jnp.where(chosen, far_bias[hh], NEG_INF)
                tiles.append(t)
            head_rows.append(jnp.concatenate(tiles, axis=1))
        return jnp.concatenate(head_rows, axis=0)

    def weights(s):
        p = jnp.exp(s - jnp.max(s, axis=-1, keepdims=True))
        return p, jnp.sum(p, axis=-1, keepdims=True)

    def output(own, p, l):
        keys = (own + 1) * blk
        p_pair = jnp.concatenate([p[:blk], p[blk:]], axis=1).astype(BF16)
        v_pair = jnp.concatenate([vh_scr[0, :, 0:keys], vh_scr[1, :, 0:keys]], axis=1)
        o_ref[own * blk:(own + 1) * blk, :] = nt(p_pair, v_pair) / jnp.where(second, l[blk:], l[:blk])

    order = list(range(n_blocks))
    groups = [(order[i], order[-1 - i]) for i in range(n_blocks // 2)] + ([(order[n_blocks // 2],)] * (n_blocks % 2))
    for group in groups:
        picked = [select(own) for own in group]
        raw = [scores(own, qs) for own, (qs, _) in zip(group, picked)]
        masked = [mask(own, s, sel) for own, s, (_, sel) in zip(group, raw, picked)]
        probs = [weights(s) for s in masked]
        for own, (p, l) in zip(group, probs):
            output(own, p, l)


def _moba_prompt(rel_bias, q, k, v, bias_tiles, *, batch, seq, head_dim):
    n, d = q.shape
    blk = MOBA_BLOCK
    hpt = LANES // head_dim
    return pl.pallas_call(
        functools.partial(_moba_prompt_kernel, head_dim=head_dim),
        out_shape=jax.ShapeDtypeStruct((n, d), F32), grid=(batch, d // LANES),
        in_specs=[pl.BlockSpec(memory_space=pltpu.SMEM),
                  pl.BlockSpec((seq, LANES), lambda b, p: (b, p)),
                  pl.BlockSpec((1, LANES, seq), lambda b, p: (b, p, 0)),
                  pl.BlockSpec((1, LANES, seq), lambda b, p: (b, p, 0)),
                  pl.BlockSpec((hpt, 2, blk, blk), lambda b, p: (p, 0, 0, 0))],
        out_specs=pl.BlockSpec((seq, LANES), lambda b, p: (b, p)),
        scratch_shapes=[pltpu.VMEM((LANES, seq), BF16), pltpu.VMEM((hpt, LANES, seq), BF16),
                        pltpu.VMEM((2 * SUBLANES, LANES), F32)],
        compiler_params=_cparams("parallel", "parallel"), name="moba_prompt_attn",
    )(rel_bias, q, k, v, bias_tiles)


def _moba_sample_kernel(pt_ref, q_ref, kn_ref, vn_ref, *rest, n_pages, head_dim):
    del pt_ref
    k_refs, v_refs = rest[:n_pages], rest[n_pages:2 * n_pages]
    bias_ref, rb0_ref, o_ref = rest[2 * n_pages:]
    d = q_ref.shape[2]
    heads = d // head_dim
    shift = int(math.log2(head_dim))
    per_block = MOBA_BLOCK // PAGE_SIZE
    blocks = range(n_pages // per_block)
    scale = head_dim ** -0.5
    head_lanes = jnp.where((lax.broadcasted_iota(I32, (heads, d), 1) >> shift)
                           == lax.broadcasted_iota(I32, (heads, d), 0), 1.0, 0.0)
    q = head_lanes * q_ref[0]
    q_hi, q_lo = _split(q)
    q2 = jnp.concatenate([q_hi, q_lo], axis=0)
    block_of = lambda refs, n: jnp.concatenate(
        [refs[per_block * n + j][0, 0].reshape(d, PAGE_SIZE) for j in range(per_block)], axis=1).astype(BF16)
    raw = [jnp.dot(q2, block_of(k_refs, n), preferred_element_type=F32) for n in blocks]
    raw = [r[:heads] + r[heads:] for r in raw]
    block_id = lax.broadcasted_iota(I32, (heads, LANES), 1)
    gate = jnp.full((heads, LANES), NEG_INF, F32)
    for n in blocks:
        gate = jnp.where(block_id == n, jnp.sum(raw[n], axis=-1, keepdims=True), gate)
    sel = _top_mask(gate, block_id, -1, LANES)
    s_new = jnp.sum(q * kn_ref[0], axis=-1, keepdims=True) * scale + rb0_ref[:, 0:1]
    s = [raw[n] * scale + bias_ref[n] for n in blocks]
    m = s_new
    for n in blocks:
        m = jnp.maximum(m, jnp.where(sel[:, n:n + 1] > 0.0, jnp.max(s[n], axis=-1, keepdims=True), NEG_INF))
    p = [jnp.exp(jnp.where(sel[:, n:n + 1] > 0.0, s[n] - m, NEG_INF)) for n in blocks]
    p_new = jnp.exp(s_new - m)
    den = p_new
    for n in blocks:
        den = den + jnp.sum(p[n], axis=-1, keepdims=True)
    p_all = jnp.concatenate(p, axis=1).astype(BF16)
    v_all = jnp.concatenate([block_of(v_refs, n) for n in blocks], axis=1)
    acc = lax.dot_general(p_all, v_all, (((1,), (1,)), ((), ())), preferred_element_type=F32)
    out = (acc + p_new * vn_ref[0]) * head_lanes / den
    o_ref[0] = jnp.sum(out, axis=0, keepdims=True)


def _moba_sample(page_table, q, k_new, v_new, cache_k, cache_v, layer, bias_rows, rel_bias, *, head_dim):
    nb, d = q.shape
    n_pages = page_table.shape[1]
    n_blocks = n_pages * PAGE_SIZE // MOBA_BLOCK
    heads = d // head_dim
    cache_kt = jnp.transpose(cache_k, (0, 1, 3, 4, 2))
    cache_vt = jnp.transpose(cache_v, (0, 1, 3, 4, 2))
    row = lambda x: x.reshape(nb, 1, d)
    vec = pl.BlockSpec((1, 1, d), lambda s, pt: (s, 0, 0))
    page = lambda j: pl.BlockSpec((1, 1, heads, head_dim, PAGE_SIZE), lambda s, pt: (layer, pt[s, j], 0, 0, 0))
    const = lambda shape: pl.BlockSpec(shape, lambda s, pt: (0,) * len(shape))
    bias = bias_rows[:, :heads].reshape(n_blocks, MOBA_BLOCK, heads).transpose(0, 2, 1)
    rb0 = jnp.broadcast_to(rel_bias[0][:, None], (heads, LANES))
    pages = [page(j) for j in range(n_pages)]
    out = pl.pallas_call(
        functools.partial(_moba_sample_kernel, n_pages=n_pages, head_dim=head_dim),
        out_shape=jax.ShapeDtypeStruct((nb, 1, d), F32),
        grid_spec=pltpu.PrefetchScalarGridSpec(
            num_scalar_prefetch=1, grid=(nb,),
            in_specs=[vec, vec, vec, *pages, *pages, const(bias.shape), const(rb0.shape)],
            out_specs=vec),
        compiler_params=_cparams("parallel"), name="moba_sample_attn",
    )(page_table, row(q), row(k_new), row(v_new), *([cache_kt] * n_pages), *([cache_vt] * n_pages), bias, rb0)
    return out.reshape(nb, d)


POOL_PAD = 32


def _pool_prompt_kernel(y_ref, ng_ref, w_ref, sc_ref, yout_ref, st_ref, b0, b1, b2, b3, b4, *, tps):
    i = pl.program_id(0)
    y = y_ref[...]
    tm, d = y.shape
    gd = d // len(POOL_WINDOWS)
    h = _rms(y, ng_ref[...])
    half = POOL_PAD // 2

    @pl.when(i == 0)
    def _():
        for b in (b0, b1, b2, b3, b4):
            b[0:half, :] = jnp.zeros((half, b.shape[1]), F32)

    @pl.when(i % tps == 0)
    def _():
        b0[half:POOL_PAD, :] = jnp.zeros((half, d), F32)

    b0[POOL_PAD:, :] = h
    stages = (b0, b1, b2, b3, b4)
    for k in range(1, len(stages)):
        src, dst = stages[k - 1], stages[k]
        back = 2 ** (k - 1)
        off = src.shape[1] - dst.shape[1]
        dst[SUBLANES:, :] = (src[SUBLANES:, off:] + src[SUBLANES - back:tm + POOL_PAD - back, off:])
    pos = (i % tps) * tm + lax.broadcasted_iota(I32, (tm, 1), 0) + 1
    for g, win in enumerate(POOL_WINDOWS):
        cols = slice(g * gd, (g + 1) * gd)
        wsum = stages[g + 1][POOL_PAD:, 0:gd]
        cnt = jnp.minimum(pos, win).astype(F32)
        mixed = _bdot(wsum / cnt - h[:, cols], w_ref[g])
        yout_ref[:, cols] = y[:, cols] + mixed * sc_ref[:, cols]
    st_ref[0] = b0[tm + POOL_PAD - POOL_CTX:, :]
    b0[half:POOL_PAD, :] = b0[tm + half:, :]


def _pool_prompt(y, ng, w, sc, *, tm, seq):
    n, d = y.shape
    gd = d // len(POOL_WINDOWS)
    tps = seq // tm
    rows = tm + POOL_PAD
    return pl.pallas_call(
        functools.partial(_pool_prompt_kernel, tps=tps),
        out_shape=[jax.ShapeDtypeStruct((n, d), F32), jax.ShapeDtypeStruct((n // seq, POOL_CTX, d), F32)],
        grid=(n // tm,),
        in_specs=[_rows(tm, d), _whole(ng.shape), _whole(w.shape), _whole(sc.shape)],
        out_specs=[_rows(tm, d), pl.BlockSpec((1, POOL_CTX, d), lambda i: (i // tps, 0, 0))],
        scratch_shapes=[pltpu.VMEM((rows, d - k * gd), F32) for k in (0, 0, 1, 2, 3)],
        compiler_params=_cparams("arbitrary"), name="pool_prompt",
    )(y, ng, w, sc)


def _pool_sample_kernel(y_ref, ng_ref, prev_ref, w_ref, sc_ref, yout_ref, h_ref):
    y = y_ref[...]
    d = y.shape[1]
    gd = d // len(POOL_WINDOWS)
    h = _rms(y, ng_ref[...])
    h_ref[...] = h
    for g, win in enumerate(POOL_WINDOWS):
        cols = slice(g * gd, (g + 1) * gd)
        wsum = h[:, cols]
        for back in range(1, win):
            wsum = wsum + prev_ref[POOL_CTX - back, :, cols]
        mixed = _bdot(wsum / float(win) - h[:, cols], w_ref[g])
        yout_ref[:, cols] = y[:, cols] + mixed * sc_ref[:, cols]


def _pool_sample(y, ng, prev_t, w, sc):
    n, d = y.shape
    return pl.pallas_call(
        _pool_sample_kernel, out_shape=[jax.ShapeDtypeStruct((n, d), F32)] * 2, grid=(1,),
        in_specs=[_whole(y.shape), _whole(ng.shape), _whole(prev_t.shape), _whole(w.shape), _whole(sc.shape)],
        out_specs=[_whole(y.shape)] * 2,
        compiler_params=_cparams("arbitrary"), name="pool_sample",
    )(y, ng, prev_t, w, sc)


def _rwkv_proj_kernel(y_ref, ng_ref, mu_ref, wr_ref, wk_ref, wv_ref, w1_ref, w2_ref, a1_ref, a2_ref,
                      g1_ref, g2_ref, vec_ref, *rest, tps, sample):
    if sample:
        prev_ref, r_o, w_o, k_o, v_o, kk_o, kka_o, g_o, sh_o = rest
    else:
        r_o, w_o, k_o, v_o, kk_o, kka_o, g_o, sh_o, hs_scr = rest
    y = y_ref[...]
    tm = y.shape[0]
    h = _rms(y, ng_ref[...])
    if sample:
        h_prev = prev_ref[...]
        sh_o[...] = h
    else:
        i = pl.program_id(0)

        @pl.when(i % tps == 0)
        def _():
            hs_scr[0:SUBLANES, :] = jnp.zeros((SUBLANES, y.shape[1]), F32)

        hs_scr[SUBLANES:, :] = h
        h_prev = hs_scr[SUBLANES - 1:tm + SUBLANES - 1, :]
        hs_scr[0:SUBLANES, :] = h[tm - SUBLANES:, :]
        sh_o[0] = h[tm - 1:tm, :]
    xx = h_prev - h
    mix = lambda m: h + xx * mu_ref[m:m + 1, :]
    r = _bdot(mix(0), wr_ref[...])
    k = _bdot(mix(2), wk_ref[...])
    v = _bdot(mix(3), wv_ref[...])
    w_log = -_softplus(-(vec_ref[0:1, :] + _bdot(jnp.tanh(_bdot(mix(1), w1_ref[...])), w2_ref[...]))) - 0.5
    a = _sigmoid(vec_ref[1:2, :] + _bdot(_bdot(mix(4), a1_ref[...]), a2_ref[...]))
    g_o[...] = _bdot(_sigmoid(_bdot(mix(5), g1_ref[...])), g2_ref[...])
    kk = k * vec_ref[2:3, :]
    kk = kk / jnp.maximum(jnp.sqrt(_seg_sum(kk * kk, RWKV_HEAD_DIM)), 1e-12)
    r_o[...] = r
    w_o[...] = -jnp.exp(w_log)
    k_o[...] = k * (1.0 + (a - 1.0) * vec_ref[3:4, :])
    v_o[...] = v
    kk_o[...] = kk
    kka_o[...] = kk * a


def _rwkv_proj(y, ng, mu, mats, vec, prev, *, tm, seq, sample):
    n, d = y.shape
    tps = max(seq // tm, 1)
    ins = [y, ng, mu, *mats, vec]
    in_specs = [_rows(tm, d), _whole(ng.shape), _whole(mu.shape), *[_whole(m.shape) for m in mats],
                _whole(vec.shape)]
    outs = [jax.ShapeDtypeStruct((n, d), F32)] * 7
    out_specs = [_rows(tm, d)] * 7
    scratch = []
    if sample:
        ins.append(prev)
        in_specs.append(_rows(tm, d))
        outs.append(jax.ShapeDtypeStruct((n, d), F32))
        out_specs.append(_rows(tm, d))
    else:
        outs.append(jax.ShapeDtypeStruct((n // seq, 1, d), F32))
        out_specs.append(pl.BlockSpec((1, 1, d), lambda i: (i // tps, 0, 0)))
        scratch.append(pltpu.VMEM((tm + SUBLANES, d), F32))
    return pl.pallas_call(
        functools.partial(_rwkv_proj_kernel, tps=tps, sample=sample),
        out_shape=outs, grid=(n // tm,), in_specs=in_specs, out_specs=out_specs, scratch_shapes=scratch,
        compiler_params=_cparams("arbitrary"), name="rwkv_proj_sample" if sample else "rwkv_proj_prompt",
    )(*ins)


RWKV_CHUNK = 128


def _rwkv_chunk_kernel(r_ref, lw_ref, k_ref, v_ref, kk_ref, kka_ref, o_ref, st_ref, t_scr):
    c = pl.program_id(1)
    ch, d = r_ref.shape
    hd = RWKV_HEAD_DIM
    shift = int(math.log2(hd))
    n_pairs = d // LANES

    @pl.when(c == 0)
    def _():
        t_scr[...] = jnp.zeros(t_scr.shape, F32)

    ri = lax.broadcasted_iota(I32, (ch, ch), 0)
    ci = lax.broadcasted_iota(I32, (ch, ch), 1)
    incl = ci <= ri
    strict = ci < ri
    incl2 = jnp.concatenate([incl, incl], axis=1)
    strict2 = jnp.concatenate([strict, strict], axis=1)
    same_head = (ri >> shift) == (ci >> shift)
    eye = ri == ci
    head1 = (lax.broadcasted_iota(I32, (ch, LANES), 1) >> shift) == 1
    head1_wide = jnp.concatenate([head1, head1], axis=1)

    def by_head(x, mask):
        return jnp.concatenate([jnp.where(mask, 0.0, x), jnp.where(mask, x, 0.0)], axis=0).astype(BF16)

    lw = lw_ref[...]
    tri = jnp.where(incl, 1.0, 0.0).astype(BF16)
    p1 = lw.astype(BF16)
    rem = lw - p1.astype(F32)
    p2 = rem.astype(BF16)
    p3 = (rem - p2.astype(F32)).astype(BF16)
    cl = (jnp.dot(tri, p1, preferred_element_type=F32) + jnp.dot(tri, p2, preferred_element_type=F32)
          + jnp.dot(tri, p3, preferred_element_type=F32))
    mid = cl[ch // 2 - 1:ch // 2, :]
    last = cl[ch - 1:ch, :]
    e_mid = jnp.exp(mid)
    e_neg = jnp.exp(mid - cl)
    e_tail = jnp.exp(last - cl)
    p_last = jnp.exp(last)
    kk, kka, k = kk_ref[...], kka_ref[...], k_ref[...]
    a_s = -kk * jnp.exp(cl - lw - mid)
    r_s = r_ref[...] * jnp.exp(cl - mid)
    b_s = kka * e_neg
    k_s = k * e_neg
    b_t = kka * e_tail
    k_t = k * e_tail
    v = v_ref[...]
    zero = jnp.zeros((ch, ch), BF16)

    pairs = range(n_pairs)
    cols = [slice(p * LANES, (p + 1) * LANES) for p in pairs]
    mm = lambda a, b: jnp.dot(a, b, preferred_element_type=F32)
    lane_cat = lambda a, b: jnp.concatenate([a, b], axis=1)
    a_p = [a_s[:, c] for c in cols]
    r_p = [r_s[:, c] for c in cols]
    v_p = [v[:, c] for c in cols]
    gram = []
    for p in pairs:
        lhs = jnp.concatenate([jnp.where(head1, 0.0, a_p[p]), jnp.where(head1, 0.0, r_p[p]),
                               jnp.where(head1, a_p[p], 0.0), jnp.where(head1, r_p[p], 0.0)], axis=0)
        rhs = lane_cat(b_s[:, cols[p]].T, k_s[:, cols[p]].T)
        gram.append(mm(lhs.astype(BF16), rhs.astype(BF16)))
    ab0 = [jnp.where(strict2, g_[0:ch], 0.0) for g_ in gram]
    rb0 = [jnp.where(incl2, g_[ch:2 * ch], 0.0) for g_ in gram]
    ab1 = [jnp.where(strict2, g_[2 * ch:3 * ch], 0.0) for g_ in gram]
    rb1 = [jnp.where(incl2, g_[3 * ch:], 0.0) for g_ in gram]
    v_heads = [by_head(v_p[p], head1) for p in pairs]
    x = [lane_cat(a_p[p], mm(lane_cat(ab0[p][:, ch:], ab1[p][:, ch:]).astype(BF16), v_heads[p])) for p in pairs]
    pw = [lane_cat(ab0[p][:, :ch], ab1[p][:, :ch]).astype(BF16) for p in pairs]
    n_sq = int(math.log2(ch))
    for it in range(n_sq):
        x = [x[p] + mm(pw[p], by_head(x[p], head1_wide)) for p in pairs]
        if it < n_sq - 1:
            pw = [mm(pw[p], jnp.concatenate([lane_cat(pw[p][:, :ch], zero), lane_cat(zero, pw[p][:, ch:])],
                                            axis=0)).astype(BF16) for p in pairs]
    x = [lane_cat(x[p][:, :ch] * e_mid[:, cols[p]], x[p][:, ch:]) for p in pairs]
    qo = [mm(lane_cat(rb0[p][:, :ch], rb1[p][:, :ch]).astype(BF16), by_head(x[p], head1_wide)) for p in pairs]
    o_intra = [qo[p][:, ch:] + mm(lane_cat(rb0[p][:, ch:], rb1[p][:, ch:]).astype(BF16), v_heads[p])
               for p in pairs]
    q = [r_p[p] * e_mid[:, cols[p]] + qo[p][:, :ch] for p in pairs]
    gh = [mm(b_t[:, cols[p]].T.astype(BF16), x[p].astype(BF16)) for p in pairs]
    g = [jnp.where(same_head, gh[p][:, :ch], 0.0) + jnp.where(eye, p_last[:, cols[p]], 0.0) for p in pairs]
    h = [jnp.where(same_head, gh[p][:, ch:] + _bdot(k_t[:, cols[p]].T, v_p[p]), 0.0) for p in pairs]
    for p in pairs:
        t_old = t_scr[p].astype(BF16)
        o_ref[:, cols[p]] = mm(q[p].astype(BF16), t_old) + o_intra[p]
        t_scr[p] = mm(g[p].astype(BF16), t_old) + h[p]

    @pl.when(c == pl.num_programs(1) - 1)
    def _():
        for p in range(n_pairs):
            s_pair = t_scr[p].T
            st_ref[0, p] = s_pair[:hd, :] + s_pair[hd:, :]


def _rwkv_chunked(r, lw, k, v, kk, kka, *, batch, seq):
    n, d = r.shape
    ch = RWKV_CHUNK
    assert ch == LANES and seq % ch == 0
    n_chunks = seq // ch
    n_pairs = d // LANES
    blk = pl.BlockSpec((ch, d), lambda b, c: (b * n_chunks + c, 0))
    return pl.pallas_call(
        _rwkv_chunk_kernel,
        out_shape=[jax.ShapeDtypeStruct((n, d), F32),
                   jax.ShapeDtypeStruct((batch, n_pairs, RWKV_HEAD_DIM, LANES), F32)],
        grid=(batch, n_chunks), in_specs=[blk] * 6,
        out_specs=[blk, pl.BlockSpec((1, n_pairs, RWKV_HEAD_DIM, LANES), lambda b, c: (b, 0, 0, 0))],
        scratch_shapes=[pltpu.VMEM((n_pairs, LANES, LANES), F32)],
        compiler_params=_cparams("parallel", "arbitrary"), name="rwkv_chunked",
    )(r, lw, k, v, kk, kka)


def _rwkv_step_kernel(r_ref, lw_ref, k_ref, v_ref, kk_ref, kka_ref, s0_ref, o_ref, st_ref):
    hd = s0_ref.shape[1]
    w = jnp.exp(lw_ref[...])
    kk, kka, k, r = kk_ref[...], kka_ref[...], k_ref[...], r_ref[...]

    def value_row(i, carry):
        s = s0_ref[0, i]
        s_kk = jnp.sum(s * kk, axis=0, keepdims=True)
        s = s * w - s_kk * kka + v_ref[pl.ds(i, 1), :] * k
        st_ref[0, i] = s
        o_ref[pl.ds(i, 1), :] = jnp.sum(s * r, axis=0, keepdims=True)
        return carry

    lax.fori_loop(0, hd, value_row, 0, unroll=8)


def _rwkv_step(r, lw, k, v, kk, kka, s0):
    d, n = r.shape
    heads, hd = s0.shape[0], s0.shape[1]
    vec = pl.BlockSpec((hd, n), lambda h: (h, 0))
    st = pl.BlockSpec((1, hd, hd, n), lambda h: (h, 0, 0, 0))
    return pl.pallas_call(
        _rwkv_step_kernel,
        out_shape=[jax.ShapeDtypeStruct((d, n), F32), jax.ShapeDtypeStruct(s0.shape, F32)],
        grid=(heads,), in_specs=[vec] * 6 + [st], out_specs=[vec, st],
        compiler_params=_cparams("parallel"), name="rwkv_step",
    )(r, lw, k, v, kk, kka, s0)


def _rwkv_out_kernel(o_ref, r_ref, k_ref, v_ref, g_ref, y_ref, vec_ref, wo_ref, yout_ref):
    hd = RWKV_HEAD_DIM
    o = o_ref[...]
    mu = _seg_sum(o, hd) * (1.0 / hd)
    dlt = o - mu
    var = _seg_sum(dlt * dlt, hd) * (1.0 / hd)
    o = dlt * lax.rsqrt(var + RWKV_LNX_EPS) * vec_ref[1:2, :] + vec_ref[2:3, :]
    o = o + _seg_sum(r_ref[...] * k_ref[...] * vec_ref[0:1, :], hd) * v_ref[...]
    yout_ref[...] = y_ref[...] + _bdot(o * g_ref[...], wo_ref[...])


def _rwkv_out(o, r, k, v, g, y, vec, wo, *, tm, name):
    n, d = y.shape
    return pl.pallas_call(
        _rwkv_out_kernel, out_shape=jax.ShapeDtypeStruct((n, d), F32), grid=(n // tm,),
        in_specs=[_rows(tm, d)] * 6 + [_whole(vec.shape), _whole(wo.shape)], out_specs=_rows(tm, d),
        compiler_params=_cparams("parallel"), name=name,
    )(o, r, k, v, g, y, vec, wo)


FFN_CHUNK = 256


def _ffn_act(gate, g1, g2, up, cw, cb):
    return _gelu(cw[0:1] * g2 + cw[1:2] * g1 + cw[2:3] * gate + cb) * up


def _ffn_prompt_kernel(y_ref, ng_ref, win_ref, cw_ref, cb_ref, wout_ref, *rest, tps, final):
    if final:
        fg_ref, yout_ref, st_ref, yfin_ref, h_scr, gs_scr, act_scr, carry_scr = rest
    else:
        yout_ref, st_ref, h_scr, gs_scr, act_scr, carry_scr = rest
    i = pl.program_id(0)
    y = y_ref[...]
    tm = y.shape[0]
    hid = wout_ref.shape[0]
    tf = FFN_CHUNK
    h_scr[...] = _rms(y, ng_ref[...]).astype(BF16)

    @pl.when(i % tps == 0)
    def _():
        carry_scr[...] = jnp.zeros(carry_scr.shape, F32)

    for j in range(hid // tf):
        cols = slice(j * tf, (j + 1) * tf)
        hb = h_scr[...]
        gs_scr[0:SUBLANES, :] = carry_scr[:, cols]
        gs_scr[SUBLANES:, :] = jnp.dot(hb, win_ref[:, cols], preferred_element_type=F32)
        up = jnp.dot(hb, win_ref[:, hid + j * tf:hid + (j + 1) * tf], preferred_element_type=F32)
        gate = gs_scr[SUBLANES:, :]
        carry_scr[:, cols] = gate[tm - SUBLANES:, :]
        st_ref[0, :, cols] = gate[tm - (FFN_CONV - 1):, :]
        act = _ffn_act(gate, gs_scr[SUBLANES - 1:tm + SUBLANES - 1, :], gs_scr[SUBLANES - 2:tm + SUBLANES - 2, :],
                       up, cw_ref[:, cols], cb_ref[:, cols])
        act_scr[:, cols] = act.astype(BF16)
    y_new = y + jnp.dot(act_scr[...], wout_ref[...], preferred_element_type=F32)
    yout_ref[...] = y_new
    if final:
        yfin_ref[...] = _rms(y_new, fg_ref[...])


def _ffn_prompt(y, ng, win, cw, cb, wout, fg, *, tm, seq):
    n, d = y.shape
    hid = wout.shape[0]
    tps = seq // tm
    final = fg is not None
    ins = [y, ng, win, cw, cb, wout]
    in_specs = [_rows(tm, d)] + [_whole(a.shape) for a in ins[1:]]
    outs = [jax.ShapeDtypeStruct((n, d), F32), jax.ShapeDtypeStruct((n // seq, FFN_CONV - 1, hid), F32)]
    out_specs = [_rows(tm, d), pl.BlockSpec((1, FFN_CONV - 1, hid), lambda i: (i // tps, 0, 0))]
    if final:
        ins.append(fg)
        in_specs.append(_whole(fg.shape))
        outs.append(jax.ShapeDtypeStruct((n, d), F32))
        out_specs.append(_rows(tm, d))
    return pl.pallas_call(
        functools.partial(_ffn_prompt_kernel, tps=tps, final=final),
        out_shape=outs, grid=(n // tm,), in_specs=in_specs, out_specs=out_specs,
        scratch_shapes=[pltpu.VMEM((tm, d), BF16), pltpu.VMEM((tm + SUBLANES, FFN_CHUNK), F32),
                        pltpu.VMEM((tm, hid), BF16), pltpu.VMEM((SUBLANES, hid), F32)],
        compiler_params=_cparams("arbitrary"), name="ffn_prompt",
    )(*ins)


def _ffn_sample_kernel(y_ref, ng_ref, wg_ref, wu_ref, cw_ref, cb_ref, p2_ref, p1_ref, wout_ref, *rest, final):
    if final:
        fg_ref, yout_ref, gate_ref, yfin_ref, h_scr, acc_scr = rest
    else:
        yout_ref, gate_ref, h_scr, acc_scr = rest
    j = pl.program_id(0)

    @pl.when(j == 0)
    def _():
        h_scr[...] = _rms(y_ref[...], ng_ref[...]).astype(BF16)
        acc_scr[...] = jnp.zeros(acc_scr.shape, F32)

    hb = h_scr[...]
    gate = jnp.dot(hb, wg_ref[...], preferred_element_type=F32)
    up = jnp.dot(hb, wu_ref[...], preferred_element_type=F32)
    gate_ref[...] = gate
    act = _ffn_act(gate, p1_ref[...], p2_ref[...], up, cw_ref[...], cb_ref[...])
    acc_scr[...] += _bdot(act, wout_ref[...])

    @pl.when(j == pl.num_programs(0) - 1)
    def _():
        y_new = y_ref[...] + acc_scr[...]
        yout_ref[...] = y_new
        if final:
            yfin_ref[...] = _rms(y_new, fg_ref[...])


def _ffn_sample(y, ng, win, cw, cb, p2, p1, wout, fg):
    n, d = y.shape
    hid = wout.shape[0]
    tf = FFN_CHUNK
    nf = hid // tf
    final = fg is not None
    keep = lambda shape: pl.BlockSpec(shape, lambda j: (0,) * len(shape))
    chunk = lambda rows: pl.BlockSpec((rows, tf), lambda j: (0, j))
    ins = [y, ng, win, win, cw, cb, p2, p1, wout]
    in_specs = [keep((n, d)), keep(ng.shape), chunk(d), pl.BlockSpec((d, tf), lambda j: (0, nf + j)),
                chunk(FFN_CONV), chunk(1), chunk(n), chunk(n), pl.BlockSpec((tf, d), lambda j: (j, 0))]
    outs = [jax.ShapeDtypeStruct((n, d), F32), jax.ShapeDtypeStruct((n, hid), F32)]
    out_specs = [keep((n, d)), chunk(n)]
    if final:
        ins.append(fg)
        in_specs.append(keep(fg.shape))
        outs.append(jax.ShapeDtypeStruct((n, d), F32))
        out_specs.append(keep((n, d)))
    return pl.pallas_call(
        functools.partial(_ffn_sample_kernel, final=final),
        out_shape=outs, grid=(nf,), in_specs=in_specs, out_specs=out_specs,
        scratch_shapes=[pltpu.VMEM((n, d), BF16), pltpu.VMEM((n, d), F32)],
        compiler_params=_cparams("arbitrary"), name="ffn_sample",
    )(*ins)


def kernel(x_prompt, x_sample, cache_moba_k, cache_moba_v, state_pool, state_rwkv_wkv, state_rwkv_shift, state_ffn_conv, page_table, norm_mix_g, norm_ffn_g, norm_final_g, rel_bias, gm_w_in, gm_ln_g, gm_ln_b, gm_w_s, gm_b_s, gm_w_out, moba_w_qkv, moba_w_o, pool_w, pool_scale, rwkv_mu, rwkv_w_r, rwkv_w_k, rwkv_w_v, rwkv_w_o, rwkv_w0, rwkv_w1, rwkv_w2, rwkv_a0, rwkv_a1, rwkv_a2, rwkv_g1, rwkv_g2, rwkv_k_k, rwkv_k_a, rwkv_r_k, rwkv_lnx_g, rwkv_lnx_b, ffn_w_in, ffn_conv_w, ffn_conv_b, ffn_w_out):
    bp, seq, d = x_prompt.shape
    bs = x_sample.shape[0]
    depth = norm_mix_g.shape[0]
    assert x_sample.shape[1] == 1 and depth == 4
    past_len = page_table.shape[1] * PAGE_SIZE
    assert seq % MOBA_BLOCK == 0 and past_len % MOBA_BLOCK == 0 and past_len % GM_CHUNK == 0
    row = lambda vct: vct.reshape(1, -1)
    bf = lambda m: m.astype(BF16)
    yp = x_prompt.reshape(bp * seq, d)
    ys = x_sample.reshape(bs, d)
    tm = 512
    conv_p, conv_s = [], []

    def ffn(i, yp, ys):
        last = i == depth - 1
        win, wout = bf(ffn_w_in[i]), bf(ffn_w_out[i])
        cw, cb, ng = ffn_conv_w[i], row(ffn_conv_b[i]), row(norm_ffn_g[i])
        fg = row(norm_final_g) if last else None
        res_p = _ffn_prompt(yp, ng, win, cw, cb, wout, fg, tm=tm, seq=seq)
        st = state_ffn_conv[i]
        res_s = _ffn_sample(ys, ng, win, cw, cb, st[:, 0], st[:, 1], wout, fg)
        conv_p.append(res_p[1])
        conv_s.append(jnp.stack([st[:, 1], res_s[1]], axis=1))
        if last:
            return res_p[2], res_s[2]
        return res_p[0], res_s[0]

    ng = row(norm_mix_g[0])
    width = gm_w_out.shape[1]
    gd = width // GM_GROUPS
    gm_in, gm_out = bf(gm_w_in[0]), bf(gm_w_out[0])
    lng, lnb = row(gm_ln_g[0]), row(gm_ln_b[0])
    sb_prompt = jnp.repeat(gm_b_s[0].T, gd, axis=1)
    (yp,) = _gmlp(yp, ng, gm_in, lng, lnb, gm_w_s[0], sb_prompt, gm_out, tm=tm, sample=False)
    sa_first = row(jnp.repeat(gm_w_s[0][:, 0, 0], gd))
    sb_first = row(jnp.repeat(gm_b_s[0][:, 0], gd))
    ys, gm_v = _gmlp(ys, ng, gm_in, lng, lnb, sa_first, sb_first, gm_out, tm=bs, sample=True)
    gm_v_sample = gm_v.reshape(1, bs, 1, width)
    yp, ys = ffn(0, yp, ys)

    ng = row(norm_mix_g[1])
    w_qkv, w_o = bf(moba_w_qkv[0]), bf(moba_w_o[0])
    heads = MOBA_HEADS
    hd = d // heads
    qp, kp_t, vp_t = _qkv_prompt(yp, ng, w_qkv[:, :d], w_qkv[:, d:2 * d].T, w_qkv[:, 2 * d:].T,
                                 tm=tm, batch=bp, seq=seq)
    qs, ks, vs = _norm_linear(ys, ng, w_qkv, 3, tm=bs, name="moba_qkv_sample")
    blk = MOBA_BLOCK
    qi = jnp.arange(blk, dtype=I32)[:, None]
    ki = jnp.arange(blk, dtype=I32)[None, :]
    bkt_tiles = _t5_bucket_table(jnp.stack([qi - ki, blk + qi - ki]))
    bias_tiles = _bias_tiles(rel_bias, bkt_tiles)
    op = _moba_prompt(rel_bias, qp, kp_t, vp_t, bias_tiles, batch=bp, seq=seq, head_dim=hd)
    bkt_rows = _t5_bucket_table(past_len - jnp.arange(past_len, dtype=I32))
    rb_pad = jnp.pad(rel_bias, ((0, 0), (0, LANES - heads)))
    bias_rows = _bias_rows(rb_pad, jnp.broadcast_to(bkt_rows[:, None], (past_len, LANES)))
    os_ = _moba_sample(page_table, qs, ks, vs, cache_moba_k, cache_moba_v, 0, bias_rows, rel_bias, head_dim=hd)
    yp = _linear_res(op, w_o, yp, tm=tm, name="moba_out_prompt")
    ys = _linear_res(os_, w_o, ys, tm=bs, name="moba_out_sample")
    moba_k_prompt = kp_t.reshape(1, bp, heads, hd, seq).transpose(0, 1, 4, 2, 3)
    moba_v_prompt = vp_t.reshape(1, bp, heads, hd, seq).transpose(0, 1, 4, 2, 3)
    moba_k_sample = ks.reshape(1, bs, 1, heads, hd)
    moba_v_sample = vs.reshape(1, bs, 1, heads, hd)
    yp, ys = ffn(1, yp, ys)

    ng = row(norm_mix_g[2])
    pw, psc = bf(pool_w[0]), row(pool_scale[0])
    yp, pool_p = _pool_prompt(yp, ng, pw, psc, tm=tm, seq=seq)
    ys, hs = _pool_sample(ys, ng, jnp.swapaxes(state_pool[0], 0, 1), pw, psc)
    pool_prompt = pool_p[None]
    pool_sample = jnp.concatenate([state_pool[0][:, 1:], hs[:, None]], axis=1)[None]
    yp, ys = ffn(2, yp, ys)

    ng = row(norm_mix_g[3])
    mats = [bf(m[0]) for m in (rwkv_w_r, rwkv_w_k, rwkv_w_v, rwkv_w1, rwkv_w2, rwkv_a1, rwkv_a2,
                               rwkv_g1, rwkv_g2)]
    vec_in = jnp.stack([rwkv_w0[0], rwkv_a0[0], rwkv_k_k[0], rwkv_k_a[0]])
    vec_out = jnp.stack([rwkv_r_k[0].reshape(-1), rwkv_lnx_g[0], rwkv_lnx_b[0]])
    rh = d // RWKV_HEAD_DIM
    *seqs_p, gp, shp = _rwkv_proj(yp, ng, rwkv_mu[0], mats, vec_in, None, tm=256, seq=seq, sample=False)
    *seqs_s, gs, shs = _rwkv_proj(ys, ng, rwkv_mu[0], mats, vec_in, state_rwkv_shift[0], tm=bs, seq=1,
                                  sample=True)
    o_p, st_p = _rwkv_chunked(*seqs_p, batch=bp, seq=seq)
    hd_r = RWKV_HEAD_DIM
    wkv_p = st_p.reshape(bp, rh // 2, hd_r, 2, hd_r).transpose(0, 1, 3, 2, 4).reshape(bp, rh, hd_r, hd_r)
    o_s_t, wkv_s_t = _rwkv_step(*[a.T for a in seqs_s], jnp.transpose(state_rwkv_wkv[0], (1, 2, 3, 0)))
    o_s, wkv_s = o_s_t.T, jnp.transpose(wkv_s_t, (3, 0, 1, 2))
    w_o = bf(rwkv_w_o[0])
    r_p, _, k_p, v_p = seqs_p[:4]
    r_s, _, k_s, v_s = seqs_s[:4]
    yp = _rwkv_out(o_p, r_p, k_p, v_p, gp, yp, vec_out, w_o, tm=tm, name="rwkv_out_prompt")
    ys = _rwkv_out(o_s.reshape(bs, d), r_s, k_s, v_s, gs, ys, vec_out, w_o, tm=bs, name="rwkv_out_sample")
    yp, ys = ffn(3, yp, ys)

    return (yp.reshape(bp, seq, d), ys.reshape(bs, 1, d), gm_v_sample, moba_k_prompt, moba_v_prompt,
            moba_k_sample, moba_v_sample, pool_prompt, pool_sample, wkv_p[None], wkv_s[None],
            shp.reshape(1, bp, d), shs[None], jnp.stack(conv_p), jnp.stack(conv_s))
```

```python
import functools
import math

import jax
import jax.numpy as jnp
from jax import lax
from jax.experimental import pallas as pl
from jax.experimental.pallas import tpu as pltpu

F32 = jnp.float32
BF16 = jnp.bfloat16
I32 = jnp.int32

LANES = 128
SUBLANES = 8
VMEM_LIMIT_BYTES = 56 * 2**20

TOKEN_TILE = 512
RWKV_PROJ_TILE = 256

RMS_EPS = 1e-6
GM_LN_EPS = 1e-5
GM_CHUNK = 128
GM_GROUPS = 8
MOBA_HEADS = 16
MOBA_BLOCK = 256
MOBA_TOPK = 3
REL_BUCKETS = 32
REL_MAX_DIST = 128
PAGE_SIZE = 128
POOL_WINDOWS = (2, 4, 8, 16)
POOL_CTX = max(POOL_WINDOWS) - 1
RWKV_HEAD_DIM = 64
RWKV_LNX_EPS = 64e-5
FFN_CONV = 3
NEG_INF = float("-inf")

assert MOBA_BLOCK >= REL_MAX_DIST


def _cparams(*sem):
    return pltpu.CompilerParams(dimension_semantics=sem, vmem_limit_bytes=VMEM_LIMIT_BYTES)


def _whole(shape):
    nd = len(shape)
    return pl.BlockSpec(shape, lambda *_: (0,) * nd, pipeline_mode=pl.Buffered(1))


def _rows(tm, width):
    return pl.BlockSpec((tm, width), lambda i: (i, 0))


def _rms(x, g):
    return x * lax.rsqrt(jnp.mean(x * x, axis=-1, keepdims=True) + RMS_EPS) * g


def _bdot(a, b):
    return jnp.dot(a.astype(BF16), b.astype(BF16), preferred_element_type=F32)


def _nt_dot(a, b):
    return lax.dot_general(a.astype(BF16), b.astype(BF16), (((1,), (1,)), ((), ())),
                           preferred_element_type=F32)


def _split(x):
    hi = x.astype(BF16)
    lo = (x - hi.astype(F32)).astype(BF16)
    return hi, lo


def _split_dot(x, m):
    hi, lo = _split(x)
    return (jnp.dot(hi, m, preferred_element_type=F32)
            + jnp.dot(lo, m, preferred_element_type=F32))


def _same_head(n, head_dim):
    shift = int(math.log2(head_dim))
    r = lax.broadcasted_iota(I32, (n, n), 0) >> shift
    c = lax.broadcasted_iota(I32, (n, n), 1) >> shift
    return jnp.where(r == c, 1.0, 0.0).astype(BF16)


def _seg_sum(x, head_dim):
    g = _same_head(LANES, head_dim)
    parts = [_split_dot(x[:, i:i + LANES], g) for i in range(0, x.shape[1], LANES)]
    return jnp.concatenate(parts, axis=1)


def _gelu(x):
    return 0.5 * x * (1.0 + jnp.tanh(0.7978845608028654 * (x + 0.044715 * x * x * x)))


def _sigmoid(x):
    return 1.0 / (1.0 + jnp.exp(-x))


def _top_mask(gate, idx, axis, n_valid):
    cur = gate
    sel = jnp.zeros(gate.shape, F32)
    for _ in range(MOBA_TOPK):
        m = jnp.max(cur, axis=axis, keepdims=True)
        first = jnp.min(jnp.where(cur == m, idx, n_valid), axis=axis, keepdims=True)
        pick = (idx == first) & (m > NEG_INF)
        sel = jnp.where(pick, 1.0, sel)
        cur = jnp.where(pick, NEG_INF, cur)
    return sel


def _gmlp_kernel(x_ref, ng_ref, win_ref, lng_ref, lnb_ref, sa_ref, sb_ref, wout_ref, y_ref, aux_ref,
                 *, sample):
    x = x_ref[...]
    tm, width = x.shape[0], wout_ref.shape[0]
    h = _rms(x, ng_ref[...])
    z = _gelu(_bdot(h, win_ref[...]))
    u, v = z[:, :width], z[:, width:]
    mu = jnp.mean(v, axis=-1, keepdims=True)
    d = v - mu
    var = jnp.mean(d * d, axis=-1, keepdims=True)
    v = d * lax.rsqrt(var + GM_LN_EPS) * lng_ref[...] + lnb_ref[...]
    if sample:
        aux_ref[...] = v
        s = v * sa_ref[...] + sb_ref[...]
    else:
        gd = width // GM_GROUPS
        causal = (lax.broadcasted_iota(I32, (GM_CHUNK, GM_CHUNK), 0)
                  >= lax.broadcasted_iota(I32, (GM_CHUNK, GM_CHUNK), 1))
        for g in range(GM_GROUPS):
            wg = jnp.where(causal, sa_ref[g], 0.0).astype(BF16)
            cols = slice(g * gd, (g + 1) * gd)
            for c in range(tm // GM_CHUNK):
                rows = slice(c * GM_CHUNK, (c + 1) * GM_CHUNK)
                aux_ref[rows, cols] = (jnp.dot(wg, v[rows, cols].astype(BF16), preferred_element_type=F32)
                                       + sb_ref[:, cols])
        s = aux_ref[...]
    y_ref[...] = x + _bdot(u * s, wout_ref[...])


def _gmlp(x, ng, win, lng, lnb, sa, sb, wout, *, tm, sample):
    n, d = x.shape
    width = wout.shape[0]
    outs = [jax.ShapeDtypeStruct((n, d), F32)]
    out_specs = [_rows(tm, d)]
    scratch = []
    if sample:
        outs.append(jax.ShapeDtypeStruct((n, width), F32))
        out_specs.append(_rows(tm, width))
    else:
        scratch.append(pltpu.VMEM((tm, width), F32))
    return pl.pallas_call(
        functools.partial(_gmlp_kernel, sample=sample),
        out_shape=outs, grid=(n // tm,),
        in_specs=[_rows(tm, d), _whole(ng.shape), _whole(win.shape), _whole(lng.shape), _whole(lnb.shape),
                  _whole(sa.shape), _whole(sb.shape), _whole(wout.shape)],
        out_specs=out_specs, scratch_shapes=scratch,
        compiler_params=_cparams("parallel"), name="gmlp_sample" if sample else "gmlp_prompt",
    )(x, ng, win, lng, lnb, sa, sb, wout)


def _norm_linear_kernel(x_ref, ng_ref, w_ref, *o_refs):
    z = _bdot(_rms(x_ref[...], ng_ref[...]), w_ref[...])
    wd = z.shape[1] // len(o_refs)
    for i, o_ref in enumerate(o_refs):
        o_ref[...] = z[:, i * wd:(i + 1) * wd]


def _norm_linear(x, ng, w, n_out, *, tm, name):
    n, d = x.shape
    wd = w.shape[1] // n_out
    return pl.pallas_call(
        _norm_linear_kernel,
        out_shape=[jax.ShapeDtypeStruct((n, wd), F32)] * n_out, grid=(n // tm,),
        in_specs=[_rows(tm, d), _whole(ng.shape), _whole(w.shape)],
        out_specs=[_rows(tm, wd)] * n_out,
        compiler_params=_cparams("parallel"), name=name,
    )(x, ng, w)


def _qkv_prompt_kernel(x_ref, ng_ref, wq_ref, wkt_ref, wvt_ref, q_ref, kt_ref, vt_ref):
    h = _rms(x_ref[...], ng_ref[...]).astype(BF16)
    q_ref[...] = jnp.dot(h, wq_ref[...], preferred_element_type=F32)
    kt_ref[0] = _nt_dot(wkt_ref[...], h)
    vt_ref[0] = _nt_dot(wvt_ref[...], h)


def _qkv_prompt(x, ng, wq, wkt, wvt, *, tm, batch, seq):
    n, d = x.shape
    tps = seq // tm
    t_spec = pl.BlockSpec((1, d, tm), lambda i: (i // tps, 0, i % tps))
    return pl.pallas_call(
        _qkv_prompt_kernel,
        out_shape=[jax.ShapeDtypeStruct((n, d), F32)] + [jax.ShapeDtypeStruct((batch, d, seq), F32)] * 2,
        grid=(n // tm,),
        in_specs=[_rows(tm, d), _whole(ng.shape), _whole(wq.shape), _whole(wkt.shape), _whole(wvt.shape)],
        out_specs=[_rows(tm, d), t_spec, t_spec],
        compiler_params=_cparams("parallel"), name="moba_qkv_prompt",
    )(x, ng, wq, wkt, wvt)


def _linear_res_kernel(a_ref, w_ref, y_ref, o_ref):
    o_ref[...] = y_ref[...] + _bdot(a_ref[...], w_ref[...])


def _linear_res(a, w, y, *, tm, name):
    n, d = y.shape
    return pl.pallas_call(
        _linear_res_kernel, out_shape=jax.ShapeDtypeStruct((n, d), F32), grid=(n // tm,),
        in_specs=[_rows(tm, a.shape[1]), _whole(w.shape), _rows(tm, d)], out_specs=_rows(tm, d),
        compiler_params=_cparams("parallel"), name=name,
    )(a, w, y)


def _t5_bucket_table(rel):
    n = jnp.maximum(rel, 0)
    exact = REL_BUCKETS // 2
    nf = jnp.maximum(n, 1).astype(F32)
    large = exact + (jnp.log(nf / exact) / math.log(REL_MAX_DIST / exact)
                     * (REL_BUCKETS - exact)).astype(I32)
    return jnp.where(n < exact, n, jnp.minimum(large, REL_BUCKETS - 1)).astype(I32)


def _bias_tiles_kernel(rb_ref, bkt_ref, o_ref):
    h = pl.program_id(0)
    bkt = bkt_ref[...]
    acc = jnp.zeros(bkt.shape, F32)
    for b in range(REL_BUCKETS):
        acc = jnp.where(bkt == b, rb_ref[b, h], acc)
    future = (lax.broadcasted_iota(I32, bkt.shape, 2) > lax.broadcasted_iota(I32, bkt.shape, 1))
    own_block = lax.broadcasted_iota(I32, bkt.shape, 0) == 0
    o_ref[0] = jnp.where(future & own_block, NEG_INF, acc)


def _bias_tiles(rel_bias, bkt):
    heads = rel_bias.shape[1]
    return pl.pallas_call(
        _bias_tiles_kernel, out_shape=jax.ShapeDtypeStruct((heads,) + bkt.shape, F32), grid=(heads,),
        in_specs=[pl.BlockSpec(memory_space=pltpu.SMEM), _whole(bkt.shape)],
        out_specs=pl.BlockSpec((1,) + bkt.shape, lambda h: (h, 0, 0, 0)),
        compiler_params=_cparams("parallel"), name="moba_bias_tiles",
    )(rel_bias, bkt)


def _bias_rows_kernel(rb_ref, bkt_ref, o_ref):
    bkt = bkt_ref[...]
    acc = jnp.zeros(bkt.shape, F32)
    for b in range(REL_BUCKETS):
        acc = jnp.where(bkt == b, rb_ref[b:b + 1, :], acc)
    o_ref[...] = acc


def _bias_rows(rb_pad, bkt):
    return pl.pallas_call(
        _bias_rows_kernel, out_shape=jax.ShapeDtypeStruct(bkt.shape, F32), grid=(1,),
        in_specs=[_whole(rb_pad.shape), _whole(bkt.shape)], out_specs=_whole(bkt.shape),
        compiler_params=_cparams("arbitrary"), name="moba_bias_rows",
    )(rb_pad, bkt)


def _moba_prompt_kernel(rb_ref, q_ref, k_ref, v_ref, bias_ref, o_ref, kb_scr, vh_scr, kmean_scr, *, head_dim):
    pair = pl.program_id(1)
    blk = MOBA_BLOCK
    seq = k_ref.shape[2]
    n_blocks = seq // blk
    assert LANES == 2 * head_dim
    shift = int(math.log2(head_dim))
    scale = head_dim ** -0.5
    assert shift % 2 == 0 and n_blocks <= kmean_scr.shape[0]

    def prepare():
        kt, vt = k_ref[0], v_ref[0]
        second = (lax.broadcasted_iota(I32, (LANES, seq), 0) >> shift) == 1
        kb_scr[...] = kt.astype(BF16)
        vh_scr[0] = jnp.where(second, 0.0, vt).astype(BF16)
        vh_scr[1] = jnp.where(second, vt, 0.0).astype(BF16)
        rows = kmean_scr.shape[0]
        in_block = jnp.where((lax.broadcasted_iota(I32, (rows, seq), 1) >> int(math.log2(blk)))
                             == lax.broadcasted_iota(I32, (rows, seq), 0), 1.0, 0.0).astype(BF16)
        k_hi, k_lo = _split(kt)
        nt = lambda a, b: lax.dot_general(a, b, (((1,), (1,)), ((), ())), preferred_element_type=F32)
        kmean_scr[...] = (nt(in_block, k_hi) + nt(in_block, k_lo)) * (1.0 / blk)

    prepare()
    second = (lax.broadcasted_iota(I32, (blk, LANES), 1) >> shift) == 1
    km_hi, km_lo = _split(kmean_scr[...])
    nt = lambda a, b: lax.dot_general(a, b, (((1,), (1,)), ((), ())), preferred_element_type=F32)
    block_id = lax.broadcasted_iota(I32, (kmean_scr.shape[0], 2 * blk), 0)
    far_bias = [rb_ref[REL_BUCKETS - 1, pair * 2 + hh] for hh in range(2)]

    def select(own):
        q = q_ref[own * blk:(own + 1) * blk, :]
        q2 = jnp.concatenate([jnp.where(second, 0.0, q), jnp.where(second, q, 0.0)], axis=0)
        q_hi, q_lo = _split(q2)
        gate = nt(km_hi, q_hi) + nt(km_lo, q_hi) + nt(km_hi, q_lo)
        sel = _top_mask(jnp.where(block_id < own, gate, NEG_INF), block_id, 0, gate.shape[0]).T
        return (q2 * scale).astype(BF16), sel

    def scores(own, q_scaled):
        return jnp.dot(q_scaled, kb_scr[:, 0:(own + 1) * blk], preferred_element_type=F32)

    def mask(own, s, sel):
        head_rows = []
        for hh in range(2):
            rows = slice(hh * blk, (hh + 1) * blk)
            tiles = []
            for n in range(own + 1):
                t = s[rows, n * blk:(n + 1) * blk]
                chosen = sel[rows, n:n + 1] > 0.0
                if n == own:
                    t = t + bias_ref[hh, 0]
                elif n == own - 1:
                    t = jnp.where(chosen, t + bias_ref[hh, 1], NEG_INF)
                else:
                    t = t + jnp.where(chosen, far_bias[hh], NEG_INF)
                tiles.append(t)
            head_rows.append(jnp.concatenate(tiles, axis=1))
        return jnp.concatenate(head_rows, axis=0)

    def weights(s):
        p = jnp.exp(s - jnp.max(s, axis=-1, keepdims=True))
        return p, jnp.sum(p, axis=-1, keepdims=True)

    def output(own, p, l):
        keys = (own + 1) * blk
        p_pair = jnp.concatenate([p[:blk], p[blk:]], axis=1).astype(BF16)
        v_pair = jnp.concatenate([vh_scr[0, :, 0:keys], vh_scr[1, :, 0:keys]], axis=1)
        o_ref[own * blk:(own + 1) * blk, :] = nt(p_pair, v_pair) / jnp.where(second, l[blk:], l[:blk])

    order = list(range(n_blocks))
    groups = [(order[i], order[-1 - i]) for i in range(n_blocks // 2)] + ([(order[n_blocks // 2],)] * (n_blocks % 2))
    for group in groups:
        picked = [select(own) for own in group]
        raw = [scores(own, qs) for own, (qs, _) in zip(group, picked)]
        masked = [mask(own, s, sel) for own, s, (_, sel) in zip(group, raw, picked)]
        probs = [weights(s) for s in masked]
        for own, (p, l) in zip(group, probs):
            output(own, p, l)


def _moba_prompt(rel_bias, q, k, v, bias_tiles, *, batch, seq, head_dim):
    n, d = q.shape
    blk = MOBA_BLOCK
    hpt = LANES // head_dim
    return pl.pallas_call(
        functools.partial(_moba_prompt_kernel, head_dim=head_dim),
        out_shape=jax.ShapeDtypeStruct((n, d), F32), grid=(batch, d // LANES),
        in_specs=[pl.BlockSpec(memory_space=pltpu.SMEM),
                  pl.BlockSpec((seq, LANES), lambda b, p: (b, p)),
                  pl.BlockSpec((1, LANES, seq), lambda b, p: (b, p, 0)),
                  pl.BlockSpec((1, LANES, seq), lambda b, p: (b, p, 0)),
                  pl.BlockSpec((hpt, 2, blk, blk), lambda b, p: (p, 0, 0, 0))],
        out_specs=pl.BlockSpec((seq, LANES), lambda b, p: (b, p)),
        scratch_shapes=[pltpu.VMEM((LANES, seq), BF16), pltpu.VMEM((hpt, LANES, seq), BF16),
                        pltpu.VMEM((2 * SUBLANES, LANES), F32)],
        compiler_params=_cparams("parallel", "parallel"), name="moba_prompt_attn",
    )(rel_bias, q, k, v, bias_tiles)


def _moba_sample_kernel(pt_ref, q_ref, kn_ref, vn_ref, *rest, n_pages, head_dim):
    del pt_ref
    k_refs, v_refs = rest[:n_pages], rest[n_pages:2 * n_pages]
    bias_ref, rb0_ref, o_ref = rest[2 * n_pages:]
    d = q_ref.shape[2]
    heads = d // head_dim
    shift = int(math.log2(head_dim))
    per_block = MOBA_BLOCK // PAGE_SIZE
    blocks = range(n_pages // per_block)
    scale = head_dim ** -0.5
    head_lanes = jnp.where((lax.broadcasted_iota(I32, (heads, d), 1) >> shift)
                           == lax.broadcasted_iota(I32, (heads, d), 0), 1.0, 0.0)
    q = head_lanes * q_ref[0]
    q_hi, q_lo = _split(q)
    q2 = jnp.concatenate([q_hi, q_lo], axis=0)
    block_of = lambda refs, n: jnp.concatenate(
        [refs[per_block * n + j][0, 0].reshape(d, PAGE_SIZE) for j in range(per_block)], axis=1).astype(BF16)
    raw = [jnp.dot(q2, block_of(k_refs, n), preferred_element_type=F32) for n in blocks]
    raw = [r[:heads] + r[heads:] for r in raw]
    block_id = lax.broadcasted_iota(I32, (heads, LANES), 1)
    gate = jnp.full((heads, LANES), NEG_INF, F32)
    for n in blocks:
        gate = jnp.where(block_id == n, jnp.sum(raw[n], axis=-1, keepdims=True), gate)
    sel = _top_mask(gate, block_id, -1, LANES)
    s_new = jnp.sum(q * kn_ref[0], axis=-1, keepdims=True) * scale + rb0_ref[:, 0:1]
    s = [raw[n] * scale + bias_ref[n] for n in blocks]
    m = s_new
    for n in blocks:
        m = jnp.maximum(m, jnp.where(sel[:, n:n + 1] > 0.0, jnp.max(s[n], axis=-1, keepdims=True), NEG_INF))
    p = [jnp.exp(jnp.where(sel[:, n:n + 1] > 0.0, s[n] - m, NEG_INF)) for n in blocks]
    p_new = jnp.exp(s_new - m)
    den = p_new
    for n in blocks:
        den = den + jnp.sum(p[n], axis=-1, keepdims=True)
    p_all = jnp.concatenate(p, axis=1).astype(BF16)
    v_all = jnp.concatenate([block_of(v_refs, n) for n in blocks], axis=1)
    acc = lax.dot_general(p_all, v_all, (((1,), (1,)), ((), ())), preferred_element_type=F32)
    out = (acc + p_new * vn_ref[0]) * head_lanes / den
    o_ref[0] = jnp.sum(out, axis=0, keepdims=True)


def _moba_sample(page_table, q, k_new, v_new, cache_k, cache_v, layer, bias_rows, rel_bias, *, head_dim):
    nb, d = q.shape
    n_pages = page_table.shape[1]
    n_blocks = n_pages * PAGE_SIZE // MOBA_BLOCK
    heads = d // head_dim
    cache_kt = jnp.transpose(cache_k, (0, 1, 3, 4, 2))
    cache_vt = jnp.transpose(cache_v, (0, 1, 3, 4, 2))
    row = lambda x: x.reshape(nb, 1, d)
    vec = pl.BlockSpec((1, 1, d), lambda s, pt: (s, 0, 0))
    page = lambda j: pl.BlockSpec((1, 1, heads, head_dim, PAGE_SIZE), lambda s, pt: (layer, pt[s, j], 0, 0, 0))
    const = lambda shape: pl.BlockSpec(shape, lambda s, pt: (0,) * len(shape))
    bias = bias_rows[:, :heads].reshape(n_blocks, MOBA_BLOCK, heads).transpose(0, 2, 1)
    rb0 = jnp.broadcast_to(rel_bias[0][:, None], (heads, LANES))
    pages = [page(j) for j in range(n_pages)]
    out = pl.pallas_call(
        functools.partial(_moba_sample_kernel, n_pages=n_pages, head_dim=head_dim),
        out_shape=jax.ShapeDtypeStruct((nb, 1, d), F32),
        grid_spec=pltpu.PrefetchScalarGridSpec(
            num_scalar_prefetch=1, grid=(nb,),
            in_specs=[vec, vec, vec, *pages, *pages, const(bias.shape), const(rb0.shape)],
            out_specs=vec),
        compiler_params=_cparams("parallel"), name="moba_sample_attn",
    )(page_table, row(q), row(k_new), row(v_new), *([cache_kt] * n_pages), *([cache_vt] * n_pages), bias, rb0)
    return out.reshape(nb, d)


POOL_PAD = 32


def _pool_prompt_kernel(y_ref, ng_ref, w_ref, sc_ref, yout_ref, st_ref, b0, b1, b2, b3, b4, *, tps):
    i = pl.program_id(0)
    y = y_ref[...]
    tm, d = y.shape
    gd = d // len(POOL_WINDOWS)
    h = _rms(y, ng_ref[...])
    half = POOL_PAD // 2

    @pl.when(i == 0)
    def _():
        for b in (b0, b1, b2, b3, b4):
            b[0:half, :] = jnp.zeros((half, b.shape[1]), F32)

    @pl.when(i % tps == 0)
    def _():
        b0[half:POOL_PAD, :] = jnp.zeros((half, d), F32)

    b0[POOL_PAD:, :] = h
    stages = (b0, b1, b2, b3, b4)
    for k in range(1, len(stages)):
        src, dst = stages[k - 1], stages[k]
        back = 2 ** (k - 1)
        off = src.shape[1] - dst.shape[1]
        dst[SUBLANES:, :] = (src[SUBLANES:, off:] + src[SUBLANES - back:tm + POOL_PAD - back, off:])
    pos = (i % tps) * tm + lax.broadcasted_iota(I32, (tm, 1), 0) + 1
    for g, win in enumerate(POOL_WINDOWS):
        cols = slice(g * gd, (g + 1) * gd)
        wsum = stages[g + 1][POOL_PAD:, 0:gd]
        cnt = jnp.minimum(pos, win).astype(F32)
        mixed = _bdot(wsum / cnt - h[:, cols], w_ref[g])
        yout_ref[:, cols] = y[:, cols] + mixed * sc_ref[:, cols]
    st_ref[0] = b0[tm + POOL_PAD - POOL_CTX:, :]
    b0[half:POOL_PAD, :] = b0[tm + half:, :]


def _pool_prompt(y, ng, w, sc, *, tm, seq):
    n, d = y.shape
    gd = d // len(POOL_WINDOWS)
    tps = seq // tm
    rows = tm + POOL_PAD
    return pl.pallas_call(
        functools.partial(_pool_prompt_kernel, tps=tps),
        out_shape=[jax.ShapeDtypeStruct((n, d), F32), jax.ShapeDtypeStruct((n // seq, POOL_CTX, d), F32)],
        grid=(n // tm,),
        in_specs=[_rows(tm, d), _whole(ng.shape), _whole(w.shape), _whole(sc.shape)],
        out_specs=[_rows(tm, d), pl.BlockSpec((1, POOL_CTX, d), lambda i: (i // tps, 0, 0))],
        scratch_shapes=[pltpu.VMEM((rows, d - k * gd), F32) for k in (0, 0, 1, 2, 3)],
        compiler_params=_cparams("arbitrary"), name="pool_prompt",
    )(y, ng, w, sc)


def _pool_sample_kernel(y_ref, ng_ref, prev_ref, w_ref, sc_ref, yout_ref, h_ref):
    y = y_ref[...]
    d = y.shape[1]
    gd = d // len(POOL_WINDOWS)
    h = _rms(y, ng_ref[...])
    h_ref[...] = h
    for g, win in enumerate(POOL_WINDOWS):
        cols = slice(g * gd, (g + 1) * gd)
        wsum = h[:, cols]
        for back in range(1, win):
            wsum = wsum + prev_ref[POOL_CTX - back, :, cols]
        mixed = _bdot(wsum / float(win) - h[:, cols], w_ref[g])
        yout_ref[:, cols] = y[:, cols] + mixed * sc_ref[:, cols]


def _pool_sample(y, ng, prev_t, w, sc):
    n, d = y.shape
    return pl.pallas_call(
        _pool_sample_kernel, out_shape=[jax.ShapeDtypeStruct((n, d), F32)] * 2, grid=(1,),
        in_specs=[_whole(y.shape), _whole(ng.shape), _whole(prev_t.shape), _whole(w.shape), _whole(sc.shape)],
        out_specs=[_whole(y.shape)] * 2,
        compiler_params=_cparams("arbitrary"), name="pool_sample",
    )(y, ng, prev_t, w, sc)


def _rwkv_proj_kernel(y_ref, ng_ref, mu_ref, wr_ref, wk_ref, wv_ref, w1_ref, w2_ref, a1_ref, a2_ref,
                      g1_ref, g2_ref, vec_ref, *rest, tps, sample):
    if sample:
        prev_ref, r_o, w_o, k_o, v_o, kk_o, kka_o, g_o, sh_o = rest
    else:
        r_o, w_o, k_o, v_o, kk_o, kka_o, g_o, sh_o, hs_scr = rest
    y = y_ref[...]
    tm = y.shape[0]
    h = _rms(y, ng_ref[...])
    if sample:
        h_prev = prev_ref[...]
        sh_o[...] = h
    else:
        i = pl.program_id(0)

        @pl.when(i % tps == 0)
        def _():
            hs_scr[0:SUBLANES, :] = jnp.zeros((SUBLANES, y.shape[1]), F32)

        hs_scr[SUBLANES:, :] = h
        h_prev = hs_scr[SUBLANES - 1:tm + SUBLANES - 1, :]
        hs_scr[0:SUBLANES, :] = h[tm - SUBLANES:, :]
        sh_o[0] = h[tm - 1:tm, :]
    xx = h_prev - h
    mix = lambda m: h + xx * mu_ref[m:m + 1, :]
    r = _bdot(mix(0), wr_ref[...])
    k = _bdot(mix(2), wk_ref[...])
    v = _bdot(mix(3), wv_ref[...])
    z = vec_ref[0:1, :] + _bdot(jnp.tanh(_bdot(mix(1), w1_ref[...])), w2_ref[...])
    w_log = jnp.minimum(z, 0.0) - jnp.log(1.0 + jnp.exp(-jnp.abs(z))) - 0.5
    a = _sigmoid(vec_ref[1:2, :] + _bdot(_bdot(mix(4), a1_ref[...]), a2_ref[...]))
    g_o[...] = _bdot(_sigmoid(_bdot(mix(5), g1_ref[...])), g2_ref[...])
    kk = k * vec_ref[2:3, :]
    kk = kk / jnp.maximum(jnp.sqrt(_seg_sum(kk * kk, RWKV_HEAD_DIM)), 1e-12)
    r_o[...] = r
    w_o[...] = -jnp.exp(w_log)
    k_o[...] = k * (1.0 + (a - 1.0) * vec_ref[3:4, :])
    v_o[...] = v
    kk_o[...] = kk
    kka_o[...] = kk * a


def _rwkv_proj(y, ng, mu, mats, vec, prev, *, tm, seq, sample):
    n, d = y.shape
    tps = max(seq // tm, 1)
    ins = [y, ng, mu, *mats, vec]
    in_specs = [_rows(tm, d), _whole(ng.shape), _whole(mu.shape), *[_whole(m.shape) for m in mats],
                _whole(vec.shape)]
    outs = [jax.ShapeDtypeStruct((n, d), F32)] * 7
    out_specs = [_rows(tm, d)] * 7
    scratch = []
    if sample:
        ins.append(prev)
        in_specs.append(_rows(tm, d))
        outs.append(jax.ShapeDtypeStruct((n, d), F32))
        out_specs.append(_rows(tm, d))
    else:
        outs.append(jax.ShapeDtypeStruct((n // seq, 1, d), F32))
        out_specs.append(pl.BlockSpec((1, 1, d), lambda i: (i // tps, 0, 0)))
        scratch.append(pltpu.VMEM((tm + SUBLANES, d), F32))
    return pl.pallas_call(
        functools.partial(_rwkv_proj_kernel, tps=tps, sample=sample),
        out_shape=outs, grid=(n // tm,), in_specs=in_specs, out_specs=out_specs, scratch_shapes=scratch,
        compiler_params=_cparams("arbitrary"), name="rwkv_proj_sample" if sample else "rwkv_proj_prompt",
    )(*ins)


RWKV_CHUNK = 128


def _rwkv_chunk_kernel(r_ref, lw_ref, k_ref, v_ref, kk_ref, kka_ref, o_ref, st_ref, t_scr):
    c = pl.program_id(1)
    ch, d = r_ref.shape
    hd = RWKV_HEAD_DIM
    shift = int(math.log2(hd))
    n_pairs = d // LANES

    @pl.when(c == 0)
    def _():
        t_scr[...] = jnp.zeros(t_scr.shape, F32)

    ri = lax.broadcasted_iota(I32, (ch, ch), 0)
    ci = lax.broadcasted_iota(I32, (ch, ch), 1)
    incl = ci <= ri
    strict = ci < ri
    incl2 = jnp.concatenate([incl, incl], axis=1)
    strict2 = jnp.concatenate([strict, strict], axis=1)
    same_head = (ri >> shift) == (ci >> shift)
    eye = ri == ci
    head1 = (lax.broadcasted_iota(I32, (ch, LANES), 1) >> shift) == 1
    head1_wide = jnp.concatenate([head1, head1], axis=1)

    def by_head(x, mask):
        return jnp.concatenate([jnp.where(mask, 0.0, x), jnp.where(mask, x, 0.0)], axis=0).astype(BF16)

    lw = lw_ref[...]
    tri = jnp.where(incl, 1.0, 0.0).astype(BF16)
    p1 = lw.astype(BF16)
    rem = lw - p1.astype(F32)
    p2 = rem.astype(BF16)
    p3 = (rem - p2.astype(F32)).astype(BF16)
    cl = (jnp.dot(tri, p1, preferred_element_type=F32) + jnp.dot(tri, p2, preferred_element_type=F32)
          + jnp.dot(tri, p3, preferred_element_type=F32))
    mid = cl[ch // 2 - 1:ch // 2, :]
    last = cl[ch - 1:ch, :]
    e_mid = jnp.exp(mid)
    e_neg = jnp.exp(mid - cl)
    e_tail = jnp.exp(last - cl)
    p_last = jnp.exp(last)
    kk, kka, k = kk_ref[...], kka_ref[...], k_ref[...]
    a_s = -kk * jnp.exp(cl - lw - mid)
    r_s = r_ref[...] * jnp.exp(cl - mid)
    b_s = kka * e_neg
    k_s = k * e_neg
    b_t = kka * e_tail
    k_t = k * e_tail
    v = v_ref[...]
    zero = jnp.zeros((ch, ch), BF16)

    pairs = range(n_pairs)
    cols = [slice(p * LANES, (p + 1) * LANES) for p in pairs]
    mm = lambda a, b: jnp.dot(a, b, preferred_element_type=F32)
    lane_cat = lambda a, b: jnp.concatenate([a, b], axis=1)
    a_p = [a_s[:, c] for c in cols]
    r_p = [r_s[:, c] for c in cols]
    v_p = [v[:, c] for c in cols]
    gram = []
    for p in pairs:
        lhs = jnp.concatenate([jnp.where(head1, 0.0, a_p[p]), jnp.where(head1, 0.0, r_p[p]),
                               jnp.where(head1, a_p[p], 0.0), jnp.where(head1, r_p[p], 0.0)], axis=0)
        rhs = lane_cat(b_s[:, cols[p]].T, k_s[:, cols[p]].T)
        gram.append(mm(lhs.astype(BF16), rhs.astype(BF16)))
    ab0 = [jnp.where(strict2, g_[0:ch], 0.0) for g_ in gram]
    rb0 = [jnp.where(incl2, g_[ch:2 * ch], 0.0) for g_ in gram]
    ab1 = [jnp.where(strict2, g_[2 * ch:3 * ch], 0.0) for g_ in gram]
    rb1 = [jnp.where(incl2, g_[3 * ch:], 0.0) for g_ in gram]
    v_heads = [by_head(v_p[p], head1) for p in pairs]
    x = [lane_cat(a_p[p], mm(lane_cat(ab0[p][:, ch:], ab1[p][:, ch:]).astype(BF16), v_heads[p])) for p in pairs]
    pw = [lane_cat(ab0[p][:, :ch], ab1[p][:, :ch]).astype(BF16) for p in pairs]
    n_sq = int(math.log2(ch))
    for it in range(n_sq):
        x = [x[p] + mm(pw[p], by_head(x[p], head1_wide)) for p in pairs]
        if it < n_sq - 1:
            pw = [mm(pw[p], jnp.concatenate([lane_cat(pw[p][:, :ch], zero), lane_cat(zero, pw[p][:, ch:])],
                                            axis=0)).astype(BF16) for p in pairs]
    x = [lane_cat(x[p][:, :ch] * e_mid[:, cols[p]], x[p][:, ch:]) for p in pairs]
    qo = [mm(lane_cat(rb0[p][:, :ch], rb1[p][:, :ch]).astype(BF16), by_head(x[p], head1_wide)) for p in pairs]
    o_intra = [qo[p][:, ch:] + mm(lane_cat(rb0[p][:, ch:], rb1[p][:, ch:]).astype(BF16), v_heads[p])
               for p in pairs]
    q = [r_p[p] * e_mid[:, cols[p]] + qo[p][:, :ch] for p in pairs]
    gh = [mm(b_t[:, cols[p]].T.astype(BF16), x[p].astype(BF16)) for p in pairs]
    g = [jnp.where(same_head, gh[p][:, :ch], 0.0) + jnp.where(eye, p_last[:, cols[p]], 0.0) for p in pairs]
    h = [jnp.where(same_head, gh[p][:, ch:] + _bdot(k_t[:, cols[p]].T, v_p[p]), 0.0) for p in pairs]
    for p in pairs:
        t_old = t_scr[p].astype(BF16)
        o_ref[:, cols[p]] = mm(q[p].astype(BF16), t_old) + o_intra[p]
        t_scr[p] = mm(g[p].astype(BF16), t_old) + h[p]

    @pl.when(c == pl.num_programs(1) - 1)
    def _():
        for p in range(n_pairs):
            s_pair = t_scr[p].T
            st_ref[0, p] = s_pair[:hd, :] + s_pair[hd:, :]


def _rwkv_chunked(r, lw, k, v, kk, kka, *, batch, seq):
    n, d = r.shape
    ch = RWKV_CHUNK
    assert ch == LANES and seq % ch == 0
    n_chunks = seq // ch
    n_pairs = d // LANES
    blk = pl.BlockSpec((ch, d), lambda b, c: (b * n_chunks + c, 0))
    return pl.pallas_call(
        _rwkv_chunk_kernel,
        out_shape=[jax.ShapeDtypeStruct((n, d), F32),
                   jax.ShapeDtypeStruct((batch, n_pairs, RWKV_HEAD_DIM, LANES), F32)],
        grid=(batch, n_chunks), in_specs=[blk] * 6,
        out_specs=[blk, pl.BlockSpec((1, n_pairs, RWKV_HEAD_DIM, LANES), lambda b, c: (b, 0, 0, 0))],
        scratch_shapes=[pltpu.VMEM((n_pairs, LANES, LANES), F32)],
        compiler_params=_cparams("parallel", "arbitrary"), name="rwkv_chunked",
    )(r, lw, k, v, kk, kka)


def _rwkv_step_kernel(r_ref, lw_ref, k_ref, v_ref, kk_ref, kka_ref, s0_ref, o_ref, st_ref):
    hd = s0_ref.shape[1]
    w = jnp.exp(lw_ref[...])
    kk, kka, k, r = kk_ref[...], kka_ref[...], k_ref[...], r_ref[...]

    def value_row(i, carry):
        s = s0_ref[0, i]
        s_kk = jnp.sum(s * kk, axis=0, keepdims=True)
        s = s * w - s_kk * kka + v_ref[pl.ds(i, 1), :] * k
        st_ref[0, i] = s
        o_ref[pl.ds(i, 1), :] = jnp.sum(s * r, axis=0, keepdims=True)
        return carry

    lax.fori_loop(0, hd, value_row, 0, unroll=8)


def _rwkv_step(r, lw, k, v, kk, kka, s0):
    d, n = r.shape
    heads, hd = s0.shape[0], s0.shape[1]
    vec = pl.BlockSpec((hd, n), lambda h: (h, 0))
    st = pl.BlockSpec((1, hd, hd, n), lambda h: (h, 0, 0, 0))
    return pl.pallas_call(
        _rwkv_step_kernel,
        out_shape=[jax.ShapeDtypeStruct((d, n), F32), jax.ShapeDtypeStruct(s0.shape, F32)],
        grid=(heads,), in_specs=[vec] * 6 + [st], out_specs=[vec, st],
        compiler_params=_cparams("parallel"), name="rwkv_step",
    )(r, lw, k, v, kk, kka, s0)


def _rwkv_out_kernel(o_ref, r_ref, k_ref, v_ref, g_ref, y_ref, vec_ref, wo_ref, yout_ref):
    hd = RWKV_HEAD_DIM
    o = o_ref[...]
    mu = _seg_sum(o, hd) * (1.0 / hd)
    dlt = o - mu
    var = _seg_sum(dlt * dlt, hd) * (1.0 / hd)
    o = dlt * lax.rsqrt(var + RWKV_LNX_EPS) * vec_ref[1:2, :] + vec_ref[2:3, :]
    o = o + _seg_sum(r_ref[...] * k_ref[...] * vec_ref[0:1, :], hd) * v_ref[...]
    yout_ref[...] = y_ref[...] + _bdot(o * g_ref[...], wo_ref[...])


def _rwkv_out(o, r, k, v, g, y, vec, wo, *, tm, name):
    n, d = y.shape
    return pl.pallas_call(
        _rwkv_out_kernel, out_shape=jax.ShapeDtypeStruct((n, d), F32), grid=(n // tm,),
        in_specs=[_rows(tm, d)] * 6 + [_whole(vec.shape), _whole(wo.shape)], out_specs=_rows(tm, d),
        compiler_params=_cparams("parallel"), name=name,
    )(o, r, k, v, g, y, vec, wo)


FFN_CHUNK = 256


def _ffn_act(gate, g1, g2, up, cw, cb):
    return _gelu(cw[0:1] * g2 + cw[1:2] * g1 + cw[2:3] * gate + cb) * up


def _ffn_prompt_kernel(y_ref, ng_ref, win_ref, cw_ref, cb_ref, wout_ref, *rest, tps, final):
    if final:
        fg_ref, yout_ref, st_ref, yfin_ref, h_scr, gs_scr, act_scr, carry_scr = rest
    else:
        yout_ref, st_ref, h_scr, gs_scr, act_scr, carry_scr = rest
    i = pl.program_id(0)
    y = y_ref[...]
    tm = y.shape[0]
    hid = wout_ref.shape[0]
    tf = FFN_CHUNK
    h_scr[...] = _rms(y, ng_ref[...]).astype(BF16)

    @pl.when(i % tps == 0)
    def _():
        carry_scr[...] = jnp.zeros(carry_scr.shape, F32)

    for j in range(hid // tf):
        cols = slice(j * tf, (j + 1) * tf)
        hb = h_scr[...]
        gs_scr[0:SUBLANES, :] = carry_scr[:, cols]
        gs_scr[SUBLANES:, :] = jnp.dot(hb, win_ref[:, cols], preferred_element_type=F32)
        up = jnp.dot(hb, win_ref[:, hid + j * tf:hid + (j + 1) * tf], preferred_element_type=F32)
        gate = gs_scr[SUBLANES:, :]
        carry_scr[:, cols] = gate[tm - SUBLANES:, :]
        st_ref[0, :, cols] = gate[tm - (FFN_CONV - 1):, :]
        act = _ffn_act(gate, gs_scr[SUBLANES - 1:tm + SUBLANES - 1, :], gs_scr[SUBLANES - 2:tm + SUBLANES - 2, :],
                       up, cw_ref[:, cols], cb_ref[:, cols])
        act_scr[:, cols] = act.astype(BF16)
    y_new = y + jnp.dot(act_scr[...], wout_ref[...], preferred_element_type=F32)
    yout_ref[...] = y_new
    if final:
        yfin_ref[...] = _rms(y_new, fg_ref[...])


def _ffn_prompt(y, ng, win, cw, cb, wout, fg, *, tm, seq):
    n, d = y.shape
    hid = wout.shape[0]
    tps = seq // tm
    final = fg is not None
    ins = [y, ng, win, cw, cb, wout]
    in_specs = [_rows(tm, d)] + [_whole(a.shape) for a in ins[1:]]
    outs = [jax.ShapeDtypeStruct((n, d), F32), jax.ShapeDtypeStruct((n // seq, FFN_CONV - 1, hid), F32)]
    out_specs = [_rows(tm, d), pl.BlockSpec((1, FFN_CONV - 1, hid), lambda i: (i // tps, 0, 0))]
    if final:
        ins.append(fg)
        in_specs.append(_whole(fg.shape))
        outs.append(jax.ShapeDtypeStruct((n, d), F32))
        out_specs.append(_rows(tm, d))
    return pl.pallas_call(
        functools.partial(_ffn_prompt_kernel, tps=tps, final=final),
        out_shape=outs, grid=(n // tm,), in_specs=in_specs, out_specs=out_specs,
        scratch_shapes=[pltpu.VMEM((tm, d), BF16), pltpu.VMEM((tm + SUBLANES, FFN_CHUNK), F32),
                        pltpu.VMEM((tm, hid), BF16), pltpu.VMEM((SUBLANES, hid), F32)],
        compiler_params=_cparams("arbitrary"), name="ffn_prompt",
    )(*ins)


def _ffn_sample_kernel(y_ref, ng_ref, wg_ref, wu_ref, cw_ref, cb_ref, p2_ref, p1_ref, wout_ref, *rest, final):
    if final:
        fg_ref, yout_ref, gate_ref, yfin_ref, h_scr, acc_scr = rest
    else:
        yout_ref, gate_ref, h_scr, acc_scr = rest
    j = pl.program_id(0)

    @pl.when(j == 0)
    def _():
        h_scr[...] = _rms(y_ref[...], ng_ref[...]).astype(BF16)
        acc_scr[...] = jnp.zeros(acc_scr.shape, F32)

    hb = h_scr[...]
    gate = jnp.dot(hb, wg_ref[...], preferred_element_type=F32)
    up = jnp.dot(hb, wu_ref[...], preferred_element_type=F32)
    gate_ref[...] = gate
    act = _ffn_act(gate, p1_ref[...], p2_ref[...], up, cw_ref[...], cb_ref[...])
    acc_scr[...] += _bdot(act, wout_ref[...])

    @pl.when(j == pl.num_programs(0) - 1)
    def _():
        y_new = y_ref[...] + acc_scr[...]
        yout_ref[...] = y_new
        if final:
            yfin_ref[...] = _rms(y_new, fg_ref[...])


def _ffn_sample(y, ng, win, cw, cb, p2, p1, wout, fg):
    n, d = y.shape
    hid = wout.shape[0]
    tf = hid // 2 if (hid // 2) % LANES == 0 else FFN_CHUNK
    nf = hid // tf
    final = fg is not None
    keep = lambda shape: pl.BlockSpec(shape, lambda j: (0,) * len(shape))
    chunk = lambda rows: pl.BlockSpec((rows, tf), lambda j: (0, j))
    ins = [y, ng, win, win, cw, cb, p2, p1, wout]
    in_specs = [keep((n, d)), keep(ng.shape), chunk(d), pl.BlockSpec((d, tf), lambda j: (0, nf + j)),
                chunk(FFN_CONV), chunk(1), chunk(n), chunk(n), pl.BlockSpec((tf, d), lambda j: (j, 0))]
    outs = [jax.ShapeDtypeStruct((n, d), F32), jax.ShapeDtypeStruct((n, hid), F32)]
    out_specs = [keep((n, d)), chunk(n)]
    if final:
        ins.append(fg)
        in_specs.append(keep(fg.shape))
        outs.append(jax.ShapeDtypeStruct((n, d), F32))
        out_specs.append(keep((n, d)))
    return pl.pallas_call(
        functools.partial(_ffn_sample_kernel, final=final),
        out_shape=outs, grid=(nf,), in_specs=in_specs, out_specs=out_specs,
        scratch_shapes=[pltpu.VMEM((n, d), BF16), pltpu.VMEM((n, d), F32)],
        compiler_params=_cparams("arbitrary"), name="ffn_sample",
    )(*ins)


def kernel(x_prompt, x_sample, cache_moba_k, cache_moba_v, state_pool, state_rwkv_wkv, state_rwkv_shift, state_ffn_conv, page_table, norm_mix_g, norm_ffn_g, norm_final_g, rel_bias, gm_w_in, gm_ln_g, gm_ln_b, gm_w_s, gm_b_s, gm_w_out, moba_w_qkv, moba_w_o, pool_w, pool_scale, rwkv_mu, rwkv_w_r, rwkv_w_k, rwkv_w_v, rwkv_w_o, rwkv_w0, rwkv_w1, rwkv_w2, rwkv_a0, rwkv_a1, rwkv_a2, rwkv_g1, rwkv_g2, rwkv_k_k, rwkv_k_a, rwkv_r_k, rwkv_lnx_g, rwkv_lnx_b, ffn_w_in, ffn_conv_w, ffn_conv_b, ffn_w_out):
    bp, seq, d = x_prompt.shape
    bs = x_sample.shape[0]
    depth = norm_mix_g.shape[0]
    assert x_sample.shape[1] == 1 and depth == 4
    past_len = page_table.shape[1] * PAGE_SIZE
    assert seq % MOBA_BLOCK == 0 and past_len % MOBA_BLOCK == 0 and past_len % GM_CHUNK == 0
    row = lambda vct: vct.reshape(1, -1)
    bf = lambda m: m.astype(BF16)
    yp = x_prompt.reshape(bp * seq, d)
    ys = x_sample.reshape(bs, d)
    tm = TOKEN_TILE
    assert seq % tm == 0 and seq % RWKV_PROJ_TILE == 0 and seq % RWKV_CHUNK == 0
    conv_p, conv_s = [], []

    def ffn(i, yp, ys):
        last = i == depth - 1
        win, wout = bf(ffn_w_in[i]), bf(ffn_w_out[i])
        cw, cb, ng = ffn_conv_w[i], row(ffn_conv_b[i]), row(norm_ffn_g[i])
        fg = row(norm_final_g) if last else None
        res_p = _ffn_prompt(yp, ng, win, cw, cb, wout, fg, tm=tm, seq=seq)
        st = state_ffn_conv[i]
        res_s = _ffn_sample(ys, ng, win, cw, cb, st[:, 0], st[:, 1], wout, fg)
        conv_p.append(res_p[1])
        conv_s.append(jnp.stack([st[:, 1], res_s[1]], axis=1))
        if last:
            return res_p[2], res_s[2]
        return res_p[0], res_s[0]

    ng = row(norm_mix_g[0])
    width = gm_w_out.shape[1]
    gd = width // GM_GROUPS
    gm_in, gm_out = bf(gm_w_in[0]), bf(gm_w_out[0])
    lng, lnb = row(gm_ln_g[0]), row(gm_ln_b[0])
    sb_prompt = jnp.repeat(gm_b_s[0].T, gd, axis=1)
    (yp,) = _gmlp(yp, ng, gm_in, lng, lnb, gm_w_s[0], sb_prompt, gm_out, tm=tm, sample=False)
    sa_first = row(jnp.repeat(gm_w_s[0][:, 0, 0], gd))
    sb_first = row(jnp.repeat(gm_b_s[0][:, 0], gd))
    ys, gm_v = _gmlp(ys, ng, gm_in, lng, lnb, sa_first, sb_first, gm_out, tm=bs, sample=True)
    gm_v_sample = gm_v.reshape(1, bs, 1, width)
    yp, ys = ffn(0, yp, ys)

    ng = row(norm_mix_g[1])
    w_qkv, w_o = bf(moba_w_qkv[0]), bf(moba_w_o[0])
    heads = MOBA_HEADS
    hd = d // heads
    qp, kp_t, vp_t = _qkv_prompt(yp, ng, w_qkv[:, :d], w_qkv[:, d:2 * d].T, w_qkv[:, 2 * d:].T,
                                 tm=tm, batch=bp, seq=seq)
    qs, ks, vs = _norm_linear(ys, ng, w_qkv, 3, tm=bs, name="moba_qkv_sample")
    blk = MOBA_BLOCK
    qi = jnp.arange(blk, dtype=I32)[:, None]
    ki = jnp.arange(blk, dtype=I32)[None, :]
    bkt_tiles = _t5_bucket_table(jnp.stack([qi - ki, blk + qi - ki]))
    bias_tiles = _bias_tiles(rel_bias, bkt_tiles)
    op = _moba_prompt(rel_bias, qp, kp_t, vp_t, bias_tiles, batch=bp, seq=seq, head_dim=hd)
    bkt_rows = _t5_bucket_table(past_len - jnp.arange(past_len, dtype=I32))
    rb_pad = jnp.pad(rel_bias, ((0, 0), (0, LANES - heads)))
    bias_rows = _bias_rows(rb_pad, jnp.broadcast_to(bkt_rows[:, None], (past_len, LANES)))
    os_ = _moba_sample(page_table, qs, ks, vs, cache_moba_k, cache_moba_v, 0, bias_rows, rel_bias, head_dim=hd)
    yp = _linear_res(op, w_o, yp, tm=tm, name="moba_out_prompt")
    ys = _linear_res(os_, w_o, ys, tm=bs, name="moba_out_sample")
    moba_k_prompt = kp_t.reshape(1, bp, heads, hd, seq).transpose(0, 1, 4, 2, 3)
    moba_v_prompt = vp_t.reshape(1, bp, heads, hd, seq).transpose(0, 1, 4, 2, 3)
    moba_k_sample = ks.reshape(1, bs, 1, heads, hd)
    moba_v_sample = vs.reshape(1, bs, 1, heads, hd)
    yp, ys = ffn(1, yp, ys)

    ng = row(norm_mix_g[2])
    pw, psc = bf(pool_w[0]), row(pool_scale[0])
    yp, pool_p = _pool_prompt(yp, ng, pw, psc, tm=tm, seq=seq)
    ys, hs = _pool_sample(ys, ng, jnp.swapaxes(state_pool[0], 0, 1), pw, psc)
    pool_prompt = pool_p[None]
    pool_sample = jnp.concatenate([state_pool[0][:, 1:], hs[:, None]], axis=1)[None]
    yp, ys = ffn(2, yp, ys)

    ng = row(norm_mix_g[3])
    mats = [bf(m[0]) for m in (rwkv_w_r, rwkv_w_k, rwkv_w_v, rwkv_w1, rwkv_w2, rwkv_a1, rwkv_a2,
                               rwkv_g1, rwkv_g2)]
    vec_in = jnp.stack([rwkv_w0[0], rwkv_a0[0], rwkv_k_k[0], rwkv_k_a[0]])
    vec_out = jnp.stack([rwkv_r_k[0].reshape(-1), rwkv_lnx_g[0], rwkv_lnx_b[0]])
    rh = d // RWKV_HEAD_DIM
    *seqs_p, gp, shp = _rwkv_proj(yp, ng, rwkv_mu[0], mats, vec_in, None, tm=RWKV_PROJ_TILE, seq=seq,
                                  sample=False)
    *seqs_s, gs, shs = _rwkv_proj(ys, ng, rwkv_mu[0], mats, vec_in, state_rwkv_shift[0], tm=bs, seq=1,
                                  sample=True)
    o_p, st_p = _rwkv_chunked(*seqs_p, batch=bp, seq=seq)
    hd_r = RWKV_HEAD_DIM
    wkv_p = st_p.reshape(bp, rh // 2, hd_r, 2, hd_r).transpose(0, 1, 3, 2, 4).reshape(bp, rh, hd_r, hd_r)
    o_s_t, wkv_s_t = _rwkv_step(*[a.T for a in seqs_s], jnp.transpose(state_rwkv_wkv[0], (1, 2, 3, 0)))
    o_s, wkv_s = o_s_t.T, jnp.transpose(wkv_s_t, (3, 0, 1, 2))
    w_o = bf(rwkv_w_o[0])
    r_p, _, k_p, v_p = seqs_p[:4]
    r_s, _, k_s, v_s = seqs_s[:4]
    yp = _rwkv_out(o_p, r_p, k_p, v_p, gp, yp, vec_out, w_o, tm=tm, name="rwkv_out_prompt")
    ys = _rwkv_out(o_s.reshape(bs, d), r_s, k_s, v_s, gs, ys, vec_out, w_o, tm=bs, name="rwkv_out_sample")
    yp, ys = ffn(3, yp, ys)

    return (yp.reshape(bp, seq, d), ys.reshape(bs, 1, d), gm_v_sample, moba_k_prompt, moba_v_prompt,
            moba_k_sample, moba_v_sample, pool_prompt, pool_sample, wkv_p[None], wkv_s[None],
            shp.reshape(1, bp, d), shs[None], jnp.stack(conv_p), jnp.stack(conv_s))
```

```python
import functools
import math

import jax
import jax.numpy as jnp
from jax import lax
from jax.experimental import pallas as pl
from jax.experimental.pallas import tpu as pltpu

F32 = jnp.float32
BF16 = jnp.bfloat16
I32 = jnp.int32

LANES = 128
SUBLANES = 8
VMEM_LIMIT_BYTES = 56 * 2**20

TOKEN_TILE = 512
RWKV_PROJ_TILE = 256

RMS_EPS = 1e-6
GM_LN_EPS = 1e-5
GM_CHUNK = 128
GM_GROUPS = 8
MOBA_HEADS = 16
MOBA_BLOCK = 256
MOBA_TOPK = 3
REL_BUCKETS = 32
REL_MAX_DIST = 128
PAGE_SIZE = 128
POOL_WINDOWS = (2, 4, 8, 16)
POOL_CTX = max(POOL_WINDOWS) - 1
RWKV_HEAD_DIM = 64
RWKV_LNX_EPS = 64e-5
FFN_CONV = 3
NEG_INF = float("-inf")

assert MOBA_BLOCK >= REL_MAX_DIST


def _cparams(*sem):
    return pltpu.CompilerParams(dimension_semantics=sem, vmem_limit_bytes=VMEM_LIMIT_BYTES)


def _whole(shape):
    nd = len(shape)
    return pl.BlockSpec(shape, lambda *_: (0,) * nd, pipeline_mode=pl.Buffered(1))


def _rows(tm, width):
    return pl.BlockSpec((tm, width), lambda i: (i, 0))


def _rms(x, g):
    return x * lax.rsqrt(jnp.mean(x * x, axis=-1, keepdims=True) + RMS_EPS) * g


def _bdot(a, b):
    return jnp.dot(a.astype(BF16), b.astype(BF16), preferred_element_type=F32)


def _nt_dot(a, b):
    return lax.dot_general(a.astype(BF16), b.astype(BF16), (((1,), (1,)), ((), ())),
                           preferred_element_type=F32)


def _split(x):
    hi = x.astype(BF16)
    lo = (x - hi.astype(F32)).astype(BF16)
    return hi, lo


def _split_dot(x, m):
    hi, lo = _split(x)
    return (jnp.dot(hi, m, preferred_element_type=F32)
            + jnp.dot(lo, m, preferred_element_type=F32))


def _same_head(n, head_dim):
    shift = int(math.log2(head_dim))
    r = lax.broadcasted_iota(I32, (n, n), 0) >> shift
    c = lax.broadcasted_iota(I32, (n, n), 1) >> shift
    return jnp.where(r == c, 1.0, 0.0).astype(BF16)


def _seg_sum(x, head_dim):
    g = _same_head(LANES, head_dim)
    parts = [_split_dot(x[:, i:i + LANES], g) for i in range(0, x.shape[1], LANES)]
    return jnp.concatenate(parts, axis=1)


def _gelu(x):
    c = 0.7978845608028654
    half = 0.5 * x
    return half + half * jnp.tanh(x * (c + (c * 0.044715) * (x * x)))


def _sigmoid(x):
    return 1.0 / (1.0 + jnp.exp(-x))


def _top_mask(gate, idx, axis, n_valid):
    cur = gate
    sel = jnp.zeros(gate.shape, F32)
    for _ in range(MOBA_TOPK):
        m = jnp.max(cur, axis=axis, keepdims=True)
        first = jnp.min(jnp.where(cur == m, idx, n_valid), axis=axis, keepdims=True)
        pick = (idx == first) & (m > NEG_INF)
        sel = jnp.where(pick, 1.0, sel)
        cur = jnp.where(pick, NEG_INF, cur)
    return sel


def _gmlp_kernel(x_ref, ng_ref, win_ref, lng_ref, lnb_ref, sa_ref, sb_ref, wout_ref, y_ref, aux_ref,
                 *, sample):
    x = x_ref[...]
    tm, width = x.shape[0], wout_ref.shape[0]
    h = _rms(x, ng_ref[...])
    z = _gelu(_bdot(h, win_ref[...]))
    u, v = z[:, :width], z[:, width:]
    mu = jnp.mean(v, axis=-1, keepdims=True)
    d = v - mu
    var = jnp.mean(d * d, axis=-1, keepdims=True)
    v = d * lax.rsqrt(var + GM_LN_EPS) * lng_ref[...] + lnb_ref[...]
    if sample:
        aux_ref[...] = v
        s = v * sa_ref[...] + sb_ref[...]
    else:
        gd = width // GM_GROUPS
        causal = (lax.broadcasted_iota(I32, (GM_CHUNK, GM_CHUNK), 0)
                  >= lax.broadcasted_iota(I32, (GM_CHUNK, GM_CHUNK), 1))
        for g in range(GM_GROUPS):
            wg = jnp.where(causal, sa_ref[g], 0.0).astype(BF16)
            cols = slice(g * gd, (g + 1) * gd)
            for c in range(tm // GM_CHUNK):
                rows = slice(c * GM_CHUNK, (c + 1) * GM_CHUNK)
                aux_ref[rows, cols] = (jnp.dot(wg, v[rows, cols].astype(BF16), preferred_element_type=F32)
                                       + sb_ref[:, cols])
        s = aux_ref[...]
    y_ref[...] = x + _bdot(u * s, wout_ref[...])


def _gmlp(x, ng, win, lng, lnb, sa, sb, wout, *, tm, sample):
    n, d = x.shape
    width = wout.shape[0]
    outs = [jax.ShapeDtypeStruct((n, d), F32)]
    out_specs = [_rows(tm, d)]
    scratch = []
    if sample:
        outs.append(jax.ShapeDtypeStruct((n, width), F32))
        out_specs.append(_rows(tm, width))
    else:
        scratch.append(pltpu.VMEM((tm, width), F32))
    return pl.pallas_call(
        functools.partial(_gmlp_kernel, sample=sample),
        out_shape=outs, grid=(n // tm,),
        in_specs=[_rows(tm, d), _whole(ng.shape), _whole(win.shape), _whole(lng.shape), _whole(lnb.shape),
                  _whole(sa.shape), _whole(sb.shape), _whole(wout.shape)],
        out_specs=out_specs, scratch_shapes=scratch,
        compiler_params=_cparams("parallel"), name="gmlp_sample" if sample else "gmlp_prompt",
    )(x, ng, win, lng, lnb, sa, sb, wout)


def _norm_linear_kernel(x_ref, ng_ref, w_ref, *o_refs):
    z = _bdot(_rms(x_ref[...], ng_ref[...]), w_ref[...])
    wd = z.shape[1] // len(o_refs)
    for i, o_ref in enumerate(o_refs):
        o_ref[...] = z[:, i * wd:(i + 1) * wd]


def _norm_linear(x, ng, w, n_out, *, tm, name):
    n, d = x.shape
    wd = w.shape[1] // n_out
    return pl.pallas_call(
        _norm_linear_kernel,
        out_shape=[jax.ShapeDtypeStruct((n, wd), F32)] * n_out, grid=(n // tm,),
        in_specs=[_rows(tm, d), _whole(ng.shape), _whole(w.shape)],
        out_specs=[_rows(tm, wd)] * n_out,
        compiler_params=_cparams("parallel"), name=name,
    )(x, ng, w)


def _qkv_prompt_kernel(x_ref, ng_ref, wq_ref, wkt_ref, wvt_ref, q_ref, kt_ref, vt_ref):
    h = _rms(x_ref[...], ng_ref[...]).astype(BF16)
    q_ref[...] = jnp.dot(h, wq_ref[...], preferred_element_type=F32)
    kt_ref[0] = _nt_dot(wkt_ref[...], h)
    vt_ref[0] = _nt_dot(wvt_ref[...], h)


def _qkv_prompt(x, ng, wq, wkt, wvt, *, tm, batch, seq):
    n, d = x.shape
    tps = seq // tm
    t_spec = pl.BlockSpec((1, d, tm), lambda i: (i // tps, 0, i % tps))
    return pl.pallas_call(
        _qkv_prompt_kernel,
        out_shape=[jax.ShapeDtypeStruct((n, d), F32)] + [jax.ShapeDtypeStruct((batch, d, seq), F32)] * 2,
        grid=(n // tm,),
        in_specs=[_rows(tm, d), _whole(ng.shape), _whole(wq.shape), _whole(wkt.shape), _whole(wvt.shape)],
        out_specs=[_rows(tm, d), t_spec, t_spec],
        compiler_params=_cparams("parallel"), name="moba_qkv_prompt",
    )(x, ng, wq, wkt, wvt)


def _linear_res_kernel(a_ref, w_ref, y_ref, o_ref):
    o_ref[...] = y_ref[...] + _bdot(a_ref[...], w_ref[...])


def _linear_res(a, w, y, *, tm, name):
    n, d = y.shape
    return pl.pallas_call(
        _linear_res_kernel, out_shape=jax.ShapeDtypeStruct((n, d), F32), grid=(n // tm,),
        in_specs=[_rows(tm, a.shape[1]), _whole(w.shape), _rows(tm, d)], out_specs=_rows(tm, d),
        compiler_params=_cparams("parallel"), name=name,
    )(a, w, y)


def _t5_bucket_table(rel):
    n = jnp.maximum(rel, 0)
    exact = REL_BUCKETS // 2
    nf = jnp.maximum(n, 1).astype(F32)
    large = exact + (jnp.log(nf / exact) / math.log(REL_MAX_DIST / exact)
                     * (REL_BUCKETS - exact)).astype(I32)
    return jnp.where(n < exact, n, jnp.minimum(large, REL_BUCKETS - 1)).astype(I32)


def _bias_tiles_kernel(rb_ref, bkt_ref, o_ref):
    h = pl.program_id(0)
    bkt = bkt_ref[...]
    acc = jnp.zeros(bkt.shape, F32)
    for b in range(REL_BUCKETS):
        acc = jnp.where(bkt == b, rb_ref[b, h], acc)
    future = (lax.broadcasted_iota(I32, bkt.shape, 2) > lax.broadcasted_iota(I32, bkt.shape, 1))
    own_block = lax.broadcasted_iota(I32, bkt.shape, 0) == 0
    o_ref[0] = jnp.where(future & own_block, NEG_INF, acc)


def _bias_tiles(rel_bias, bkt):
    heads = rel_bias.shape[1]
    return pl.pallas_call(
        _bias_tiles_kernel, out_shape=jax.ShapeDtypeStruct((heads,) + bkt.shape, F32), grid=(heads,),
        in_specs=[pl.BlockSpec(memory_space=pltpu.SMEM), _whole(bkt.shape)],
        out_specs=pl.BlockSpec((1,) + bkt.shape, lambda h: (h, 0, 0, 0)),
        compiler_params=_cparams("parallel"), name="moba_bias_tiles",
    )(rel_bias, bkt)


def _bias_rows_kernel(rb_ref, bkt_ref, o_ref):
    bkt = bkt_ref[...]
    acc = jnp.zeros(bkt.shape, F32)
    for b in range(REL_BUCKETS):
        acc = jnp.where(bkt == b, rb_ref[b:b + 1, :], acc)
    o_ref[...] = acc


def _bias_rows(rb_pad, bkt):
    return pl.pallas_call(
        _bias_rows_kernel, out_shape=jax.ShapeDtypeStruct(bkt.shape, F32), grid=(1,),
        in_specs=[_whole(rb_pad.shape), _whole(bkt.shape)], out_specs=_whole(bkt.shape),
        compiler_params=_cparams("arbitrary"), name="moba_bias_rows",
    )(rb_pad, bkt)


def _moba_prompt_kernel(rb_ref, q_ref, k_ref, v_ref, bias_ref, o_ref, kb_scr, vh_scr, kmean_scr, *, head_dim):
    pair = pl.program_id(1)
    blk = MOBA_BLOCK
    seq = k_ref.shape[2]
    n_blocks = seq // blk
    assert LANES == 2 * head_dim
    shift = int(math.log2(head_dim))
    scale = head_dim ** -0.5
    assert shift % 2 == 0 and n_blocks <= kmean_scr.shape[0]

    def prepare():
        kt, vt = k_ref[0], v_ref[0]
        second = (lax.broadcasted_iota(I32, (LANES, seq), 0) >> shift) == 1
        kb_scr[...] = kt.astype(BF16)
        vh_scr[0] = jnp.where(second, 0.0, vt).astype(BF16)
        vh_scr[1] = jnp.where(second, vt, 0.0).astype(BF16)
        rows = kmean_scr.shape[0]
        in_block = jnp.where((lax.broadcasted_iota(I32, (rows, seq), 1) >> int(math.log2(blk)))
                             == lax.broadcasted_iota(I32, (rows, seq), 0), 1.0, 0.0).astype(BF16)
        k_hi, k_lo = _split(kt)
        nt = lambda a, b: lax.dot_general(a, b, (((1,), (1,)), ((), ())), preferred_element_type=F32)
        kmean_scr[...] = (nt(in_block, k_hi) + nt(in_block, k_lo)) * (1.0 / blk)

    prepare()
    second = (lax.broadcasted_iota(I32, (blk, LANES), 1) >> shift) == 1
    km_hi, km_lo = _split(kmean_scr[...])
    nt = lambda a, b: lax.dot_general(a, b, (((1,), (1,)), ((), ())), preferred_element_type=F32)
    block_id = lax.broadcasted_iota(I32, (kmean_scr.shape[0], 2 * blk), 0)
    far_bias = [rb_ref[REL_BUCKETS - 1, pair * 2 + hh] for hh in range(2)]

    def select(own):
        q = q_ref[own * blk:(own + 1) * blk, :]
        q2 = jnp.concatenate([jnp.where(second, 0.0, q), jnp.where(second, q, 0.0)], axis=0)
        q_hi, q_lo = _split(q2)
        gate = nt(km_hi, q_hi) + nt(km_lo, q_hi) + nt(km_hi, q_lo)
        sel = _top_mask(jnp.where(block_id < own, gate, NEG_INF), block_id, 0, gate.shape[0]).T
        return (q2 * scale).astype(BF16), sel

    def scores(own, q_scaled):
        return jnp.dot(q_scaled, kb_scr[:, 0:(own + 1) * blk], preferred_element_type=F32)

    def mask(own, s, sel):
        head_rows = []
        for hh in range(2):
            rows = slice(hh * blk, (hh + 1) * blk)
            tiles = []
            for n in range(own + 1):
                t = s[rows, n * blk:(n + 1) * blk]
                chosen = sel[rows, n:n + 1] > 0.0
                if n == own:
                    t = t + bias_ref[hh, 0]
                elif n == own - 1:
                    t = jnp.where(chosen, t + bias_ref[hh, 1], NEG_INF)
                else:
                    t = t + jnp.where(chosen, far_bias[hh], NEG_INF)
                tiles.append(t)
            head_rows.append(jnp.concatenate(tiles, axis=1))
        return jnp.concatenate(head_rows, axis=0)

    def weights(s):
        p = jnp.exp(s - jnp.max(s, axis=-1, keepdims=True))
        return p, jnp.sum(p, axis=-1, keepdims=True)

    def output(own, p, l):
        keys = (own + 1) * blk
        p_pair = jnp.concatenate([p[:blk], p[blk:]], axis=1).astype(BF16)
        v_pair = jnp.concatenate([vh_scr[0, :, 0:keys], vh_scr[1, :, 0:keys]], axis=1)
        o_ref[own * blk:(own + 1) * blk, :] = nt(p_pair, v_pair) / jnp.where(second, l[blk:], l[:blk])

    order = list(range(n_blocks))
    groups = [(order[i], order[-1 - i]) for i in range(n_blocks // 2)] + ([(order[n_blocks // 2],)] * (n_blocks % 2))
    for group in groups:
        picked = [select(own) for own in group]
        raw = [scores(own, qs) for own, (qs, _) in zip(group, picked)]
        masked = [mask(own, s, sel) for own, s, (_, sel) in zip(group, raw, picked)]
        probs = [weights(s) for s in masked]
        for own, (p, l) in zip(group, probs):
            output(own, p, l)


def _moba_prompt(rel_bias, q, k, v, bias_tiles, *, batch, seq, head_dim):
    n, d = q.shape
    blk = MOBA_BLOCK
    hpt = LANES // head_dim
    return pl.pallas_call(
        functools.partial(_moba_prompt_kernel, head_dim=head_dim),
        out_shape=jax.ShapeDtypeStruct((n, d), F32), grid=(batch, d // LANES),
        in_specs=[pl.BlockSpec(memory_space=pltpu.SMEM),
                  pl.BlockSpec((seq, LANES), lambda b, p: (b, p)),
                  pl.BlockSpec((1, LANES, seq), lambda b, p: (b, p, 0)),
                  pl.BlockSpec((1, LANES, seq), lambda b, p: (b, p, 0)),
                  pl.BlockSpec((hpt, 2, blk, blk), lambda b, p: (p, 0, 0, 0))],
        out_specs=pl.BlockSpec((seq, LANES), lambda b, p: (b, p)),
        scratch_shapes=[pltpu.VMEM((LANES, seq), BF16), pltpu.VMEM((hpt, LANES, seq), BF16),
                        pltpu.VMEM((2 * SUBLANES, LANES), F32)],
        compiler_params=_cparams("parallel", "parallel"), name="moba_prompt_attn",
    )(rel_bias, q, k, v, bias_tiles)


def _moba_sample_kernel(pt_ref, q_ref, kn_ref, vn_ref, *rest, n_pages, head_dim):
    del pt_ref
    k_refs, v_refs = rest[:n_pages], rest[n_pages:2 * n_pages]
    bias_ref, rb0_ref, o_ref = rest[2 * n_pages:]
    d = q_ref.shape[2]
    heads = d // head_dim
    shift = int(math.log2(head_dim))
    per_block = MOBA_BLOCK // PAGE_SIZE
    blocks = range(n_pages // per_block)
    scale = head_dim ** -0.5
    head_lanes = jnp.where((lax.broadcasted_iota(I32, (heads, d), 1) >> shift)
                           == lax.broadcasted_iota(I32, (heads, d), 0), 1.0, 0.0)
    q = head_lanes * q_ref[0]
    q_hi, q_lo = _split(q)
    q2 = jnp.concatenate([q_hi, q_lo], axis=0)
    block_of = lambda refs, n: jnp.concatenate(
        [refs[per_block * n + j][0, 0].reshape(d, PAGE_SIZE) for j in range(per_block)], axis=1).astype(BF16)
    raw = [jnp.dot(q2, block_of(k_refs, n), preferred_element_type=F32) for n in blocks]
    raw = [r[:heads] + r[heads:] for r in raw]
    block_id = lax.broadcasted_iota(I32, (heads, LANES), 1)
    gate = jnp.full((heads, LANES), NEG_INF, F32)
    for n in blocks:
        gate = jnp.where(block_id == n, jnp.sum(raw[n], axis=-1, keepdims=True), gate)
    sel = _top_mask(gate, block_id, -1, LANES)
    s_new = jnp.sum(q * kn_ref[0], axis=-1, keepdims=True) * scale + rb0_ref[:, 0:1]
    s = [raw[n] * scale + bias_ref[n] for n in blocks]
    m = s_new
    for n in blocks:
        m = jnp.maximum(m, jnp.where(sel[:, n:n + 1] > 0.0, jnp.max(s[n], axis=-1, keepdims=True), NEG_INF))
    p = [jnp.exp(jnp.where(sel[:, n:n + 1] > 0.0, s[n] - m, NEG_INF)) for n in blocks]
    p_new = jnp.exp(s_new - m)
    den = p_new
    for n in blocks:
        den = den + jnp.sum(p[n], axis=-1, keepdims=True)
    p_all = jnp.concatenate(p, axis=1).astype(BF16)
    v_all = jnp.concatenate([block_of(v_refs, n) for n in blocks], axis=1)
    acc = lax.dot_general(p_all, v_all, (((1,), (1,)), ((), ())), preferred_element_type=F32)
    out = (acc + p_new * vn_ref[0]) * head_lanes / den
    o_ref[0] = jnp.sum(out, axis=0, keepdims=True)


def _moba_sample(page_table, q, k_new, v_new, cache_k, cache_v, layer, bias_rows, rel_bias, *, head_dim):
    nb, d = q.shape
    n_pages = page_table.shape[1]
    n_blocks = n_pages * PAGE_SIZE // MOBA_BLOCK
    heads = d // head_dim
    cache_kt = jnp.transpose(cache_k, (0, 1, 3, 4, 2))
    cache_vt = jnp.transpose(cache_v, (0, 1, 3, 4, 2))
    row = lambda x: x.reshape(nb, 1, d)
    vec = pl.BlockSpec((1, 1, d), lambda s, pt: (s, 0, 0))
    page = lambda j: pl.BlockSpec((1, 1, heads, head_dim, PAGE_SIZE), lambda s, pt: (layer, pt[s, j], 0, 0, 0))
    const = lambda shape: pl.BlockSpec(shape, lambda s, pt: (0,) * len(shape))
    bias = bias_rows[:, :heads].reshape(n_blocks, MOBA_BLOCK, heads).transpose(0, 2, 1)
    rb0 = jnp.broadcast_to(rel_bias[0][:, None], (heads, LANES))
    pages = [page(j) for j in range(n_pages)]
    out = pl.pallas_call(
        functools.partial(_moba_sample_kernel, n_pages=n_pages, head_dim=head_dim),
        out_shape=jax.ShapeDtypeStruct((nb, 1, d), F32),
        grid_spec=pltpu.PrefetchScalarGridSpec(
            num_scalar_prefetch=1, grid=(nb,),
            in_specs=[vec, vec, vec, *pages, *pages, const(bias.shape), const(rb0.shape)],
            out_specs=vec),
        compiler_params=_cparams("parallel"), name="moba_sample_attn",
    )(page_table, row(q), row(k_new), row(v_new), *([cache_kt] * n_pages), *([cache_vt] * n_pages), bias, rb0)
    return out.reshape(nb, d)


POOL_PAD = 32


def _pool_prompt_kernel(y_ref, ng_ref, w_ref, sc_ref, yout_ref, st_ref, b0, b1, b2, b3, b4, *, tps):
    i = pl.program_id(0)
    y = y_ref[...]
    tm, d = y.shape
    gd = d // len(POOL_WINDOWS)
    h = _rms(y, ng_ref[...])
    half = POOL_PAD // 2

    @pl.when(i == 0)
    def _():
        for b in (b0, b1, b2, b3, b4):
            b[0:half, :] = jnp.zeros((half, b.shape[1]), F32)

    @pl.when(i % tps == 0)
    def _():
        b0[half:POOL_PAD, :] = jnp.zeros((half, d), F32)

    b0[POOL_PAD:, :] = h
    stages = (b0, b1, b2, b3, b4)
    for k in range(1, len(stages)):
        src, dst = stages[k - 1], stages[k]
        back = 2 ** (k - 1)
        off = src.shape[1] - dst.shape[1]
        dst[SUBLANES:, :] = (src[SUBLANES:, off:] + src[SUBLANES - back:tm + POOL_PAD - back, off:])
    pos = (i % tps) * tm + lax.broadcasted_iota(I32, (tm, 1), 0) + 1
    for g, win in enumerate(POOL_WINDOWS):
        cols = slice(g * gd, (g + 1) * gd)
        wsum = stages[g + 1][POOL_PAD:, 0:gd]
        cnt = jnp.minimum(pos, win).astype(F32)
        mixed = _bdot(wsum / cnt - h[:, cols], w_ref[g])
        yout_ref[:, cols] = y[:, cols] + mixed * sc_ref[:, cols]
    st_ref[0] = b0[tm + POOL_PAD - POOL_CTX:, :]
    b0[half:POOL_PAD, :] = b0[tm + half:, :]


def _pool_prompt(y, ng, w, sc, *, tm, seq):
    n, d = y.shape
    gd = d // len(POOL_WINDOWS)
    tps = seq // tm
    rows = tm + POOL_PAD
    return pl.pallas_call(
        functools.partial(_pool_prompt_kernel, tps=tps),
        out_shape=[jax.ShapeDtypeStruct((n, d), F32), jax.ShapeDtypeStruct((n // seq, POOL_CTX, d), F32)],
        grid=(n // tm,),
        in_specs=[_rows(tm, d), _whole(ng.shape), _whole(w.shape), _whole(sc.shape)],
        out_specs=[_rows(tm, d), pl.BlockSpec((1, POOL_CTX, d), lambda i: (i // tps, 0, 0))],
        scratch_shapes=[pltpu.VMEM((rows, d - k * gd), F32) for k in (0, 0, 1, 2, 3)],
        compiler_params=_cparams("arbitrary"), name="pool_prompt",
    )(y, ng, w, sc)


def _pool_sample_kernel(y_ref, ng_ref, prev_ref, w_ref, sc_ref, yout_ref, h_ref):
    y = y_ref[...]
    d = y.shape[1]
    gd = d // len(POOL_WINDOWS)
    h = _rms(y, ng_ref[...])
    h_ref[...] = h
    for g, win in enumerate(POOL_WINDOWS):
        cols = slice(g * gd, (g + 1) * gd)
        wsum = h[:, cols]
        for back in range(1, win):
            wsum = wsum + prev_ref[POOL_CTX - back, :, cols]
        mixed = _bdot(wsum / float(win) - h[:, cols], w_ref[g])
        yout_ref[:, cols] = y[:, cols] + mixed * sc_ref[:, cols]


def _pool_sample(y, ng, prev_t, w, sc):
    n, d = y.shape
    return pl.pallas_call(
        _pool_sample_kernel, out_shape=[jax.ShapeDtypeStruct((n, d), F32)] * 2, grid=(1,),
        in_specs=[_whole(y.shape), _whole(ng.shape), _whole(prev_t.shape), _whole(w.shape), _whole(sc.shape)],
        out_specs=[_whole(y.shape)] * 2,
        compiler_params=_cparams("arbitrary"), name="pool_sample",
    )(y, ng, prev_t, w, sc)


def _rwkv_proj_kernel(y_ref, ng_ref, mu_ref, wr_ref, wk_ref, wv_ref, w1_ref, w2_ref, a1_ref, a2_ref,
                      g1_ref, g2_ref, vec_ref, *rest, tps, sample):
    if sample:
        prev_ref, r_o, w_o, k_o, v_o, kk_o, kka_o, g_o, sh_o = rest
    else:
        r_o, w_o, k_o, v_o, kk_o, kka_o, g_o, sh_o, hs_scr = rest
    y = y_ref[...]
    tm = y.shape[0]
    h = _rms(y, ng_ref[...])
    if sample:
        h_prev = prev_ref[...]
        sh_o[...] = h
    else:
        i = pl.program_id(0)

        @pl.when(i % tps == 0)
        def _():
            hs_scr[...] = jnp.zeros(hs_scr.shape, F32)

        h_prev = pltpu.roll(jnp.concatenate([hs_scr[...], h], axis=0), 1, axis=0)[SUBLANES:]
        hs_scr[...] = h[tm - SUBLANES:, :]
        sh_o[0] = h[tm - 1:tm, :]
    xx = h_prev - h
    mix = lambda m: h + xx * mu_ref[m:m + 1, :]
    r = _bdot(mix(0), wr_ref[...])
    k = _bdot(mix(2), wk_ref[...])
    v = _bdot(mix(3), wv_ref[...])
    z = vec_ref[0:1, :] + _bdot(jnp.tanh(_bdot(mix(1), w1_ref[...])), w2_ref[...])
    w_log = jnp.minimum(z, 0.0) - jnp.log(1.0 + jnp.exp(-jnp.abs(z))) - 0.5
    a = _sigmoid(vec_ref[1:2, :] + _bdot(_bdot(mix(4), a1_ref[...]), a2_ref[...]))
    g_o[...] = _bdot(_sigmoid(_bdot(mix(5), g1_ref[...])), g2_ref[...])
    kk = k * vec_ref[2:3, :]
    kk = kk / jnp.maximum(jnp.sqrt(_seg_sum(kk * kk, RWKV_HEAD_DIM)), 1e-12)
    r_o[...] = r
    w_o[...] = -jnp.exp(w_log)
    k_o[...] = k * (1.0 + (a - 1.0) * vec_ref[3:4, :])
    v_o[...] = v
    kk_o[...] = kk
    kka_o[...] = kk * a


def _rwkv_proj(y, ng, mu, mats, vec, prev, *, tm, seq, sample):
    n, d = y.shape
    tps = max(seq // tm, 1)
    ins = [y, ng, mu, *mats, vec]
    in_specs = [_rows(tm, d), _whole(ng.shape), _whole(mu.shape), *[_whole(m.shape) for m in mats],
                _whole(vec.shape)]
    outs = [jax.ShapeDtypeStruct((n, d), F32)] * 7
    out_specs = [_rows(tm, d)] * 7
    scratch = []
    if sample:
        ins.append(prev)
        in_specs.append(_rows(tm, d))
        outs.append(jax.ShapeDtypeStruct((n, d), F32))
        out_specs.append(_rows(tm, d))
    else:
        outs.append(jax.ShapeDtypeStruct((n // seq, 1, d), F32))
        out_specs.append(pl.BlockSpec((1, 1, d), lambda i: (i // tps, 0, 0)))
        scratch.append(pltpu.VMEM((SUBLANES, d), F32))
    return pl.pallas_call(
        functools.partial(_rwkv_proj_kernel, tps=tps, sample=sample),
        out_shape=outs, grid=(n // tm,), in_specs=in_specs, out_specs=out_specs, scratch_shapes=scratch,
        compiler_params=_cparams("arbitrary"), name="rwkv_proj_sample" if sample else "rwkv_proj_prompt",
    )(*ins)


RWKV_CHUNK = 128


def _rwkv_chunk_kernel(r_ref, lw_ref, k_ref, v_ref, kk_ref, kka_ref, o_ref, st_ref, t_scr):
    c = pl.program_id(1)
    ch, d = r_ref.shape
    hd = RWKV_HEAD_DIM
    shift = int(math.log2(hd))
    n_pairs = d // LANES

    @pl.when(c == 0)
    def _():
        t_scr[...] = jnp.zeros(t_scr.shape, F32)

    ri = lax.broadcasted_iota(I32, (ch, ch), 0)
    ci = lax.broadcasted_iota(I32, (ch, ch), 1)
    incl = ci <= ri
    strict = ci < ri
    incl2 = jnp.concatenate([incl, incl], axis=1)
    strict2 = jnp.concatenate([strict, strict], axis=1)
    same_head = (ri >> shift) == (ci >> shift)
    eye = ri == ci
    head1 = (lax.broadcasted_iota(I32, (ch, LANES), 1) >> shift) == 1
    head1_wide = jnp.concatenate([head1, head1], axis=1)

    def by_head(x, mask):
        return jnp.concatenate([jnp.where(mask, 0.0, x), jnp.where(mask, x, 0.0)], axis=0).astype(BF16)

    lw = lw_ref[...]
    tri = jnp.where(incl, 1.0, 0.0).astype(BF16)
    p1 = lw.astype(BF16)
    rem = lw - p1.astype(F32)
    p2 = rem.astype(BF16)
    p3 = (rem - p2.astype(F32)).astype(BF16)
    cl = (jnp.dot(tri, p1, preferred_element_type=F32) + jnp.dot(tri, p2, preferred_element_type=F32)
          + jnp.dot(tri, p3, preferred_element_type=F32))
    mid = cl[ch // 2 - 1:ch // 2, :]
    last = cl[ch - 1:ch, :]
    e_mid = jnp.exp(mid)
    e_neg = jnp.exp(mid - cl)
    e_tail = jnp.exp(last - cl)
    p_last = jnp.exp(last)
    kk, kka, k = kk_ref[...], kka_ref[...], k_ref[...]
    a_s = -kk * jnp.exp(cl - lw - mid)
    r_s = r_ref[...] * jnp.exp(cl - mid)
    b_s = kka * e_neg
    k_s = k * e_neg
    b_t = kka * e_tail
    k_t = k * e_tail
    v = v_ref[...]
    zero = jnp.zeros((ch, ch), BF16)

    pairs = range(n_pairs)
    cols = [slice(p * LANES, (p + 1) * LANES) for p in pairs]
    mm = lambda a, b: jnp.dot(a, b, preferred_element_type=F32)
    lane_cat = lambda a, b: jnp.concatenate([a, b], axis=1)
    a_p = [a_s[:, c] for c in cols]
    r_p = [r_s[:, c] for c in cols]
    v_p = [v[:, c] for c in cols]
    gram = []
    for p in pairs:
        lhs = jnp.concatenate([jnp.where(head1, 0.0, a_p[p]), jnp.where(head1, 0.0, r_p[p]),
                               jnp.where(head1, a_p[p], 0.0), jnp.where(head1, r_p[p], 0.0)], axis=0)
        rhs = lane_cat(b_s[:, cols[p]].T, k_s[:, cols[p]].T)
        gram.append(mm(lhs.astype(BF16), rhs.astype(BF16)))
    ab0 = [jnp.where(strict2, g_[0:ch], 0.0) for g_ in gram]
    rb0 = [jnp.where(incl2, g_[ch:2 * ch], 0.0) for g_ in gram]
    ab1 = [jnp.where(strict2, g_[2 * ch:3 * ch], 0.0) for g_ in gram]
    rb1 = [jnp.where(incl2, g_[3 * ch:], 0.0) for g_ in gram]
    v_heads = [by_head(v_p[p], head1) for p in pairs]
    x = [lane_cat(a_p[p], mm(lane_cat(ab0[p][:, ch:], ab1[p][:, ch:]).astype(BF16), v_heads[p])) for p in pairs]
    pw = [lane_cat(ab0[p][:, :ch], ab1[p][:, :ch]).astype(BF16) for p in pairs]
    n_sq = int(math.log2(ch))
    for it in range(n_sq):
        x = [x[p] + mm(pw[p], by_head(x[p], head1_wide)) for p in pairs]
        if it < n_sq - 1:
            pw = [mm(pw[p], jnp.concatenate([lane_cat(pw[p][:, :ch], zero), lane_cat(zero, pw[p][:, ch:])],
                                            axis=0)).astype(BF16) for p in pairs]
    x = [lane_cat(x[p][:, :ch] * e_mid[:, cols[p]], x[p][:, ch:]) for p in pairs]
    qo = [mm(lane_cat(rb0[p][:, :ch], rb1[p][:, :ch]).astype(BF16), by_head(x[p], head1_wide)) for p in pairs]
    o_intra = [qo[p][:, ch:] + mm(lane_cat(rb0[p][:, ch:], rb1[p][:, ch:]).astype(BF16), v_heads[p])
               for p in pairs]
    q = [r_p[p] * e_mid[:, cols[p]] + qo[p][:, :ch] for p in pairs]
    gh = [mm(b_t[:, cols[p]].T.astype(BF16), x[p].astype(BF16)) for p in pairs]
    g = [jnp.where(same_head, gh[p][:, :ch], 0.0) + jnp.where(eye, p_last[:, cols[p]], 0.0) for p in pairs]
    h = [jnp.where(same_head, gh[p][:, ch:] + _bdot(k_t[:, cols[p]].T, v_p[p]), 0.0) for p in pairs]
    for p in pairs:
        t_old = t_scr[p].astype(BF16)
        o_ref[:, cols[p]] = mm(q[p].astype(BF16), t_old) + o_intra[p]
        t_scr[p] = mm(g[p].astype(BF16), t_old) + h[p]

    @pl.when(c == pl.num_programs(1) - 1)
    def _():
        for p in range(n_pairs):
            s_pair = t_scr[p].T
            st_ref[0, p] = s_pair[:hd, :] + s_pair[hd:, :]


def _rwkv_chunked(r, lw, k, v, kk, kka, *, batch, seq):
    n, d = r.shape
    ch = RWKV_CHUNK
    assert ch == LANES and seq % ch == 0
    n_chunks = seq // ch
    n_pairs = d // LANES
    blk = pl.BlockSpec((ch, d), lambda b, c: (b * n_chunks + c, 0))
    return pl.pallas_call(
        _rwkv_chunk_kernel,
        out_shape=[jax.ShapeDtypeStruct((n, d), F32),
                   jax.ShapeDtypeStruct((batch, n_pairs, RWKV_HEAD_DIM, LANES), F32)],
        grid=(batch, n_chunks), in_specs=[blk] * 6,
        out_specs=[blk, pl.BlockSpec((1, n_pairs, RWKV_HEAD_DIM, LANES), lambda b, c: (b, 0, 0, 0))],
        scratch_shapes=[pltpu.VMEM((n_pairs, LANES, LANES), F32)],
        compiler_params=_cparams("parallel", "arbitrary"), name="rwkv_chunked",
    )(r, lw, k, v, kk, kka)


def _rwkv_step_kernel(r_ref, lw_ref, k_ref, v_ref, kk_ref, kka_ref, s0_ref, o_ref, st_ref):
    hd = s0_ref.shape[1]
    w = jnp.exp(lw_ref[...])
    kk, kka, k, r = kk_ref[...], kka_ref[...], k_ref[...], r_ref[...]

    def value_row(i, carry):
        s = s0_ref[0, i]
        s_kk = jnp.sum(s * kk, axis=0, keepdims=True)
        s = s * w - s_kk * kka + v_ref[pl.ds(i, 1), :] * k
        st_ref[0, i] = s
        o_ref[pl.ds(i, 1), :] = jnp.sum(s * r, axis=0, keepdims=True)
        return carry

    lax.fori_loop(0, hd, value_row, 0, unroll=8)


def _rwkv_step(r, lw, k, v, kk, kka, s0):
    d, n = r.shape
    heads, hd = s0.shape[0], s0.shape[1]
    vec = pl.BlockSpec((hd, n), lambda h: (h, 0))
    st = pl.BlockSpec((1, hd, hd, n), lambda h: (h, 0, 0, 0))
    return pl.pallas_call(
        _rwkv_step_kernel,
        out_shape=[jax.ShapeDtypeStruct((d, n), F32), jax.ShapeDtypeStruct(s0.shape, F32)],
        grid=(heads,), in_specs=[vec] * 6 + [st], out_specs=[vec, st],
        compiler_params=_cparams("parallel"), name="rwkv_step",
    )(r, lw, k, v, kk, kka, s0)


def _rwkv_out_kernel(o_ref, r_ref, k_ref, v_ref, g_ref, y_ref, vec_ref, wo_ref, yout_ref):
    hd = RWKV_HEAD_DIM
    o = o_ref[...]
    mu = _seg_sum(o, hd) * (1.0 / hd)
    dlt = o - mu
    var = _seg_sum(dlt * dlt, hd) * (1.0 / hd)
    o = dlt * lax.rsqrt(var + RWKV_LNX_EPS) * vec_ref[1:2, :] + vec_ref[2:3, :]
    o = o + _seg_sum(r_ref[...] * k_ref[...] * vec_ref[0:1, :], hd) * v_ref[...]
    yout_ref[...] = y_ref[...] + _bdot(o * g_ref[...], wo_ref[...])


def _rwkv_out(o, r, k, v, g, y, vec, wo, *, tm, name):
    n, d = y.shape
    return pl.pallas_call(
        _rwkv_out_kernel, out_shape=jax.ShapeDtypeStruct((n, d), F32), grid=(n // tm,),
        in_specs=[_rows(tm, d)] * 6 + [_whole(vec.shape), _whole(wo.shape)], out_specs=_rows(tm, d),
        compiler_params=_cparams("parallel"), name=name,
    )(o, r, k, v, g, y, vec, wo)


FFN_CHUNK = 256


def _ffn_act(gate, g1, g2, up, cw, cb):
    return _gelu(cw[0:1] * g2 + cw[1:2] * g1 + cw[2:3] * gate + cb) * up


def _ffn_prompt_kernel(y_ref, ng_ref, win_ref, cw_ref, cb_ref, wout_ref, *rest, tps, final):
    if final:
        fg_ref, yout_ref, st_ref, yfin_ref, h_scr, act_scr, carry_scr = rest
    else:
        yout_ref, st_ref, h_scr, act_scr, carry_scr = rest
    i = pl.program_id(0)
    y = y_ref[...]
    tm = y.shape[0]
    hid = wout_ref.shape[0]
    tf = FFN_CHUNK
    h_scr[...] = _rms(y, ng_ref[...]).astype(BF16)

    @pl.when(i % tps == 0)
    def _():
        carry_scr[...] = jnp.zeros(carry_scr.shape, F32)

    for j in range(hid // tf):
        cols = slice(j * tf, (j + 1) * tf)
        hb = h_scr[...]
        gate = jnp.dot(hb, win_ref[:, cols], preferred_element_type=F32)
        up = jnp.dot(hb, win_ref[:, hid + j * tf:hid + (j + 1) * tf], preferred_element_type=F32)
        with_prev = jnp.concatenate([carry_scr[:, cols], gate], axis=0)
        g1 = pltpu.roll(with_prev, 1, axis=0)[SUBLANES:]
        g2 = pltpu.roll(with_prev, 2, axis=0)[SUBLANES:]
        carry_scr[:, cols] = gate[tm - SUBLANES:, :]
        st_ref[0, :, cols] = gate[tm - (FFN_CONV - 1):, :]
        act_scr[:, cols] = _ffn_act(gate, g1, g2, up, cw_ref[:, cols], cb_ref[:, cols]).astype(BF16)
    y_new = y + jnp.dot(act_scr[...], wout_ref[...], preferred_element_type=F32)
    yout_ref[...] = y_new
    if final:
        yfin_ref[...] = _rms(y_new, fg_ref[...])


def _ffn_prompt(y, ng, win, cw, cb, wout, fg, *, tm, seq):
    n, d = y.shape
    hid = wout.shape[0]
    tps = seq // tm
    final = fg is not None
    ins = [y, ng, win, cw, cb, wout]
    in_specs = [_rows(tm, d)] + [_whole(a.shape) for a in ins[1:]]
    outs = [jax.ShapeDtypeStruct((n, d), F32), jax.ShapeDtypeStruct((n // seq, FFN_CONV - 1, hid), F32)]
    out_specs = [_rows(tm, d), pl.BlockSpec((1, FFN_CONV - 1, hid), lambda i: (i // tps, 0, 0))]
    if final:
        ins.append(fg)
        in_specs.append(_whole(fg.shape))
        outs.append(jax.ShapeDtypeStruct((n, d), F32))
        out_specs.append(_rows(tm, d))
    return pl.pallas_call(
        functools.partial(_ffn_prompt_kernel, tps=tps, final=final),
        out_shape=outs, grid=(n // tm,), in_specs=in_specs, out_specs=out_specs,
        scratch_shapes=[pltpu.VMEM((tm, d), BF16), pltpu.VMEM((tm, hid), BF16), pltpu.VMEM((SUBLANES, hid), F32)],
        compiler_params=_cparams("arbitrary"), name="ffn_prompt",
    )(*ins)


def _ffn_sample_kernel(y_ref, ng_ref, wg_ref, wu_ref, cw_ref, cb_ref, p2_ref, p1_ref, wout_ref, *rest, final):
    if final:
        fg_ref, yout_ref, gate_ref, yfin_ref, h_scr, acc_scr = rest
    else:
        yout_ref, gate_ref, h_scr, acc_scr = rest
    j = pl.program_id(0)

    @pl.when(j == 0)
    def _():
        h_scr[...] = _rms(y_ref[...], ng_ref[...]).astype(BF16)
        acc_scr[...] = jnp.zeros(acc_scr.shape, F32)

    hb = h_scr[...]
    gate = jnp.dot(hb, wg_ref[...], preferred_element_type=F32)
    up = jnp.dot(hb, wu_ref[...], preferred_element_type=F32)
    gate_ref[...] = gate
    act = _ffn_act(gate, p1_ref[...], p2_ref[...], up, cw_ref[...], cb_ref[...])
    acc_scr[...] += _bdot(act, wout_ref[...])

    @pl.when(j == pl.num_programs(0) - 1)
    def _():
        y_new = y_ref[...] + acc_scr[...]
        yout_ref[...] = y_new
        if final:
            yfin_ref[...] = _rms(y_new, fg_ref[...])


def _ffn_sample(y, ng, win, cw, cb, p2, p1, wout, fg):
    n, d = y.shape
    hid = wout.shape[0]
    tf = hid // 2 if (hid // 2) % LANES == 0 else FFN_CHUNK
    nf = hid // tf
    final = fg is not None
    keep = lambda shape: pl.BlockSpec(shape, lambda j: (0,) * len(shape))
    chunk = lambda rows: pl.BlockSpec((rows, tf), lambda j: (0, j))
    ins = [y, ng, win, win, cw, cb, p2, p1, wout]
    in_specs = [keep((n, d)), keep(ng.shape), chunk(d), pl.BlockSpec((d, tf), lambda j: (0, nf + j)),
                chunk(FFN_CONV), chunk(1), chunk(n), chunk(n), pl.BlockSpec((tf, d), lambda j: (j, 0))]
    outs = [jax.ShapeDtypeStruct((n, d), F32), jax.ShapeDtypeStruct((n, hid), F32)]
    out_specs = [keep((n, d)), chunk(n)]
    if final:
        ins.append(fg)
        in_specs.append(keep(fg.shape))
        outs.append(jax.ShapeDtypeStruct((n, d), F32))
        out_specs.append(keep((n, d)))
    return pl.pallas_call(
        functools.partial(_ffn_sample_kernel, final=final),
        out_shape=outs, grid=(nf,), in_specs=in_specs, out_specs=out_specs,
        scratch_shapes=[pltpu.VMEM((n, d), BF16), pltpu.VMEM((n, d), F32)],
        compiler_params=_cparams("arbitrary"), name="ffn_sample",
    )(*ins)


def kernel(x_prompt, x_sample, cache_moba_k, cache_moba_v, state_pool, state_rwkv_wkv, state_rwkv_shift, state_ffn_conv, page_table, norm_mix_g, norm_ffn_g, norm_final_g, rel_bias, gm_w_in, gm_ln_g, gm_ln_b, gm_w_s, gm_b_s, gm_w_out, moba_w_qkv, moba_w_o, pool_w, pool_scale, rwkv_mu, rwkv_w_r, rwkv_w_k, rwkv_w_v, rwkv_w_o, rwkv_w0, rwkv_w1, rwkv_w2, rwkv_a0, rwkv_a1, rwkv_a2, rwkv_g1, rwkv_g2, rwkv_k_k, rwkv_k_a, rwkv_r_k, rwkv_lnx_g, rwkv_lnx_b, ffn_w_in, ffn_conv_w, ffn_conv_b, ffn_w_out):
    bp, seq, d = x_prompt.shape
    bs = x_sample.shape[0]
    depth = norm_mix_g.shape[0]
    assert x_sample.shape[1] == 1 and depth == 4
    past_len = page_table.shape[1] * PAGE_SIZE
    assert seq % MOBA_BLOCK == 0 and past_len % MOBA_BLOCK == 0 and past_len % GM_CHUNK == 0
    row = lambda vct: vct.reshape(1, -1)
    bf = lambda m: m.astype(BF16)
    yp = x_prompt.reshape(bp * seq, d)
    ys = x_sample.reshape(bs, d)
    tm = TOKEN_TILE
    assert seq % tm == 0 and seq % RWKV_PROJ_TILE == 0 and seq % RWKV_CHUNK == 0
    conv_p, conv_s = [], []

    def ffn(i, yp, ys):
        last = i == depth - 1
        win, wout = bf(ffn_w_in[i]), bf(ffn_w_out[i])
        cw, cb, ng = ffn_conv_w[i], row(ffn_conv_b[i]), row(norm_ffn_g[i])
        fg = row(norm_final_g) if last else None
        res_p = _ffn_prompt(yp, ng, win, cw, cb, wout, fg, tm=tm, seq=seq)
        st = state_ffn_conv[i]
        res_s = _ffn_sample(ys, ng, win, cw, cb, st[:, 0], st[:, 1], wout, fg)
        conv_p.append(res_p[1])
        conv_s.append(jnp.stack([st[:, 1], res_s[1]], axis=1))
        if last:
            return res_p[2], res_s[2]
        return res_p[0], res_s[0]

    ng = row(norm_mix_g[0])
    width = gm_w_out.shape[1]
    gd = width // GM_GROUPS
    gm_in, gm_out = bf(gm_w_in[0]), bf(gm_w_out[0])
    lng, lnb = row(gm_ln_g[0]), row(gm_ln_b[0])
    sb_prompt = jnp.repeat(gm_b_s[0].T, gd, axis=1)
    (yp,) = _gmlp(yp, ng, gm_in, lng, lnb, gm_w_s[0], sb_prompt, gm_out, tm=tm, sample=False)
    sa_first = row(jnp.repeat(gm_w_s[0][:, 0, 0], gd))
    sb_first = row(jnp.repeat(gm_b_s[0][:, 0], gd))
    ys, gm_v = _gmlp(ys, ng, gm_in, lng, lnb, sa_first, sb_first, gm_out, tm=bs, sample=True)
    gm_v_sample = gm_v.reshape(1, bs, 1, width)
    yp, ys = ffn(0, yp, ys)

    ng = row(norm_mix_g[1])
    w_qkv, w_o = bf(moba_w_qkv[0]), bf(moba_w_o[0])
    heads = MOBA_HEADS
    hd = d // heads
    qp, kp_t, vp_t = _qkv_prompt(yp, ng, w_qkv[:, :d], w_qkv[:, d:2 * d].T, w_qkv[:, 2 * d:].T,
                                 tm=tm, batch=bp, seq=seq)
    qs, ks, vs = _norm_linear(ys, ng, w_qkv, 3, tm=bs, name="moba_qkv_sample")
    blk = MOBA_BLOCK
    qi = jnp.arange(blk, dtype=I32)[:, None]
    ki = jnp.arange(blk, dtype=I32)[None, :]
    bkt_tiles = _t5_bucket_table(jnp.stack([qi - ki, blk + qi - ki]))
    bias_tiles = _bias_tiles(rel_bias, bkt_tiles)
    op = _moba_prompt(rel_bias, qp, kp_t, vp_t, bias_tiles, batch=bp, seq=seq, head_dim=hd)
    bkt_rows = _t5_bucket_table(past_len - jnp.arange(past_len, dtype=I32))
    rb_pad = jnp.pad(rel_bias, ((0, 0), (0, LANES - heads)))
    bias_rows = _bias_rows(rb_pad, jnp.broadcast_to(bkt_rows[:, None], (past_len, LANES)))
    os_ = _moba_sample(page_table, qs, ks, vs, cache_moba_k, cache_moba_v, 0, bias_rows, rel_bias, head_dim=hd)
    yp = _linear_res(op, w_o, yp, tm=tm, name="moba_out_prompt")
    ys = _linear_res(os_, w_o, ys, tm=bs, name="moba_out_sample")
    moba_k_prompt = kp_t.reshape(1, bp, heads, hd, seq).transpose(0, 1, 4, 2, 3)
    moba_v_prompt = vp_t.reshape(1, bp, heads, hd, seq).transpose(0, 1, 4, 2, 3)
    moba_k_sample = ks.reshape(1, bs, 1, heads, hd)
    moba_v_sample = vs.reshape(1, bs, 1, heads, hd)
    yp, ys = ffn(1, yp, ys)

    ng = row(norm_mix_g[2])
    pw, psc = bf(pool_w[0]), row(pool_scale[0])
    yp, pool_p = _pool_prompt(yp, ng, pw, psc, tm=tm, seq=seq)
    ys, hs = _pool_sample(ys, ng, jnp.swapaxes(state_pool[0], 0, 1), pw, psc)
    pool_prompt = pool_p[None]
    pool_sample = jnp.concatenate([state_pool[0][:, 1:], hs[:, None]], axis=1)[None]
    yp, ys = ffn(2, yp, ys)

    ng = row(norm_mix_g[3])
    mats = [bf(m[0]) for m in (rwkv_w_r, rwkv_w_k, rwkv_w_v, rwkv_w1, rwkv_w2, rwkv_a1, rwkv_a2,
                               rwkv_g1, rwkv_g2)]
    vec_in = jnp.stack([rwkv_w0[0], rwkv_a0[0], rwkv_k_k[0], rwkv_k_a[0]])
    vec_out = jnp.stack([rwkv_r_k[0].reshape(-1), rwkv_lnx_g[0], rwkv_lnx_b[0]])
    rh = d // RWKV_HEAD_DIM
    *seqs_p, gp, shp = _rwkv_proj(yp, ng, rwkv_mu[0], mats, vec_in, None, tm=RWKV_PROJ_TILE, seq=seq,
                                  sample=False)
    *seqs_s, gs, shs = _rwkv_proj(ys, ng, rwkv_mu[0], mats, vec_in, state_rwkv_shift[0], tm=bs, seq=1,
                                  sample=True)
    o_p, st_p = _rwkv_chunked(*seqs_p, batch=bp, seq=seq)
    hd_r = RWKV_HEAD_DIM
    wkv_p = st_p.reshape(bp, rh // 2, hd_r, 2, hd_r).transpose(0, 1, 3, 2, 4).reshape(bp, rh, hd_r, hd_r)
    o_s_t, wkv_s_t = _rwkv_step(*[a.T for a in seqs_s], jnp.transpose(state_rwkv_wkv[0], (1, 2, 3, 0)))
    o_s, wkv_s = o_s_t.T, jnp.transpose(wkv_s_t, (3, 0, 1, 2))
    w_o = bf(rwkv_w_o[0])
    r_p, _, k_p, v_p = seqs_p[:4]
    r_s, _, k_s, v_s = seqs_s[:4]
    yp = _rwkv_out(o_p, r_p, k_p, v_p, gp, yp, vec_out, w_o, tm=tm, name="rwkv_out_prompt")
    ys = _rwkv_out(o_s.reshape(bs, d), r_s, k_s, v_s, gs, ys, vec_out, w_o, tm=bs, name="rwkv_out_sample")
    yp, ys = ffn(3, yp, ys)

    return (yp.reshape(bp, seq, d), ys.reshape(bs, 1, d), gm_v_sample, moba_k_prompt, moba_v_prompt,
            moba_k_sample, moba_v_sample, pool_prompt, pool_sample, wkv_p[None], wkv_s[None],
            shp.reshape(1, bp, d), shs[None], jnp.stack(conv_p), jnp.stack(conv_s))
```

```python
import functools
import math

import jax
import jax.numpy as jnp
from jax import lax
from jax.experimental import pallas as pl
from jax.experimental.pallas import tpu as pltpu

F32 = jnp.float32
BF16 = jnp.bfloat16
I32 = jnp.int32

LANES = 128
SUBLANES = 8
VMEM_LIMIT_BYTES = 56 * 2**20

TOKEN_TILE = 512
RWKV_PROJ_TILE = 256

RMS_EPS = 1e-6
GM_LN_EPS = 1e-5
GM_CHUNK = 128
GM_GROUPS = 8
MOBA_HEADS = 16
MOBA_BLOCK = 256
MOBA_TOPK = 3
REL_BUCKETS = 32
REL_MAX_DIST = 128
PAGE_SIZE = 128
POOL_WINDOWS = (2, 4, 8, 16)
POOL_CTX = max(POOL_WINDOWS) - 1
RWKV_HEAD_DIM = 64
RWKV_LNX_EPS = 64e-5
FFN_CONV = 3
NEG_INF = float("-inf")

assert MOBA_BLOCK >= REL_MAX_DIST


def _cparams(*sem):
    return pltpu.CompilerParams(dimension_semantics=sem, vmem_limit_bytes=VMEM_LIMIT_BYTES)


def _whole(shape):
    nd = len(shape)
    return pl.BlockSpec(shape, lambda *_: (0,) * nd, pipeline_mode=pl.Buffered(1))


def _rows(tm, width):
    return pl.BlockSpec((tm, width), lambda i: (i, 0))


def _rms(x, g):
    return x * lax.rsqrt(jnp.mean(x * x, axis=-1, keepdims=True) + RMS_EPS) * g


def _bdot(a, b):
    return jnp.dot(a.astype(BF16), b.astype(BF16), preferred_element_type=F32)


def _nt_dot(a, b):
    return lax.dot_general(a.astype(BF16), b.astype(BF16), (((1,), (1,)), ((), ())),
                           preferred_element_type=F32)


def _split(x):
    hi = x.astype(BF16)
    lo = (x - hi.astype(F32)).astype(BF16)
    return hi, lo


def _split_dot(x, m):
    hi, lo = _split(x)
    return (jnp.dot(hi, m, preferred_element_type=F32)
            + jnp.dot(lo, m, preferred_element_type=F32))


def _same_head(n, head_dim):
    shift = int(math.log2(head_dim))
    r = lax.broadcasted_iota(I32, (n, n), 0) >> shift
    c = lax.broadcasted_iota(I32, (n, n), 1) >> shift
    return jnp.where(r == c, 1.0, 0.0).astype(BF16)


def _seg_sum(x, head_dim):
    g = _same_head(LANES, head_dim)
    parts = [_split_dot(x[:, i:i + LANES], g) for i in range(0, x.shape[1], LANES)]
    return jnp.concatenate(parts, axis=1)


def _gelu(x):
    return 0.5 * x * (1.0 + jnp.tanh(0.7978845608028654 * (x + 0.044715 * x * x * x)))


def _sigmoid(x):
    return 1.0 / (1.0 + jnp.exp(-x))


def _top_mask(gate, idx, axis, n_valid):
    cur = gate
    sel = jnp.zeros(gate.shape, F32)
    for _ in range(MOBA_TOPK):
        m = jnp.max(cur, axis=axis, keepdims=True)
        first = jnp.min(jnp.where(cur == m, idx, n_valid), axis=axis, keepdims=True)
        pick = (idx == first) & (m > NEG_INF)
        sel = jnp.where(pick, 1.0, sel)
        cur = jnp.where(pick, NEG_INF, cur)
    return sel


def _gmlp_kernel(x_ref, ng_ref, win_ref, lng_ref, lnb_ref, sa_ref, sb_ref, wout_ref, y_ref, aux_ref,
                 *, sample):
    x = x_ref[...]
    tm, width = x.shape[0], wout_ref.shape[0]
    h = _rms(x, ng_ref[...])
    z = _gelu(_bdot(h, win_ref[...]))
    u, v = z[:, :width], z[:, width:]
    mu = jnp.mean(v, axis=-1, keepdims=True)
    d = v - mu
    var = jnp.mean(d * d, axis=-1, keepdims=True)
    v = d * lax.rsqrt(var + GM_LN_EPS) * lng_ref[...] + lnb_ref[...]
    if sample:
        aux_ref[...] = v
        s = v * sa_ref[...] + sb_ref[...]
    else:
        gd = width // GM_GROUPS
        causal = (lax.broadcasted_iota(I32, (GM_CHUNK, GM_CHUNK), 0)
                  >= lax.broadcasted_iota(I32, (GM_CHUNK, GM_CHUNK), 1))
        for g in range(GM_GROUPS):
            wg = jnp.where(causal, sa_ref[g], 0.0).astype(BF16)
            cols = slice(g * gd, (g + 1) * gd)
            for c in range(tm // GM_CHUNK):
                rows = slice(c * GM_CHUNK, (c + 1) * GM_CHUNK)
                aux_ref[rows, cols] = (jnp.dot(wg, v[rows, cols].astype(BF16), preferred_element_type=F32)
                                       + sb_ref[:, cols])
        s = aux_ref[...]
    y_ref[...] = x + _bdot(u * s, wout_ref[...])


def _gmlp(x, ng, win, lng, lnb, sa, sb, wout, *, tm, sample):
    n, d = x.shape
    width = wout.shape[0]
    outs = [jax.ShapeDtypeStruct((n, d), F32)]
    out_specs = [_rows(tm, d)]
    scratch = []
    if sample:
        outs.append(jax.ShapeDtypeStruct((n, width), F32))
        out_specs.append(_rows(tm, width))
    else:
        scratch.append(pltpu.VMEM((tm, width), F32))
    return pl.pallas_call(
        functools.partial(_gmlp_kernel, sample=sample),
        out_shape=outs, grid=(n // tm,),
        in_specs=[_rows(tm, d), _whole(ng.shape), _whole(win.shape), _whole(lng.shape), _whole(lnb.shape),
                  _whole(sa.shape), _whole(sb.shape), _whole(wout.shape)],
        out_specs=out_specs, scratch_shapes=scratch,
        compiler_params=_cparams("parallel"), name="gmlp_sample" if sample else "gmlp_prompt",
    )(x, ng, win, lng, lnb, sa, sb, wout)


def _norm_linear_kernel(x_ref, ng_ref, w_ref, *o_refs):
    z = _bdot(_rms(x_ref[...], ng_ref[...]), w_ref[...])
    wd = z.shape[1] // len(o_refs)
    for i, o_ref in enumerate(o_refs):
        o_ref[...] = z[:, i * wd:(i + 1) * wd]


def _norm_linear(x, ng, w, n_out, *, tm, name):
    n, d = x.shape
    wd = w.shape[1] // n_out
    return pl.pallas_call(
        _norm_linear_kernel,
        out_shape=[jax.ShapeDtypeStruct((n, wd), F32)] * n_out, grid=(n // tm,),
        in_specs=[_rows(tm, d), _whole(ng.shape), _whole(w.shape)],
        out_specs=[_rows(tm, wd)] * n_out,
        compiler_params=_cparams("parallel"), name=name,
    )(x, ng, w)


def _qkv_prompt_kernel(x_ref, ng_ref, wq_ref, wkt_ref, wvt_ref, q_ref, kt_ref, vt_ref):
    h = _rms(x_ref[...], ng_ref[...]).astype(BF16)
    q_ref[...] = jnp.dot(h, wq_ref[...], preferred_element_type=F32)
    kt_ref[0] = _nt_dot(wkt_ref[...], h)
    vt_ref[0] = _nt_dot(wvt_ref[...], h)


def _qkv_prompt(x, ng, wq, wkt, wvt, *, tm, batch, seq):
    n, d = x.shape
    tps = seq // tm
    t_spec = pl.BlockSpec((1, d, tm), lambda i: (i // tps, 0, i % tps))
    return pl.pallas_call(
        _qkv_prompt_kernel,
        out_shape=[jax.ShapeDtypeStruct((n, d), F32)] + [jax.ShapeDtypeStruct((batch, d, seq), F32)] * 2,
        grid=(n // tm,),
        in_specs=[_rows(tm, d), _whole(ng.shape), _whole(wq.shape), _whole(wkt.shape), _whole(wvt.shape)],
        out_specs=[_rows(tm, d), t_spec, t_spec],
        compiler_params=_cparams("parallel"), name="moba_qkv_prompt",
    )(x, ng, wq, wkt, wvt)


def _linear_res_kernel(a_ref, w_ref, y_ref, o_ref):
    o_ref[...] = y_ref[...] + _bdot(a_ref[...], w_ref[...])


def _linear_res(a, w, y, *, tm, name):
    n, d = y.shape
    return pl.pallas_call(
        _linear_res_kernel, out_shape=jax.ShapeDtypeStruct((n, d), F32), grid=(n // tm,),
        in_specs=[_rows(tm, a.shape[1]), _whole(w.shape), _rows(tm, d)], out_specs=_rows(tm, d),
        compiler_params=_cparams("parallel"), name=name,
    )(a, w, y)


def _t5_bucket_table(rel):
    n = jnp.maximum(rel, 0)
    exact = REL_BUCKETS // 2
    nf = jnp.maximum(n, 1).astype(F32)
    large = exact + (jnp.log(nf / exact) / math.log(REL_MAX_DIST / exact)
                     * (REL_BUCKETS - exact)).astype(I32)
    return jnp.where(n < exact, n, jnp.minimum(large, REL_BUCKETS - 1)).astype(I32)


def _bias_tiles_kernel(rb_ref, bkt_ref, o_ref):
    h = pl.program_id(0)
    bkt = bkt_ref[...]
    acc = jnp.zeros(bkt.shape, F32)
    for b in range(REL_BUCKETS):
        acc = jnp.where(bkt == b, rb_ref[b, h], acc)
    future = (lax.broadcasted_iota(I32, bkt.shape, 2) > lax.broadcasted_iota(I32, bkt.shape, 1))
    own_block = lax.broadcasted_iota(I32, bkt.shape, 0) == 0
    o_ref[0] = jnp.where(future & own_block, NEG_INF, acc)


def _bias_tiles(rel_bias, bkt):
    heads = rel_bias.shape[1]
    return pl.pallas_call(
        _bias_tiles_kernel, out_shape=jax.ShapeDtypeStruct((heads,) + bkt.shape, F32), grid=(heads,),
        in_specs=[pl.BlockSpec(memory_space=pltpu.SMEM), _whole(bkt.shape)],
        out_specs=pl.BlockSpec((1,) + bkt.shape, lambda h: (h, 0, 0, 0)),
        compiler_params=_cparams("parallel"), name="moba_bias_tiles",
    )(rel_bias, bkt)


def _bias_rows_kernel(rb_ref, bkt_ref, o_ref):
    bkt = bkt_ref[...]
    acc = jnp.zeros(bkt.shape, F32)
    for b in range(REL_BUCKETS):
        acc = jnp.where(bkt == b, rb_ref[b:b + 1, :], acc)
    o_ref[...] = acc


def _bias_rows(rb_pad, bkt):
    return pl.pallas_call(
        _bias_rows_kernel, out_shape=jax.ShapeDtypeStruct(bkt.shape, F32), grid=(1,),
        in_specs=[_whole(rb_pad.shape), _whole(bkt.shape)], out_specs=_whole(bkt.shape),
        compiler_params=_cparams("arbitrary"), name="moba_bias_rows",
    )(rb_pad, bkt)


def _moba_prompt_kernel(rb_ref, q_ref, k_ref, v_ref, bias_ref, o_ref, kb_scr, vh_scr, kmean_scr, *, head_dim):
    pair = pl.program_id(1)
    blk = MOBA_BLOCK
    seq = k_ref.shape[2]
    n_blocks = seq // blk
    assert LANES == 2 * head_dim
    shift = int(math.log2(head_dim))
    scale = head_dim ** -0.5
    assert shift % 2 == 0 and n_blocks <= kmean_scr.shape[0]

    def prepare():
        kt, vt = k_ref[0], v_ref[0]
        second = (lax.broadcasted_iota(I32, (LANES, seq), 0) >> shift) == 1
        kb_scr[...] = kt.astype(BF16)
        vh_scr[0] = jnp.where(second, 0.0, vt).astype(BF16)
        vh_scr[1] = jnp.where(second, vt, 0.0).astype(BF16)
        rows = kmean_scr.shape[0]
        in_block = jnp.where((lax.broadcasted_iota(I32, (rows, seq), 1) >> int(math.log2(blk)))
                             == lax.broadcasted_iota(I32, (rows, seq), 0), 1.0, 0.0).astype(BF16)
        k_hi, k_lo = _split(kt)
        nt = lambda a, b: lax.dot_general(a, b, (((1,), (1,)), ((), ())), preferred_element_type=F32)
        kmean_scr[...] = (nt(in_block, k_hi) + nt(in_block, k_lo)) * (1.0 / blk)

    prepare()
    second = (lax.broadcasted_iota(I32, (blk, LANES), 1) >> shift) == 1
    km_hi, km_lo = _split(kmean_scr[...])
    nt = lambda a, b: lax.dot_general(a, b, (((1,), (1,)), ((), ())), preferred_element_type=F32)
    block_id = lax.broadcasted_iota(I32, (kmean_scr.shape[0], 2 * blk), 0)
    far_bias = [rb_ref[REL_BUCKETS - 1, pair * 2 + hh] for hh in range(2)]

    def select(own):
        q = q_ref[own * blk:(own + 1) * blk, :]
        q2 = jnp.concatenate([jnp.where(second, 0.0, q), jnp.where(second, q, 0.0)], axis=0)
        q_hi, q_lo = _split(q2)
        gate = nt(km_hi, q_hi) + nt(km_lo, q_hi) + nt(km_hi, q_lo)
        sel = _top_mask(jnp.where(block_id < own, gate, NEG_INF), block_id, 0, gate.shape[0]).T
        return (q2 * scale).astype(BF16), sel

    def scores(own, q_scaled):
        return jnp.dot(q_scaled, kb_scr[:, 0:(own + 1) * blk], preferred_element_type=F32)

    def mask(own, s, sel):
        head_rows = []
        for hh in range(2):
            rows = slice(hh * blk, (hh + 1) * blk)
            tiles = []
            for n in range(own + 1):
                t = s[rows, n * blk:(n + 1) * blk]
                chosen = sel[rows, n:n + 1] > 0.0
                if n == own:
                    t = t + bias_ref[hh, 0]
                elif n == own - 1:
                    t = jnp.where(chosen, t + bias_ref[hh, 1], NEG_INF)
                else:
                    t = t + jnp.where(chosen, far_bias[hh], NEG_INF)
                tiles.append(t)
            head_rows.append(jnp.concatenate(tiles, axis=1))
        return jnp.concatenate(head_rows, axis=0)

    def weights(s):
        p = jnp.exp(s - jnp.max(s, axis=-1, keepdims=True))
        return p, jnp.sum(p, axis=-1, keepdims=True)

    def output(own, p, l):
        keys = (own + 1) * blk
        p_pair = jnp.concatenate([p[:blk], p[blk:]], axis=1).astype(BF16)
        v_pair = jnp.concatenate([vh_scr[0, :, 0:keys], vh_scr[1, :, 0:keys]], axis=1)
        o_ref[own * blk:(own + 1) * blk, :] = nt(p_pair, v_pair) / jnp.where(second, l[blk:], l[:blk])

    order = list(range(n_blocks))
    groups = [(order[i], order[-1 - i]) for i in range(n_blocks // 2)] + ([(order[n_blocks // 2],)] * (n_blocks % 2))
    for group in groups:
        picked = [select(own) for own in group]
        raw = [scores(own, qs) for own, (qs, _) in zip(group, picked)]
        masked = [mask(own, s, sel) for own, s, (_, sel) in zip(group, raw, picked)]
        probs = [weights(s) for s in masked]
        for own, (p, l) in zip(group, probs):
            output(own, p, l)


def _moba_prompt(rel_bias, q, k, v, bias_tiles, *, batch, seq, head_dim):
    n, d = q.shape
    blk = MOBA_BLOCK
    hpt = LANES // head_dim
    return pl.pallas_call(
        functools.partial(_moba_prompt_kernel, head_dim=head_dim),
        out_shape=jax.ShapeDtypeStruct((n, d), F32), grid=(batch, d // LANES),
        in_specs=[pl.BlockSpec(memory_space=pltpu.SMEM),
                  pl.BlockSpec((seq, LANES), lambda b, p: (b, p)),
                  pl.BlockSpec((1, LANES, seq), lambda b, p: (b, p, 0)),
                  pl.BlockSpec((1, LANES, seq), lambda b, p: (b, p, 0)),
                  pl.BlockSpec((hpt, 2, blk, blk), lambda b, p: (p, 0, 0, 0))],
        out_specs=pl.BlockSpec((seq, LANES), lambda b, p: (b, p)),
        scratch_shapes=[pltpu.VMEM((LANES, seq), BF16), pltpu.VMEM((hpt, LANES, seq), BF16),
                        pltpu.VMEM((2 * SUBLANES, LANES), F32)],
        compiler_params=_cparams("parallel", "parallel"), name="moba_prompt_attn",
    )(rel_bias, q, k, v, bias_tiles)


def _moba_sample_kernel(pt_ref, q_ref, kn_ref, vn_ref, *rest, n_pages, head_dim):
    del pt_ref
    k_refs, v_refs = rest[:n_pages], rest[n_pages:2 * n_pages]
    bias_ref, rb0_ref, o_ref = rest[2 * n_pages:]
    d = q_ref.shape[2]
    heads = d // head_dim
    shift = int(math.log2(head_dim))
    per_block = MOBA_BLOCK // PAGE_SIZE
    blocks = range(n_pages // per_block)
    scale = head_dim ** -0.5
    head_lanes = jnp.where((lax.broadcasted_iota(I32, (heads, d), 1) >> shift)
                           == lax.broadcasted_iota(I32, (heads, d), 0), 1.0, 0.0)
    q = head_lanes * q_ref[0]
    q_hi, q_lo = _split(q)
    q2 = jnp.concatenate([q_hi, q_lo], axis=0)
    block_of = lambda refs, n: jnp.concatenate(
        [refs[per_block * n + j][0, 0].reshape(d, PAGE_SIZE) for j in range(per_block)], axis=1).astype(BF16)
    raw = [jnp.dot(q2, block_of(k_refs, n), preferred_element_type=F32) for n in blocks]
    raw = [r[:heads] + r[heads:] for r in raw]
    block_id = lax.broadcasted_iota(I32, (heads, LANES), 1)
    gate = jnp.full((heads, LANES), NEG_INF, F32)
    for n in blocks:
        gate = jnp.where(block_id == n, jnp.sum(raw[n], axis=-1, keepdims=True), gate)
    sel = _top_mask(gate, block_id, -1, LANES)
    s_new = jnp.sum(q * kn_ref[0], axis=-1, keepdims=True) * scale + rb0_ref[:, 0:1]
    s = [raw[n] * scale + bias_ref[n] for n in blocks]
    m = s_new
    for n in blocks:
        m = jnp.maximum(m, jnp.where(sel[:, n:n + 1] > 0.0, jnp.max(s[n], axis=-1, keepdims=True), NEG_INF))
    p = [jnp.exp(jnp.where(sel[:, n:n + 1] > 0.0, s[n] - m, NEG_INF)) for n in blocks]
    p_new = jnp.exp(s_new - m)
    den = p_new
    for n in blocks:
        den = den + jnp.sum(p[n], axis=-1, keepdims=True)
    p_all = jnp.concatenate(p, axis=1).astype(BF16)
    v_all = jnp.concatenate([block_of(v_refs, n) for n in blocks], axis=1)
    acc = lax.dot_general(p_all, v_all, (((1,), (1,)), ((), ())), preferred_element_type=F32)
    out = (acc + p_new * vn_ref[0]) * head_lanes / den
    o_ref[0] = jnp.sum(out, axis=0, keepdims=True)


def _moba_sample(page_table, q, k_new, v_new, cache_k, cache_v, layer, bias_rows, rel_bias, *, head_dim):
    nb, d = q.shape
    n_pages = page_table.shape[1]
    n_blocks = n_pages * PAGE_SIZE // MOBA_BLOCK
    heads = d // head_dim
    cache_kt = jnp.transpose(cache_k, (0, 1, 3, 4, 2))
    cache_vt = jnp.transpose(cache_v, (0, 1, 3, 4, 2))
    row = lambda x: x.reshape(nb, 1, d)
    vec = pl.BlockSpec((1, 1, d), lambda s, pt: (s, 0, 0))
    page = lambda j: pl.BlockSpec((1, 1, heads, head_dim, PAGE_SIZE), lambda s, pt: (layer, pt[s, j], 0, 0, 0))
    const = lambda shape: pl.BlockSpec(shape, lambda s, pt: (0,) * len(shape))
    bias = bias_rows[:, :heads].reshape(n_blocks, MOBA_BLOCK, heads).transpose(0, 2, 1)
    rb0 = jnp.broadcast_to(rel_bias[0][:, None], (heads, LANES))
    pages = [page(j) for j in range(n_pages)]
    out = pl.pallas_call(
        functools.partial(_moba_sample_kernel, n_pages=n_pages, head_dim=head_dim),
        out_shape=jax.ShapeDtypeStruct((nb, 1, d), F32),
        grid_spec=pltpu.PrefetchScalarGridSpec(
            num_scalar_prefetch=1, grid=(nb,),
            in_specs=[vec, vec, vec, *pages, *pages, const(bias.shape), const(rb0.shape)],
            out_specs=vec),
        compiler_params=_cparams("parallel"), name="moba_sample_attn",
    )(page_table, row(q), row(k_new), row(v_new), *([cache_kt] * n_pages), *([cache_vt] * n_pages), bias, rb0)
    return out.reshape(nb, d)


POOL_CARRY = 16
assert all(b == 2 * a for a, b in zip((1,) + POOL_WINDOWS, POOL_WINDOWS)) and POOL_CARRY > POOL_CTX


def _pool_prompt_kernel(y_ref, ng_ref, w_ref, sc_ref, yout_ref, st_ref, carry_scr, *, tps):
    i = pl.program_id(0)
    y = y_ref[...]
    tm, d = y.shape
    gd = d // len(POOL_WINDOWS)
    h = _rms(y, ng_ref[...])

    @pl.when(i % tps == 0)
    def _():
        carry_scr[...] = jnp.zeros(carry_scr.shape, F32)

    rows = jnp.concatenate([carry_scr[...], h], axis=0)
    carry_scr[...] = h[tm - POOL_CARRY:, :]
    st_ref[0] = rows[tm + POOL_CARRY - POOL_CTX:, :]
    pos = (i % tps) * tm + lax.broadcasted_iota(I32, (tm, 1), 0) + 1
    sums = rows
    for g, win in enumerate(POOL_WINDOWS):
        cols = slice(g * gd, (g + 1) * gd)
        sums = sums[:, (gd if g else 0):]
        sums = sums + pltpu.roll(sums, win // 2, axis=0)
        cnt = jnp.minimum(pos, win).astype(F32)
        mixed = _bdot(sums[POOL_CARRY:, 0:gd] / cnt - h[:, cols], w_ref[g])
        yout_ref[:, cols] = y[:, cols] + mixed * sc_ref[:, cols]


def _pool_prompt(y, ng, w, sc, *, tm, seq):
    n, d = y.shape
    tps = seq // tm
    return pl.pallas_call(
        functools.partial(_pool_prompt_kernel, tps=tps),
        out_shape=[jax.ShapeDtypeStruct((n, d), F32), jax.ShapeDtypeStruct((n // seq, POOL_CTX, d), F32)],
        grid=(n // tm,),
        in_specs=[_rows(tm, d), _whole(ng.shape), _whole(w.shape), _whole(sc.shape)],
        out_specs=[_rows(tm, d), pl.BlockSpec((1, POOL_CTX, d), lambda i: (i // tps, 0, 0))],
        scratch_shapes=[pltpu.VMEM((POOL_CARRY, d), F32)],
        compiler_params=_cparams("arbitrary"), name="pool_prompt",
    )(y, ng, w, sc)


def _pool_sample_kernel(y_ref, ng_ref, prev_ref, w_ref, sc_ref, yout_ref, h_ref):
    y = y_ref[...]
    d = y.shape[1]
    gd = d // len(POOL_WINDOWS)
    h = _rms(y, ng_ref[...])
    h_ref[...] = h
    for g, win in enumerate(POOL_WINDOWS):
        cols = slice(g * gd, (g + 1) * gd)
        wsum = h[:, cols]
        for back in range(1, win):
            wsum = wsum + prev_ref[POOL_CTX - back, :, cols]
        mixed = _bdot(wsum / float(win) - h[:, cols], w_ref[g])
        yout_ref[:, cols] = y[:, cols] + mixed * sc_ref[:, cols]


def _pool_sample(y, ng, prev_t, w, sc):
    n, d = y.shape
    return pl.pallas_call(
        _pool_sample_kernel, out_shape=[jax.ShapeDtypeStruct((n, d), F32)] * 2, grid=(1,),
        in_specs=[_whole(y.shape), _whole(ng.shape), _whole(prev_t.shape), _whole(w.shape), _whole(sc.shape)],
        out_specs=[_whole(y.shape)] * 2,
        compiler_params=_cparams("arbitrary"), name="pool_sample",
    )(y, ng, prev_t, w, sc)


def _rwkv_proj_kernel(y_ref, ng_ref, mu_ref, wr_ref, wk_ref, wv_ref, w1_ref, w2_ref, a1_ref, a2_ref,
                      g1_ref, g2_ref, vec_ref, *rest, tps, sample):
    if sample:
        prev_ref, r_o, w_o, k_o, v_o, kk_o, kka_o, g_o, sh_o = rest
    else:
        r_o, w_o, k_o, v_o, kk_o, kka_o, g_o, sh_o, hs_scr = rest
    y = y_ref[...]
    tm = y.shape[0]
    h = _rms(y, ng_ref[...])
    if sample:
        h_prev = prev_ref[...]
        sh_o[...] = h
    else:
        i = pl.program_id(0)

        @pl.when(i % tps == 0)
        def _():
            hs_scr[...] = jnp.zeros(hs_scr.shape, F32)

        h_prev = pltpu.roll(jnp.concatenate([hs_scr[...], h], axis=0), 1, axis=0)[SUBLANES:]
        hs_scr[...] = h[tm - SUBLANES:, :]
        sh_o[0] = h[tm - 1:tm, :]
    xx = h_prev - h
    mix = lambda m: h + xx * mu_ref[m:m + 1, :]
    r = _bdot(mix(0), wr_ref[...])
    k = _bdot(mix(2), wk_ref[...])
    v = _bdot(mix(3), wv_ref[...])
    z = vec_ref[0:1, :] + _bdot(jnp.tanh(_bdot(mix(1), w1_ref[...])), w2_ref[...])
    w_log = jnp.minimum(z, 0.0) - jnp.log(1.0 + jnp.exp(-jnp.abs(z))) - 0.5
    a = _sigmoid(vec_ref[1:2, :] + _bdot(_bdot(mix(4), a1_ref[...]), a2_ref[...]))
    g_o[...] = _bdot(_sigmoid(_bdot(mix(5), g1_ref[...])), g2_ref[...])
    kk = k * vec_ref[2:3, :]
    kk = kk / jnp.maximum(jnp.sqrt(_seg_sum(kk * kk, RWKV_HEAD_DIM)), 1e-12)
    r_o[...] = r
    w_o[...] = -jnp.exp(w_log)
    k_o[...] = k * (1.0 + (a - 1.0) * vec_ref[3:4, :])
    v_o[...] = v
    kk_o[...] = kk
    kka_o[...] = kk * a


def _rwkv_proj(y, ng, mu, mats, vec, prev, *, tm, seq, sample):
    n, d = y.shape
    tps = max(seq // tm, 1)
    ins = [y, ng, mu, *mats, vec]
    in_specs = [_rows(tm, d), _whole(ng.shape), _whole(mu.shape), *[_whole(m.shape) for m in mats],
                _whole(vec.shape)]
    outs = [jax.ShapeDtypeStruct((n, d), F32)] * 7
    out_specs = [_rows(tm, d)] * 7
    scratch = []
    if sample:
        ins.append(prev)
        in_specs.append(_rows(tm, d))
        outs.append(jax.ShapeDtypeStruct((n, d), F32))
        out_specs.append(_rows(tm, d))
    else:
        outs.append(jax.ShapeDtypeStruct((n // seq, 1, d), F32))
        out_specs.append(pl.BlockSpec((1, 1, d), lambda i: (i // tps, 0, 0)))
        scratch.append(pltpu.VMEM((SUBLANES, d), F32))
    return pl.pallas_call(
        functools.partial(_rwkv_proj_kernel, tps=tps, sample=sample),
        out_shape=outs, grid=(n // tm,), in_specs=in_specs, out_specs=out_specs, scratch_shapes=scratch,
        compiler_params=_cparams("arbitrary"), name="rwkv_proj_sample" if sample else "rwkv_proj_prompt",
    )(*ins)


RWKV_CHUNK = 128


def _rwkv_chunk_kernel(r_ref, lw_ref, k_ref, v_ref, kk_ref, kka_ref, o_ref, st_ref, t_scr):
    c = pl.program_id(1)
    ch, d = r_ref.shape
    hd = RWKV_HEAD_DIM
    shift = int(math.log2(hd))
    n_pairs = d // LANES

    @pl.when(c == 0)
    def _():
        t_scr[...] = jnp.zeros(t_scr.shape, F32)

    ri = lax.broadcasted_iota(I32, (ch, ch), 0)
    ci = lax.broadcasted_iota(I32, (ch, ch), 1)
    incl = ci <= ri
    strict = ci < ri
    incl2 = jnp.concatenate([incl, incl], axis=1)
    strict2 = jnp.concatenate([strict, strict], axis=1)
    same_head = (ri >> shift) == (ci >> shift)
    eye = ri == ci
    head1 = (lax.broadcasted_iota(I32, (ch, LANES), 1) >> shift) == 1
    head1_wide = jnp.concatenate([head1, head1], axis=1)

    def by_head(x, mask):
        return jnp.concatenate([jnp.where(mask, 0.0, x), jnp.where(mask, x, 0.0)], axis=0).astype(BF16)

    lw = lw_ref[...]
    tri = jnp.where(incl, 1.0, 0.0).astype(BF16)
    p1 = lw.astype(BF16)
    rem = lw - p1.astype(F32)
    p2 = rem.astype(BF16)
    p3 = (rem - p2.astype(F32)).astype(BF16)
    cl = (jnp.dot(tri, p1, preferred_element_type=F32) + jnp.dot(tri, p2, preferred_element_type=F32)
          + jnp.dot(tri, p3, preferred_element_type=F32))
    mid = cl[ch // 2 - 1:ch // 2, :]
    last = cl[ch - 1:ch, :]
    e_mid = jnp.exp(mid)
    e_neg = jnp.exp(mid - cl)
    e_tail = jnp.exp(last - cl)
    p_last = jnp.exp(last)
    kk, kka, k = kk_ref[...], kka_ref[...], k_ref[...]
    a_s = -kk * jnp.exp(cl - lw - mid)
    r_s = r_ref[...] * jnp.exp(cl - mid)
    b_s = kka * e_neg
    k_s = k * e_neg
    b_t = kka * e_tail
    k_t = k * e_tail
    v = v_ref[...]
    zero = jnp.zeros((ch, ch), BF16)

    pairs = range(n_pairs)
    cols = [slice(p * LANES, (p + 1) * LANES) for p in pairs]
    mm = lambda a, b: jnp.dot(a, b, preferred_element_type=F32)
    lane_cat = lambda a, b: jnp.concatenate([a, b], axis=1)
    a_p = [a_s[:, c] for c in cols]
    r_p = [r_s[:, c] for c in cols]
    v_p = [v[:, c] for c in cols]
    gram = []
    for p in pairs:
        lhs = jnp.concatenate([jnp.where(head1, 0.0, a_p[p]), jnp.where(head1, 0.0, r_p[p]),
                               jnp.where(head1, a_p[p], 0.0), jnp.where(head1, r_p[p], 0.0)], axis=0)
        rhs = lane_cat(b_s[:, cols[p]].T, k_s[:, cols[p]].T)
        gram.append(mm(lhs.astype(BF16), rhs.astype(BF16)))
    ab0 = [jnp.where(strict2, g_[0:ch], 0.0) for g_ in gram]
    rb0 = [jnp.where(incl2, g_[ch:2 * ch], 0.0) for g_ in gram]
    ab1 = [jnp.where(strict2, g_[2 * ch:3 * ch], 0.0) for g_ in gram]
    rb1 = [jnp.where(incl2, g_[3 * ch:], 0.0) for g_ in gram]
    v_heads = [by_head(v_p[p], head1) for p in pairs]
    x = [lane_cat(a_p[p], mm(lane_cat(ab0[p][:, ch:], ab1[p][:, ch:]).astype(BF16), v_heads[p])) for p in pairs]
    pw = [lane_cat(ab0[p][:, :ch], ab1[p][:, :ch]).astype(BF16) for p in pairs]
    n_sq = int(math.log2(ch))
    for it in range(n_sq):
        x = [x[p] + mm(pw[p], by_head(x[p], head1_wide)) for p in pairs]
        if it < n_sq - 1:
            pw = [mm(pw[p], jnp.concatenate([lane_cat(pw[p][:, :ch], zero), lane_cat(zero, pw[p][:, ch:])],
                                            axis=0)).astype(BF16) for p in pairs]
    x = [lane_cat(x[p][:, :ch] * e_mid[:, cols[p]], x[p][:, ch:]) for p in pairs]
    qo = [mm(lane_cat(rb0[p][:, :ch], rb1[p][:, :ch]).astype(BF16), by_head(x[p], head1_wide)) for p in pairs]
    o_intra = [qo[p][:, ch:] + mm(lane_cat(rb0[p][:, ch:], rb1[p][:, ch:]).astype(BF16), v_heads[p])
               for p in pairs]
    q = [r_p[p] * e_mid[:, cols[p]] + qo[p][:, :ch] for p in pairs]
    gh = [mm(b_t[:, cols[p]].T.astype(BF16), x[p].astype(BF16)) for p in pairs]
    g = [jnp.where(same_head, gh[p][:, :ch], 0.0) + jnp.where(eye, p_last[:, cols[p]], 0.0) for p in pairs]
    h = [jnp.where(same_head, gh[p][:, ch:] + _bdot(k_t[:, cols[p]].T, v_p[p]), 0.0) for p in pairs]
    for p in pairs:
        t_old = t_scr[p].astype(BF16)
        o_ref[:, cols[p]] = mm(q[p].astype(BF16), t_old) + o_intra[p]
        t_scr[p] = mm(g[p].astype(BF16), t_old) + h[p]

    @pl.when(c == pl.num_programs(1) - 1)
    def _():
        for p in range(n_pairs):
            s_pair = t_scr[p].T
            st_ref[0, p] = s_pair[:hd, :] + s_pair[hd:, :]


def _rwkv_chunked(r, lw, k, v, kk, kka, *, batch, seq):
    n, d = r.shape
    ch = RWKV_CHUNK
    assert ch == LANES and seq % ch == 0
    n_chunks = seq // ch
    n_pairs = d // LANES
    blk = pl.BlockSpec((ch, d), lambda b, c: (b * n_chunks + c, 0))
    return pl.pallas_call(
        _rwkv_chunk_kernel,
        out_shape=[jax.ShapeDtypeStruct((n, d), F32),
                   jax.ShapeDtypeStruct((batch, n_pairs, RWKV_HEAD_DIM, LANES), F32)],
        grid=(batch, n_chunks), in_specs=[blk] * 6,
        out_specs=[blk, pl.BlockSpec((1, n_pairs, RWKV_HEAD_DIM, LANES), lambda b, c: (b, 0, 0, 0))],
        scratch_shapes=[pltpu.VMEM((n_pairs, LANES, LANES), F32)],
        compiler_params=_cparams("parallel", "arbitrary"), name="rwkv_chunked",
    )(r, lw, k, v, kk, kka)


def _rwkv_step_kernel(r_ref, lw_ref, k_ref, v_ref, kk_ref, kka_ref, s0_ref, o_ref, st_ref):
    hd = s0_ref.shape[1]
    w = jnp.exp(lw_ref[...])
    kk, kka, k, r = kk_ref[...], kka_ref[...], k_ref[...], r_ref[...]

    def value_row(i, carry):
        s = s0_ref[0, i]
        s_kk = jnp.sum(s * kk, axis=0, keepdims=True)
        s = s * w - s_kk * kka + v_ref[pl.ds(i, 1), :] * k
        st_ref[0, i] = s
        o_ref[pl.ds(i, 1), :] = jnp.sum(s * r, axis=0, keepdims=True)
        return carry

    lax.fori_loop(0, hd, value_row, 0, unroll=8)


def _rwkv_step(r, lw, k, v, kk, kka, s0):
    d, n = r.shape
    heads, hd = s0.shape[0], s0.shape[1]
    vec = pl.BlockSpec((hd, n), lambda h: (h, 0))
    st = pl.BlockSpec((1, hd, hd, n), lambda h: (h, 0, 0, 0))
    return pl.pallas_call(
        _rwkv_step_kernel,
        out_shape=[jax.ShapeDtypeStruct((d, n), F32), jax.ShapeDtypeStruct(s0.shape, F32)],
        grid=(heads,), in_specs=[vec] * 6 + [st], out_specs=[vec, st],
        compiler_params=_cparams("parallel"), name="rwkv_step",
    )(r, lw, k, v, kk, kka, s0)


def _rwkv_out_kernel(o_ref, r_ref, k_ref, v_ref, g_ref, y_ref, vec_ref, wo_ref, yout_ref):
    hd = RWKV_HEAD_DIM
    o = o_ref[...]
    mu = _seg_sum(o, hd) * (1.0 / hd)
    dlt = o - mu
    var = _seg_sum(dlt * dlt, hd) * (1.0 / hd)
    o = dlt * lax.rsqrt(var + RWKV_LNX_EPS) * vec_ref[1:2, :] + vec_ref[2:3, :]
    o = o + _seg_sum(r_ref[...] * k_ref[...] * vec_ref[0:1, :], hd) * v_ref[...]
    yout_ref[...] = y_ref[...] + _bdot(o * g_ref[...], wo_ref[...])


def _rwkv_out(o, r, k, v, g, y, vec, wo, *, tm, name):
    n, d = y.shape
    return pl.pallas_call(
        _rwkv_out_kernel, out_shape=jax.ShapeDtypeStruct((n, d), F32), grid=(n // tm,),
        in_specs=[_rows(tm, d)] * 6 + [_whole(vec.shape), _whole(wo.shape)], out_specs=_rows(tm, d),
        compiler_params=_cparams("parallel"), name=name,
    )(o, r, k, v, g, y, vec, wo)


FFN_CHUNK = 256


def _ffn_act(gate, g1, g2, up, cw, cb):
    return _gelu(cw[0:1] * g2 + cw[1:2] * g1 + cw[2:3] * gate + cb) * up


def _ffn_prompt_kernel(y_ref, ng_ref, win_ref, cw_ref, cb_ref, wout_ref, *rest, tps, final):
    if final:
        fg_ref, yout_ref, st_ref, yfin_ref, h_scr, gs_scr, act_scr, carry_scr = rest
    else:
        yout_ref, st_ref, h_scr, gs_scr, act_scr, carry_scr = rest
    i = pl.program_id(0)
    y = y_ref[...]
    tm = y.shape[0]
    hid = wout_ref.shape[0]
    tf = FFN_CHUNK
    h_scr[...] = _rms(y, ng_ref[...]).astype(BF16)

    @pl.when(i % tps == 0)
    def _():
        carry_scr[...] = jnp.zeros(carry_scr.shape, F32)

    for j in range(hid // tf):
        cols = slice(j * tf, (j + 1) * tf)
        hb = h_scr[...]
        gs_scr[0:SUBLANES, :] = carry_scr[:, cols]
        gs_scr[SUBLANES:, :] = jnp.dot(hb, win_ref[:, cols], preferred_element_type=F32)
        up = jnp.dot(hb, win_ref[:, hid + j * tf:hid + (j + 1) * tf], preferred_element_type=F32)
        gate = gs_scr[SUBLANES:, :]
        carry_scr[:, cols] = gate[tm - SUBLANES:, :]
        st_ref[0, :, cols] = gate[tm - (FFN_CONV - 1):, :]
        act = _ffn_act(gate, gs_scr[SUBLANES - 1:tm + SUBLANES - 1, :], gs_scr[SUBLANES - 2:tm + SUBLANES - 2, :],
                       up, cw_ref[:, cols], cb_ref[:, cols])
        act_scr[:, cols] = act.astype(BF16)
    y_new = y + jnp.dot(act_scr[...], wout_ref[...], preferred_element_type=F32)
    yout_ref[...] = y_new
    if final:
        yfin_ref[...] = _rms(y_new, fg_ref[...])


def _ffn_prompt(y, ng, win, cw, cb, wout, fg, *, tm, seq):
    n, d = y.shape
    hid = wout.shape[0]
    tps = seq // tm
    final = fg is not None
    ins = [y, ng, win, cw, cb, wout]
    in_specs = [_rows(tm, d)] + [_whole(a.shape) for a in ins[1:]]
    outs = [jax.ShapeDtypeStruct((n, d), F32), jax.ShapeDtypeStruct((n // seq, FFN_CONV - 1, hid), F32)]
    out_specs = [_rows(tm, d), pl.BlockSpec((1, FFN_CONV - 1, hid), lambda i: (i // tps, 0, 0))]
    if final:
        ins.append(fg)
        in_specs.append(_whole(fg.shape))
        outs.append(jax.ShapeDtypeStruct((n, d), F32))
        out_specs.append(_rows(tm, d))
    return pl.pallas_call(
        functools.partial(_ffn_prompt_kernel, tps=tps, final=final),
        out_shape=outs, grid=(n // tm,), in_specs=in_specs, out_specs=out_specs,
        scratch_shapes=[pltpu.VMEM((tm, d), BF16), pltpu.VMEM((tm + SUBLANES, FFN_CHUNK), F32),
                        pltpu.VMEM((tm, hid), BF16), pltpu.VMEM((SUBLANES, hid), F32)],
        compiler_params=_cparams("arbitrary"), name="ffn_prompt",
    )(*ins)


def _ffn_sample_kernel(y_ref, ng_ref, wg_ref, wu_ref, cw_ref, cb_ref, p2_ref, p1_ref, wout_ref, *rest, final):
    if final:
        fg_ref, yout_ref, gate_ref, yfin_ref, h_scr, acc_scr = rest
    else:
        yout_ref, gate_ref, h_scr, acc_scr = rest
    j = pl.program_id(0)

    @pl.when(j == 0)
    def _():
        h_scr[...] = _rms(y_ref[...], ng_ref[...]).astype(BF16)
        acc_scr[...] = jnp.zeros(acc_scr.shape, F32)

    hb = h_scr[...]
    gate = jnp.dot(hb, wg_ref[...], preferred_element_type=F32)
    up = jnp.dot(hb, wu_ref[...], preferred_element_type=F32)
    gate_ref[...] = gate
    act = _ffn_act(gate, p1_ref[...], p2_ref[...], up, cw_ref[...], cb_ref[...])
    acc_scr[...] += _bdot(act, wout_ref[...])

    @pl.when(j == pl.num_programs(0) - 1)
    def _():
        y_new = y_ref[...] + acc_scr[...]
        yout_ref[...] = y_new
        if final:
            yfin_ref[...] = _rms(y_new, fg_ref[...])


def _ffn_sample(y, ng, win, cw, cb, p2, p1, wout, fg):
    n, d = y.shape
    hid = wout.shape[0]
    tf = hid // 2 if (hid // 2) % LANES == 0 else FFN_CHUNK
    nf = hid // tf
    final = fg is not None
    keep = lambda shape: pl.BlockSpec(shape, lambda j: (0,) * len(shape))
    chunk = lambda rows: pl.BlockSpec((rows, tf), lambda j: (0, j))
    ins = [y, ng, win, win, cw, cb, p2, p1, wout]
    in_specs = [keep((n, d)), keep(ng.shape), chunk(d), pl.BlockSpec((d, tf), lambda j: (0, nf + j)),
                chunk(FFN_CONV), chunk(1), chunk(n), chunk(n), pl.BlockSpec((tf, d), lambda j: (j, 0))]
    outs = [jax.ShapeDtypeStruct((n, d), F32), jax.ShapeDtypeStruct((n, hid), F32)]
    out_specs = [keep((n, d)), chunk(n)]
    if final:
        ins.append(fg)
        in_specs.append(keep(fg.shape))
        outs.append(jax.ShapeDtypeStruct((n, d), F32))
        out_specs.append(keep((n, d)))
    return pl.pallas_call(
        functools.partial(_ffn_sample_kernel, final=final),
        out_shape=outs, grid=(nf,), in_specs=in_specs, out_specs=out_specs,
        scratch_shapes=[pltpu.VMEM((n, d), BF16), pltpu.VMEM((n, d), F32)],
        compiler_params=_cparams("arbitrary"), name="ffn_sample",
    )(*ins)


def kernel(x_prompt, x_sample, cache_moba_k, cache_moba_v, state_pool, state_rwkv_wkv, state_rwkv_shift, state_ffn_conv, page_table, norm_mix_g, norm_ffn_g, norm_final_g, rel_bias, gm_w_in, gm_ln_g, gm_ln_b, gm_w_s, gm_b_s, gm_w_out, moba_w_qkv, moba_w_o, pool_w, pool_scale, rwkv_mu, rwkv_w_r, rwkv_w_k, rwkv_w_v, rwkv_w_o, rwkv_w0, rwkv_w1, rwkv_w2, rwkv_a0, rwkv_a1, rwkv_a2, rwkv_g1, rwkv_g2, rwkv_k_k, rwkv_k_a, rwkv_r_k, rwkv_lnx_g, rwkv_lnx_b, ffn_w_in, ffn_conv_w, ffn_conv_b, ffn_w_out):
    bp, seq, d = x_prompt.shape
    bs = x_sample.shape[0]
    depth = norm_mix_g.shape[0]
    assert x_sample.shape[1] == 1 and depth == 4
    past_len = page_table.shape[1] * PAGE_SIZE
    assert seq % MOBA_BLOCK == 0 and past_len % MOBA_BLOCK == 0 and past_len % GM_CHUNK == 0
    row = lambda vct: vct.reshape(1, -1)
    bf = lambda m: m.astype(BF16)
    yp = x_prompt.reshape(bp * seq, d)
    ys = x_sample.reshape(bs, d)
    tm = TOKEN_TILE
    assert seq % tm == 0 and seq % RWKV_PROJ_TILE == 0 and seq % RWKV_CHUNK == 0
    conv_p, conv_s = [], []

    def ffn(i, yp, ys):
        last = i == depth - 1
        win, wout = bf(ffn_w_in[i]), bf(ffn_w_out[i])
        cw, cb, ng = ffn_conv_w[i], row(ffn_conv_b[i]), row(norm_ffn_g[i])
        fg = row(norm_final_g) if last else None
        res_p = _ffn_prompt(yp, ng, win, cw, cb, wout, fg, tm=tm, seq=seq)
        st = state_ffn_conv[i]
        res_s = _ffn_sample(ys, ng, win, cw, cb, st[:, 0], st[:, 1], wout, fg)
        conv_p.append(res_p[1])
        conv_s.append(jnp.stack([st[:, 1], res_s[1]], axis=1))
        if last:
            return res_p[2], res_s[2]
        return res_p[0], res_s[0]

    ng = row(norm_mix_g[0])
    width = gm_w_out.shape[1]
    gd = width // GM_GROUPS
    gm_in, gm_out = bf(gm_w_in[0]), bf(gm_w_out[0])
    lng, lnb = row(gm_ln_g[0]), row(gm_ln_b[0])
    sb_prompt = jnp.repeat(gm_b_s[0].T, gd, axis=1)
    (yp,) = _gmlp(yp, ng, gm_in, lng, lnb, gm_w_s[0], sb_prompt, gm_out, tm=tm, sample=False)
    sa_first = row(jnp.repeat(gm_w_s[0][:, 0, 0], gd))
    sb_first = row(jnp.repeat(gm_b_s[0][:, 0], gd))
    ys, gm_v = _gmlp(ys, ng, gm_in, lng, lnb, sa_first, sb_first, gm_out, tm=bs, sample=True)
    gm_v_sample = gm_v.reshape(1, bs, 1, width)
    yp, ys = ffn(0, yp, ys)

    ng = row(norm_mix_g[1])
    w_qkv, w_o = bf(moba_w_qkv[0]), bf(moba_w_o[0])
    heads = MOBA_HEADS
    hd = d // heads
    qp, kp_t, vp_t = _qkv_prompt(yp, ng, w_qkv[:, :d], w_qkv[:, d:2 * d].T, w_qkv[:, 2 * d:].T,
                                 tm=tm, batch=bp, seq=seq)
    qs, ks, vs = _norm_linear(ys, ng, w_qkv, 3, tm=bs, name="moba_qkv_sample")
    blk = MOBA_BLOCK
    qi = jnp.arange(blk, dtype=I32)[:, None]
    ki = jnp.arange(blk, dtype=I32)[None, :]
    bkt_tiles = _t5_bucket_table(jnp.stack([qi - ki, blk + qi - ki]))
    bias_tiles = _bias_tiles(rel_bias, bkt_tiles)
    op = _moba_prompt(rel_bias, qp, kp_t, vp_t, bias_tiles, batch=bp, seq=seq, head_dim=hd)
    bkt_rows = _t5_bucket_table(past_len - jnp.arange(past_len, dtype=I32))
    rb_pad = jnp.pad(rel_bias, ((0, 0), (0, LANES - heads)))
    bias_rows = _bias_rows(rb_pad, jnp.broadcast_to(bkt_rows[:, None], (past_len, LANES)))
    os_ = _moba_sample(page_table, qs, ks, vs, cache_moba_k, cache_moba_v, 0, bias_rows, rel_bias, head_dim=hd)
    yp = _linear_res(op, w_o, yp, tm=tm, name="moba_out_prompt")
    ys = _linear_res(os_, w_o, ys, tm=bs, name="moba_out_sample")
    moba_k_prompt = kp_t.reshape(1, bp, heads, hd, seq).transpose(0, 1, 4, 2, 3)
    moba_v_prompt = vp_t.reshape(1, bp, heads, hd, seq).transpose(0, 1, 4, 2, 3)
    moba_k_sample = ks.reshape(1, bs, 1, heads, hd)
    moba_v_sample = vs.reshape(1, bs, 1, heads, hd)
    yp, ys = ffn(1, yp, ys)

    ng = row(norm_mix_g[2])
    pw, psc = bf(pool_w[0]), row(pool_scale[0])
    yp, pool_p = _pool_prompt(yp, ng, pw, psc, tm=tm, seq=seq)
    ys, hs = _pool_sample(ys, ng, jnp.swapaxes(state_pool[0], 0, 1), pw, psc)
    pool_prompt = pool_p[None]
    pool_sample = jnp.concatenate([state_pool[0][:, 1:], hs[:, None]], axis=1)[None]
    yp, ys = ffn(2, yp, ys)

    ng = row(norm_mix_g[3])
    mats = [bf(m[0]) for m in (rwkv_w_r, rwkv_w_k, rwkv_w_v, rwkv_w1, rwkv_w2, rwkv_a1, rwkv_a2,
                               rwkv_g1, rwkv_g2)]
    vec_in = jnp.stack([rwkv_w0[0], rwkv_a0[0], rwkv_k_k[0], rwkv_k_a[0]])
    vec_out = jnp.stack([rwkv_r_k[0].reshape(-1), rwkv_lnx_g[0], rwkv_lnx_b[0]])
    rh = d // RWKV_HEAD_DIM
    *seqs_p, gp, shp = _rwkv_proj(yp, ng, rwkv_mu[0], mats, vec_in, None, tm=RWKV_PROJ_TILE, seq=seq,
                                  sample=False)
    *seqs_s, gs, shs = _rwkv_proj(ys, ng, rwkv_mu[0], mats, vec_in, state_rwkv_shift[0], tm=bs, seq=1,
                                  sample=True)
    o_p, st_p = _rwkv_chunked(*seqs_p, batch=bp, seq=seq)
    hd_r = RWKV_HEAD_DIM
    wkv_p = st_p.reshape(bp, rh // 2, hd_r, 2, hd_r).transpose(0, 1, 3, 2, 4).reshape(bp, rh, hd_r, hd_r)
    o_s_t, wkv_s_t = _rwkv_step(*[a.T for a in seqs_s], jnp.transpose(state_rwkv_wkv[0], (1, 2, 3, 0)))
    o_s, wkv_s = o_s_t.T, jnp.transpose(wkv_s_t, (3, 0, 1, 2))
    w_o = bf(rwkv_w_o[0])
    r_p, _, k_p, v_p = seqs_p[:4]
    r_s, _, k_s, v_s = seqs_s[:4]
    yp = _rwkv_out(o_p, r_p, k_p, v_p, gp, yp, vec_out, w_o, tm=tm, name="rwkv_out_prompt")
    ys = _rwkv_out(o_s.reshape(bs, d), r_s, k_s, v_s, gs, ys, vec_out, w_o, tm=bs, name="rwkv_out_sample")
    yp, ys = ffn(3, yp, ys)

    return (yp.reshape(bp, seq, d), ys.reshape(bs, 1, d), gm_v_sample, moba_k_prompt, moba_v_prompt,
            moba_k_sample, moba_v_sample, pool_prompt, pool_sample, wkv_p[None], wkv_s[None],
            shp.reshape(1, bp, d), shs[None], jnp.stack(conv_p), jnp.stack(conv_s))
```

```python
import functools
import math

import jax
import jax.numpy as jnp
from jax import lax
from jax.experimental import pallas as pl
from jax.experimental.pallas import tpu as pltpu

F32 = jnp.float32
BF16 = jnp.bfloat16
I32 = jnp.int32

LANES = 128
SUBLANES = 8
VMEM_LIMIT_BYTES = 56 * 2**20

TOKEN_TILE = 512
RWKV_PROJ_TILE = 256

RMS_EPS = 1e-6
GM_LN_EPS = 1e-5
GM_CHUNK = 128
GM_GROUPS = 8
MOBA_HEADS = 16
MOBA_BLOCK = 256
MOBA_TOPK = 3
REL_BUCKETS = 32
REL_MAX_DIST = 128
PAGE_SIZE = 128
POOL_WINDOWS = (2, 4, 8, 16)
POOL_CTX = max(POOL_WINDOWS) - 1
RWKV_HEAD_DIM = 64
RWKV_LNX_EPS = 64e-5
FFN_CONV = 3
NEG_INF = float("-inf")

assert MOBA_BLOCK >= REL_MAX_DIST


def _cparams(*sem):
    return pltpu.CompilerParams(dimension_semantics=sem, vmem_limit_bytes=VMEM_LIMIT_BYTES)


def _whole(shape):
    nd = len(shape)
    return pl.BlockSpec(shape, lambda *_: (0,) * nd, pipeline_mode=pl.Buffered(1))


def _rows(tm, width):
    return pl.BlockSpec((tm, width), lambda i: (i, 0))


def _rms(x, g):
    return x * lax.rsqrt(jnp.mean(x * x, axis=-1, keepdims=True) + RMS_EPS) * g


def _bdot(a, b):
    return jnp.dot(a.astype(BF16), b.astype(BF16), preferred_element_type=F32)


def _nt_dot(a, b):
    return lax.dot_general(a.astype(BF16), b.astype(BF16), (((1,), (1,)), ((), ())),
                           preferred_element_type=F32)


def _split(x):
    hi = x.astype(BF16)
    lo = (x - hi.astype(F32)).astype(BF16)
    return hi, lo


def _split_dot(x, m):
    hi, lo = _split(x)
    return (jnp.dot(hi, m, preferred_element_type=F32)
            + jnp.dot(lo, m, preferred_element_type=F32))


def _same_head(n, head_dim):
    shift = int(math.log2(head_dim))
    r = lax.broadcasted_iota(I32, (n, n), 0) >> shift
    c = lax.broadcasted_iota(I32, (n, n), 1) >> shift
    return jnp.where(r == c, 1.0, 0.0).astype(BF16)


def _seg_sum(x, head_dim):
    g = _same_head(LANES, head_dim)
    parts = [_split_dot(x[:, i:i + LANES], g) for i in range(0, x.shape[1], LANES)]
    return jnp.concatenate(parts, axis=1)


def _gelu(x):
    return 0.5 * x * (1.0 + jnp.tanh(0.7978845608028654 * (x + 0.044715 * x * x * x)))


def _sigmoid(x):
    return 1.0 / (1.0 + jnp.exp(-x))


def _top_mask(gate, idx, axis, n_valid):
    cur = gate
    sel = jnp.zeros(gate.shape, F32)
    for _ in range(MOBA_TOPK):
        m = jnp.max(cur, axis=axis, keepdims=True)
        first = jnp.min(jnp.where(cur == m, idx, n_valid), axis=axis, keepdims=True)
        pick = (idx == first) & (m > NEG_INF)
        sel = jnp.where(pick, 1.0, sel)
        cur = jnp.where(pick, NEG_INF, cur)
    return sel


def _gmlp_kernel(x_ref, ng_ref, win_ref, lng_ref, lnb_ref, sa_ref, sb_ref, wout_ref, y_ref, aux_ref,
                 *, sample):
    x = x_ref[...]
    tm, width = x.shape[0], wout_ref.shape[0]
    h = _rms(x, ng_ref[...])
    z = _gelu(_bdot(h, win_ref[...]))
    u, v = z[:, :width], z[:, width:]
    mu = jnp.mean(v, axis=-1, keepdims=True)
    d = v - mu
    var = jnp.mean(d * d, axis=-1, keepdims=True)
    v = d * lax.rsqrt(var + GM_LN_EPS) * lng_ref[...] + lnb_ref[...]
    if sample:
        aux_ref[...] = v
        s = v * sa_ref[...] + sb_ref[...]
    else:
        gd = width // GM_GROUPS
        causal = (lax.broadcasted_iota(I32, (GM_CHUNK, GM_CHUNK), 0)
                  >= lax.broadcasted_iota(I32, (GM_CHUNK, GM_CHUNK), 1))
        for g in range(GM_GROUPS):
            wg = jnp.where(causal, sa_ref[g], 0.0).astype(BF16)
            cols = slice(g * gd, (g + 1) * gd)
            for c in range(tm // GM_CHUNK):
                rows = slice(c * GM_CHUNK, (c + 1) * GM_CHUNK)
                aux_ref[rows, cols] = (jnp.dot(wg, v[rows, cols].astype(BF16), preferred_element_type=F32)
                                       + sb_ref[:, cols])
        s = aux_ref[...]
    y_ref[...] = x + _bdot(u * s, wout_ref[...])


def _gmlp(x, ng, win, lng, lnb, sa, sb, wout, *, tm, sample):
    n, d = x.shape
    width = wout.shape[0]
    outs = [jax.ShapeDtypeStruct((n, d), F32)]
    out_specs = [_rows(tm, d)]
    scratch = []
    if sample:
        outs.append(jax.ShapeDtypeStruct((n, width), F32))
        out_specs.append(_rows(tm, width))
    else:
        scratch.append(pltpu.VMEM((tm, width), F32))
    return pl.pallas_call(
        functools.partial(_gmlp_kernel, sample=sample),
        out_shape=outs, grid=(n // tm,),
        in_specs=[_rows(tm, d), _whole(ng.shape), _whole(win.shape), _whole(lng.shape), _whole(lnb.shape),
                  _whole(sa.shape), _whole(sb.shape), _whole(wout.shape)],
        out_specs=out_specs, scratch_shapes=scratch,
        compiler_params=_cparams("parallel"), name="gmlp_sample" if sample else "gmlp_prompt",
    )(x, ng, win, lng, lnb, sa, sb, wout)


def _norm_linear_kernel(x_ref, ng_ref, w_ref, *o_refs):
    z = _bdot(_rms(x_ref[...], ng_ref[...]), w_ref[...])
    wd = z.shape[1] // len(o_refs)
    for i, o_ref in enumerate(o_refs):
        o_ref[...] = z[:, i * wd:(i + 1) * wd]


def _norm_linear(x, ng, w, n_out, *, tm, name):
    n, d = x.shape
    wd = w.shape[1] // n_out
    return pl.pallas_call(
        _norm_linear_kernel,
        out_shape=[jax.ShapeDtypeStruct((n, wd), F32)] * n_out, grid=(n // tm,),
        in_specs=[_rows(tm, d), _whole(ng.shape), _whole(w.shape)],
        out_specs=[_rows(tm, wd)] * n_out,
        compiler_params=_cparams("parallel"), name=name,
    )(x, ng, w)


def _qkv_prompt_kernel(x_ref, ng_ref, wq_ref, wkt_ref, wvt_ref, q_ref, kt_ref, vt_ref):
    h = _rms(x_ref[...], ng_ref[...]).astype(BF16)
    q_ref[...] = jnp.dot(h, wq_ref[...], preferred_element_type=F32)
    kt_ref[0] = _nt_dot(wkt_ref[...], h)
    vt_ref[0] = _nt_dot(wvt_ref[...], h)


def _qkv_prompt(x, ng, wq, wkt, wvt, *, tm, batch, seq):
    n, d = x.shape
    tps = seq // tm
    t_spec = pl.BlockSpec((1, d, tm), lambda i: (i // tps, 0, i % tps))
    return pl.pallas_call(
        _qkv_prompt_kernel,
        out_shape=[jax.ShapeDtypeStruct((n, d), F32)] + [jax.ShapeDtypeStruct((batch, d, seq), F32)] * 2,
        grid=(n // tm,),
        in_specs=[_rows(tm, d), _whole(ng.shape), _whole(wq.shape), _whole(wkt.shape), _whole(wvt.shape)],
        out_specs=[_rows(tm, d), t_spec, t_spec],
        compiler_params=_cparams("parallel"), name="moba_qkv_prompt",
    )(x, ng, wq, wkt, wvt)


def _linear_res_kernel(a_ref, w_ref, y_ref, o_ref):
    o_ref[...] = y_ref[...] + _bdot(a_ref[...], w_ref[...])


def _linear_res(a, w, y, *, tm, name):
    n, d = y.shape
    return pl.pallas_call(
        _linear_res_kernel, out_shape=jax.ShapeDtypeStruct((n, d), F32), grid=(n // tm,),
        in_specs=[_rows(tm, a.shape[1]), _whole(w.shape), _rows(tm, d)], out_specs=_rows(tm, d),
        compiler_params=_cparams("parallel"), name=name,
    )(a, w, y)


def _t5_bucket_table(rel):
    n = jnp.maximum(rel, 0)
    exact = REL_BUCKETS // 2
    nf = jnp.maximum(n, 1).astype(F32)
    large = exact + (jnp.log(nf / exact) / math.log(REL_MAX_DIST / exact)
                     * (REL_BUCKETS - exact)).astype(I32)
    return jnp.where(n < exact, n, jnp.minimum(large, REL_BUCKETS - 1)).astype(I32)


def _bias_tiles_kernel(rb_ref, bkt_ref, o_ref):
    h = pl.program_id(0)
    bkt = bkt_ref[...]
    acc = jnp.zeros(bkt.shape, F32)
    for b in range(REL_BUCKETS):
        acc = jnp.where(bkt == b, rb_ref[b, h], acc)
    future = (lax.broadcasted_iota(I32, bkt.shape, 2) > lax.broadcasted_iota(I32, bkt.shape, 1))
    own_block = lax.broadcasted_iota(I32, bkt.shape, 0) == 0
    o_ref[0] = jnp.where(future & own_block, NEG_INF, acc)


def _bias_tiles(rel_bias, bkt):
    heads = rel_bias.shape[1]
    return pl.pallas_call(
        _bias_tiles_kernel, out_shape=jax.ShapeDtypeStruct((heads,) + bkt.shape, F32), grid=(heads,),
        in_specs=[pl.BlockSpec(memory_space=pltpu.SMEM), _whole(bkt.shape)],
        out_specs=pl.BlockSpec((1,) + bkt.shape, lambda h: (h, 0, 0, 0)),
        compiler_params=_cparams("parallel"), name="moba_bias_tiles",
    )(rel_bias, bkt)


def _bias_rows_kernel(rb_ref, bkt_ref, o_ref):
    bkt = bkt_ref[...]
    acc = jnp.zeros(bkt.shape, F32)
    for b in range(REL_BUCKETS):
        acc = jnp.where(bkt == b, rb_ref[b:b + 1, :], acc)
    o_ref[...] = acc


def _bias_rows(rb_pad, bkt):
    return pl.pallas_call(
        _bias_rows_kernel, out_shape=jax.ShapeDtypeStruct(bkt.shape, F32), grid=(1,),
        in_specs=[_whole(rb_pad.shape), _whole(bkt.shape)], out_specs=_whole(bkt.shape),
        compiler_params=_cparams("arbitrary"), name="moba_bias_rows",
    )(rb_pad, bkt)


def _moba_prompt_kernel(rb_ref, q_ref, k_ref, v_ref, bias_ref, o_ref, kb_scr, vh_scr, kmean_scr, *, head_dim):
    pair = pl.program_id(1)
    blk = MOBA_BLOCK
    seq = k_ref.shape[2]
    n_blocks = seq // blk
    assert LANES == 2 * head_dim
    shift = int(math.log2(head_dim))
    scale = head_dim ** -0.5
    assert shift % 2 == 0 and n_blocks <= kmean_scr.shape[0]

    def prepare():
        kt, vt = k_ref[0], v_ref[0]
        second = (lax.broadcasted_iota(I32, (LANES, seq), 0) >> shift) == 1
        kb_scr[...] = kt.astype(BF16)
        vh_scr[0] = jnp.where(second, 0.0, vt).astype(BF16)
        vh_scr[1] = jnp.where(second, vt, 0.0).astype(BF16)
        rows = kmean_scr.shape[0]
        in_block = jnp.where((lax.broadcasted_iota(I32, (rows, seq), 1) >> int(math.log2(blk)))
                             == lax.broadcasted_iota(I32, (rows, seq), 0), 1.0, 0.0).astype(BF16)
        k_hi, k_lo = _split(kt)
        nt = lambda a, b: lax.dot_general(a, b, (((1,), (1,)), ((), ())), preferred_element_type=F32)
        kmean_scr[...] = (nt(in_block, k_hi) + nt(in_block, k_lo)) * (1.0 / blk)

    prepare()
    second = (lax.broadcasted_iota(I32, (blk, LANES), 1) >> shift) == 1
    km_hi, km_lo = _split(kmean_scr[...])
    nt = lambda a, b: lax.dot_general(a, b, (((1,), (1,)), ((), ())), preferred_element_type=F32)
    block_id = lax.broadcasted_iota(I32, (kmean_scr.shape[0], 2 * blk), 0)
    far_bias = [rb_ref[REL_BUCKETS - 1, pair * 2 + hh] for hh in range(2)]

    def select(own):
        q = q_ref[own * blk:(own + 1) * blk, :]
        q2 = jnp.concatenate([jnp.where(second, 0.0, q), jnp.where(second, q, 0.0)], axis=0)
        q_hi, q_lo = _split(q2)
        gate = nt(km_hi, q_hi) + nt(km_lo, q_hi) + nt(km_hi, q_lo)
        sel = _top_mask(jnp.where(block_id < own, gate, NEG_INF), block_id, 0, gate.shape[0]).T
        return (q2 * scale).astype(BF16), sel

    def scores(own, q_scaled):
        return jnp.dot(q_scaled, kb_scr[:, 0:(own + 1) * blk], preferred_element_type=F32)

    def mask(own, s, sel):
        head_rows = []
        for hh in range(2):
            rows = slice(hh * blk, (hh + 1) * blk)
            tiles = []
            for n in range(own + 1):
                t = s[rows, n * blk:(n + 1) * blk]
                chosen = sel[rows, n:n + 1] > 0.0
                if n == own:
                    t = t + bias_ref[hh, 0]
                elif n == own - 1:
                    t = jnp.where(chosen, t + bias_ref[hh, 1], NEG_INF)
                else:
                    t = t + jnp.where(chosen, far_bias[hh], NEG_INF)
                tiles.append(t)
            head_rows.append(jnp.concatenate(tiles, axis=1))
        return jnp.concatenate(head_rows, axis=0)

    def weights(s):
        p = jnp.exp(s - jnp.max(s, axis=-1, keepdims=True))
        return p, jnp.sum(p, axis=-1, keepdims=True)

    def output(own, p, l):
        keys = (own + 1) * blk
        p_pair = jnp.concatenate([p[:blk], p[blk:]], axis=1).astype(BF16)
        v_pair = jnp.concatenate([vh_scr[0, :, 0:keys], vh_scr[1, :, 0:keys]], axis=1)
        o_ref[own * blk:(own + 1) * blk, :] = nt(p_pair, v_pair) / jnp.where(second, l[blk:], l[:blk])

    order = list(range(n_blocks))
    groups = [(order[i], order[-1 - i]) for i in range(n_blocks // 2)] + ([(order[n_blocks // 2],)] * (n_blocks % 2))
    for group in groups:
        picked = [select(own) for own in group]
        raw = [scores(own, qs) for own, (qs, _) in zip(group, picked)]
        masked = [mask(own, s, sel) for own, s, (_, sel) in zip(group, raw, picked)]
        probs = [weights(s) for s in masked]
        for own, (p, l) in zip(group, probs):
            output(own, p, l)


def _moba_prompt(rel_bias, q, k, v, bias_tiles, *, batch, seq, head_dim):
    n, d = q.shape
    blk = MOBA_BLOCK
    hpt = LANES // head_dim
    return pl.pallas_call(
        functools.partial(_moba_prompt_kernel, head_dim=head_dim),
        out_shape=jax.ShapeDtypeStruct((n, d), F32), grid=(batch, d // LANES),
        in_specs=[pl.BlockSpec(memory_space=pltpu.SMEM),
                  pl.BlockSpec((seq, LANES), lambda b, p: (b, p)),
                  pl.BlockSpec((1, LANES, seq), lambda b, p: (b, p, 0)),
                  pl.BlockSpec((1, LANES, seq), lambda b, p: (b, p, 0)),
                  pl.BlockSpec((hpt, 2, blk, blk), lambda b, p: (p, 0, 0, 0))],
        out_specs=pl.BlockSpec((seq, LANES), lambda b, p: (b, p)),
        scratch_shapes=[pltpu.VMEM((LANES, seq), BF16), pltpu.VMEM((hpt, LANES, seq), BF16),
                        pltpu.VMEM((2 * SUBLANES, LANES), F32)],
        compiler_params=_cparams("parallel", "parallel"), name="moba_prompt_attn",
    )(rel_bias, q, k, v, bias_tiles)


def _moba_sample_kernel(pt_ref, q_ref, kn_ref, vn_ref, *rest, n_pages, head_dim):
    del pt_ref
    k_refs, v_refs = rest[:n_pages], rest[n_pages:2 * n_pages]
    bias_ref, rb0_ref, o_ref = rest[2 * n_pages:]
    d = q_ref.shape[2]
    heads = d // head_dim
    shift = int(math.log2(head_dim))
    per_block = MOBA_BLOCK // PAGE_SIZE
    blocks = range(n_pages // per_block)
    scale = head_dim ** -0.5
    head_lanes = jnp.where((lax.broadcasted_iota(I32, (heads, d), 1) >> shift)
                           == lax.broadcasted_iota(I32, (heads, d), 0), 1.0, 0.0)
    q = head_lanes * q_ref[0]
    q_hi, q_lo = _split(q)
    q2 = jnp.concatenate([q_hi, q_lo], axis=0)
    block_of = lambda refs, n: jnp.concatenate(
        [refs[per_block * n + j][0, 0].reshape(d, PAGE_SIZE) for j in range(per_block)], axis=1).astype(BF16)
    raw = [jnp.dot(q2, block_of(k_refs, n), preferred_element_type=F32) for n in blocks]
    raw = [r[:heads] + r[heads:] for r in raw]
    block_id = lax.broadcasted_iota(I32, (heads, LANES), 1)
    gate = jnp.full((heads, LANES), NEG_INF, F32)
    for n in blocks:
        gate = jnp.where(block_id == n, jnp.sum(raw[n], axis=-1, keepdims=True), gate)
    sel = _top_mask(gate, block_id, -1, LANES)
    s_new = jnp.sum(q * kn_ref[0], axis=-1, keepdims=True) * scale + rb0_ref[:, 0:1]
    s = [raw[n] * scale + bias_ref[n] for n in blocks]
    m = s_new
    for n in blocks:
        m = jnp.maximum(m, jnp.where(sel[:, n:n + 1] > 0.0, jnp.max(s[n], axis=-1, keepdims=True), NEG_INF))
    p = [jnp.exp(jnp.where(sel[:, n:n + 1] > 0.0, s[n] - m, NEG_INF)) for n in blocks]
    p_new = jnp.exp(s_new - m)
    den = p_new
    for n in blocks:
        den = den + jnp.sum(p[n], axis=-1, keepdims=True)
    p_all = jnp.concatenate(p, axis=1).astype(BF16)
    v_all = jnp.concatenate([block_of(v_refs, n) for n in blocks], axis=1)
    acc = lax.dot_general(p_all, v_all, (((1,), (1,)), ((), ())), preferred_element_type=F32)
    out = (acc + p_new * vn_ref[0]) * head_lanes / den
    o_ref[0] = jnp.sum(out, axis=0, keepdims=True)


def _moba_sample(page_table, q, k_new, v_new, cache_k, cache_v, layer, bias_rows, rel_bias, *, head_dim):
    nb, d = q.shape
    n_pages = page_table.shape[1]
    n_blocks = n_pages * PAGE_SIZE // MOBA_BLOCK
    heads = d // head_dim
    cache_kt = jnp.transpose(cache_k, (0, 1, 3, 4, 2))
    cache_vt = jnp.transpose(cache_v, (0, 1, 3, 4, 2))
    row = lambda x: x.reshape(nb, 1, d)
    vec = pl.BlockSpec((1, 1, d), lambda s, pt: (s, 0, 0))
    page = lambda j: pl.BlockSpec((1, 1, heads, head_dim, PAGE_SIZE), lambda s, pt: (layer, pt[s, j], 0, 0, 0))
    const = lambda shape: pl.BlockSpec(shape, lambda s, pt: (0,) * len(shape))
    bias = bias_rows[:, :heads].reshape(n_blocks, MOBA_BLOCK, heads).transpose(0, 2, 1)
    rb0 = jnp.broadcast_to(rel_bias[0][:, None], (heads, LANES))
    pages = [page(j) for j in range(n_pages)]
    out = pl.pallas_call(
        functools.partial(_moba_sample_kernel, n_pages=n_pages, head_dim=head_dim),
        out_shape=jax.ShapeDtypeStruct((nb, 1, d), F32),
        grid_spec=pltpu.PrefetchScalarGridSpec(
            num_scalar_prefetch=1, grid=(nb,),
            in_specs=[vec, vec, vec, *pages, *pages, const(bias.shape), const(rb0.shape)],
            out_specs=vec),
        compiler_params=_cparams("parallel"), name="moba_sample_attn",
    )(page_table, row(q), row(k_new), row(v_new), *([cache_kt] * n_pages), *([cache_vt] * n_pages), bias, rb0)
    return out.reshape(nb, d)


POOL_CARRY = 16
assert all(b == 2 * a for a, b in zip((1,) + POOL_WINDOWS, POOL_WINDOWS)) and POOL_CARRY > POOL_CTX


def _pool_prompt_kernel(y_ref, ng_ref, w_ref, sc_ref, yout_ref, st_ref, carry_scr, *, tps):
    i = pl.program_id(0)
    y = y_ref[...]
    tm, d = y.shape
    gd = d // len(POOL_WINDOWS)
    h = _rms(y, ng_ref[...])

    @pl.when(i % tps == 0)
    def _():
        carry_scr[...] = jnp.zeros(carry_scr.shape, F32)

    rows = jnp.concatenate([carry_scr[...], h], axis=0)
    carry_scr[...] = h[tm - POOL_CARRY:, :]
    st_ref[0] = rows[tm + POOL_CARRY - POOL_CTX:, :]
    pos = (i % tps) * tm + lax.broadcasted_iota(I32, (tm, 1), 0) + 1
    sums = rows
    for g, win in enumerate(POOL_WINDOWS):
        cols = slice(g * gd, (g + 1) * gd)
        sums = sums[:, (gd if g else 0):]
        sums = sums + pltpu.roll(sums, win // 2, axis=0)
        cnt = jnp.minimum(pos, win).astype(F32)
        mixed = _bdot(sums[POOL_CARRY:, 0:gd] / cnt - h[:, cols], w_ref[g])
        yout_ref[:, cols] = y[:, cols] + mixed * sc_ref[:, cols]


def _pool_prompt(y, ng, w, sc, *, tm, seq):
    n, d = y.shape
    tps = seq // tm
    return pl.pallas_call(
        functools.partial(_pool_prompt_kernel, tps=tps),
        out_shape=[jax.ShapeDtypeStruct((n, d), F32), jax.ShapeDtypeStruct((n // seq, POOL_CTX, d), F32)],
        grid=(n // tm,),
        in_specs=[_rows(tm, d), _whole(ng.shape), _whole(w.shape), _whole(sc.shape)],
        out_specs=[_rows(tm, d), pl.BlockSpec((1, POOL_CTX, d), lambda i: (i // tps, 0, 0))],
        scratch_shapes=[pltpu.VMEM((POOL_CARRY, d), F32)],
        compiler_params=_cparams("arbitrary"), name="pool_prompt",
    )(y, ng, w, sc)


def _pool_sample_kernel(y_ref, ng_ref, prev_ref, w_ref, sc_ref, yout_ref, h_ref):
    y = y_ref[...]
    d = y.shape[1]
    gd = d // len(POOL_WINDOWS)
    h = _rms(y, ng_ref[...])
    h_ref[...] = h
    for g, win in enumerate(POOL_WINDOWS):
        cols = slice(g * gd, (g + 1) * gd)
        wsum = h[:, cols]
        for back in range(1, win):
            wsum = wsum + prev_ref[POOL_CTX - back, :, cols]
        mixed = _bdot(wsum / float(win) - h[:, cols], w_ref[g])
        yout_ref[:, cols] = y[:, cols] + mixed * sc_ref[:, cols]


def _pool_sample(y, ng, prev_t, w, sc):
    n, d = y.shape
    return pl.pallas_call(
        _pool_sample_kernel, out_shape=[jax.ShapeDtypeStruct((n, d), F32)] * 2, grid=(1,),
        in_specs=[_whole(y.shape), _whole(ng.shape), _whole(prev_t.shape), _whole(w.shape), _whole(sc.shape)],
        out_specs=[_whole(y.shape)] * 2,
        compiler_params=_cparams("arbitrary"), name="pool_sample",
    )(y, ng, prev_t, w, sc)


def _rwkv_proj_kernel(y_ref, ng_ref, mu_ref, wr_ref, wk_ref, wv_ref, w1_ref, w2_ref, a1_ref, a2_ref,
                      g1_ref, g2_ref, vec_ref, *rest, tps, sample):
    if sample:
        prev_ref, r_o, w_o, k_o, v_o, kk_o, kka_o, g_o, sh_o = rest
    else:
        r_o, w_o, k_o, v_o, kk_o, kka_o, g_o, sh_o, hs_scr = rest
    y = y_ref[...]
    tm = y.shape[0]
    h = _rms(y, ng_ref[...])
    if sample:
        h_prev = prev_ref[...]
        sh_o[...] = h
    else:
        i = pl.program_id(0)

        @pl.when(i % tps == 0)
        def _():
            hs_scr[...] = jnp.zeros(hs_scr.shape, F32)

        h_prev = pltpu.roll(jnp.concatenate([hs_scr[...], h], axis=0), 1, axis=0)[SUBLANES:]
        hs_scr[...] = h[tm - SUBLANES:, :]
        sh_o[0] = h[tm - 1:tm, :]
    xx = h_prev - h
    mix = lambda m: h + xx * mu_ref[m:m + 1, :]
    r = _bdot(mix(0), wr_ref[...])
    k = _bdot(mix(2), wk_ref[...])
    v = _bdot(mix(3), wv_ref[...])
    z = vec_ref[0:1, :] + _bdot(jnp.tanh(_bdot(mix(1), w1_ref[...])), w2_ref[...])
    w_log = jnp.minimum(z, 0.0) - jnp.log(1.0 + jnp.exp(-jnp.abs(z))) - 0.5
    a = _sigmoid(vec_ref[1:2, :] + _bdot(_bdot(mix(4), a1_ref[...]), a2_ref[...]))
    g_o[...] = _bdot(_sigmoid(_bdot(mix(5), g1_ref[...])), g2_ref[...])
    kk = k * vec_ref[2:3, :]
    kk = kk / jnp.maximum(jnp.sqrt(_seg_sum(kk * kk, RWKV_HEAD_DIM)), 1e-12)
    r_o[...] = r
    w_o[...] = -jnp.exp(w_log)
    k_o[...] = k * (1.0 + (a - 1.0) * vec_ref[3:4, :])
    v_o[...] = v
    kk_o[...] = kk
    kka_o[...] = kk * a


def _rwkv_proj(y, ng, mu, mats, vec, prev, *, tm, seq, sample):
    n, d = y.shape
    tps = max(seq // tm, 1)
    ins = [y, ng, mu, *mats, vec]
    in_specs = [_rows(tm, d), _whole(ng.shape), _whole(mu.shape), *[_whole(m.shape) for m in mats],
                _whole(vec.shape)]
    outs = [jax.ShapeDtypeStruct((n, d), F32)] * 7
    out_specs = [_rows(tm, d)] * 7
    scratch = []
    if sample:
        ins.append(prev)
        in_specs.append(_rows(tm, d))
        outs.append(jax.ShapeDtypeStruct((n, d), F32))
        out_specs.append(_rows(tm, d))
    else:
        outs.append(jax.ShapeDtypeStruct((n // seq, 1, d), F32))
        out_specs.append(pl.BlockSpec((1, 1, d), lambda i: (i // tps, 0, 0)))
        scratch.append(pltpu.VMEM((SUBLANES, d), F32))
    return pl.pallas_call(
        functools.partial(_rwkv_proj_kernel, tps=tps, sample=sample),
        out_shape=outs, grid=(n // tm,), in_specs=in_specs, out_specs=out_specs, scratch_shapes=scratch,
        compiler_params=_cparams("arbitrary"), name="rwkv_proj_sample" if sample else "rwkv_proj_prompt",
    )(*ins)


RWKV_CHUNK = 128


def _rwkv_chunk_kernel(r_ref, lw_ref, k_ref, v_ref, kk_ref, kka_ref, o_ref, st_ref, t_scr):
    c = pl.program_id(1)
    ch, d = r_ref.shape
    hd = RWKV_HEAD_DIM
    shift = int(math.log2(hd))
    n_pairs = d // LANES

    @pl.when(c == 0)
    def _():
        t_scr[...] = jnp.zeros(t_scr.shape, F32)

    ri = lax.broadcasted_iota(I32, (ch, ch), 0)
    ci = lax.broadcasted_iota(I32, (ch, ch), 1)
    incl = ci <= ri
    strict = ci < ri
    incl2 = jnp.concatenate([incl, incl], axis=1)
    strict2 = jnp.concatenate([strict, strict], axis=1)
    same_head = (ri >> shift) == (ci >> shift)
    eye = ri == ci
    head1 = (lax.broadcasted_iota(I32, (ch, LANES), 1) >> shift) == 1
    head1_wide = jnp.concatenate([head1, head1], axis=1)

    def by_head(x, mask):
        return jnp.concatenate([jnp.where(mask, 0.0, x), jnp.where(mask, x, 0.0)], axis=0).astype(BF16)

    lw = lw_ref[...]
    tri = jnp.where(incl, 1.0, 0.0).astype(BF16)
    p1 = lw.astype(BF16)
    rem = lw - p1.astype(F32)
    p2 = rem.astype(BF16)
    p3 = (rem - p2.astype(F32)).astype(BF16)
    cl = (jnp.dot(tri, p1, preferred_element_type=F32) + jnp.dot(tri, p2, preferred_element_type=F32)
          + jnp.dot(tri, p3, preferred_element_type=F32))
    mid = cl[ch // 2 - 1:ch // 2, :]
    last = cl[ch - 1:ch, :]
    e_mid = jnp.exp(mid)
    e_neg = jnp.exp(mid - cl)
    e_tail = jnp.exp(last - cl)
    p_last = jnp.exp(last)
    kk, kka, k = kk_ref[...], kka_ref[...], k_ref[...]
    a_s = -kk * jnp.exp(cl - lw - mid)
    r_s = r_ref[...] * jnp.exp(cl - mid)
    b_s = kka * e_neg
    k_s = k * e_neg
    b_t = kka * e_tail
    k_t = k * e_tail
    v = v_ref[...]
    zero = jnp.zeros((ch, ch), BF16)

    pairs = range(n_pairs)
    cols = [slice(p * LANES, (p + 1) * LANES) for p in pairs]
    mm = lambda a, b: jnp.dot(a, b, preferred_element_type=F32)
    lane_cat = lambda a, b: jnp.concatenate([a, b], axis=1)
    a_p = [a_s[:, c] for c in cols]
    r_p = [r_s[:, c] for c in cols]
    v_p = [v[:, c] for c in cols]
    gram = []
    for p in pairs:
        lhs = jnp.concatenate([jnp.where(head1, 0.0, a_p[p]), jnp.where(head1, 0.0, r_p[p]),
                               jnp.where(head1, a_p[p], 0.0), jnp.where(head1, r_p[p], 0.0)], axis=0)
        rhs = lane_cat(b_s[:, cols[p]].T, k_s[:, cols[p]].T)
        gram.append(mm(lhs.astype(BF16), rhs.astype(BF16)))
    ab0 = [jnp.where(strict2, g_[0:ch], 0.0) for g_ in gram]
    rb0 = [jnp.where(incl2, g_[ch:2 * ch], 0.0) for g_ in gram]
    ab1 = [jnp.where(strict2, g_[2 * ch:3 * ch], 0.0) for g_ in gram]
    rb1 = [jnp.where(incl2, g_[3 * ch:], 0.0) for g_ in gram]
    v_heads = [by_head(v_p[p], head1) for p in pairs]
    x = [lane_cat(a_p[p], mm(lane_cat(ab0[p][:, ch:], ab1[p][:, ch:]).astype(BF16), v_heads[p])) for p in pairs]
    pw = [lane_cat(ab0[p][:, :ch], ab1[p][:, :ch]).astype(BF16) for p in pairs]
    n_sq = int(math.log2(ch))
    for it in range(n_sq):
        x = [x[p] + mm(pw[p], by_head(x[p], head1_wide)) for p in pairs]
        if it < n_sq - 1:
            pw = [mm(pw[p], jnp.concatenate([lane_cat(pw[p][:, :ch], zero), lane_cat(zero, pw[p][:, ch:])],
                                            axis=0)).astype(BF16) for p in pairs]
    x = [lane_cat(x[p][:, :ch] * e_mid[:, cols[p]], x[p][:, ch:]) for p in pairs]
    qo = [mm(lane_cat(rb0[p][:, :ch], rb1[p][:, :ch]).astype(BF16), by_head(x[p], head1_wide)) for p in pairs]
    o_intra = [qo[p][:, ch:] + mm(lane_cat(rb0[p][:, ch:], rb1[p][:, ch:]).astype(BF16), v_heads[p])
               for p in pairs]
    q = [r_p[p] * e_mid[:, cols[p]] + qo[p][:, :ch] for p in pairs]
    gh = [mm(b_t[:, cols[p]].T.astype(BF16), x[p].astype(BF16)) for p in pairs]
    g = [jnp.where(same_head, gh[p][:, :ch], 0.0) + jnp.where(eye, p_last[:, cols[p]], 0.0) for p in pairs]
    h = [jnp.where(same_head, gh[p][:, ch:] + _bdot(k_t[:, cols[p]].T, v_p[p]), 0.0) for p in pairs]
    for p in pairs:
        t_old = t_scr[p].astype(BF16)
        o_ref[:, cols[p]] = mm(q[p].astype(BF16), t_old) + o_intra[p]
        t_scr[p] = mm(g[p].astype(BF16), t_old) + h[p]

    @pl.when(c == pl.num_programs(1) - 1)
    def _():
        for p in range(n_pairs):
            s_pair = t_scr[p].T
            st_ref[0, p] = s_pair[:hd, :] + s_pair[hd:, :]


def _rwkv_chunked(r, lw, k, v, kk, kka, *, batch, seq):
    n, d = r.shape
    ch = RWKV_CHUNK
    assert ch == LANES and seq % ch == 0
    n_chunks = seq // ch
    n_pairs = d // LANES
    blk = pl.BlockSpec((ch, d), lambda b, c: (b * n_chunks + c, 0))
    return pl.pallas_call(
        _rwkv_chunk_kernel,
        out_shape=[jax.ShapeDtypeStruct((n, d), F32),
                   jax.ShapeDtypeStruct((batch, n_pairs, RWKV_HEAD_DIM, LANES), F32)],
        grid=(batch, n_chunks), in_specs=[blk] * 6,
        out_specs=[blk, pl.BlockSpec((1, n_pairs, RWKV_HEAD_DIM, LANES), lambda b, c: (b, 0, 0, 0))],
        scratch_shapes=[pltpu.VMEM((n_pairs, LANES, LANES), F32)],
        compiler_params=_cparams("parallel", "arbitrary"), name="rwkv_chunked",
    )(r, lw, k, v, kk, kka)


def _rwkv_step_kernel(r_ref, lw_ref, k_ref, v_ref, kk_ref, kka_ref, s0_ref, o_ref, st_ref):
    hd = s0_ref.shape[1]
    w = jnp.exp(lw_ref[...])
    kk, kka, k, r = kk_ref[...], kka_ref[...], k_ref[...], r_ref[...]

    def value_row(i, carry):
        s = s0_ref[0, i]
        s_kk = jnp.sum(s * kk, axis=0, keepdims=True)
        s = s * w - s_kk * kka + v_ref[pl.ds(i, 1), :] * k
        st_ref[0, i] = s
        o_ref[pl.ds(i, 1), :] = jnp.sum(s * r, axis=0, keepdims=True)
        return carry

    lax.fori_loop(0, hd, value_row, 0, unroll=8)


def _rwkv_step(r, lw, k, v, kk, kka, s0):
    d, n = r.shape
    heads, hd = s0.shape[0], s0.shape[1]
    vec = pl.BlockSpec((hd, n), lambda h: (h, 0))
    st = pl.BlockSpec((1, hd, hd, n), lambda h: (h, 0, 0, 0))
    return pl.pallas_call(
        _rwkv_step_kernel,
        out_shape=[jax.ShapeDtypeStruct((d, n), F32), jax.ShapeDtypeStruct(s0.shape, F32)],
        grid=(heads,), in_specs=[vec] * 6 + [st], out_specs=[vec, st],
        compiler_params=_cparams("parallel"), name="rwkv_step",
    )(r, lw, k, v, kk, kka, s0)


def _rwkv_out_kernel(o_ref, r_ref, k_ref, v_ref, g_ref, y_ref, vec_ref, wo_ref, yout_ref):
    hd = RWKV_HEAD_DIM
    o = o_ref[...]
    mu = _seg_sum(o, hd) * (1.0 / hd)
    dlt = o - mu
    var = _seg_sum(dlt * dlt, hd) * (1.0 / hd)
    o = dlt * lax.rsqrt(var + RWKV_LNX_EPS) * vec_ref[1:2, :] + vec_ref[2:3, :]
    o = o + _seg_sum(r_ref[...] * k_ref[...] * vec_ref[0:1, :], hd) * v_ref[...]
    yout_ref[...] = y_ref[...] + _bdot(o * g_ref[...], wo_ref[...])


def _rwkv_out(o, r, k, v, g, y, vec, wo, *, tm, name):
    n, d = y.shape
    return pl.pallas_call(
        _rwkv_out_kernel, out_shape=jax.ShapeDtypeStruct((n, d), F32), grid=(n // tm,),
        in_specs=[_rows(tm, d)] * 6 + [_whole(vec.shape), _whole(wo.shape)], out_specs=_rows(tm, d),
        compiler_params=_cparams("parallel"), name=name,
    )(o, r, k, v, g, y, vec, wo)


FFN_CHUNK = 256


def _ffn_act(gate, g1, g2, up, cw, cb):
    return _gelu(cw[0:1] * g2 + cw[1:2] * g1 + cw[2:3] * gate + cb) * up


def _ffn_prompt_kernel(y_ref, ng_ref, win_ref, cw_ref, cb_ref, wout_ref, *rest, tps, final):
    if final:
        fg_ref, yout_ref, st_ref, yfin_ref, h_scr, gs_scr, act_scr, carry_scr = rest
    else:
        yout_ref, st_ref, h_scr, gs_scr, act_scr, carry_scr = rest
    i = pl.program_id(0)
    y = y_ref[...]
    tm = y.shape[0]
    hid = wout_ref.shape[0]
    tf = FFN_CHUNK
    h_scr[...] = _rms(y, ng_ref[...]).astype(BF16)

    @pl.when(i % tps == 0)
    def _():
        carry_scr[...] = jnp.zeros(carry_scr.shape, F32)

    for j in range(hid // tf):
        cols = slice(j * tf, (j + 1) * tf)
        hb = h_scr[...]
        gs_scr[0:SUBLANES, :] = carry_scr[:, cols]
        gs_scr[SUBLANES:, :] = jnp.dot(hb, win_ref[:, cols], preferred_element_type=F32)
        up = jnp.dot(hb, win_ref[:, hid + j * tf:hid + (j + 1) * tf], preferred_element_type=F32)
        gate = gs_scr[SUBLANES:, :]
        carry_scr[:, cols] = gate[tm - SUBLANES:, :]
        st_ref[0, :, cols] = gate[tm - (FFN_CONV - 1):, :]
        act = _ffn_act(gate, gs_scr[SUBLANES - 1:tm + SUBLANES - 1, :], gs_scr[SUBLANES - 2:tm + SUBLANES - 2, :],
                       up, cw_ref[:, cols], cb_ref[:, cols])
        act_scr[:, cols] = act.astype(BF16)
    y_new = y + jnp.dot(act_scr[...], wout_ref[...], preferred_element_type=F32)
    yout_ref[...] = y_new
    if final:
        yfin_ref[...] = _rms(y_new, fg_ref[...])


def _ffn_prompt(y, ng, win, cw, cb, wout, layer, fg, *, tm, seq):
    n, d = y.shape
    hid = wout.shape[1]
    tps = seq // tm
    final = fg is not None
    of_layer = lambda a: pl.BlockSpec((None,) + a.shape[1:], lambda i: (layer, 0, 0), pipeline_mode=pl.Buffered(1))
    ins = [y, ng, win, cw, cb, wout]
    in_specs = [_rows(tm, d), _whole(ng.shape), of_layer(win), _whole(cw.shape), _whole(cb.shape), of_layer(wout)]
    outs = [jax.ShapeDtypeStruct((n, d), F32), jax.ShapeDtypeStruct((n // seq, FFN_CONV - 1, hid), F32)]
    out_specs = [_rows(tm, d), pl.BlockSpec((1, FFN_CONV - 1, hid), lambda i: (i // tps, 0, 0))]
    if final:
        ins.append(fg)
        in_specs.append(_whole(fg.shape))
        outs.append(jax.ShapeDtypeStruct((n, d), F32))
        out_specs.append(_rows(tm, d))
    return pl.pallas_call(
        functools.partial(_ffn_prompt_kernel, tps=tps, final=final),
        out_shape=outs, grid=(n // tm,), in_specs=in_specs, out_specs=out_specs,
        scratch_shapes=[pltpu.VMEM((tm, d), BF16), pltpu.VMEM((tm + SUBLANES, FFN_CHUNK), F32),
                        pltpu.VMEM((tm, hid), BF16), pltpu.VMEM((SUBLANES, hid), F32)],
        compiler_params=_cparams("arbitrary"), name="ffn_prompt",
    )(*ins)


def _ffn_sample_kernel(y_ref, ng_ref, wg_ref, wu_ref, cw_ref, cb_ref, p2_ref, p1_ref, wout_ref, *rest, final):
    if final:
        fg_ref, yout_ref, gate_ref, yfin_ref, h_scr, acc_scr = rest
    else:
        yout_ref, gate_ref, h_scr, acc_scr = rest
    j = pl.program_id(0)

    @pl.when(j == 0)
    def _():
        h_scr[...] = _rms(y_ref[...], ng_ref[...]).astype(BF16)
        acc_scr[...] = jnp.zeros(acc_scr.shape, F32)

    hb = h_scr[...]
    gate = jnp.dot(hb, wg_ref[...], preferred_element_type=F32)
    up = jnp.dot(hb, wu_ref[...], preferred_element_type=F32)
    gate_ref[...] = gate
    act = _ffn_act(gate, p1_ref[...], p2_ref[...], up, cw_ref[...], cb_ref[...])
    acc_scr[...] += _bdot(act, wout_ref[...])

    @pl.when(j == pl.num_programs(0) - 1)
    def _():
        y_new = y_ref[...] + acc_scr[...]
        yout_ref[...] = y_new
        if final:
            yfin_ref[...] = _rms(y_new, fg_ref[...])


def _ffn_sample(y, ng, win, cw, cb, p2, p1, wout, layer, fg):
    n, d = y.shape
    hid = wout.shape[1]
    tf = hid // 2 if (hid // 2) % LANES == 0 else FFN_CHUNK
    nf = hid // tf
    final = fg is not None
    keep = lambda shape: pl.BlockSpec(shape, lambda j: (0,) * len(shape))
    chunk = lambda rows: pl.BlockSpec((rows, tf), lambda j: (0, j))
    ins = [y, ng, win, win, cw, cb, p2, p1, wout]
    in_specs = [keep((n, d)), keep(ng.shape), pl.BlockSpec((None, d, tf), lambda j: (layer, 0, j)),
                pl.BlockSpec((None, d, tf), lambda j: (layer, 0, nf + j)),
                chunk(FFN_CONV), chunk(1), chunk(n), chunk(n), pl.BlockSpec((None, tf, d), lambda j: (layer, j, 0))]
    outs = [jax.ShapeDtypeStruct((n, d), F32), jax.ShapeDtypeStruct((n, hid), F32)]
    out_specs = [keep((n, d)), chunk(n)]
    if final:
        ins.append(fg)
        in_specs.append(keep(fg.shape))
        outs.append(jax.ShapeDtypeStruct((n, d), F32))
        out_specs.append(keep((n, d)))
    return pl.pallas_call(
        functools.partial(_ffn_sample_kernel, final=final),
        out_shape=outs, grid=(nf,), in_specs=in_specs, out_specs=out_specs,
        scratch_shapes=[pltpu.VMEM((n, d), BF16), pltpu.VMEM((n, d), F32)],
        compiler_params=_cparams("arbitrary"), name="ffn_sample",
    )(*ins)


def kernel(x_prompt, x_sample, cache_moba_k, cache_moba_v, state_pool, state_rwkv_wkv, state_rwkv_shift, state_ffn_conv, page_table, norm_mix_g, norm_ffn_g, norm_final_g, rel_bias, gm_w_in, gm_ln_g, gm_ln_b, gm_w_s, gm_b_s, gm_w_out, moba_w_qkv, moba_w_o, pool_w, pool_scale, rwkv_mu, rwkv_w_r, rwkv_w_k, rwkv_w_v, rwkv_w_o, rwkv_w0, rwkv_w1, rwkv_w2, rwkv_a0, rwkv_a1, rwkv_a2, rwkv_g1, rwkv_g2, rwkv_k_k, rwkv_k_a, rwkv_r_k, rwkv_lnx_g, rwkv_lnx_b, ffn_w_in, ffn_conv_w, ffn_conv_b, ffn_w_out):
    bp, seq, d = x_prompt.shape
    bs = x_sample.shape[0]
    depth = norm_mix_g.shape[0]
    assert x_sample.shape[1] == 1 and depth == 4
    past_len = page_table.shape[1] * PAGE_SIZE
    assert seq % MOBA_BLOCK == 0 and past_len % MOBA_BLOCK == 0 and past_len % GM_CHUNK == 0
    row = lambda vct: vct.reshape(1, -1)
    bf = lambda m: m.astype(BF16)
    yp = x_prompt.reshape(bp * seq, d)
    ys = x_sample.reshape(bs, d)
    tm = TOKEN_TILE
    assert seq % tm == 0 and seq % RWKV_PROJ_TILE == 0 and seq % RWKV_CHUNK == 0
    conv_p, conv_s = [], []
    ffn_in, ffn_out = bf(ffn_w_in), bf(ffn_w_out)

    def ffn(i, yp, ys):
        last = i == depth - 1
        cw, cb, ng = ffn_conv_w[i], row(ffn_conv_b[i]), row(norm_ffn_g[i])
        fg = row(norm_final_g) if last else None
        res_p = _ffn_prompt(yp, ng, ffn_in, cw, cb, ffn_out, i, fg, tm=tm, seq=seq)
        st = state_ffn_conv[i]
        res_s = _ffn_sample(ys, ng, ffn_in, cw, cb, st[:, 0], st[:, 1], ffn_out, i, fg)
        conv_p.append(res_p[1])
        conv_s.append(jnp.stack([st[:, 1], res_s[1]], axis=1))
        if last:
            return res_p[2], res_s[2]
        return res_p[0], res_s[0]

    ng = row(norm_mix_g[0])
    width = gm_w_out.shape[1]
    gd = width // GM_GROUPS
    gm_in, gm_out = bf(gm_w_in[0]), bf(gm_w_out[0])
    lng, lnb = row(gm_ln_g[0]), row(gm_ln_b[0])
    sb_prompt = jnp.repeat(gm_b_s[0].T, gd, axis=1)
    (yp,) = _gmlp(yp, ng, gm_in, lng, lnb, gm_w_s[0], sb_prompt, gm_out, tm=tm, sample=False)
    sa_first = row(jnp.repeat(gm_w_s[0][:, 0, 0], gd))
    sb_first = row(jnp.repeat(gm_b_s[0][:, 0], gd))
    ys, gm_v = _gmlp(ys, ng, gm_in, lng, lnb, sa_first, sb_first, gm_out, tm=bs, sample=True)
    gm_v_sample = gm_v.reshape(1, bs, 1, width)
    yp, ys = ffn(0, yp, ys)

    ng = row(norm_mix_g[1])
    w_qkv, w_o = bf(moba_w_qkv[0]), bf(moba_w_o[0])
    heads = MOBA_HEADS
    hd = d // heads
    qp, kp_t, vp_t = _qkv_prompt(yp, ng, w_qkv[:, :d], w_qkv[:, d:2 * d].T, w_qkv[:, 2 * d:].T,
                                 tm=tm, batch=bp, seq=seq)
    qs, ks, vs = _norm_linear(ys, ng, w_qkv, 3, tm=bs, name="moba_qkv_sample")
    blk = MOBA_BLOCK
    qi = jnp.arange(blk, dtype=I32)[:, None]
    ki = jnp.arange(blk, dtype=I32)[None, :]
    bkt_tiles = _t5_bucket_table(jnp.stack([qi - ki, blk + qi - ki]))
    bias_tiles = _bias_tiles(rel_bias, bkt_tiles)
    op = _moba_prompt(rel_bias, qp, kp_t, vp_t, bias_tiles, batch=bp, seq=seq, head_dim=hd)
    bkt_rows = _t5_bucket_table(past_len - jnp.arange(past_len, dtype=I32))
    rb_pad = jnp.pad(rel_bias, ((0, 0), (0, LANES - heads)))
    bias_rows = _bias_rows(rb_pad, jnp.broadcast_to(bkt_rows[:, None], (past_len, LANES)))
    os_ = _moba_sample(page_table, qs, ks, vs, cache_moba_k, cache_moba_v, 0, bias_rows, rel_bias, head_dim=hd)
    yp = _linear_res(op, w_o, yp, tm=tm, name="moba_out_prompt")
    ys = _linear_res(os_, w_o, ys, tm=bs, name="moba_out_sample")
    moba_k_prompt = kp_t.reshape(1, bp, heads, hd, seq).transpose(0, 1, 4, 2, 3)
    moba_v_prompt = vp_t.reshape(1, bp, heads, hd, seq).transpose(0, 1, 4, 2, 3)
    moba_k_sample = ks.reshape(1, bs, 1, heads, hd)
    moba_v_sample = vs.reshape(1, bs, 1, heads, hd)
    yp, ys = ffn(1, yp, ys)

    ng = row(norm_mix_g[2])
    pw, psc = bf(pool_w[0]), row(pool_scale[0])
    yp, pool_p = _pool_prompt(yp, ng, pw, psc, tm=tm, seq=seq)
    ys, hs = _pool_sample(ys, ng, jnp.swapaxes(state_pool[0], 0, 1), pw, psc)
    pool_prompt = pool_p[None]
    pool_sample = jnp.concatenate([state_pool[0][:, 1:], hs[:, None]], axis=1)[None]
    yp, ys = ffn(2, yp, ys)

    ng = row(norm_mix_g[3])
    mats = [bf(m[0]) for m in (rwkv_w_r, rwkv_w_k, rwkv_w_v, rwkv_w1, rwkv_w2, rwkv_a1, rwkv_a2,
                               rwkv_g1, rwkv_g2)]
    vec_in = jnp.stack([rwkv_w0[0], rwkv_a0[0], rwkv_k_k[0], rwkv_k_a[0]])
    vec_out = jnp.stack([rwkv_r_k[0].reshape(-1), rwkv_lnx_g[0], rwkv_lnx_b[0]])
    rh = d // RWKV_HEAD_DIM
    *seqs_p, gp, shp = _rwkv_proj(yp, ng, rwkv_mu[0], mats, vec_in, None, tm=RWKV_PROJ_TILE, seq=seq,
                                  sample=False)
    *seqs_s, gs, shs = _rwkv_proj(ys, ng, rwkv_mu[0], mats, vec_in, state_rwkv_shift[0], tm=bs, seq=1,
                                  sample=True)
    o_p, st_p = _rwkv_chunked(*seqs_p, batch=bp, seq=seq)
    hd_r = RWKV_HEAD_DIM
    wkv_p = st_p.reshape(bp, rh // 2, hd_r, 2, hd_r).transpose(0, 1, 3, 2, 4).reshape(bp, rh, hd_r, hd_r)
    o_s_t, wkv_s_t = _rwkv_step(*[a.T for a in seqs_s], jnp.transpose(state_rwkv_wkv[0], (1, 2, 3, 0)))
    o_s, wkv_s = o_s_t.T, jnp.transpose(wkv_s_t, (3, 0, 1, 2))
    w_o = bf(rwkv_w_o[0])
    r_p, _, k_p, v_p = seqs_p[:4]
    r_s, _, k_s, v_s = seqs_s[:4]
    yp = _rwkv_out(o_p, r_p, k_p, v_p, gp, yp, vec_out, w_o, tm=tm, name="rwkv_out_prompt")
    ys = _rwkv_out(o_s.reshape(bs, d), r_s, k_s, v_s, gs, ys, vec_out, w_o, tm=bs, name="rwkv_out_sample")
    yp, ys = ffn(3, yp, ys)

    return (yp.reshape(bp, seq, d), ys.reshape(bs, 1, d), gm_v_sample, moba_k_prompt, moba_v_prompt,
            moba_k_sample, moba_v_sample, pool_prompt, pool_sample, wkv_p[None], wkv_s[None],
            shp.reshape(1, bp, d), shs[None], jnp.stack(conv_p), jnp.stack(conv_s))
```

```python
import functools
import math

import jax
import jax.numpy as jnp
from jax import lax
from jax.experimental import pallas as pl
from jax.experimental.pallas import tpu as pltpu

F32 = jnp.float32
BF16 = jnp.bfloat16
I32 = jnp.int32

LANES = 128
SUBLANES = 8
VMEM_LIMIT_BYTES = 56 * 2**20

TOKEN_TILE = 512
RWKV_PROJ_TILE = 256

RMS_EPS = 1e-6
GM_LN_EPS = 1e-5
GM_CHUNK = 128
GM_GROUPS = 8
MOBA_HEADS = 16
MOBA_BLOCK = 256
MOBA_TOPK = 3
REL_BUCKETS = 32
REL_MAX_DIST = 128
PAGE_SIZE = 128
POOL_WINDOWS = (2, 4, 8, 16)
POOL_CTX = max(POOL_WINDOWS) - 1
RWKV_HEAD_DIM = 64
RWKV_LNX_EPS = 64e-5
FFN_CONV = 3
NEG_INF = float("-inf")

assert MOBA_BLOCK >= REL_MAX_DIST


def _cparams(*sem):
    return pltpu.CompilerParams(dimension_semantics=sem, vmem_limit_bytes=VMEM_LIMIT_BYTES)


def _whole(shape):
    nd = len(shape)
    return pl.BlockSpec(shape, lambda *_: (0,) * nd, pipeline_mode=pl.Buffered(1))


def _rows(tm, width):
    return pl.BlockSpec((tm, width), lambda i: (i, 0))


def _rms(x, g):
    return x * lax.rsqrt(jnp.mean(x * x, axis=-1, keepdims=True) + RMS_EPS) * g


def _bdot(a, b):
    return jnp.dot(a.astype(BF16), b.astype(BF16), preferred_element_type=F32)


def _nt_dot(a, b):
    return lax.dot_general(a.astype(BF16), b.astype(BF16), (((1,), (1,)), ((), ())),
                           preferred_element_type=F32)


def _split(x):
    hi = x.astype(BF16)
    lo = (x - hi.astype(F32)).astype(BF16)
    return hi, lo


def _split_dot(x, m):
    hi, lo = _split(x)
    return (jnp.dot(hi, m, preferred_element_type=F32)
            + jnp.dot(lo, m, preferred_element_type=F32))


def _same_head(n, head_dim):
    shift = int(math.log2(head_dim))
    r = lax.broadcasted_iota(I32, (n, n), 0) >> shift
    c = lax.broadcasted_iota(I32, (n, n), 1) >> shift
    return jnp.where(r == c, 1.0, 0.0).astype(BF16)


def _seg_sum(x, head_dim):
    g = _same_head(LANES, head_dim)
    parts = [_split_dot(x[:, i:i + LANES], g) for i in range(0, x.shape[1], LANES)]
    return jnp.concatenate(parts, axis=1)


def _gelu(x):
    return 0.5 * x * (1.0 + jnp.tanh(0.7978845608028654 * (x + 0.044715 * x * x * x)))


def _sigmoid(x):
    return 1.0 / (1.0 + jnp.exp(-x))


def _top_mask(gate, idx, axis, n_valid):
    cur = gate
    sel = jnp.zeros(gate.shape, F32)
    for _ in range(MOBA_TOPK):
        m = jnp.max(cur, axis=axis, keepdims=True)
        first = jnp.min(jnp.where(cur == m, idx, n_valid), axis=axis, keepdims=True)
        pick = (idx == first) & (m > NEG_INF)
        sel = jnp.where(pick, 1.0, sel)
        cur = jnp.where(pick, NEG_INF, cur)
    return sel


def _gmlp_kernel(x_ref, ng_ref, win_ref, lng_ref, lnb_ref, sa_ref, sb_ref, wout_ref, y_ref, aux_ref,
                 *, sample):
    x = x_ref[...]
    tm, width = x.shape[0], wout_ref.shape[0]
    h = _rms(x, ng_ref[...])
    z = _gelu(_bdot(h, win_ref[...]))
    u, v = z[:, :width], z[:, width:]
    mu = jnp.mean(v, axis=-1, keepdims=True)
    d = v - mu
    var = jnp.mean(d * d, axis=-1, keepdims=True)
    v = d * lax.rsqrt(var + GM_LN_EPS) * lng_ref[...] + lnb_ref[...]
    if sample:
        aux_ref[...] = v
        s = v * sa_ref[...] + sb_ref[...]
    else:
        gd = width // GM_GROUPS
        causal = (lax.broadcasted_iota(I32, (GM_CHUNK, GM_CHUNK), 0)
                  >= lax.broadcasted_iota(I32, (GM_CHUNK, GM_CHUNK), 1))
        for g in range(GM_GROUPS):
            wg = jnp.where(causal, sa_ref[g], 0.0).astype(BF16)
            cols = slice(g * gd, (g + 1) * gd)
            for c in range(tm // GM_CHUNK):
                rows = slice(c * GM_CHUNK, (c + 1) * GM_CHUNK)
                aux_ref[rows, cols] = (jnp.dot(wg, v[rows, cols].astype(BF16), preferred_element_type=F32)
                                       + sb_ref[:, cols])
        s = aux_ref[...]
    y_ref[...] = x + _bdot(u * s, wout_ref[...])


def _gmlp(x, ng, win, lng, lnb, sa, sb, wout, *, tm, sample):
    n, d = x.shape
    width = wout.shape[0]
    outs = [jax.ShapeDtypeStruct((n, d), F32)]
    out_specs = [_rows(tm, d)]
    scratch = []
    if sample:
        outs.append(jax.ShapeDtypeStruct((n, width), F32))
        out_specs.append(_rows(tm, width))
    else:
        scratch.append(pltpu.VMEM((tm, width), F32))
    return pl.pallas_call(
        functools.partial(_gmlp_kernel, sample=sample),
        out_shape=outs, grid=(n // tm,),
        in_specs=[_rows(tm, d), _whole(ng.shape), _whole(win.shape), _whole(lng.shape), _whole(lnb.shape),
                  _whole(sa.shape), _whole(sb.shape), _whole(wout.shape)],
        out_specs=out_specs, scratch_shapes=scratch,
        compiler_params=_cparams("parallel"), name="gmlp_sample" if sample else "gmlp_prompt",
    )(x, ng, win, lng, lnb, sa, sb, wout)


def _norm_linear_kernel(x_ref, ng_ref, w_ref, *o_refs):
    z = _bdot(_rms(x_ref[...], ng_ref[...]), w_ref[...])
    wd = z.shape[1] // len(o_refs)
    for i, o_ref in enumerate(o_refs):
        o_ref[...] = z[:, i * wd:(i + 1) * wd]


def _norm_linear(x, ng, w, n_out, *, tm, name):
    n, d = x.shape
    wd = w.shape[1] // n_out
    return pl.pallas_call(
        _norm_linear_kernel,
        out_shape=[jax.ShapeDtypeStruct((n, wd), F32)] * n_out, grid=(n // tm,),
        in_specs=[_rows(tm, d), _whole(ng.shape), _whole(w.shape)],
        out_specs=[_rows(tm, wd)] * n_out,
        compiler_params=_cparams("parallel"), name=name,
    )(x, ng, w)


def _qkv_prompt_kernel(x_ref, ng_ref, wq_ref, wkt_ref, wvt_ref, q_ref, kt_ref, vt_ref):
    h = _rms(x_ref[...], ng_ref[...]).astype(BF16)
    q_ref[...] = jnp.dot(h, wq_ref[...], preferred_element_type=F32)
    kt_ref[0] = _nt_dot(wkt_ref[...], h)
    vt_ref[0] = _nt_dot(wvt_ref[...], h)


def _qkv_prompt(x, ng, wq, wkt, wvt, *, tm, batch, seq):
    n, d = x.shape
    tps = seq // tm
    t_spec = pl.BlockSpec((1, d, tm), lambda i: (i // tps, 0, i % tps))
    return pl.pallas_call(
        _qkv_prompt_kernel,
        out_shape=[jax.ShapeDtypeStruct((n, d), F32)] + [jax.ShapeDtypeStruct((batch, d, seq), F32)] * 2,
        grid=(n // tm,),
        in_specs=[_rows(tm, d), _whole(ng.shape), _whole(wq.shape), _whole(wkt.shape), _whole(wvt.shape)],
        out_specs=[_rows(tm, d), t_spec, t_spec],
        compiler_params=_cparams("parallel"), name="moba_qkv_prompt",
    )(x, ng, wq, wkt, wvt)


def _linear_res_kernel(a_ref, w_ref, y_ref, o_ref):
    o_ref[...] = y_ref[...] + _bdot(a_ref[...], w_ref[...])


def _linear_res(a, w, y, *, tm, name):
    n, d = y.shape
    return pl.pallas_call(
        _linear_res_kernel, out_shape=jax.ShapeDtypeStruct((n, d), F32), grid=(n // tm,),
        in_specs=[_rows(tm, a.shape[1]), _whole(w.shape), _rows(tm, d)], out_specs=_rows(tm, d),
        compiler_params=_cparams("parallel"), name=name,
    )(a, w, y)


def _t5_bucket_table(rel):
    n = jnp.maximum(rel, 0)
    exact = REL_BUCKETS // 2
    nf = jnp.maximum(n, 1).astype(F32)
    large = exact + (jnp.log(nf / exact) / math.log(REL_MAX_DIST / exact)
                     * (REL_BUCKETS - exact)).astype(I32)
    return jnp.where(n < exact, n, jnp.minimum(large, REL_BUCKETS - 1)).astype(I32)


def _bias_tiles_kernel(rb_ref, bkt_ref, o_ref):
    h = pl.program_id(0)
    bkt = bkt_ref[...]
    acc = jnp.zeros(bkt.shape, F32)
    for b in range(REL_BUCKETS):
        acc = jnp.where(bkt == b, rb_ref[b, h], acc)
    future = (lax.broadcasted_iota(I32, bkt.shape, 2) > lax.broadcasted_iota(I32, bkt.shape, 1))
    own_block = lax.broadcasted_iota(I32, bkt.shape, 0) == 0
    o_ref[0] = jnp.where(future & own_block, NEG_INF, acc)


def _bias_tiles(rel_bias, bkt):
    heads = rel_bias.shape[1]
    return pl.pallas_call(
        _bias_tiles_kernel, out_shape=jax.ShapeDtypeStruct((heads,) + bkt.shape, F32), grid=(heads,),
        in_specs=[pl.BlockSpec(memory_space=pltpu.SMEM), _whole(bkt.shape)],
        out_specs=pl.BlockSpec((1,) + bkt.shape, lambda h: (h, 0, 0, 0)),
        compiler_params=_cparams("parallel"), name="moba_bias_tiles",
    )(rel_bias, bkt)


def _bias_rows_kernel(rb_ref, bkt_ref, o_ref):
    bkt = bkt_ref[...]
    acc = jnp.zeros(bkt.shape, F32)
    for b in range(REL_BUCKETS):
        acc = jnp.where(bkt == b, rb_ref[b:b + 1, :], acc)
    o_ref[...] = acc


def _bias_rows(rb_pad, bkt):
    return pl.pallas_call(
        _bias_rows_kernel, out_shape=jax.ShapeDtypeStruct(bkt.shape, F32), grid=(1,),
        in_specs=[_whole(rb_pad.shape), _whole(bkt.shape)], out_specs=_whole(bkt.shape),
        compiler_params=_cparams("arbitrary"), name="moba_bias_rows",
    )(rb_pad, bkt)


def _moba_prompt_kernel(rb_ref, q_ref, k_ref, v_ref, bias_ref, o_ref, kb_scr, vh_scr, kmean_scr, *, head_dim):
    pair = pl.program_id(1)
    blk = MOBA_BLOCK
    seq = k_ref.shape[2]
    n_blocks = seq // blk
    assert LANES == 2 * head_dim
    shift = int(math.log2(head_dim))
    scale = head_dim ** -0.5
    assert shift % 2 == 0 and n_blocks <= kmean_scr.shape[0]

    def prepare():
        kt, vt = k_ref[0], v_ref[0]
        second = (lax.broadcasted_iota(I32, (LANES, seq), 0) >> shift) == 1
        kb_scr[...] = kt.astype(BF16)
        vh_scr[0] = jnp.where(second, 0.0, vt).astype(BF16)
        vh_scr[1] = jnp.where(second, vt, 0.0).astype(BF16)
        rows = kmean_scr.shape[0]
        in_block = jnp.where((lax.broadcasted_iota(I32, (rows, seq), 1) >> int(math.log2(blk)))
                             == lax.broadcasted_iota(I32, (rows, seq), 0), 1.0, 0.0).astype(BF16)
        k_hi, k_lo = _split(kt)
        nt = lambda a, b: lax.dot_general(a, b, (((1,), (1,)), ((), ())), preferred_element_type=F32)
        kmean_scr[...] = (nt(in_block, k_hi) + nt(in_block, k_lo)) * (1.0 / blk)

    prepare()
    second = (lax.broadcasted_iota(I32, (blk, LANES), 1) >> shift) == 1
    km_hi, km_lo = _split(kmean_scr[...])
    nt = lambda a, b: lax.dot_general(a, b, (((1,), (1,)), ((), ())), preferred_element_type=F32)
    block_id = lax.broadcasted_iota(I32, (kmean_scr.shape[0], 2 * blk), 0)
    far_bias = [rb_ref[REL_BUCKETS - 1, pair * 2 + hh] for hh in range(2)]

    def select(own):
        q = q_ref[own * blk:(own + 1) * blk, :]
        q2 = jnp.concatenate([jnp.where(second, 0.0, q), jnp.where(second, q, 0.0)], axis=0)
        q_hi, q_lo = _split(q2)
        gate = nt(km_hi, q_hi) + nt(km_lo, q_hi) + nt(km_hi, q_lo)
        sel = _top_mask(jnp.where(block_id < own, gate, NEG_INF), block_id, 0, gate.shape[0]).T
        return (q2 * scale).astype(BF16), sel

    def scores(own, q_scaled):
        return jnp.dot(q_scaled, kb_scr[:, 0:(own + 1) * blk], preferred_element_type=F32)

    def mask(own, s, sel):
        head_rows = []
        for hh in range(2):
            rows = slice(hh * blk, (hh + 1) * blk)
            tiles = []
            for n in range(own + 1):
                t = s[rows, n * blk:(n + 1) * blk]
                chosen = sel[rows, n:n + 1] > 0.0
                if n == own:
                    t = t + bias_ref[hh, 0]
                elif n == own - 1:
                    t = jnp.where(chosen, t + bias_ref[hh, 1], NEG_INF)
                else:
                    t = t + jnp.where(chosen, far_bias[hh], NEG_INF)
                tiles.append(t)
            head_rows.append(jnp.concatenate(tiles, axis=1))
        return jnp.concatenate(head_rows, axis=0)

    def weights(s):
        p = jnp.exp(s - jnp.max(s, axis=-1, keepdims=True))
        return p, jnp.sum(p, axis=-1, keepdims=True)

    def output(own, p, l):
        keys = (own + 1) * blk
        p_pair = jnp.concatenate([p[:blk], p[blk:]], axis=1).astype(BF16)
        v_pair = jnp.concatenate([vh_scr[0, :, 0:keys], vh_scr[1, :, 0:keys]], axis=1)
        o_ref[own * blk:(own + 1) * blk, :] = nt(p_pair, v_pair) / jnp.where(second, l[blk:], l[:blk])

    order = list(range(n_blocks))
    groups = [(order[i], order[-1 - i]) for i in range(n_blocks // 2)] + ([(order[n_blocks // 2],)] * (n_blocks % 2))
    groups = [sum(groups[i:i + 2], ()) for i in range(0, len(groups), 2)]
    for group in groups:
        picked = [select(own) for own in group]
        raw = [scores(own, qs) for own, (qs, _) in zip(group, picked)]
        masked = [mask(own, s, sel) for own, s, (_, sel) in zip(group, raw, picked)]
        probs = [weights(s) for s in masked]
        for own, (p, l) in zip(group, probs):
            output(own, p, l)


def _moba_prompt(rel_bias, q, k, v, bias_tiles, *, batch, seq, head_dim):
    n, d = q.shape
    blk = MOBA_BLOCK
    hpt = LANES // head_dim
    return pl.pallas_call(
        functools.partial(_moba_prompt_kernel, head_dim=head_dim),
        out_shape=jax.ShapeDtypeStruct((n, d), F32), grid=(batch, d // LANES),
        in_specs=[pl.BlockSpec(memory_space=pltpu.SMEM),
                  pl.BlockSpec((seq, LANES), lambda b, p: (b, p)),
                  pl.BlockSpec((1, LANES, seq), lambda b, p: (b, p, 0)),
                  pl.BlockSpec((1, LANES, seq), lambda b, p: (b, p, 0)),
                  pl.BlockSpec((hpt, 2, blk, blk), lambda b, p: (p, 0, 0, 0))],
        out_specs=pl.BlockSpec((seq, LANES), lambda b, p: (b, p)),
        scratch_shapes=[pltpu.VMEM((LANES, seq), BF16), pltpu.VMEM((hpt, LANES, seq), BF16),
                        pltpu.VMEM((2 * SUBLANES, LANES), F32)],
        compiler_params=_cparams("parallel", "parallel"), name="moba_prompt_attn",
    )(rel_bias, q, k, v, bias_tiles)


def _moba_sample_kernel(pt_ref, q_ref, kn_ref, vn_ref, *rest, n_pages, head_dim):
    del pt_ref
    k_refs, v_refs = rest[:n_pages], rest[n_pages:2 * n_pages]
    bias_ref, rb0_ref, o_ref = rest[2 * n_pages:]
    d = q_ref.shape[2]
    heads = d // head_dim
    shift = int(math.log2(head_dim))
    per_block = MOBA_BLOCK // PAGE_SIZE
    blocks = range(n_pages // per_block)
    scale = head_dim ** -0.5
    head_lanes = jnp.where((lax.broadcasted_iota(I32, (heads, d), 1) >> shift)
                           == lax.broadcasted_iota(I32, (heads, d), 0), 1.0, 0.0)
    q = head_lanes * q_ref[0]
    q_hi, q_lo = _split(q)
    q2 = jnp.concatenate([q_hi, q_lo], axis=0)
    block_of = lambda refs, n: jnp.concatenate(
        [refs[per_block * n + j][0, 0].reshape(d, PAGE_SIZE) for j in range(per_block)], axis=1).astype(BF16)
    raw = [jnp.dot(q2, block_of(k_refs, n), preferred_element_type=F32) for n in blocks]
    raw = [r[:heads] + r[heads:] for r in raw]
    block_id = lax.broadcasted_iota(I32, (heads, LANES), 1)
    gate = jnp.full((heads, LANES), NEG_INF, F32)
    for n in blocks:
        gate = jnp.where(block_id == n, jnp.sum(raw[n], axis=-1, keepdims=True), gate)
    sel = _top_mask(gate, block_id, -1, LANES)
    s_new = jnp.sum(q * kn_ref[0], axis=-1, keepdims=True) * scale + rb0_ref[:, 0:1]
    s = [raw[n] * scale + bias_ref[n] for n in blocks]
    m = s_new
    for n in blocks:
        m = jnp.maximum(m, jnp.where(sel[:, n:n + 1] > 0.0, jnp.max(s[n], axis=-1, keepdims=True), NEG_INF))
    p = [jnp.exp(jnp.where(sel[:, n:n + 1] > 0.0, s[n] - m, NEG_INF)) for n in blocks]
    p_new = jnp.exp(s_new - m)
    den = p_new
    for n in blocks:
        den = den + jnp.sum(p[n], axis=-1, keepdims=True)
    p_all = jnp.concatenate(p, axis=1).astype(BF16)
    v_all = jnp.concatenate([block_of(v_refs, n) for n in blocks], axis=1)
    acc = lax.dot_general(p_all, v_all, (((1,), (1,)), ((), ())), preferred_element_type=F32)
    out = (acc + p_new * vn_ref[0]) * head_lanes / den
    o_ref[0] = jnp.sum(out, axis=0, keepdims=True)


def _moba_sample(page_table, q, k_new, v_new, cache_k, cache_v, layer, bias_rows, rel_bias, *, head_dim):
    nb, d = q.shape
    n_pages = page_table.shape[1]
    n_blocks = n_pages * PAGE_SIZE // MOBA_BLOCK
    heads = d // head_dim
    cache_kt = jnp.transpose(cache_k, (0, 1, 3, 4, 2))
    cache_vt = jnp.transpose(cache_v, (0, 1, 3, 4, 2))
    row = lambda x: x.reshape(nb, 1, d)
    vec = pl.BlockSpec((1, 1, d), lambda s, pt: (s, 0, 0))
    page = lambda j: pl.BlockSpec((1, 1, heads, head_dim, PAGE_SIZE), lambda s, pt: (layer, pt[s, j], 0, 0, 0))
    const = lambda shape: pl.BlockSpec(shape, lambda s, pt: (0,) * len(shape))
    bias = bias_rows[:, :heads].reshape(n_blocks, MOBA_BLOCK, heads).transpose(0, 2, 1)
    rb0 = jnp.broadcast_to(rel_bias[0][:, None], (heads, LANES))
    pages = [page(j) for j in range(n_pages)]
    out = pl.pallas_call(
        functools.partial(_moba_sample_kernel, n_pages=n_pages, head_dim=head_dim),
        out_shape=jax.ShapeDtypeStruct((nb, 1, d), F32),
        grid_spec=pltpu.PrefetchScalarGridSpec(
            num_scalar_prefetch=1, grid=(nb,),
            in_specs=[vec, vec, vec, *pages, *pages, const(bias.shape), const(rb0.shape)],
            out_specs=vec),
        compiler_params=_cparams("parallel"), name="moba_sample_attn",
    )(page_table, row(q), row(k_new), row(v_new), *([cache_kt] * n_pages), *([cache_vt] * n_pages), bias, rb0)
    return out.reshape(nb, d)


POOL_CARRY = 16
assert all(b == 2 * a for a, b in zip((1,) + POOL_WINDOWS, POOL_WINDOWS)) and POOL_CARRY > POOL_CTX


def _pool_prompt_kernel(y_ref, ng_ref, w_ref, sc_ref, yout_ref, st_ref, carry_scr, *, tps):
    i = pl.program_id(0)
    y = y_ref[...]
    tm, d = y.shape
    gd = d // len(POOL_WINDOWS)
    h = _rms(y, ng_ref[...])

    @pl.when(i % tps == 0)
    def _():
        carry_scr[...] = jnp.zeros(carry_scr.shape, F32)

    rows = jnp.concatenate([carry_scr[...], h], axis=0)
    carry_scr[...] = h[tm - POOL_CARRY:, :]
    st_ref[0] = rows[tm + POOL_CARRY - POOL_CTX:, :]
    pos = (i % tps) * tm + lax.broadcasted_iota(I32, (tm, 1), 0) + 1
    sums = rows
    for g, win in enumerate(POOL_WINDOWS):
        cols = slice(g * gd, (g + 1) * gd)
        sums = sums[:, (gd if g else 0):]
        sums = sums + pltpu.roll(sums, win // 2, axis=0)
        cnt = jnp.minimum(pos, win).astype(F32)
        mixed = _bdot(sums[POOL_CARRY:, 0:gd] / cnt - h[:, cols], w_ref[g])
        yout_ref[:, cols] = y[:, cols] + mixed * sc_ref[:, cols]


def _pool_prompt(y, ng, w, sc, *, tm, seq):
    n, d = y.shape
    tps = seq // tm
    return pl.pallas_call(
        functools.partial(_pool_prompt_kernel, tps=tps),
        out_shape=[jax.ShapeDtypeStruct((n, d), F32), jax.ShapeDtypeStruct((n // seq, POOL_CTX, d), F32)],
        grid=(n // tm,),
        in_specs=[_rows(tm, d), _whole(ng.shape), _whole(w.shape), _whole(sc.shape)],
        out_specs=[_rows(tm, d), pl.BlockSpec((1, POOL_CTX, d), lambda i: (i // tps, 0, 0))],
        scratch_shapes=[pltpu.VMEM((POOL_CARRY, d), F32)],
        compiler_params=_cparams("arbitrary"), name="pool_prompt",
    )(y, ng, w, sc)


def _pool_sample_kernel(y_ref, ng_ref, prev_ref, w_ref, sc_ref, yout_ref, h_ref):
    y = y_ref[...]
    d = y.shape[1]
    gd = d // len(POOL_WINDOWS)
    h = _rms(y, ng_ref[...])
    h_ref[...] = h
    for g, win in enumerate(POOL_WINDOWS):
        cols = slice(g * gd, (g + 1) * gd)
        wsum = h[:, cols]
        for back in range(1, win):
            wsum = wsum + prev_ref[POOL_CTX - back, :, cols]
        mixed = _bdot(wsum / float(win) - h[:, cols], w_ref[g])
        yout_ref[:, cols] = y[:, cols] + mixed * sc_ref[:, cols]


def _pool_sample(y, ng, prev_t, w, sc):
    n, d = y.shape
    return pl.pallas_call(
        _pool_sample_kernel, out_shape=[jax.ShapeDtypeStruct((n, d), F32)] * 2, grid=(1,),
        in_specs=[_whole(y.shape), _whole(ng.shape), _whole(prev_t.shape), _whole(w.shape), _whole(sc.shape)],
        out_specs=[_whole(y.shape)] * 2,
        compiler_params=_cparams("arbitrary"), name="pool_sample",
    )(y, ng, prev_t, w, sc)


def _rwkv_proj_kernel(y_ref, ng_ref, mu_ref, wr_ref, wk_ref, wv_ref, w1_ref, w2_ref, a1_ref, a2_ref,
                      g1_ref, g2_ref, vec_ref, *rest, tps, sample):
    if sample:
        prev_ref, r_o, w_o, k_o, v_o, kk_o, kka_o, g_o, sh_o = rest
    else:
        r_o, w_o, k_o, v_o, kk_o, kka_o, g_o, sh_o, hs_scr = rest
    y = y_ref[...]
    tm = y.shape[0]
    h = _rms(y, ng_ref[...])
    if sample:
        h_prev = prev_ref[...]
        sh_o[...] = h
    else:
        i = pl.program_id(0)

        @pl.when(i % tps == 0)
        def _():
            hs_scr[...] = jnp.zeros(hs_scr.shape, F32)

        h_prev = pltpu.roll(jnp.concatenate([hs_scr[...], h], axis=0), 1, axis=0)[SUBLANES:]
        hs_scr[...] = h[tm - SUBLANES:, :]
        sh_o[0] = h[tm - 1:tm, :]
    xx = h_prev - h
    mix = lambda m: h + xx * mu_ref[m:m + 1, :]
    r = _bdot(mix(0), wr_ref[...])
    k = _bdot(mix(2), wk_ref[...])
    v = _bdot(mix(3), wv_ref[...])
    z = vec_ref[0:1, :] + _bdot(jnp.tanh(_bdot(mix(1), w1_ref[...])), w2_ref[...])
    w_log = jnp.minimum(z, 0.0) - jnp.log(1.0 + jnp.exp(-jnp.abs(z))) - 0.5
    a = _sigmoid(vec_ref[1:2, :] + _bdot(_bdot(mix(4), a1_ref[...]), a2_ref[...]))
    g_o[...] = _bdot(_sigmoid(_bdot(mix(5), g1_ref[...])), g2_ref[...])
    kk = k * vec_ref[2:3, :]
    kk = kk / jnp.maximum(jnp.sqrt(_seg_sum(kk * kk, RWKV_HEAD_DIM)), 1e-12)
    r_o[...] = r
    w_o[...] = -jnp.exp(w_log)
    k_o[...] = k * (1.0 + (a - 1.0) * vec_ref[3:4, :])
    v_o[...] = v
    kk_o[...] = kk
    kka_o[...] = kk * a


def _rwkv_proj(y, ng, mu, mats, vec, prev, *, tm, seq, sample):
    n, d = y.shape
    tps = max(seq // tm, 1)
    ins = [y, ng, mu, *mats, vec]
    in_specs = [_rows(tm, d), _whole(ng.shape), _whole(mu.shape), *[_whole(m.shape) for m in mats],
                _whole(vec.shape)]
    outs = [jax.ShapeDtypeStruct((n, d), F32)] * 7
    out_specs = [_rows(tm, d)] * 7
    scratch = []
    if sample:
        ins.append(prev)
        in_specs.append(_rows(tm, d))
        outs.append(jax.ShapeDtypeStruct((n, d), F32))
        out_specs.append(_rows(tm, d))
    else:
        outs.append(jax.ShapeDtypeStruct((n // seq, 1, d), F32))
        out_specs.append(pl.BlockSpec((1, 1, d), lambda i: (i // tps, 0, 0)))
        scratch.append(pltpu.VMEM((SUBLANES, d), F32))
    return pl.pallas_call(
        functools.partial(_rwkv_proj_kernel, tps=tps, sample=sample),
        out_shape=outs, grid=(n // tm,), in_specs=in_specs, out_specs=out_specs, scratch_shapes=scratch,
        compiler_params=_cparams("arbitrary"), name="rwkv_proj_sample" if sample else "rwkv_proj_prompt",
    )(*ins)


RWKV_CHUNK = 128


def _rwkv_chunk_kernel(r_ref, lw_ref, k_ref, v_ref, kk_ref, kka_ref, o_ref, st_ref, t_scr):
    c = pl.program_id(1)
    ch, d = r_ref.shape
    hd = RWKV_HEAD_DIM
    shift = int(math.log2(hd))
    n_pairs = d // LANES

    @pl.when(c == 0)
    def _():
        t_scr[...] = jnp.zeros(t_scr.shape, F32)

    ri = lax.broadcasted_iota(I32, (ch, ch), 0)
    ci = lax.broadcasted_iota(I32, (ch, ch), 1)
    incl = ci <= ri
    strict = ci < ri
    incl2 = jnp.concatenate([incl, incl], axis=1)
    strict2 = jnp.concatenate([strict, strict], axis=1)
    same_head = (ri >> shift) == (ci >> shift)
    eye = ri == ci
    head1 = (lax.broadcasted_iota(I32, (ch, LANES), 1) >> shift) == 1
    head1_wide = jnp.concatenate([head1, head1], axis=1)

    def by_head(x, mask):
        return jnp.concatenate([jnp.where(mask, 0.0, x), jnp.where(mask, x, 0.0)], axis=0).astype(BF16)

    lw = lw_ref[...]
    tri = jnp.where(incl, 1.0, 0.0).astype(BF16)
    p1 = lw.astype(BF16)
    rem = lw - p1.astype(F32)
    p2 = rem.astype(BF16)
    p3 = (rem - p2.astype(F32)).astype(BF16)
    cl = (jnp.dot(tri, p1, preferred_element_type=F32) + jnp.dot(tri, p2, preferred_element_type=F32)
          + jnp.dot(tri, p3, preferred_element_type=F32))
    mid = cl[ch // 2 - 1:ch // 2, :]
    last = cl[ch - 1:ch, :]
    e_mid = jnp.exp(mid)
    e_neg = jnp.exp(mid - cl)
    e_tail = jnp.exp(last - cl)
    p_last = jnp.exp(last)
    kk, kka, k = kk_ref[...], kka_ref[...], k_ref[...]
    a_s = -kk * jnp.exp(cl - lw - mid)
    r_s = r_ref[...] * jnp.exp(cl - mid)
    b_s = kka * e_neg
    k_s = k * e_neg
    b_t = kka * e_tail
    k_t = k * e_tail
    v = v_ref[...]
    zero = jnp.zeros((ch, ch), BF16)

    pairs = range(n_pairs)
    cols = [slice(p * LANES, (p + 1) * LANES) for p in pairs]
    mm = lambda a, b: jnp.dot(a, b, preferred_element_type=F32)
    lane_cat = lambda a, b: jnp.concatenate([a, b], axis=1)
    a_p = [a_s[:, c] for c in cols]
    r_p = [r_s[:, c] for c in cols]
    v_p = [v[:, c] for c in cols]
    gram = []
    for p in pairs:
        lhs = jnp.concatenate([jnp.where(head1, 0.0, a_p[p]), jnp.where(head1, 0.0, r_p[p]),
                               jnp.where(head1, a_p[p], 0.0), jnp.where(head1, r_p[p], 0.0)], axis=0)
        rhs = lane_cat(b_s[:, cols[p]].T, k_s[:, cols[p]].T)
        gram.append(mm(lhs.astype(BF16), rhs.astype(BF16)))
    ab0 = [jnp.where(strict2, g_[0:ch], 0.0) for g_ in gram]
    rb0 = [jnp.where(incl2, g_[ch:2 * ch], 0.0) for g_ in gram]
    ab1 = [jnp.where(strict2, g_[2 * ch:3 * ch], 0.0) for g_ in gram]
    rb1 = [jnp.where(incl2, g_[3 * ch:], 0.0) for g_ in gram]
    v_heads = [by_head(v_p[p], head1) for p in pairs]
    x = [lane_cat(a_p[p], mm(lane_cat(ab0[p][:, ch:], ab1[p][:, ch:]).astype(BF16), v_heads[p])) for p in pairs]
    pw = [lane_cat(ab0[p][:, :ch], ab1[p][:, :ch]).astype(BF16) for p in pairs]
    n_sq = int(math.log2(ch))
    for it in range(n_sq):
        x = [x[p] + mm(pw[p], by_head(x[p], head1_wide)) for p in pairs]
        if it < n_sq - 1:
            pw = [mm(pw[p], jnp.concatenate([lane_cat(pw[p][:, :ch], zero), lane_cat(zero, pw[p][:, ch:])],
                                            axis=0)).astype(BF16) for p in pairs]
    x = [lane_cat(x[p][:, :ch] * e_mid[:, cols[p]], x[p][:, ch:]) for p in pairs]
    qo = [mm(lane_cat(rb0[p][:, :ch], rb1[p][:, :ch]).astype(BF16), by_head(x[p], head1_wide)) for p in pairs]
    o_intra = [qo[p][:, ch:] + mm(lane_cat(rb0[p][:, ch:], rb1[p][:, ch:]).astype(BF16), v_heads[p])
               for p in pairs]
    q = [r_p[p] * e_mid[:, cols[p]] + qo[p][:, :ch] for p in pairs]
    gh = [mm(b_t[:, cols[p]].T.astype(BF16), x[p].astype(BF16)) for p in pairs]
    g = [jnp.where(same_head, gh[p][:, :ch], 0.0) + jnp.where(eye, p_last[:, cols[p]], 0.0) for p in pairs]
    h = [jnp.where(same_head, gh[p][:, ch:] + _bdot(k_t[:, cols[p]].T, v_p[p]), 0.0) for p in pairs]
    for p in pairs:
        t_old = t_scr[p].astype(BF16)
        o_ref[:, cols[p]] = mm(q[p].astype(BF16), t_old) + o_intra[p]
        t_scr[p] = mm(g[p].astype(BF16), t_old) + h[p]

    @pl.when(c == pl.num_programs(1) - 1)
    def _():
        for p in range(n_pairs):
            s_pair = t_scr[p].T
            st_ref[0, p] = s_pair[:hd, :] + s_pair[hd:, :]


def _rwkv_chunked(r, lw, k, v, kk, kka, *, batch, seq):
    n, d = r.shape
    ch = RWKV_CHUNK
    assert ch == LANES and seq % ch == 0
    n_chunks = seq // ch
    n_pairs = d // LANES
    blk = pl.BlockSpec((ch, d), lambda b, c: (b * n_chunks + c, 0))
    return pl.pallas_call(
        _rwkv_chunk_kernel,
        out_shape=[jax.ShapeDtypeStruct((n, d), F32),
                   jax.ShapeDtypeStruct((batch, n_pairs, RWKV_HEAD_DIM, LANES), F32)],
        grid=(batch, n_chunks), in_specs=[blk] * 6,
        out_specs=[blk, pl.BlockSpec((1, n_pairs, RWKV_HEAD_DIM, LANES), lambda b, c: (b, 0, 0, 0))],
        scratch_shapes=[pltpu.VMEM((n_pairs, LANES, LANES), F32)],
        compiler_params=_cparams("parallel", "arbitrary"), name="rwkv_chunked",
    )(r, lw, k, v, kk, kka)


def _rwkv_step_kernel(r_ref, lw_ref, k_ref, v_ref, kk_ref, kka_ref, s0_ref, o_ref, st_ref):
    hd = s0_ref.shape[1]
    w = jnp.exp(lw_ref[...])
    kk, kka, k, r = kk_ref[...], kka_ref[...], k_ref[...], r_ref[...]

    def value_row(i, carry):
        s = s0_ref[0, i]
        s_kk = jnp.sum(s * kk, axis=0, keepdims=True)
        s = s * w - s_kk * kka + v_ref[pl.ds(i, 1), :] * k
        st_ref[0, i] = s
        o_ref[pl.ds(i, 1), :] = jnp.sum(s * r, axis=0, keepdims=True)
        return carry

    lax.fori_loop(0, hd, value_row, 0, unroll=8)


def _rwkv_step(r, lw, k, v, kk, kka, s0):
    d, n = r.shape
    heads, hd = s0.shape[0], s0.shape[1]
    vec = pl.BlockSpec((hd, n), lambda h: (h, 0))
    st = pl.BlockSpec((1, hd, hd, n), lambda h: (h, 0, 0, 0))
    return pl.pallas_call(
        _rwkv_step_kernel,
        out_shape=[jax.ShapeDtypeStruct((d, n), F32), jax.ShapeDtypeStruct(s0.shape, F32)],
        grid=(heads,), in_specs=[vec] * 6 + [st], out_specs=[vec, st],
        compiler_params=_cparams("parallel"), name="rwkv_step",
    )(r, lw, k, v, kk, kka, s0)


def _rwkv_out_kernel(o_ref, r_ref, k_ref, v_ref, g_ref, y_ref, vec_ref, wo_ref, yout_ref):
    hd = RWKV_HEAD_DIM
    o = o_ref[...]
    mu = _seg_sum(o, hd) * (1.0 / hd)
    dlt = o - mu
    var = _seg_sum(dlt * dlt, hd) * (1.0 / hd)
    o = dlt * lax.rsqrt(var + RWKV_LNX_EPS) * vec_ref[1:2, :] + vec_ref[2:3, :]
    o = o + _seg_sum(r_ref[...] * k_ref[...] * vec_ref[0:1, :], hd) * v_ref[...]
    yout_ref[...] = y_ref[...] + _bdot(o * g_ref[...], wo_ref[...])


def _rwkv_out(o, r, k, v, g, y, vec, wo, *, tm, name):
    n, d = y.shape
    return pl.pallas_call(
        _rwkv_out_kernel, out_shape=jax.ShapeDtypeStruct((n, d), F32), grid=(n // tm,),
        in_specs=[_rows(tm, d)] * 6 + [_whole(vec.shape), _whole(wo.shape)], out_specs=_rows(tm, d),
        compiler_params=_cparams("parallel"), name=name,
    )(o, r, k, v, g, y, vec, wo)


FFN_CHUNK = 256


def _ffn_act(gate, g1, g2, up, cw, cb):
    return _gelu(cw[0:1] * g2 + cw[1:2] * g1 + cw[2:3] * gate + cb) * up


def _ffn_prompt_kernel(y_ref, ng_ref, win_ref, cw_ref, cb_ref, wout_ref, *rest, tps, final):
    if final:
        fg_ref, yout_ref, st_ref, yfin_ref, h_scr, gs_scr, act_scr, carry_scr = rest
    else:
        yout_ref, st_ref, h_scr, gs_scr, act_scr, carry_scr = rest
    i = pl.program_id(0)
    y = y_ref[...]
    tm = y.shape[0]
    hid = wout_ref.shape[0]
    tf = FFN_CHUNK
    h_scr[...] = _rms(y, ng_ref[...]).astype(BF16)

    @pl.when(i % tps == 0)
    def _():
        carry_scr[...] = jnp.zeros(carry_scr.shape, F32)

    for j in range(hid // tf):
        cols = slice(j * tf, (j + 1) * tf)
        hb = h_scr[...]
        gs_scr[0:SUBLANES, :] = carry_scr[:, cols]
        gs_scr[SUBLANES:, :] = jnp.dot(hb, win_ref[:, cols], preferred_element_type=F32)
        up = jnp.dot(hb, win_ref[:, hid + j * tf:hid + (j + 1) * tf], preferred_element_type=F32)
        gate = gs_scr[SUBLANES:, :]
        carry_scr[:, cols] = gate[tm - SUBLANES:, :]
        st_ref[0, :, cols] = gate[tm - (FFN_CONV - 1):, :]
        act = _ffn_act(gate, gs_scr[SUBLANES - 1:tm + SUBLANES - 1, :], gs_scr[SUBLANES - 2:tm + SUBLANES - 2, :],
                       up, cw_ref[:, cols], cb_ref[:, cols])
        act_scr[:, cols] = act.astype(BF16)
    y_new = y + jnp.dot(act_scr[...], wout_ref[...], preferred_element_type=F32)
    yout_ref[...] = y_new
    if final:
        yfin_ref[...] = _rms(y_new, fg_ref[...])


def _ffn_prompt(y, ng, win, cw, cb, wout, layer, fg, *, tm, seq):
    n, d = y.shape
    hid = wout.shape[1]
    tps = seq // tm
    final = fg is not None
    of_layer = lambda a: pl.BlockSpec((None,) + a.shape[1:], lambda i: (layer, 0, 0), pipeline_mode=pl.Buffered(1))
    ins = [y, ng, win, cw, cb, wout]
    in_specs = [_rows(tm, d), _whole(ng.shape), of_layer(win), _whole(cw.shape), _whole(cb.shape), of_layer(wout)]
    outs = [jax.ShapeDtypeStruct((n, d), F32), jax.ShapeDtypeStruct((n // seq, FFN_CONV - 1, hid), F32)]
    out_specs = [_rows(tm, d), pl.BlockSpec((1, FFN_CONV - 1, hid), lambda i: (i // tps, 0, 0))]
    if final:
        ins.append(fg)
        in_specs.append(_whole(fg.shape))
        outs.append(jax.ShapeDtypeStruct((n, d), F32))
        out_specs.append(_rows(tm, d))
    return pl.pallas_call(
        functools.partial(_ffn_prompt_kernel, tps=tps, final=final),
        out_shape=outs, grid=(n // tm,), in_specs=in_specs, out_specs=out_specs,
        scratch_shapes=[pltpu.VMEM((tm, d), BF16), pltpu.VMEM((tm + SUBLANES, FFN_CHUNK), F32),
                        pltpu.VMEM((tm, hid), BF16), pltpu.VMEM((SUBLANES, hid), F32)],
        compiler_params=_cparams("arbitrary"), name="ffn_prompt",
    )(*ins)


def _ffn_sample_kernel(y_ref, ng_ref, wg_ref, wu_ref, cw_ref, cb_ref, p2_ref, p1_ref, wout_ref, *rest, final):
    if final:
        fg_ref, yout_ref, gate_ref, yfin_ref, h_scr, acc_scr = rest
    else:
        yout_ref, gate_ref, h_scr, acc_scr = rest
    j = pl.program_id(0)

    @pl.when(j == 0)
    def _():
        h_scr[...] = _rms(y_ref[...], ng_ref[...]).astype(BF16)
        acc_scr[...] = jnp.zeros(acc_scr.shape, F32)

    hb = h_scr[...]
    gate = jnp.dot(hb, wg_ref[...], preferred_element_type=F32)
    up = jnp.dot(hb, wu_ref[...], preferred_element_type=F32)
    gate_ref[...] = gate
    act = _ffn_act(gate, p1_ref[...], p2_ref[...], up, cw_ref[...], cb_ref[...])
    acc_scr[...] += _bdot(act, wout_ref[...])

    @pl.when(j == pl.num_programs(0) - 1)
    def _():
        y_new = y_ref[...] + acc_scr[...]
        yout_ref[...] = y_new
        if final:
            yfin_ref[...] = _rms(y_new, fg_ref[...])


def _ffn_sample(y, ng, win, cw, cb, p2, p1, wout, layer, fg):
    n, d = y.shape
    hid = wout.shape[1]
    tf = hid // 2 if (hid // 2) % LANES == 0 else FFN_CHUNK
    nf = hid // tf
    final = fg is not None
    keep = lambda shape: pl.BlockSpec(shape, lambda j: (0,) * len(shape))
    chunk = lambda rows: pl.BlockSpec((rows, tf), lambda j: (0, j))
    ins = [y, ng, win, win, cw, cb, p2, p1, wout]
    in_specs = [keep((n, d)), keep(ng.shape), pl.BlockSpec((None, d, tf), lambda j: (layer, 0, j)),
                pl.BlockSpec((None, d, tf), lambda j: (layer, 0, nf + j)),
                chunk(FFN_CONV), chunk(1), chunk(n), chunk(n), pl.BlockSpec((None, tf, d), lambda j: (layer, j, 0))]
    outs = [jax.ShapeDtypeStruct((n, d), F32), jax.ShapeDtypeStruct((n, hid), F32)]
    out_specs = [keep((n, d)), chunk(n)]
    if final:
        ins.append(fg)
        in_specs.append(keep(fg.shape))
        outs.append(jax.ShapeDtypeStruct((n, d), F32))
        out_specs.append(keep((n, d)))
    return pl.pallas_call(
        functools.partial(_ffn_sample_kernel, final=final),
        out_shape=outs, grid=(nf,), in_specs=in_specs, out_specs=out_specs,
        scratch_shapes=[pltpu.VMEM((n, d), BF16), pltpu.VMEM((n, d), F32)],
        compiler_params=_cparams("arbitrary"), name="ffn_sample",
    )(*ins)


def kernel(x_prompt, x_sample, cache_moba_k, cache_moba_v, state_pool, state_rwkv_wkv, state_rwkv_shift, state_ffn_conv, page_table, norm_mix_g, norm_ffn_g, norm_final_g, rel_bias, gm_w_in, gm_ln_g, gm_ln_b, gm_w_s, gm_b_s, gm_w_out, moba_w_qkv, moba_w_o, pool_w, pool_scale, rwkv_mu, rwkv_w_r, rwkv_w_k, rwkv_w_v, rwkv_w_o, rwkv_w0, rwkv_w1, rwkv_w2, rwkv_a0, rwkv_a1, rwkv_a2, rwkv_g1, rwkv_g2, rwkv_k_k, rwkv_k_a, rwkv_r_k, rwkv_lnx_g, rwkv_lnx_b, ffn_w_in, ffn_conv_w, ffn_conv_b, ffn_w_out):
    bp, seq, d = x_prompt.shape
    bs = x_sample.shape[0]
    depth = norm_mix_g.shape[0]
    assert x_sample.shape[1] == 1 and depth == 4
    past_len = page_table.shape[1] * PAGE_SIZE
    assert seq % MOBA_BLOCK == 0 and past_len % MOBA_BLOCK == 0 and past_len % GM_CHUNK == 0
    row = lambda vct: vct.reshape(1, -1)
    bf = lambda m: m.astype(BF16)
    yp = x_prompt.reshape(bp * seq, d)
    ys = x_sample.reshape(bs, d)
    tm = TOKEN_TILE
    assert seq % tm == 0 and seq % RWKV_PROJ_TILE == 0 and seq % RWKV_CHUNK == 0
    conv_p, conv_s = [], []
    ffn_in, ffn_out = bf(ffn_w_in), bf(ffn_w_out)

    def ffn(i, yp, ys):
        last = i == depth - 1
        cw, cb, ng = ffn_conv_w[i], row(ffn_conv_b[i]), row(norm_ffn_g[i])
        fg = row(norm_final_g) if last else None
        res_p = _ffn_prompt(yp, ng, ffn_in, cw, cb, ffn_out, i, fg, tm=tm, seq=seq)
        st = state_ffn_conv[i]
        res_s = _ffn_sample(ys, ng, ffn_in, cw, cb, st[:, 0], st[:, 1], ffn_out, i, fg)
        conv_p.append(res_p[1])
        conv_s.append(jnp.stack([st[:, 1], res_s[1]], axis=1))
        if last:
            return res_p[2], res_s[2]
        return res_p[0], res_s[0]

    ng = row(norm_mix_g[0])
    width = gm_w_out.shape[1]
    gd = width // GM_GROUPS
    gm_in, gm_out = bf(gm_w_in[0]), bf(gm_w_out[0])
    lng, lnb = row(gm_ln_g[0]), row(gm_ln_b[0])
    sb_prompt = jnp.repeat(gm_b_s[0].T, gd, axis=1)
    (yp,) = _gmlp(yp, ng, gm_in, lng, lnb, gm_w_s[0], sb_prompt, gm_out, tm=tm, sample=False)
    sa_first = row(jnp.repeat(gm_w_s[0][:, 0, 0], gd))
    sb_first = row(jnp.repeat(gm_b_s[0][:, 0], gd))
    ys, gm_v = _gmlp(ys, ng, gm_in, lng, lnb, sa_first, sb_first, gm_out, tm=bs, sample=True)
    gm_v_sample = gm_v.reshape(1, bs, 1, width)
    yp, ys = ffn(0, yp, ys)

    ng = row(norm_mix_g[1])
    w_qkv, w_o = bf(moba_w_qkv[0]), bf(moba_w_o[0])
    heads = MOBA_HEADS
    hd = d // heads
    qp, kp_t, vp_t = _qkv_prompt(yp, ng, w_qkv[:, :d], w_qkv[:, d:2 * d].T, w_qkv[:, 2 * d:].T,
                                 tm=2 * tm, batch=bp, seq=seq)
    qs, ks, vs = _norm_linear(ys, ng, w_qkv, 3, tm=bs, name="moba_qkv_sample")
    blk = MOBA_BLOCK
    qi = jnp.arange(blk, dtype=I32)[:, None]
    ki = jnp.arange(blk, dtype=I32)[None, :]
    bkt_tiles = _t5_bucket_table(jnp.stack([qi - ki, blk + qi - ki]))
    bias_tiles = _bias_tiles(rel_bias, bkt_tiles)
    op = _moba_prompt(rel_bias, qp, kp_t, vp_t, bias_tiles, batch=bp, seq=seq, head_dim=hd)
    bkt_rows = _t5_bucket_table(past_len - jnp.arange(past_len, dtype=I32))
    rb_pad = jnp.pad(rel_bias, ((0, 0), (0, LANES - heads)))
    bias_rows = _bias_rows(rb_pad, jnp.broadcast_to(bkt_rows[:, None], (past_len, LANES)))
    os_ = _moba_sample(page_table, qs, ks, vs, cache_moba_k, cache_moba_v, 0, bias_rows, rel_bias, head_dim=hd)
    yp = _linear_res(op, w_o, yp, tm=tm, name="moba_out_prompt")
    ys = _linear_res(os_, w_o, ys, tm=bs, name="moba_out_sample")
    moba_k_prompt = kp_t.reshape(1, bp, heads, hd, seq).transpose(0, 1, 4, 2, 3)
    moba_v_prompt = vp_t.reshape(1, bp, heads, hd, seq).transpose(0, 1, 4, 2, 3)
    moba_k_sample = ks.reshape(1, bs, 1, heads, hd)
    moba_v_sample = vs.reshape(1, bs, 1, heads, hd)
    yp, ys = ffn(1, yp, ys)

    ng = row(norm_mix_g[2])
    pw, psc = bf(pool_w[0]), row(pool_scale[0])
    yp, pool_p = _pool_prompt(yp, ng, pw, psc, tm=tm, seq=seq)
    ys, hs = _pool_sample(ys, ng, jnp.swapaxes(state_pool[0], 0, 1), pw, psc)
    pool_prompt = pool_p[None]
    pool_sample = jnp.concatenate([state_pool[0][:, 1:], hs[:, None]], axis=1)[None]
    yp, ys = ffn(2, yp, ys)

    ng = row(norm_mix_g[3])
    mats = [bf(m[0]) for m in (rwkv_w_r, rwkv_w_k, rwkv_w_v, rwkv_w1, rwkv_w2, rwkv_a1, rwkv_a2,
                               rwkv_g1, rwkv_g2)]
    vec_in = jnp.stack([rwkv_w0[0], rwkv_a0[0], rwkv_k_k[0], rwkv_k_a[0]])
    vec_out = jnp.stack([rwkv_r_k[0].reshape(-1), rwkv_lnx_g[0], rwkv_lnx_b[0]])
    rh = d // RWKV_HEAD_DIM
    *seqs_p, gp, shp = _rwkv_proj(yp, ng, rwkv_mu[0], mats, vec_in, None, tm=RWKV_PROJ_TILE, seq=seq,
                                  sample=False)
    *seqs_s, gs, shs = _rwkv_proj(ys, ng, rwkv_mu[0], mats, vec_in, state_rwkv_shift[0], tm=bs, seq=1,
                                  sample=True)
    o_p, st_p = _rwkv_chunked(*seqs_p, batch=bp, seq=seq)
    hd_r = RWKV_HEAD_DIM
    wkv_p = st_p.reshape(bp, rh // 2, hd_r, 2, hd_r).transpose(0, 1, 3, 2, 4).reshape(bp, rh, hd_r, hd_r)
    o_s_t, wkv_s_t = _rwkv_step(*[a.T for a in seqs_s], jnp.transpose(state_rwkv_wkv[0], (1, 2, 3, 0)))
    o_s, wkv_s = o_s_t.T, jnp.transpose(wkv_s_t, (3, 0, 1, 2))
    w_o = bf(rwkv_w_o[0])
    r_p, _, k_p, v_p = seqs_p[:4]
    r_s, _, k_s, v_s = seqs_s[:4]
    yp = _rwkv_out(o_p, r_p, k_p, v_p, gp, yp, vec_out, w_o, tm=tm, name="rwkv_out_prompt")
    ys = _rwkv_out(o_s.reshape(bs, d), r_s, k_s, v_s, gs, ys, vec_out, w_o, tm=bs, name="rwkv_out_sample")
    yp, ys = ffn(3, yp, ys)

    return (yp.reshape(bp, seq, d), ys.reshape(bs, 1, d), gm_v_sample, moba_k_prompt, moba_v_prompt,
            moba_k_sample, moba_v_sample, pool_prompt, pool_sample, wkv_p[None], wkv_s[None],
            shp.reshape(1, bp, d), shs[None], jnp.stack(conv_p), jnp.stack(conv_s))
```

```python
import functools
import math

import jax
import jax.numpy as jnp
from jax import lax
from jax.experimental import pallas as pl
from jax.experimental.pallas import tpu as pltpu

F32 = jnp.float32
BF16 = jnp.bfloat16
I32 = jnp.int32

LANES = 128
SUBLANES = 8
VMEM_LIMIT_BYTES = 56 * 2**20

TOKEN_TILE = 512
RWKV_PROJ_TILE = 256

RMS_EPS = 1e-6
GM_LN_EPS = 1e-5
GM_CHUNK = 128
GM_GROUPS = 8
MOBA_HEADS = 16
MOBA_BLOCK = 256
MOBA_TOPK = 3
REL_BUCKETS = 32
REL_MAX_DIST = 128
PAGE_SIZE = 128
POOL_WINDOWS = (2, 4, 8, 16)
POOL_CTX = max(POOL_WINDOWS) - 1
RWKV_HEAD_DIM = 64
RWKV_LNX_EPS = 64e-5
FFN_CONV = 3
NEG_INF = float("-inf")

assert MOBA_BLOCK >= REL_MAX_DIST


def _cparams(*sem):
    return pltpu.CompilerParams(dimension_semantics=sem, vmem_limit_bytes=VMEM_LIMIT_BYTES)


def _whole(shape):
    nd = len(shape)
    return pl.BlockSpec(shape, lambda *_: (0,) * nd, pipeline_mode=pl.Buffered(1))


def _rows(tm, width):
    return pl.BlockSpec((tm, width), lambda i: (i, 0))


def _rms(x, g):
    return x * lax.rsqrt(jnp.mean(x * x, axis=-1, keepdims=True) + RMS_EPS) * g


def _bdot(a, b):
    return jnp.dot(a.astype(BF16), b.astype(BF16), preferred_element_type=F32)


def _nt_dot(a, b):
    return lax.dot_general(a.astype(BF16), b.astype(BF16), (((1,), (1,)), ((), ())),
                           preferred_element_type=F32)


def _split(x):
    hi = x.astype(BF16)
    lo = (x - hi.astype(F32)).astype(BF16)
    return hi, lo


def _split_dot(x, m):
    hi, lo = _split(x)
    return (jnp.dot(hi, m, preferred_element_type=F32)
            + jnp.dot(lo, m, preferred_element_type=F32))


def _same_head(n, head_dim):
    shift = int(math.log2(head_dim))
    r = lax.broadcasted_iota(I32, (n, n), 0) >> shift
    c = lax.broadcasted_iota(I32, (n, n), 1) >> shift
    return jnp.where(r == c, 1.0, 0.0).astype(BF16)


def _seg_sum(x, head_dim):
    g = _same_head(LANES, head_dim)
    parts = [_split_dot(x[:, i:i + LANES], g) for i in range(0, x.shape[1], LANES)]
    return jnp.concatenate(parts, axis=1)


def _gelu(x):
    return 0.5 * x * (1.0 + jnp.tanh(0.7978845608028654 * (x + 0.044715 * x * x * x)))


def _sigmoid(x):
    return 1.0 / (1.0 + jnp.exp(-x))


def _top_mask(gate, idx, axis, n_valid):
    cur = gate
    sel = jnp.zeros(gate.shape, F32)
    for _ in range(MOBA_TOPK):
        m = jnp.max(cur, axis=axis, keepdims=True)
        first = jnp.min(jnp.where(cur == m, idx, n_valid), axis=axis, keepdims=True)
        pick = (idx == first) & (m > NEG_INF)
        sel = jnp.where(pick, 1.0, sel)
        cur = jnp.where(pick, NEG_INF, cur)
    return sel


def _gmlp_kernel(x_ref, ng_ref, win_ref, lng_ref, lnb_ref, sa_ref, sb_ref, wout_ref, y_ref, aux_ref,
                 *, sample):
    x = x_ref[...]
    tm, width = x.shape[0], wout_ref.shape[0]
    h = _rms(x, ng_ref[...])
    z = _gelu(_bdot(h, win_ref[...]))
    u, v = z[:, :width], z[:, width:]
    mu = jnp.mean(v, axis=-1, keepdims=True)
    d = v - mu
    var = jnp.mean(d * d, axis=-1, keepdims=True)
    v = d * lax.rsqrt(var + GM_LN_EPS) * lng_ref[...] + lnb_ref[...]
    if sample:
        aux_ref[...] = v
        s = v * sa_ref[...] + sb_ref[...]
    else:
        gd = width // GM_GROUPS
        causal = (lax.broadcasted_iota(I32, (GM_CHUNK, GM_CHUNK), 0)
                  >= lax.broadcasted_iota(I32, (GM_CHUNK, GM_CHUNK), 1))
        for g in range(GM_GROUPS):
            wg = jnp.where(causal, sa_ref[g], 0.0).astype(BF16)
            cols = slice(g * gd, (g + 1) * gd)
            for c in range(tm // GM_CHUNK):
                rows = slice(c * GM_CHUNK, (c + 1) * GM_CHUNK)
                aux_ref[rows, cols] = (jnp.dot(wg, v[rows, cols].astype(BF16), preferred_element_type=F32)
                                       + sb_ref[:, cols])
        s = aux_ref[...]
    y_ref[...] = x + _bdot(u * s, wout_ref[...])


def _gmlp(x, ng, win, lng, lnb, sa, sb, wout, *, tm, sample):
    n, d = x.shape
    width = wout.shape[0]
    outs = [jax.ShapeDtypeStruct((n, d), F32)]
    out_specs = [_rows(tm, d)]
    scratch = []
    if sample:
        outs.append(jax.ShapeDtypeStruct((n, width), F32))
        out_specs.append(_rows(tm, width))
    else:
        scratch.append(pltpu.VMEM((tm, width), F32))
    return pl.pallas_call(
        functools.partial(_gmlp_kernel, sample=sample),
        out_shape=outs, grid=(n // tm,),
        in_specs=[_rows(tm, d), _whole(ng.shape), _whole(win.shape), _whole(lng.shape), _whole(lnb.shape),
                  _whole(sa.shape), _whole(sb.shape), _whole(wout.shape)],
        out_specs=out_specs, scratch_shapes=scratch,
        compiler_params=_cparams("parallel"), name="gmlp_sample" if sample else "gmlp_prompt",
    )(x, ng, win, lng, lnb, sa, sb, wout)


def _norm_linear_kernel(x_ref, ng_ref, w_ref, *o_refs):
    z = _bdot(_rms(x_ref[...], ng_ref[...]), w_ref[...])
    wd = z.shape[1] // len(o_refs)
    for i, o_ref in enumerate(o_refs):
        o_ref[...] = z[:, i * wd:(i + 1) * wd]


def _norm_linear(x, ng, w, n_out, *, tm, name):
    n, d = x.shape
    wd = w.shape[1] // n_out
    return pl.pallas_call(
        _norm_linear_kernel,
        out_shape=[jax.ShapeDtypeStruct((n, wd), F32)] * n_out, grid=(n // tm,),
        in_specs=[_rows(tm, d), _whole(ng.shape), _whole(w.shape)],
        out_specs=[_rows(tm, wd)] * n_out,
        compiler_params=_cparams("parallel"), name=name,
    )(x, ng, w)


def _qkv_prompt_kernel(x_ref, ng_ref, wq_ref, wkt_ref, wvt_ref, q_ref, kt_ref, vt_ref):
    h = _rms(x_ref[...], ng_ref[...]).astype(BF16)
    q_ref[...] = jnp.dot(h, wq_ref[...], preferred_element_type=F32)
    kt_ref[0] = _nt_dot(wkt_ref[...], h)
    vt_ref[0] = _nt_dot(wvt_ref[...], h)


def _qkv_prompt(x, ng, wq, wkt, wvt, *, tm, batch, seq):
    n, d = x.shape
    tps = seq // tm
    t_spec = pl.BlockSpec((1, d, tm), lambda i: (i // tps, 0, i % tps))
    return pl.pallas_call(
        _qkv_prompt_kernel,
        out_shape=[jax.ShapeDtypeStruct((n, d), F32)] + [jax.ShapeDtypeStruct((batch, d, seq), F32)] * 2,
        grid=(n // tm,),
        in_specs=[_rows(tm, d), _whole(ng.shape), _whole(wq.shape), _whole(wkt.shape), _whole(wvt.shape)],
        out_specs=[_rows(tm, d), t_spec, t_spec],
        compiler_params=_cparams("parallel"), name="moba_qkv_prompt",
    )(x, ng, wq, wkt, wvt)


def _linear_res_kernel(a_ref, w_ref, y_ref, o_ref):
    o_ref[...] = y_ref[...] + _bdot(a_ref[...], w_ref[...])


def _linear_res(a, w, y, *, tm, name):
    n, d = y.shape
    return pl.pallas_call(
        _linear_res_kernel, out_shape=jax.ShapeDtypeStruct((n, d), F32), grid=(n // tm,),
        in_specs=[_rows(tm, a.shape[1]), _whole(w.shape), _rows(tm, d)], out_specs=_rows(tm, d),
        compiler_params=_cparams("parallel"), name=name,
    )(a, w, y)


def _t5_bucket_table(rel):
    n = jnp.maximum(rel, 0)
    exact = REL_BUCKETS // 2
    nf = jnp.maximum(n, 1).astype(F32)
    large = exact + (jnp.log(nf / exact) / math.log(REL_MAX_DIST / exact)
                     * (REL_BUCKETS - exact)).astype(I32)
    return jnp.where(n < exact, n, jnp.minimum(large, REL_BUCKETS - 1)).astype(I32)


def _bias_tiles_kernel(rb_ref, bkt_ref, o_ref):
    h = pl.program_id(0)
    bkt = bkt_ref[...]
    acc = jnp.zeros(bkt.shape, F32)
    for b in range(REL_BUCKETS):
        acc = jnp.where(bkt == b, rb_ref[b, h], acc)
    future = (lax.broadcasted_iota(I32, bkt.shape, 2) > lax.broadcasted_iota(I32, bkt.shape, 1))
    own_block = lax.broadcasted_iota(I32, bkt.shape, 0) == 0
    o_ref[0] = jnp.where(future & own_block, NEG_INF, acc)


def _bias_tiles(rel_bias, bkt):
    heads = rel_bias.shape[1]
    return pl.pallas_call(
        _bias_tiles_kernel, out_shape=jax.ShapeDtypeStruct((heads,) + bkt.shape, F32), grid=(heads,),
        in_specs=[pl.BlockSpec(memory_space=pltpu.SMEM), _whole(bkt.shape)],
        out_specs=pl.BlockSpec((1,) + bkt.shape, lambda h: (h, 0, 0, 0)),
        compiler_params=_cparams("parallel"), name="moba_bias_tiles",
    )(rel_bias, bkt)


def _bias_rows_kernel(rb_ref, bkt_ref, o_ref):
    bkt = bkt_ref[...]
    acc = jnp.zeros(bkt.shape, F32)
    for b in range(REL_BUCKETS):
        acc = jnp.where(bkt == b, rb_ref[b:b + 1, :], acc)
    o_ref[...] = acc


def _bias_rows(rb_pad, bkt):
    return pl.pallas_call(
        _bias_rows_kernel, out_shape=jax.ShapeDtypeStruct(bkt.shape, F32), grid=(1,),
        in_specs=[_whole(rb_pad.shape), _whole(bkt.shape)], out_specs=_whole(bkt.shape),
        compiler_params=_cparams("arbitrary"), name="moba_bias_rows",
    )(rb_pad, bkt)


def _moba_prompt_kernel(rb_ref, q_ref, k_ref, v_ref, bias_ref, o_ref, kb_scr, vh_scr, kmean_scr, *, head_dim):
    pair = pl.program_id(1)
    blk = MOBA_BLOCK
    seq = k_ref.shape[2]
    n_blocks = seq // blk
    assert LANES == 2 * head_dim
    shift = int(math.log2(head_dim))
    scale = head_dim ** -0.5
    assert shift % 2 == 0 and n_blocks <= kmean_scr.shape[0]

    def prepare():
        kt, vt = k_ref[0], v_ref[0]
        second = (lax.broadcasted_iota(I32, (LANES, seq), 0) >> shift) == 1
        kb_scr[...] = kt.astype(BF16)
        vh_scr[0] = jnp.where(second, 0.0, vt).astype(BF16)
        vh_scr[1] = jnp.where(second, vt, 0.0).astype(BF16)
        rows = kmean_scr.shape[0]
        in_block = jnp.where((lax.broadcasted_iota(I32, (rows, seq), 1) >> int(math.log2(blk)))
                             == lax.broadcasted_iota(I32, (rows, seq), 0), 1.0, 0.0).astype(BF16)
        k_hi, k_lo = _split(kt)
        nt = lambda a, b: lax.dot_general(a, b, (((1,), (1,)), ((), ())), preferred_element_type=F32)
        kmean_scr[...] = (nt(in_block, k_hi) + nt(in_block, k_lo)) * (1.0 / blk)

    prepare()
    second = (lax.broadcasted_iota(I32, (blk, LANES), 1) >> shift) == 1
    km_hi, km_lo = _split(kmean_scr[...])
    nt = lambda a, b: lax.dot_general(a, b, (((1,), (1,)), ((), ())), preferred_element_type=F32)
    block_id = lax.broadcasted_iota(I32, (kmean_scr.shape[0], 2 * blk), 0)
    far_bias = [rb_ref[REL_BUCKETS - 1, pair * 2 + hh] for hh in range(2)]

    def select(own):
        q = q_ref[own * blk:(own + 1) * blk, :]
        q2 = jnp.concatenate([jnp.where(second, 0.0, q), jnp.where(second, q, 0.0)], axis=0)
        q_hi, q_lo = _split(q2)
        gate = nt(km_hi, q_hi) + nt(km_lo, q_hi) + nt(km_hi, q_lo)
        sel = _top_mask(jnp.where(block_id < own, gate, NEG_INF), block_id, 0, gate.shape[0]).T
        return (q2 * scale).astype(BF16), sel

    def scores(own, q_scaled):
        return jnp.dot(q_scaled, kb_scr[:, 0:(own + 1) * blk], preferred_element_type=F32)

    def mask(own, s, sel):
        head_rows = []
        for hh in range(2):
            rows = slice(hh * blk, (hh + 1) * blk)
            tiles = []
            for n in range(own + 1):
                t = s[rows, n * blk:(n + 1) * blk]
                chosen = sel[rows, n:n + 1] > 0.0
                if n == own:
                    t = t + bias_ref[hh, 0]
                elif n == own - 1:
                    t = jnp.where(chosen, t + bias_ref[hh, 1], NEG_INF)
                else:
                    t = t + jnp.where(chosen, far_bias[hh], NEG_INF)
                tiles.append(t)
            head_rows.append(jnp.concatenate(tiles, axis=1))
        return jnp.concatenate(head_rows, axis=0)

    def weights(s):
        p = jnp.exp(s - jnp.max(s, axis=-1, keepdims=True))
        return p, jnp.sum(p, axis=-1, keepdims=True)

    def output(own, p, l):
        keys = (own + 1) * blk
        p_pair = jnp.concatenate([p[:blk], p[blk:]], axis=1).astype(BF16)
        v_pair = jnp.concatenate([vh_scr[0, :, 0:keys], vh_scr[1, :, 0:keys]], axis=1)
        o_ref[own * blk:(own + 1) * blk, :] = nt(p_pair, v_pair) / jnp.where(second, l[blk:], l[:blk])

    order = list(range(n_blocks))
    groups = [(order[i], order[-1 - i]) for i in range(n_blocks // 2)] + ([(order[n_blocks // 2],)] * (n_blocks % 2))
    groups = [sum(groups[i:i + 2], ()) for i in range(0, len(groups), 2)]
    for group in groups:
        picked = [select(own) for own in group]
        raw = [scores(own, qs) for own, (qs, _) in zip(group, picked)]
        masked = [mask(own, s, sel) for own, s, (_, sel) in zip(group, raw, picked)]
        probs = [weights(s) for s in masked]
        for own, (p, l) in zip(group, probs):
            output(own, p, l)


def _moba_prompt(rel_bias, q, k, v, bias_tiles, *, batch, seq, head_dim):
    n, d = q.shape
    blk = MOBA_BLOCK
    hpt = LANES // head_dim
    return pl.pallas_call(
        functools.partial(_moba_prompt_kernel, head_dim=head_dim),
        out_shape=jax.ShapeDtypeStruct((n, d), F32), grid=(batch, d // LANES),
        in_specs=[pl.BlockSpec(memory_space=pltpu.SMEM),
                  pl.BlockSpec((seq, LANES), lambda b, p: (b, p)),
                  pl.BlockSpec((1, LANES, seq), lambda b, p: (b, p, 0)),
                  pl.BlockSpec((1, LANES, seq), lambda b, p: (b, p, 0)),
                  pl.BlockSpec((hpt, 2, blk, blk), lambda b, p: (p, 0, 0, 0))],
        out_specs=pl.BlockSpec((seq, LANES), lambda b, p: (b, p)),
        scratch_shapes=[pltpu.VMEM((LANES, seq), BF16), pltpu.VMEM((hpt, LANES, seq), BF16),
                        pltpu.VMEM((2 * SUBLANES, LANES), F32)],
        compiler_params=_cparams("parallel", "parallel"), name="moba_prompt_attn",
    )(rel_bias, q, k, v, bias_tiles)


def _moba_sample_kernel(pt_ref, q_ref, kn_ref, vn_ref, *rest, n_pages, head_dim):
    del pt_ref
    k_refs, v_refs = rest[:n_pages], rest[n_pages:2 * n_pages]
    bias_ref, rb0_ref, o_ref = rest[2 * n_pages:]
    d = q_ref.shape[2]
    heads = d // head_dim
    shift = int(math.log2(head_dim))
    per_block = MOBA_BLOCK // PAGE_SIZE
    blocks = range(n_pages // per_block)
    scale = head_dim ** -0.5
    head_lanes = jnp.where((lax.broadcasted_iota(I32, (heads, d), 1) >> shift)
                           == lax.broadcasted_iota(I32, (heads, d), 0), 1.0, 0.0)
    q = head_lanes * q_ref[0]
    q_hi, q_lo = _split(q)
    q2 = jnp.concatenate([q_hi, q_lo], axis=0)
    block_of = lambda refs, n: jnp.concatenate(
        [refs[per_block * n + j][0, 0].reshape(d, PAGE_SIZE) for j in range(per_block)], axis=1).astype(BF16)
    raw = [jnp.dot(q2, block_of(k_refs, n), preferred_element_type=F32) for n in blocks]
    raw = [r[:heads] + r[heads:] for r in raw]
    block_id = lax.broadcasted_iota(I32, (heads, LANES), 1)
    gate = jnp.full((heads, LANES), NEG_INF, F32)
    for n in blocks:
        gate = jnp.where(block_id == n, jnp.sum(raw[n], axis=-1, keepdims=True), gate)
    sel = _top_mask(gate, block_id, -1, LANES)
    s_new = jnp.sum(q * kn_ref[0], axis=-1, keepdims=True) * scale + rb0_ref[:, 0:1]
    s = [raw[n] * scale + bias_ref[n] for n in blocks]
    m = s_new
    for n in blocks:
        m = jnp.maximum(m, jnp.where(sel[:, n:n + 1] > 0.0, jnp.max(s[n], axis=-1, keepdims=True), NEG_INF))
    p = [jnp.exp(jnp.where(sel[:, n:n + 1] > 0.0, s[n] - m, NEG_INF)) for n in blocks]
    p_new = jnp.exp(s_new - m)
    den = p_new
    for n in blocks:
        den = den + jnp.sum(p[n], axis=-1, keepdims=True)
    p_all = jnp.concatenate(p, axis=1).astype(BF16)
    v_all = jnp.concatenate([block_of(v_refs, n) for n in blocks], axis=1)
    acc = lax.dot_general(p_all, v_all, (((1,), (1,)), ((), ())), preferred_element_type=F32)
    out = (acc + p_new * vn_ref[0]) * head_lanes / den
    o_ref[0] = jnp.sum(out, axis=0, keepdims=True)


def _moba_sample(page_table, q, k_new, v_new, cache_k, cache_v, layer, bias_rows, rel_bias, *, head_dim):
    nb, d = q.shape
    n_pages = page_table.shape[1]
    n_blocks = n_pages * PAGE_SIZE // MOBA_BLOCK
    heads = d // head_dim
    cache_kt = jnp.transpose(cache_k, (0, 1, 3, 4, 2))
    cache_vt = jnp.transpose(cache_v, (0, 1, 3, 4, 2))
    row = lambda x: x.reshape(nb, 1, d)
    vec = pl.BlockSpec((1, 1, d), lambda s, pt: (s, 0, 0))
    page = lambda j: pl.BlockSpec((1, 1, heads, head_dim, PAGE_SIZE), lambda s, pt: (layer, pt[s, j], 0, 0, 0))
    const = lambda shape: pl.BlockSpec(shape, lambda s, pt: (0,) * len(shape))
    bias = bias_rows[:, :heads].reshape(n_blocks, MOBA_BLOCK, heads).transpose(0, 2, 1)
    rb0 = jnp.broadcast_to(rel_bias[0][:, None], (heads, LANES))
    pages = [page(j) for j in range(n_pages)]
    out = pl.pallas_call(
        functools.partial(_moba_sample_kernel, n_pages=n_pages, head_dim=head_dim),
        out_shape=jax.ShapeDtypeStruct((nb, 1, d), F32),
        grid_spec=pltpu.PrefetchScalarGridSpec(
            num_scalar_prefetch=1, grid=(nb,),
            in_specs=[vec, vec, vec, *pages, *pages, const(bias.shape), const(rb0.shape)],
            out_specs=vec),
        compiler_params=_cparams("parallel"), name="moba_sample_attn",
    )(page_table, row(q), row(k_new), row(v_new), *([cache_kt] * n_pages), *([cache_vt] * n_pages), bias, rb0)
    return out.reshape(nb, d)


POOL_CARRY = 16
assert all(b == 2 * a for a, b in zip((1,) + POOL_WINDOWS, POOL_WINDOWS)) and POOL_CARRY > POOL_CTX


def _pool_prompt_kernel(y_ref, ng_ref, w_ref, sc_ref, yout_ref, st_ref, carry_scr, *, tps):
    i = pl.program_id(0)
    y = y_ref[...]
    tm, d = y.shape
    gd = d // len(POOL_WINDOWS)
    h = _rms(y, ng_ref[...])

    @pl.when(i % tps == 0)
    def _():
        carry_scr[...] = jnp.zeros(carry_scr.shape, F32)

    rows = jnp.concatenate([carry_scr[...], h], axis=0)
    carry_scr[...] = h[tm - POOL_CARRY:, :]
    st_ref[0] = rows[tm + POOL_CARRY - POOL_CTX:, :]
    pos = (i % tps) * tm + lax.broadcasted_iota(I32, (tm, 1), 0) + 1
    sums = rows
    for g, win in enumerate(POOL_WINDOWS):
        cols = slice(g * gd, (g + 1) * gd)
        sums = sums[:, (gd if g else 0):]
        sums = sums + pltpu.roll(sums, win // 2, axis=0)
        cnt = jnp.minimum(pos, win).astype(F32)
        mixed = _bdot(sums[POOL_CARRY:, 0:gd] / cnt - h[:, cols], w_ref[g])
        yout_ref[:, cols] = y[:, cols] + mixed * sc_ref[:, cols]


def _pool_prompt(y, ng, w, sc, *, tm, seq):
    n, d = y.shape
    tps = seq // tm
    return pl.pallas_call(
        functools.partial(_pool_prompt_kernel, tps=tps),
        out_shape=[jax.ShapeDtypeStruct((n, d), F32), jax.ShapeDtypeStruct((n // seq, POOL_CTX, d), F32)],
        grid=(n // tm,),
        in_specs=[_rows(tm, d), _whole(ng.shape), _whole(w.shape), _whole(sc.shape)],
        out_specs=[_rows(tm, d), pl.BlockSpec((1, POOL_CTX, d), lambda i: (i // tps, 0, 0))],
        scratch_shapes=[pltpu.VMEM((POOL_CARRY, d), F32)],
        compiler_params=_cparams("arbitrary"), name="pool_prompt",
    )(y, ng, w, sc)


def _pool_sample_kernel(y_ref, ng_ref, prev_ref, w_ref, sc_ref, yout_ref, h_ref):
    y = y_ref[...]
    d = y.shape[1]
    gd = d // len(POOL_WINDOWS)
    h = _rms(y, ng_ref[...])
    h_ref[...] = h
    for g, win in enumerate(POOL_WINDOWS):
        cols = slice(g * gd, (g + 1) * gd)
        wsum = h[:, cols]
        for back in range(1, win):
            wsum = wsum + prev_ref[POOL_CTX - back, :, cols]
        mixed = _bdot(wsum / float(win) - h[:, cols], w_ref[g])
        yout_ref[:, cols] = y[:, cols] + mixed * sc_ref[:, cols]


def _pool_sample(y, ng, prev_t, w, sc):
    n, d = y.shape
    return pl.pallas_call(
        _pool_sample_kernel, out_shape=[jax.ShapeDtypeStruct((n, d), F32)] * 2, grid=(1,),
        in_specs=[_whole(y.shape), _whole(ng.shape), _whole(prev_t.shape), _whole(w.shape), _whole(sc.shape)],
        out_specs=[_whole(y.shape)] * 2,
        compiler_params=_cparams("arbitrary"), name="pool_sample",
    )(y, ng, prev_t, w, sc)


def _rwkv_proj_kernel(y_ref, ng_ref, mu_ref, wr_ref, wk_ref, wv_ref, w1_ref, w2_ref, a1_ref, a2_ref,
                      g1_ref, g2_ref, vec_ref, *rest, tps, sample):
    if sample:
        prev_ref, r_o, w_o, k_o, v_o, kk_o, kka_o, g_o, sh_o = rest
    else:
        r_o, w_o, k_o, v_o, kk_o, kka_o, g_o, sh_o, hs_scr = rest
    y = y_ref[...]
    tm = y.shape[0]
    h = _rms(y, ng_ref[...])
    if sample:
        h_prev = prev_ref[...]
        sh_o[...] = h
    else:
        i = pl.program_id(0)

        @pl.when(i % tps == 0)
        def _():
            hs_scr[...] = jnp.zeros(hs_scr.shape, F32)

        h_prev = pltpu.roll(jnp.concatenate([hs_scr[...], h], axis=0), 1, axis=0)[SUBLANES:]
        hs_scr[...] = h[tm - SUBLANES:, :]
        sh_o[0] = h[tm - 1:tm, :]
    xx = h_prev - h
    mix = lambda m: h + xx * mu_ref[m:m + 1, :]
    r = _bdot(mix(0), wr_ref[...])
    k = _bdot(mix(2), wk_ref[...])
    v = _bdot(mix(3), wv_ref[...])
    z = vec_ref[0:1, :] + _bdot(jnp.tanh(_bdot(mix(1), w1_ref[...])), w2_ref[...])
    w_log = jnp.minimum(z, 0.0) - jnp.log(1.0 + jnp.exp(-jnp.abs(z))) - 0.5
    a = _sigmoid(vec_ref[1:2, :] + _bdot(_bdot(mix(4), a1_ref[...]), a2_ref[...]))
    g_o[...] = _bdot(_sigmoid(_bdot(mix(5), g1_ref[...])), g2_ref[...])
    kk = k * vec_ref[2:3, :]
    kk = kk / jnp.maximum(jnp.sqrt(_seg_sum(kk * kk, RWKV_HEAD_DIM)), 1e-12)
    r_o[...] = r
    w_o[...] = -jnp.exp(w_log)
    k_o[...] = k * (1.0 + (a - 1.0) * vec_ref[3:4, :])
    v_o[...] = v
    kk_o[...] = kk
    kka_o[...] = kk * a


def _rwkv_proj(y, ng, mu, mats, vec, prev, *, tm, seq, sample):
    n, d = y.shape
    tps = max(seq // tm, 1)
    ins = [y, ng, mu, *mats, vec]
    in_specs = [_rows(tm, d), _whole(ng.shape), _whole(mu.shape), *[_whole(m.shape) for m in mats],
                _whole(vec.shape)]
    outs = [jax.ShapeDtypeStruct((n, d), F32)] * 7
    out_specs = [_rows(tm, d)] * 7
    scratch = []
    if sample:
        ins.append(prev)
        in_specs.append(_rows(tm, d))
        outs.append(jax.ShapeDtypeStruct((n, d), F32))
        out_specs.append(_rows(tm, d))
    else:
        outs.append(jax.ShapeDtypeStruct((n // seq, 1, d), F32))
        out_specs.append(pl.BlockSpec((1, 1, d), lambda i: (i // tps, 0, 0)))
        scratch.append(pltpu.VMEM((SUBLANES, d), F32))
    return pl.pallas_call(
        functools.partial(_rwkv_proj_kernel, tps=tps, sample=sample),
        out_shape=outs, grid=(n // tm,), in_specs=in_specs, out_specs=out_specs, scratch_shapes=scratch,
        compiler_params=_cparams("arbitrary"), name="rwkv_proj_sample" if sample else "rwkv_proj_prompt",
    )(*ins)


RWKV_CHUNK = 128


def _rwkv_chunk_kernel(r_ref, lw_ref, k_ref, v_ref, kk_ref, kka_ref, o_ref, st_ref, t_scr):
    c = pl.program_id(1)
    ch, d = r_ref.shape
    hd = RWKV_HEAD_DIM
    shift = int(math.log2(hd))
    n_pairs = d // LANES

    @pl.when(c == 0)
    def _():
        t_scr[...] = jnp.zeros(t_scr.shape, F32)

    ri = lax.broadcasted_iota(I32, (ch, ch), 0)
    ci = lax.broadcasted_iota(I32, (ch, ch), 1)
    incl = ci <= ri
    strict = ci < ri
    incl2 = jnp.concatenate([incl, incl], axis=1)
    strict2 = jnp.concatenate([strict, strict], axis=1)
    same_head = (ri >> shift) == (ci >> shift)
    eye = ri == ci
    head1 = (lax.broadcasted_iota(I32, (ch, LANES), 1) >> shift) == 1
    head1_wide = jnp.concatenate([head1, head1], axis=1)

    def by_head(x, mask):
        return jnp.concatenate([jnp.where(mask, 0.0, x), jnp.where(mask, x, 0.0)], axis=0).astype(BF16)

    lw = lw_ref[...]
    tri = jnp.where(incl, 1.0, 0.0).astype(BF16)
    p1 = lw.astype(BF16)
    rem = lw - p1.astype(F32)
    p2 = rem.astype(BF16)
    p3 = (rem - p2.astype(F32)).astype(BF16)
    cl = (jnp.dot(tri, p1, preferred_element_type=F32) + jnp.dot(tri, p2, preferred_element_type=F32)
          + jnp.dot(tri, p3, preferred_element_type=F32))
    mid = cl[ch // 2 - 1:ch // 2, :]
    last = cl[ch - 1:ch, :]
    e_mid = jnp.exp(mid)
    e_neg = jnp.exp(mid - cl)
    e_tail = jnp.exp(last - cl)
    p_last = jnp.exp(last)
    kk, kka, k = kk_ref[...], kka_ref[...], k_ref[...]
    a_s = -kk * jnp.exp(cl - lw - mid)
    r_s = r_ref[...] * jnp.exp(cl - mid)
    b_s = kka * e_neg
    k_s = k * e_neg
    b_t = kka * e_tail
    k_t = k * e_tail
    v = v_ref[...]
    zero = jnp.zeros((ch, ch), BF16)

    pairs = range(n_pairs)
    cols = [slice(p * LANES, (p + 1) * LANES) for p in pairs]
    mm = lambda a, b: jnp.dot(a, b, preferred_element_type=F32)
    lane_cat = lambda a, b: jnp.concatenate([a, b], axis=1)
    a_p = [a_s[:, c] for c in cols]
    r_p = [r_s[:, c] for c in cols]
    v_p = [v[:, c] for c in cols]
    gram = []
    for p in pairs:
        lhs = jnp.concatenate([jnp.where(head1, 0.0, a_p[p]), jnp.where(head1, 0.0, r_p[p]),
                               jnp.where(head1, a_p[p], 0.0), jnp.where(head1, r_p[p], 0.0)], axis=0)
        rhs = lane_cat(b_s[:, cols[p]].T, k_s[:, cols[p]].T)
        gram.append(mm(lhs.astype(BF16), rhs.astype(BF16)))
    ab0 = [jnp.where(strict2, g_[0:ch], 0.0) for g_ in gram]
    rb0 = [jnp.where(incl2, g_[ch:2 * ch], 0.0) for g_ in gram]
    ab1 = [jnp.where(strict2, g_[2 * ch:3 * ch], 0.0) for g_ in gram]
    rb1 = [jnp.where(incl2, g_[3 * ch:], 0.0) for g_ in gram]
    v_heads = [by_head(v_p[p], head1) for p in pairs]
    x = [lane_cat(a_p[p], mm(lane_cat(ab0[p][:, ch:], ab1[p][:, ch:]).astype(BF16), v_heads[p])) for p in pairs]
    pw = [lane_cat(ab0[p][:, :ch], ab1[p][:, :ch]).astype(BF16) for p in pairs]
    n_sq = int(math.log2(ch))
    for it in range(n_sq):
        x = [x[p] + mm(pw[p], by_head(x[p], head1_wide)) for p in pairs]
        if it < n_sq - 1:
            pw = [mm(pw[p], jnp.concatenate([lane_cat(pw[p][:, :ch], zero), lane_cat(zero, pw[p][:, ch:])],
                                            axis=0)).astype(BF16) for p in pairs]
    x = [lane_cat(x[p][:, :ch] * e_mid[:, cols[p]], x[p][:, ch:]) for p in pairs]
    qo = [mm(lane_cat(rb0[p][:, :ch], rb1[p][:, :ch]).astype(BF16), by_head(x[p], head1_wide)) for p in pairs]
    o_intra = [qo[p][:, ch:] + mm(lane_cat(rb0[p][:, ch:], rb1[p][:, ch:]).astype(BF16), v_heads[p])
               for p in pairs]
    q = [r_p[p] * e_mid[:, cols[p]] + qo[p][:, :ch] for p in pairs]
    gh = [mm(b_t[:, cols[p]].T.astype(BF16), x[p].astype(BF16)) for p in pairs]
    g = [jnp.where(same_head, gh[p][:, :ch], 0.0) + jnp.where(eye, p_last[:, cols[p]], 0.0) for p in pairs]
    h = [jnp.where(same_head, gh[p][:, ch:] + _bdot(k_t[:, cols[p]].T, v_p[p]), 0.0) for p in pairs]
    for p in pairs:
        t_old = t_scr[p].astype(BF16)
        o_ref[:, cols[p]] = mm(q[p].astype(BF16), t_old) + o_intra[p]
        t_scr[p] = mm(g[p].astype(BF16), t_old) + h[p]

    @pl.when(c == pl.num_programs(1) - 1)
    def _():
        for p in range(n_pairs):
            s_pair = t_scr[p].T
            st_ref[0, p] = s_pair[:hd, :] + s_pair[hd:, :]


def _rwkv_chunked(r, lw, k, v, kk, kka, *, batch, seq):
    n, d = r.shape
    ch = RWKV_CHUNK
    assert ch == LANES and seq % ch == 0
    n_chunks = seq // ch
    n_pairs = d // LANES
    blk = pl.BlockSpec((ch, d), lambda b, c: (b * n_chunks + c, 0))
    return pl.pallas_call(
        _rwkv_chunk_kernel,
        out_shape=[jax.ShapeDtypeStruct((n, d), F32),
                   jax.ShapeDtypeStruct((batch, n_pairs, RWKV_HEAD_DIM, LANES), F32)],
        grid=(batch, n_chunks), in_specs=[blk] * 6,
        out_specs=[blk, pl.BlockSpec((1, n_pairs, RWKV_HEAD_DIM, LANES), lambda b, c: (b, 0, 0, 0))],
        scratch_shapes=[pltpu.VMEM((n_pairs, LANES, LANES), F32)],
        compiler_params=_cparams("parallel", "arbitrary"), name="rwkv_chunked",
    )(r, lw, k, v, kk, kka)


def _rwkv_step_kernel(r_ref, lw_ref, k_ref, v_ref, kk_ref, kka_ref, s0_ref, o_ref, st_ref):
    hd = s0_ref.shape[1]
    w = jnp.exp(lw_ref[...])
    kk, kka, k, r = kk_ref[...], kka_ref[...], k_ref[...], r_ref[...]

    def value_row(i, carry):
        s = s0_ref[0, i]
        s_kk = jnp.sum(s * kk, axis=0, keepdims=True)
        s = s * w - s_kk * kka + v_ref[pl.ds(i, 1), :] * k
        st_ref[0, i] = s
        o_ref[pl.ds(i, 1), :] = jnp.sum(s * r, axis=0, keepdims=True)
        return carry

    lax.fori_loop(0, hd, value_row, 0, unroll=8)


def _rwkv_step(r, lw, k, v, kk, kka, s0):
    d, n = r.shape
    heads, hd = s0.shape[0], s0.shape[1]
    vec = pl.BlockSpec((hd, n), lambda h: (h, 0))
    st = pl.BlockSpec((1, hd, hd, n), lambda h: (h, 0, 0, 0))
    return pl.pallas_call(
        _rwkv_step_kernel,
        out_shape=[jax.ShapeDtypeStruct((d, n), F32), jax.ShapeDtypeStruct(s0.shape, F32)],
        grid=(heads,), in_specs=[vec] * 6 + [st], out_specs=[vec, st],
        compiler_params=_cparams("parallel"), name="rwkv_step",
    )(r, lw, k, v, kk, kka, s0)


def _rwkv_out_kernel(o_ref, r_ref, k_ref, v_ref, g_ref, y_ref, vec_ref, wo_ref, yout_ref):
    hd = RWKV_HEAD_DIM
    o = o_ref[...]
    mu = _seg_sum(o, hd) * (1.0 / hd)
    dlt = o - mu
    var = _seg_sum(dlt * dlt, hd) * (1.0 / hd)
    o = dlt * lax.rsqrt(var + RWKV_LNX_EPS) * vec_ref[1:2, :] + vec_ref[2:3, :]
    o = o + _seg_sum(r_ref[...] * k_ref[...] * vec_ref[0:1, :], hd) * v_ref[...]
    yout_ref[...] = y_ref[...] + _bdot(o * g_ref[...], wo_ref[...])


def _rwkv_out(o, r, k, v, g, y, vec, wo, *, tm, name):
    n, d = y.shape
    return pl.pallas_call(
        _rwkv_out_kernel, out_shape=jax.ShapeDtypeStruct((n, d), F32), grid=(n // tm,),
        in_specs=[_rows(tm, d)] * 6 + [_whole(vec.shape), _whole(wo.shape)], out_specs=_rows(tm, d),
        compiler_params=_cparams("parallel"), name=name,
    )(o, r, k, v, g, y, vec, wo)


FFN_CHUNK = 256


def _ffn_act(gate, g1, g2, up, cw, cb):
    return _gelu(cw[0:1] * g2 + cw[1:2] * g1 + cw[2:3] * gate + cb) * up


def _ffn_prompt_kernel(y_ref, ng_ref, win_ref, cw_ref, cb_ref, wout_ref, *rest, tps, final):
    if final:
        fg_ref, yout_ref, st_ref, yfin_ref, h_scr, gs_scr, act_scr, carry_scr = rest
    else:
        yout_ref, st_ref, h_scr, gs_scr, act_scr, carry_scr = rest
    i = pl.program_id(0)
    y = y_ref[...]
    tm = y.shape[0]
    hid = wout_ref.shape[0]
    tf = FFN_CHUNK
    h_scr[...] = _rms(y, ng_ref[...]).astype(BF16)

    @pl.when(i % tps == 0)
    def _():
        carry_scr[...] = jnp.zeros(carry_scr.shape, F32)

    for j in range(hid // tf):
        cols = slice(j * tf, (j + 1) * tf)
        hb = h_scr[...]
        gs_scr[0:SUBLANES, :] = carry_scr[:, cols]
        gs_scr[SUBLANES:, :] = jnp.dot(hb, win_ref[:, cols], preferred_element_type=F32)
        up = jnp.dot(hb, win_ref[:, hid + j * tf:hid + (j + 1) * tf], preferred_element_type=F32)
        gate = gs_scr[SUBLANES:, :]
        carry_scr[:, cols] = gate[tm - SUBLANES:, :]
        st_ref[0, :, cols] = gate[tm - (FFN_CONV - 1):, :]
        act = _ffn_act(gate, gs_scr[SUBLANES - 1:tm + SUBLANES - 1, :], gs_scr[SUBLANES - 2:tm + SUBLANES - 2, :],
                       up, cw_ref[:, cols], cb_ref[:, cols])
        act_scr[:, cols] = act.astype(BF16)
    y_new = y + jnp.dot(act_scr[...], wout_ref[...], preferred_element_type=F32)
    yout_ref[...] = y_new
    if final:
        yfin_ref[...] = _rms(y_new, fg_ref[...])


def _ffn_prompt(y, ng, win, cw, cb, wout, layer, fg, *, tm, seq):
    n, d = y.shape
    hid = wout.shape[1]
    tps = seq // tm
    final = fg is not None
    of_layer = lambda a: pl.BlockSpec((None,) + a.shape[1:], lambda i: (layer, 0, 0), pipeline_mode=pl.Buffered(1))
    ins = [y, ng, win, cw, cb, wout]
    in_specs = [_rows(tm, d), _whole(ng.shape), of_layer(win), _whole(cw.shape), _whole(cb.shape), of_layer(wout)]
    outs = [jax.ShapeDtypeStruct((n, d), F32), jax.ShapeDtypeStruct((n // seq, FFN_CONV - 1, hid), F32)]
    out_specs = [_rows(tm, d), pl.BlockSpec((1, FFN_CONV - 1, hid), lambda i: (i // tps, 0, 0))]
    if final:
        ins.append(fg)
        in_specs.append(_whole(fg.shape))
        outs.append(jax.ShapeDtypeStruct((n, d), F32))
        out_specs.append(_rows(tm, d))
    return pl.pallas_call(
        functools.partial(_ffn_prompt_kernel, tps=tps, final=final),
        out_shape=outs, grid=(n // tm,), in_specs=in_specs, out_specs=out_specs,
        scratch_shapes=[pltpu.VMEM((tm, d), BF16), pltpu.VMEM((tm + SUBLANES, FFN_CHUNK), F32),
                        pltpu.VMEM((tm, hid), BF16), pltpu.VMEM((SUBLANES, hid), F32)],
        compiler_params=_cparams("arbitrary"), name="ffn_prompt",
    )(*ins)


def _ffn_sample_kernel(y_ref, ng_ref, wg_ref, wu_ref, cw_ref, cb_ref, p2_ref, p1_ref, wout_ref, *rest, final):
    if final:
        fg_ref, yout_ref, gate_ref, yfin_ref, h_scr, acc_scr = rest
    else:
        yout_ref, gate_ref, h_scr, acc_scr = rest
    j = pl.program_id(0)

    @pl.when(j == 0)
    def _():
        h_scr[...] = _rms(y_ref[...], ng_ref[...]).astype(BF16)
        acc_scr[...] = jnp.zeros(acc_scr.shape, F32)

    hb = h_scr[...]
    gate = jnp.dot(hb, wg_ref[...], preferred_element_type=F32)
    up = jnp.dot(hb, wu_ref[...], preferred_element_type=F32)
    gate_ref[...] = gate
    act = _ffn_act(gate, p1_ref[...], p2_ref[...], up, cw_ref[...], cb_ref[...])
    acc_scr[...] += _bdot(act, wout_ref[...])

    @pl.when(j == pl.num_programs(0) - 1)
    def _():
        y_new = y_ref[...] + acc_scr[...]
        yout_ref[...] = y_new
        if final:
            yfin_ref[...] = _rms(y_new, fg_ref[...])


def _ffn_sample(y, ng, win, cw, cb, p2, p1, wout, layer, fg):
    n, d = y.shape
    hid = wout.shape[1]
    tf = hid // 2 if (hid // 2) % LANES == 0 else FFN_CHUNK
    nf = hid // tf
    final = fg is not None
    keep = lambda shape: pl.BlockSpec(shape, lambda j: (0,) * len(shape))
    chunk = lambda rows: pl.BlockSpec((rows, tf), lambda j: (0, j))
    ins = [y, ng, win, win, cw, cb, p2, p1, wout]
    in_specs = [keep((n, d)), keep(ng.shape), pl.BlockSpec((None, d, tf), lambda j: (layer, 0, j)),
                pl.BlockSpec((None, d, tf), lambda j: (layer, 0, nf + j)),
                chunk(FFN_CONV), chunk(1), chunk(n), chunk(n), pl.BlockSpec((None, tf, d), lambda j: (layer, j, 0))]
    outs = [jax.ShapeDtypeStruct((n, d), F32), jax.ShapeDtypeStruct((n, hid), F32)]
    out_specs = [keep((n, d)), chunk(n)]
    if final:
        ins.append(fg)
        in_specs.append(keep(fg.shape))
        outs.append(jax.ShapeDtypeStruct((n, d), F32))
        out_specs.append(keep((n, d)))
    return pl.pallas_call(
        functools.partial(_ffn_sample_kernel, final=final),
        out_shape=outs, grid=(nf,), in_specs=in_specs, out_specs=out_specs,
        scratch_shapes=[pltpu.VMEM((n, d), BF16), pltpu.VMEM((n, d), F32)],
        compiler_params=_cparams("arbitrary"), name="ffn_sample",
    )(*ins)


def kernel(x_prompt, x_sample, cache_moba_k, cache_moba_v, state_pool, state_rwkv_wkv, state_rwkv_shift, state_ffn_conv, page_table, norm_mix_g, norm_ffn_g, norm_final_g, rel_bias, gm_w_in, gm_ln_g, gm_ln_b, gm_w_s, gm_b_s, gm_w_out, moba_w_qkv, moba_w_o, pool_w, pool_scale, rwkv_mu, rwkv_w_r, rwkv_w_k, rwkv_w_v, rwkv_w_o, rwkv_w0, rwkv_w1, rwkv_w2, rwkv_a0, rwkv_a1, rwkv_a2, rwkv_g1, rwkv_g2, rwkv_k_k, rwkv_k_a, rwkv_r_k, rwkv_lnx_g, rwkv_lnx_b, ffn_w_in, ffn_conv_w, ffn_conv_b, ffn_w_out):
    bp, seq, d = x_prompt.shape
    bs = x_sample.shape[0]
    depth = norm_mix_g.shape[0]
    assert x_sample.shape[1] == 1 and depth == 4
    past_len = page_table.shape[1] * PAGE_SIZE
    assert seq % MOBA_BLOCK == 0 and past_len % MOBA_BLOCK == 0 and past_len % GM_CHUNK == 0
    row = lambda vct: vct.reshape(1, -1)
    bf = lambda m: m.astype(BF16)
    yp = x_prompt.reshape(bp * seq, d)
    ys = x_sample.reshape(bs, d)
    tm = TOKEN_TILE
    assert seq % tm == 0 and seq % RWKV_PROJ_TILE == 0 and seq % RWKV_CHUNK == 0
    conv_p, conv_s = [], []
    ffn_in, ffn_out = bf(ffn_w_in), bf(ffn_w_out)

    def ffn(i, yp, ys):
        last = i == depth - 1
        cw, cb, ng = ffn_conv_w[i], row(ffn_conv_b[i]), row(norm_ffn_g[i])
        fg = row(norm_final_g) if last else None
        res_p = _ffn_prompt(yp, ng, ffn_in, cw, cb, ffn_out, i, fg, tm=tm if last else 2 * tm, seq=seq)
        st = state_ffn_conv[i]
        res_s = _ffn_sample(ys, ng, ffn_in, cw, cb, st[:, 0], st[:, 1], ffn_out, i, fg)
        conv_p.append(res_p[1])
        conv_s.append(jnp.stack([st[:, 1], res_s[1]], axis=1))
        if last:
            return res_p[2], res_s[2]
        return res_p[0], res_s[0]

    ng = row(norm_mix_g[0])
    width = gm_w_out.shape[1]
    gd = width // GM_GROUPS
    gm_in, gm_out = bf(gm_w_in[0]), bf(gm_w_out[0])
    lng, lnb = row(gm_ln_g[0]), row(gm_ln_b[0])
    sb_prompt = jnp.repeat(gm_b_s[0].T, gd, axis=1)
    (yp,) = _gmlp(yp, ng, gm_in, lng, lnb, gm_w_s[0], sb_prompt, gm_out, tm=tm, sample=False)
    sa_first = row(jnp.repeat(gm_w_s[0][:, 0, 0], gd))
    sb_first = row(jnp.repeat(gm_b_s[0][:, 0], gd))
    ys, gm_v = _gmlp(ys, ng, gm_in, lng, lnb, sa_first, sb_first, gm_out, tm=bs, sample=True)
    gm_v_sample = gm_v.reshape(1, bs, 1, width)
    yp, ys = ffn(0, yp, ys)

    ng = row(norm_mix_g[1])
    w_qkv, w_o = bf(moba_w_qkv[0]), bf(moba_w_o[0])
    heads = MOBA_HEADS
    hd = d // heads
    qp, kp_t, vp_t = _qkv_prompt(yp, ng, w_qkv[:, :d], w_qkv[:, d:2 * d].T, w_qkv[:, 2 * d:].T,
                                 tm=2 * tm, batch=bp, seq=seq)
    qs, ks, vs = _norm_linear(ys, ng, w_qkv, 3, tm=bs, name="moba_qkv_sample")
    blk = MOBA_BLOCK
    qi = jnp.arange(blk, dtype=I32)[:, None]
    ki = jnp.arange(blk, dtype=I32)[None, :]
    bkt_tiles = _t5_bucket_table(jnp.stack([qi - ki, blk + qi - ki]))
    bias_tiles = _bias_tiles(rel_bias, bkt_tiles)
    op = _moba_prompt(rel_bias, qp, kp_t, vp_t, bias_tiles, batch=bp, seq=seq, head_dim=hd)
    bkt_rows = _t5_bucket_table(past_len - jnp.arange(past_len, dtype=I32))
    rb_pad = jnp.pad(rel_bias, ((0, 0), (0, LANES - heads)))
    bias_rows = _bias_rows(rb_pad, jnp.broadcast_to(bkt_rows[:, None], (past_len, LANES)))
    os_ = _moba_sample(page_table, qs, ks, vs, cache_moba_k, cache_moba_v, 0, bias_rows, rel_bias, head_dim=hd)
    yp = _linear_res(op, w_o, yp, tm=tm, name="moba_out_prompt")
    ys = _linear_res(os_, w_o, ys, tm=bs, name="moba_out_sample")
    moba_k_prompt = kp_t.reshape(1, bp, heads, hd, seq).transpose(0, 1, 4, 2, 3)
    moba_v_prompt = vp_t.reshape(1, bp, heads, hd, seq).transpose(0, 1, 4, 2, 3)
    moba_k_sample = ks.reshape(1, bs, 1, heads, hd)
    moba_v_sample = vs.reshape(1, bs, 1, heads, hd)
    yp, ys = ffn(1, yp, ys)

    ng = row(norm_mix_g[2])
    pw, psc = bf(pool_w[0]), row(pool_scale[0])
    yp, pool_p = _pool_prompt(yp, ng, pw, psc, tm=tm, seq=seq)
    ys, hs = _pool_sample(ys, ng, jnp.swapaxes(state_pool[0], 0, 1), pw, psc)
    pool_prompt = pool_p[None]
    pool_sample = jnp.concatenate([state_pool[0][:, 1:], hs[:, None]], axis=1)[None]
    yp, ys = ffn(2, yp, ys)

    ng = row(norm_mix_g[3])
    mats = [bf(m[0]) for m in (rwkv_w_r, rwkv_w_k, rwkv_w_v, rwkv_w1, rwkv_w2, rwkv_a1, rwkv_a2,
                               rwkv_g1, rwkv_g2)]
    vec_in = jnp.stack([rwkv_w0[0], rwkv_a0[0], rwkv_k_k[0], rwkv_k_a[0]])
    vec_out = jnp.stack([rwkv_r_k[0].reshape(-1), rwkv_lnx_g[0], rwkv_lnx_b[0]])
    rh = d // RWKV_HEAD_DIM
    *seqs_p, gp, shp = _rwkv_proj(yp, ng, rwkv_mu[0], mats, vec_in, None, tm=RWKV_PROJ_TILE, seq=seq,
                                  sample=False)
    *seqs_s, gs, shs = _rwkv_proj(ys, ng, rwkv_mu[0], mats, vec_in, state_rwkv_shift[0], tm=bs, seq=1,
                                  sample=True)
    o_p, st_p = _rwkv_chunked(*seqs_p, batch=bp, seq=seq)
    hd_r = RWKV_HEAD_DIM
    wkv_p = st_p.reshape(bp, rh // 2, hd_r, 2, hd_r).transpose(0, 1, 3, 2, 4).reshape(bp, rh, hd_r, hd_r)
    o_s_t, wkv_s_t = _rwkv_step(*[a.T for a in seqs_s], jnp.transpose(state_rwkv_wkv[0], (1, 2, 3, 0)))
    o_s, wkv_s = o_s_t.T, jnp.transpose(wkv_s_t, (3, 0, 1, 2))
    w_o = bf(rwkv_w_o[0])
    r_p, _, k_p, v_p = seqs_p[:4]
    r_s, _, k_s, v_s = seqs_s[:4]
    yp = _rwkv_out(o_p, r_p, k_p, v_p, gp, yp, vec_out, w_o, tm=tm, name="rwkv_out_prompt")
    ys = _rwkv_out(o_s.reshape(bs, d), r_s, k_s, v_s, gs, ys, vec_out, w_o, tm=bs, name="rwkv_out_sample")
    yp, ys = ffn(3, yp, ys)

    return (yp.reshape(bp, seq, d), ys.reshape(bs, 1, d), gm_v_sample, moba_k_prompt, moba_v_prompt,
            moba_k_sample, moba_v_sample, pool_prompt, pool_sample, wkv_p[None], wkv_s[None],
            shp.reshape(1, bp, d), shs[None], jnp.stack(conv_p), jnp.stack(conv_s))
```

```python
import functools
import math

import jax
import jax.numpy as jnp
from jax import lax
from jax.experimental import pallas as pl
from jax.experimental.pallas import tpu as pltpu

F32 = jnp.float32
BF16 = jnp.bfloat16
I32 = jnp.int32

LANES = 128
SUBLANES = 8
VMEM_LIMIT_BYTES = 56 * 2**20

TOKEN_TILE = 512
RWKV_PROJ_TILE = 512

RMS_EPS = 1e-6
GM_LN_EPS = 1e-5
GM_CHUNK = 128
GM_GROUPS = 8
MOBA_HEADS = 16
MOBA_BLOCK = 256
MOBA_TOPK = 3
REL_BUCKETS = 32
REL_MAX_DIST = 128
PAGE_SIZE = 128
POOL_WINDOWS = (2, 4, 8, 16)
POOL_CTX = max(POOL_WINDOWS) - 1
RWKV_HEAD_DIM = 64
RWKV_LNX_EPS = 64e-5
FFN_CONV = 3
NEG_INF = float("-inf")

assert MOBA_BLOCK >= REL_MAX_DIST


def _cparams(*sem):
    return pltpu.CompilerParams(dimension_semantics=sem, vmem_limit_bytes=VMEM_LIMIT_BYTES)


def _whole(shape):
    nd = len(shape)
    return pl.BlockSpec(shape, lambda *_: (0,) * nd, pipeline_mode=pl.Buffered(1))


def _rows(tm, width):
    return pl.BlockSpec((tm, width), lambda i: (i, 0))


def _rms(x, g):
    return x * lax.rsqrt(jnp.mean(x * x, axis=-1, keepdims=True) + RMS_EPS) * g


def _bdot(a, b):
    return jnp.dot(a.astype(BF16), b.astype(BF16), preferred_element_type=F32)


def _nt_dot(a, b):
    return lax.dot_general(a.astype(BF16), b.astype(BF16), (((1,), (1,)), ((), ())),
                           preferred_element_type=F32)


def _split(x):
    hi = x.astype(BF16)
    lo = (x - hi.astype(F32)).astype(BF16)
    return hi, lo


def _split_dot(x, m):
    hi, lo = _split(x)
    return (jnp.dot(hi, m, preferred_element_type=F32)
            + jnp.dot(lo, m, preferred_element_type=F32))


def _same_head(n, head_dim):
    shift = int(math.log2(head_dim))
    r = lax.broadcasted_iota(I32, (n, n), 0) >> shift
    c = lax.broadcasted_iota(I32, (n, n), 1) >> shift
    return jnp.where(r == c, 1.0, 0.0).astype(BF16)


def _seg_sum(x, head_dim):
    g = _same_head(LANES, head_dim)
    parts = [_split_dot(x[:, i:i + LANES], g) for i in range(0, x.shape[1], LANES)]
    return jnp.concatenate(parts, axis=1)


def _gelu(x):
    return 0.5 * x * (1.0 + jnp.tanh(0.7978845608028654 * (x + 0.044715 * x * x * x)))


def _sigmoid(x):
    return 1.0 / (1.0 + jnp.exp(-x))


def _top_mask(gate, idx, axis, n_valid):
    cur = gate
    sel = jnp.zeros(gate.shape, F32)
    for _ in range(MOBA_TOPK):
        m = jnp.max(cur, axis=axis, keepdims=True)
        first = jnp.min(jnp.where(cur == m, idx, n_valid), axis=axis, keepdims=True)
        pick = (idx == first) & (m > NEG_INF)
        sel = jnp.where(pick, 1.0, sel)
        cur = jnp.where(pick, NEG_INF, cur)
    return sel


def _gmlp_kernel(x_ref, ng_ref, win_ref, lng_ref, lnb_ref, sa_ref, sb_ref, wout_ref, y_ref, aux_ref,
                 *, sample):
    x = x_ref[...]
    tm, width = x.shape[0], wout_ref.shape[0]
    h = _rms(x, ng_ref[...])
    z = _gelu(_bdot(h, win_ref[...]))
    u, v = z[:, :width], z[:, width:]
    mu = jnp.mean(v, axis=-1, keepdims=True)
    d = v - mu
    var = jnp.mean(d * d, axis=-1, keepdims=True)
    v = d * lax.rsqrt(var + GM_LN_EPS) * lng_ref[...] + lnb_ref[...]
    if sample:
        aux_ref[...] = v
        s = v * sa_ref[...] + sb_ref[...]
    else:
        gd = width // GM_GROUPS
        causal = (lax.broadcasted_iota(I32, (GM_CHUNK, GM_CHUNK), 0)
                  >= lax.broadcasted_iota(I32, (GM_CHUNK, GM_CHUNK), 1))
        for g in range(GM_GROUPS):
            wg = jnp.where(causal, sa_ref[g], 0.0).astype(BF16)
            cols = slice(g * gd, (g + 1) * gd)
            for c in range(tm // GM_CHUNK):
                rows = slice(c * GM_CHUNK, (c + 1) * GM_CHUNK)
                aux_ref[rows, cols] = (jnp.dot(wg, v[rows, cols].astype(BF16), preferred_element_type=F32)
                                       + sb_ref[:, cols])
        s = aux_ref[...]
    y_ref[...] = x + _bdot(u * s, wout_ref[...])


def _gmlp(x, ng, win, lng, lnb, sa, sb, wout, *, tm, sample):
    n, d = x.shape
    width = wout.shape[0]
    outs = [jax.ShapeDtypeStruct((n, d), F32)]
    out_specs = [_rows(tm, d)]
    scratch = []
    if sample:
        outs.append(jax.ShapeDtypeStruct((n, width), F32))
        out_specs.append(_rows(tm, width))
    else:
        scratch.append(pltpu.VMEM((tm, width), F32))
    return pl.pallas_call(
        functools.partial(_gmlp_kernel, sample=sample),
        out_shape=outs, grid=(n // tm,),
        in_specs=[_rows(tm, d), _whole(ng.shape), _whole(win.shape), _whole(lng.shape), _whole(lnb.shape),
                  _whole(sa.shape), _whole(sb.shape), _whole(wout.shape)],
        out_specs=out_specs, scratch_shapes=scratch,
        compiler_params=_cparams("parallel"), name="gmlp_sample" if sample else "gmlp_prompt",
    )(x, ng, win, lng, lnb, sa, sb, wout)


def _norm_linear_kernel(x_ref, ng_ref, w_ref, *o_refs):
    z = _bdot(_rms(x_ref[...], ng_ref[...]), w_ref[...])
    wd = z.shape[1] // len(o_refs)
    for i, o_ref in enumerate(o_refs):
        o_ref[...] = z[:, i * wd:(i + 1) * wd]


def _norm_linear(x, ng, w, n_out, *, tm, name):
    n, d = x.shape
    wd = w.shape[1] // n_out
    return pl.pallas_call(
        _norm_linear_kernel,
        out_shape=[jax.ShapeDtypeStruct((n, wd), F32)] * n_out, grid=(n // tm,),
        in_specs=[_rows(tm, d), _whole(ng.shape), _whole(w.shape)],
        out_specs=[_rows(tm, wd)] * n_out,
        compiler_params=_cparams("parallel"), name=name,
    )(x, ng, w)


def _qkv_prompt_kernel(x_ref, ng_ref, wq_ref, wkt_ref, wvt_ref, q_ref, kt_ref, vt_ref):
    h = _rms(x_ref[...], ng_ref[...]).astype(BF16)
    q_ref[...] = jnp.dot(h, wq_ref[...], preferred_element_type=F32)
    kt_ref[0] = _nt_dot(wkt_ref[...], h)
    vt_ref[0] = _nt_dot(wvt_ref[...], h)


def _qkv_prompt(x, ng, wq, wkt, wvt, *, tm, batch, seq):
    n, d = x.shape
    tps = seq // tm
    t_spec = pl.BlockSpec((1, d, tm), lambda i: (i // tps, 0, i % tps))
    return pl.pallas_call(
        _qkv_prompt_kernel,
        out_shape=[jax.ShapeDtypeStruct((n, d), F32)] + [jax.ShapeDtypeStruct((batch, d, seq), F32)] * 2,
        grid=(n // tm,),
        in_specs=[_rows(tm, d), _whole(ng.shape), _whole(wq.shape), _whole(wkt.shape), _whole(wvt.shape)],
        out_specs=[_rows(tm, d), t_spec, t_spec],
        compiler_params=_cparams("parallel"), name="moba_qkv_prompt",
    )(x, ng, wq, wkt, wvt)


def _linear_res_kernel(a_ref, w_ref, y_ref, o_ref):
    o_ref[...] = y_ref[...] + _bdot(a_ref[...], w_ref[...])


def _linear_res(a, w, y, *, tm, name):
    n, d = y.shape
    return pl.pallas_call(
        _linear_res_kernel, out_shape=jax.ShapeDtypeStruct((n, d), F32), grid=(n // tm,),
        in_specs=[_rows(tm, a.shape[1]), _whole(w.shape), _rows(tm, d)], out_specs=_rows(tm, d),
        compiler_params=_cparams("parallel"), name=name,
    )(a, w, y)


def _t5_bucket_table(rel):
    n = jnp.maximum(rel, 0)
    exact = REL_BUCKETS // 2
    nf = jnp.maximum(n, 1).astype(F32)
    large = exact + (jnp.log(nf / exact) / math.log(REL_MAX_DIST / exact)
                     * (REL_BUCKETS - exact)).astype(I32)
    return jnp.where(n < exact, n, jnp.minimum(large, REL_BUCKETS - 1)).astype(I32)


def _bias_tiles_kernel(rb_ref, bkt_ref, o_ref):
    h = pl.program_id(0)
    bkt = bkt_ref[...]
    acc = jnp.zeros(bkt.shape, F32)
    for b in range(REL_BUCKETS):
        acc = jnp.where(bkt == b, rb_ref[b, h], acc)
    future = (lax.broadcasted_iota(I32, bkt.shape, 2) > lax.broadcasted_iota(I32, bkt.shape, 1))
    own_block = lax.broadcasted_iota(I32, bkt.shape, 0) == 0
    o_ref[0] = jnp.where(future & own_block, NEG_INF, acc)


def _bias_tiles(rel_bias, bkt):
    heads = rel_bias.shape[1]
    return pl.pallas_call(
        _bias_tiles_kernel, out_shape=jax.ShapeDtypeStruct((heads,) + bkt.shape, F32), grid=(heads,),
        in_specs=[pl.BlockSpec(memory_space=pltpu.SMEM), _whole(bkt.shape)],
        out_specs=pl.BlockSpec((1,) + bkt.shape, lambda h: (h, 0, 0, 0)),
        compiler_params=_cparams("parallel"), name="moba_bias_tiles",
    )(rel_bias, bkt)


def _bias_rows_kernel(rb_ref, bkt_ref, o_ref):
    bkt = bkt_ref[...]
    acc = jnp.zeros(bkt.shape, F32)
    for b in range(REL_BUCKETS):
        acc = jnp.where(bkt == b, rb_ref[b:b + 1, :], acc)
    o_ref[...] = acc


def _bias_rows(rb_pad, bkt):
    return pl.pallas_call(
        _bias_rows_kernel, out_shape=jax.ShapeDtypeStruct(bkt.shape, F32), grid=(1,),
        in_specs=[_whole(rb_pad.shape), _whole(bkt.shape)], out_specs=_whole(bkt.shape),
        compiler_params=_cparams("arbitrary"), name="moba_bias_rows",
    )(rb_pad, bkt)


def _moba_prompt_kernel(rb_ref, q_ref, k_ref, v_ref, bias_ref, o_ref, kb_scr, vh_scr, kmean_scr, *, head_dim):
    pair = pl.program_id(1)
    blk = MOBA_BLOCK
    seq = k_ref.shape[2]
    n_blocks = seq // blk
    assert LANES == 2 * head_dim
    shift = int(math.log2(head_dim))
    scale = head_dim ** -0.5
    assert shift % 2 == 0 and n_blocks <= kmean_scr.shape[0]

    def prepare():
        kt, vt = k_ref[0], v_ref[0]
        second = (lax.broadcasted_iota(I32, (LANES, seq), 0) >> shift) == 1
        kb_scr[...] = kt.astype(BF16)
        vh_scr[0] = jnp.where(second, 0.0, vt).astype(BF16)
        vh_scr[1] = jnp.where(second, vt, 0.0).astype(BF16)
        rows = kmean_scr.shape[0]
        in_block = jnp.where((lax.broadcasted_iota(I32, (rows, seq), 1) >> int(math.log2(blk)))
                             == lax.broadcasted_iota(I32, (rows, seq), 0), 1.0, 0.0).astype(BF16)
        k_hi, k_lo = _split(kt)
        nt = lambda a, b: lax.dot_general(a, b, (((1,), (1,)), ((), ())), preferred_element_type=F32)
        kmean_scr[...] = (nt(in_block, k_hi) + nt(in_block, k_lo)) * (1.0 / blk)

    prepare()
    second = (lax.broadcasted_iota(I32, (blk, LANES), 1) >> shift) == 1
    km_hi, km_lo = _split(kmean_scr[...])
    nt = lambda a, b: lax.dot_general(a, b, (((1,), (1,)), ((), ())), preferred_element_type=F32)
    block_id = lax.broadcasted_iota(I32, (kmean_scr.shape[0], 2 * blk), 0)
    far_bias = [rb_ref[REL_BUCKETS - 1, pair * 2 + hh] for hh in range(2)]

    def select(own):
        q = q_ref[own * blk:(own + 1) * blk, :]
        q2 = jnp.concatenate([jnp.where(second, 0.0, q), jnp.where(second, q, 0.0)], axis=0)
        q_hi, q_lo = _split(q2)
        gate = nt(km_hi, q_hi) + nt(km_lo, q_hi) + nt(km_hi, q_lo)
        sel = _top_mask(jnp.where(block_id < own, gate, NEG_INF), block_id, 0, gate.shape[0]).T
        return (q2 * scale).astype(BF16), sel

    def scores(own, q_scaled):
        return jnp.dot(q_scaled, kb_scr[:, 0:(own + 1) * blk], preferred_element_type=F32)

    def mask(own, s, sel):
        head_rows = []
        for hh in range(2):
            rows = slice(hh * blk, (hh + 1) * blk)
            tiles = []
            for n in range(own + 1):
                t = s[rows, n * blk:(n + 1) * blk]
                chosen = sel[rows, n:n + 1] > 0.0
                if n == own:
                    t = t + bias_ref[hh, 0]
                elif n == own - 1:
                    t = jnp.where(chosen, t + bias_ref[hh, 1], NEG_INF)
                else:
                    t = t + jnp.where(chosen, far_bias[hh], NEG_INF)
                tiles.append(t)
            head_rows.append(jnp.concatenate(tiles, axis=1))
        return jnp.concatenate(head_rows, axis=0)

    def weights(s):
        p = jnp.exp(s - jnp.max(s, axis=-1, keepdims=True))
        return p, jnp.sum(p, axis=-1, keepdims=True)

    def output(own, p, l):
        keys = (own + 1) * blk
        p_pair = jnp.concatenate([p[:blk], p[blk:]], axis=1).astype(BF16)
        v_pair = jnp.concatenate([vh_scr[0, :, 0:keys], vh_scr[1, :, 0:keys]], axis=1)
        o_ref[own * blk:(own + 1) * blk, :] = nt(p_pair, v_pair) / jnp.where(second, l[blk:], l[:blk])

    order = list(range(n_blocks))
    groups = [(order[i], order[-1 - i]) for i in range(n_blocks // 2)] + ([(order[n_blocks // 2],)] * (n_blocks % 2))
    groups = [sum(groups[i:i + 2], ()) for i in range(0, len(groups), 2)]
    for group in groups:
        picked = [select(own) for own in group]
        raw = [scores(own, qs) for own, (qs, _) in zip(group, picked)]
        masked = [mask(own, s, sel) for own, s, (_, sel) in zip(group, raw, picked)]
        probs = [weights(s) for s in masked]
        for own, (p, l) in zip(group, probs):
            output(own, p, l)


def _moba_prompt(rel_bias, q, k, v, bias_tiles, *, batch, seq, head_dim):
    n, d = q.shape
    blk = MOBA_BLOCK
    hpt = LANES // head_dim
    return pl.pallas_call(
        functools.partial(_moba_prompt_kernel, head_dim=head_dim),
        out_shape=jax.ShapeDtypeStruct((n, d), F32), grid=(batch, d // LANES),
        in_specs=[pl.BlockSpec(memory_space=pltpu.SMEM),
                  pl.BlockSpec((seq, LANES), lambda b, p: (b, p)),
                  pl.BlockSpec((1, LANES, seq), lambda b, p: (b, p, 0)),
                  pl.BlockSpec((1, LANES, seq), lambda b, p: (b, p, 0)),
                  pl.BlockSpec((hpt, 2, blk, blk), lambda b, p: (p, 0, 0, 0))],
        out_specs=pl.BlockSpec((seq, LANES), lambda b, p: (b, p)),
        scratch_shapes=[pltpu.VMEM((LANES, seq), BF16), pltpu.VMEM((hpt, LANES, seq), BF16),
                        pltpu.VMEM((2 * SUBLANES, LANES), F32)],
        compiler_params=_cparams("parallel", "parallel"), name="moba_prompt_attn",
    )(rel_bias, q, k, v, bias_tiles)


def _moba_sample_kernel(pt_ref, q_ref, kn_ref, vn_ref, *rest, n_pages, head_dim):
    del pt_ref
    k_refs, v_refs = rest[:n_pages], rest[n_pages:2 * n_pages]
    bias_ref, rb0_ref, o_ref = rest[2 * n_pages:]
    d = q_ref.shape[2]
    heads = d // head_dim
    shift = int(math.log2(head_dim))
    per_block = MOBA_BLOCK // PAGE_SIZE
    blocks = range(n_pages // per_block)
    scale = head_dim ** -0.5
    head_lanes = jnp.where((lax.broadcasted_iota(I32, (heads, d), 1) >> shift)
                           == lax.broadcasted_iota(I32, (heads, d), 0), 1.0, 0.0)
    q = head_lanes * q_ref[0]
    q_hi, q_lo = _split(q)
    q2 = jnp.concatenate([q_hi, q_lo], axis=0)
    block_of = lambda refs, n: jnp.concatenate(
        [refs[per_block * n + j][0, 0].reshape(d, PAGE_SIZE) for j in range(per_block)], axis=1).astype(BF16)
    raw = [jnp.dot(q2, block_of(k_refs, n), preferred_element_type=F32) for n in blocks]
    raw = [r[:heads] + r[heads:] for r in raw]
    block_id = lax.broadcasted_iota(I32, (heads, LANES), 1)
    gate = jnp.full((heads, LANES), NEG_INF, F32)
    for n in blocks:
        gate = jnp.where(block_id == n, jnp.sum(raw[n], axis=-1, keepdims=True), gate)
    sel = _top_mask(gate, block_id, -1, LANES)
    s_new = jnp.sum(q * kn_ref[0], axis=-1, keepdims=True) * scale + rb0_ref[:, 0:1]
    s = [raw[n] * scale + bias_ref[n] for n in blocks]
    m = s_new
    for n in blocks:
        m = jnp.maximum(m, jnp.where(sel[:, n:n + 1] > 0.0, jnp.max(s[n], axis=-1, keepdims=True), NEG_INF))
    p = [jnp.exp(jnp.where(sel[:, n:n + 1] > 0.0, s[n] - m, NEG_INF)) for n in blocks]
    p_new = jnp.exp(s_new - m)
    den = p_new
    for n in blocks:
        den = den + jnp.sum(p[n], axis=-1, keepdims=True)
    p_all = jnp.concatenate(p, axis=1).astype(BF16)
    v_all = jnp.concatenate([block_of(v_refs, n) for n in blocks], axis=1)
    acc = lax.dot_general(p_all, v_all, (((1,), (1,)), ((), ())), preferred_element_type=F32)
    out = (acc + p_new * vn_ref[0]) * head_lanes / den
    o_ref[0] = jnp.sum(out, axis=0, keepdims=True)


def _moba_sample(page_table, q, k_new, v_new, cache_k, cache_v, layer, bias_rows, rel_bias, *, head_dim):
    nb, d = q.shape
    n_pages = page_table.shape[1]
    n_blocks = n_pages * PAGE_SIZE // MOBA_BLOCK
    heads = d // head_dim
    cache_kt = jnp.transpose(cache_k, (0, 1, 3, 4, 2))
    cache_vt = jnp.transpose(cache_v, (0, 1, 3, 4, 2))
    row = lambda x: x.reshape(nb, 1, d)
    vec = pl.BlockSpec((1, 1, d), lambda s, pt: (s, 0, 0))
    page = lambda j: pl.BlockSpec((1, 1, heads, head_dim, PAGE_SIZE), lambda s, pt: (layer, pt[s, j], 0, 0, 0))
    const = lambda shape: pl.BlockSpec(shape, lambda s, pt: (0,) * len(shape))
    bias = bias_rows[:, :heads].reshape(n_blocks, MOBA_BLOCK, heads).transpose(0, 2, 1)
    rb0 = jnp.broadcast_to(rel_bias[0][:, None], (heads, LANES))
    pages = [page(j) for j in range(n_pages)]
    out = pl.pallas_call(
        functools.partial(_moba_sample_kernel, n_pages=n_pages, head_dim=head_dim),
        out_shape=jax.ShapeDtypeStruct((nb, 1, d), F32),
        grid_spec=pltpu.PrefetchScalarGridSpec(
            num_scalar_prefetch=1, grid=(nb,),
            in_specs=[vec, vec, vec, *pages, *pages, const(bias.shape), const(rb0.shape)],
            out_specs=vec),
        compiler_params=_cparams("parallel"), name="moba_sample_attn",
    )(page_table, row(q), row(k_new), row(v_new), *([cache_kt] * n_pages), *([cache_vt] * n_pages), bias, rb0)
    return out.reshape(nb, d)


POOL_CARRY = 16
assert all(b == 2 * a for a, b in zip((1,) + POOL_WINDOWS, POOL_WINDOWS)) and POOL_CARRY > POOL_CTX


def _pool_prompt_kernel(y_ref, ng_ref, w_ref, sc_ref, yout_ref, st_ref, carry_scr, *, tps):
    i = pl.program_id(0)
    y = y_ref[...]
    tm, d = y.shape
    gd = d // len(POOL_WINDOWS)
    h = _rms(y, ng_ref[...])

    @pl.when(i % tps == 0)
    def _():
        carry_scr[...] = jnp.zeros(carry_scr.shape, F32)

    rows = jnp.concatenate([carry_scr[...], h], axis=0)
    carry_scr[...] = h[tm - POOL_CARRY:, :]
    st_ref[0] = rows[tm + POOL_CARRY - POOL_CTX:, :]
    pos = (i % tps) * tm + lax.broadcasted_iota(I32, (tm, 1), 0) + 1
    sums = rows
    for g, win in enumerate(POOL_WINDOWS):
        cols = slice(g * gd, (g + 1) * gd)
        sums = sums[:, (gd if g else 0):]
        sums = sums + pltpu.roll(sums, win // 2, axis=0)
        cnt = jnp.minimum(pos, win).astype(F32)
        mixed = _bdot(sums[POOL_CARRY:, 0:gd] / cnt - h[:, cols], w_ref[g])
        yout_ref[:, cols] = y[:, cols] + mixed * sc_ref[:, cols]


def _pool_prompt(y, ng, w, sc, *, tm, seq):
    n, d = y.shape
    tps = seq // tm
    return pl.pallas_call(
        functools.partial(_pool_prompt_kernel, tps=tps),
        out_shape=[jax.ShapeDtypeStruct((n, d), F32), jax.ShapeDtypeStruct((n // seq, POOL_CTX, d), F32)],
        grid=(n // tm,),
        in_specs=[_rows(tm, d), _whole(ng.shape), _whole(w.shape), _whole(sc.shape)],
        out_specs=[_rows(tm, d), pl.BlockSpec((1, POOL_CTX, d), lambda i: (i // tps, 0, 0))],
        scratch_shapes=[pltpu.VMEM((POOL_CARRY, d), F32)],
        compiler_params=_cparams("arbitrary"), name="pool_prompt",
    )(y, ng, w, sc)


def _pool_sample_kernel(y_ref, ng_ref, prev_ref, w_ref, sc_ref, yout_ref, h_ref):
    y = y_ref[...]
    d = y.shape[1]
    gd = d // len(POOL_WINDOWS)
    h = _rms(y, ng_ref[...])
    h_ref[...] = h
    for g, win in enumerate(POOL_WINDOWS):
        cols = slice(g * gd, (g + 1) * gd)
        wsum = h[:, cols]
        for back in range(1, win):
            wsum = wsum + prev_ref[POOL_CTX - back, :, cols]
        mixed = _bdot(wsum / float(win) - h[:, cols], w_ref[g])
        yout_ref[:, cols] = y[:, cols] + mixed * sc_ref[:, cols]


def _pool_sample(y, ng, prev_t, w, sc):
    n, d = y.shape
    return pl.pallas_call(
        _pool_sample_kernel, out_shape=[jax.ShapeDtypeStruct((n, d), F32)] * 2, grid=(1,),
        in_specs=[_whole(y.shape), _whole(ng.shape), _whole(prev_t.shape), _whole(w.shape), _whole(sc.shape)],
        out_specs=[_whole(y.shape)] * 2,
        compiler_params=_cparams("arbitrary"), name="pool_sample",
    )(y, ng, prev_t, w, sc)


def _rwkv_proj_kernel(y_ref, ng_ref, mu_ref, wr_ref, wk_ref, wv_ref, w1_ref, w2_ref, a1_ref, a2_ref,
                      g1_ref, g2_ref, vec_ref, *rest, tps, sample):
    if sample:
        prev_ref, r_o, w_o, k_o, v_o, kk_o, kka_o, g_o, sh_o = rest
    else:
        r_o, w_o, k_o, v_o, kk_o, kka_o, g_o, sh_o, hs_scr = rest
    y = y_ref[...]
    tm = y.shape[0]
    h = _rms(y, ng_ref[...])
    if sample:
        h_prev = prev_ref[...]
        sh_o[...] = h
    else:
        i = pl.program_id(0)

        @pl.when(i % tps == 0)
        def _():
            hs_scr[...] = jnp.zeros(hs_scr.shape, F32)

        h_prev = pltpu.roll(jnp.concatenate([hs_scr[...], h], axis=0), 1, axis=0)[SUBLANES:]
        hs_scr[...] = h[tm - SUBLANES:, :]
        sh_o[0] = h[tm - 1:tm, :]
    xx = h_prev - h
    mix = lambda m: h + xx * mu_ref[m:m + 1, :]
    r = _bdot(mix(0), wr_ref[...])
    k = _bdot(mix(2), wk_ref[...])
    v = _bdot(mix(3), wv_ref[...])
    z = vec_ref[0:1, :] + _bdot(jnp.tanh(_bdot(mix(1), w1_ref[...])), w2_ref[...])
    w_log = jnp.minimum(z, 0.0) - jnp.log(1.0 + jnp.exp(-jnp.abs(z))) - 0.5
    a = _sigmoid(vec_ref[1:2, :] + _bdot(_bdot(mix(4), a1_ref[...]), a2_ref[...]))
    g_o[...] = _bdot(_sigmoid(_bdot(mix(5), g1_ref[...])), g2_ref[...])
    kk = k * vec_ref[2:3, :]
    kk = kk / jnp.maximum(jnp.sqrt(_seg_sum(kk * kk, RWKV_HEAD_DIM)), 1e-12)
    r_o[...] = r
    w_o[...] = -jnp.exp(w_log)
    k_o[...] = k * (1.0 + (a - 1.0) * vec_ref[3:4, :])
    v_o[...] = v
    kk_o[...] = kk
    kka_o[...] = kk * a


def _rwkv_proj(y, ng, mu, mats, vec, prev, *, tm, seq, sample):
    n, d = y.shape
    tps = max(seq // tm, 1)
    ins = [y, ng, mu, *mats, vec]
    in_specs = [_rows(tm, d), _whole(ng.shape), _whole(mu.shape), *[_whole(m.shape) for m in mats],
                _whole(vec.shape)]
    outs = [jax.ShapeDtypeStruct((n, d), F32)] * 7
    out_specs = [_rows(tm, d)] * 7
    scratch = []
    if sample:
        ins.append(prev)
        in_specs.append(_rows(tm, d))
        outs.append(jax.ShapeDtypeStruct((n, d), F32))
        out_specs.append(_rows(tm, d))
    else:
        outs.append(jax.ShapeDtypeStruct((n // seq, 1, d), F32))
        out_specs.append(pl.BlockSpec((1, 1, d), lambda i: (i // tps, 0, 0)))
        scratch.append(pltpu.VMEM((SUBLANES, d), F32))
    return pl.pallas_call(
        functools.partial(_rwkv_proj_kernel, tps=tps, sample=sample),
        out_shape=outs, grid=(n // tm,), in_specs=in_specs, out_specs=out_specs, scratch_shapes=scratch,
        compiler_params=_cparams("arbitrary"), name="rwkv_proj_sample" if sample else "rwkv_proj_prompt",
    )(*ins)


RWKV_CHUNK = 128


def _rwkv_chunk_kernel(r_ref, lw_ref, k_ref, v_ref, kk_ref, kka_ref, o_ref, st_ref, t_scr):
    c = pl.program_id(1)
    ch, d = r_ref.shape
    hd = RWKV_HEAD_DIM
    shift = int(math.log2(hd))
    n_pairs = d // LANES

    @pl.when(c == 0)
    def _():
        t_scr[...] = jnp.zeros(t_scr.shape, F32)

    ri = lax.broadcasted_iota(I32, (ch, ch), 0)
    ci = lax.broadcasted_iota(I32, (ch, ch), 1)
    incl = ci <= ri
    strict = ci < ri
    incl2 = jnp.concatenate([incl, incl], axis=1)
    strict2 = jnp.concatenate([strict, strict], axis=1)
    same_head = (ri >> shift) == (ci >> shift)
    eye = ri == ci
    head1 = (lax.broadcasted_iota(I32, (ch, LANES), 1) >> shift) == 1
    head1_wide = jnp.concatenate([head1, head1], axis=1)

    def by_head(x, mask):
        return jnp.concatenate([jnp.where(mask, 0.0, x), jnp.where(mask, x, 0.0)], axis=0).astype(BF16)

    lw = lw_ref[...]
    tri = jnp.where(incl, 1.0, 0.0).astype(BF16)
    p1 = lw.astype(BF16)
    rem = lw - p1.astype(F32)
    p2 = rem.astype(BF16)
    p3 = (rem - p2.astype(F32)).astype(BF16)
    cl = (jnp.dot(tri, p1, preferred_element_type=F32) + jnp.dot(tri, p2, preferred_element_type=F32)
          + jnp.dot(tri, p3, preferred_element_type=F32))
    mid = cl[ch // 2 - 1:ch // 2, :]
    last = cl[ch - 1:ch, :]
    e_mid = jnp.exp(mid)
    e_neg = jnp.exp(mid - cl)
    e_tail = jnp.exp(last - cl)
    p_last = jnp.exp(last)
    kk, kka, k = kk_ref[...], kka_ref[...], k_ref[...]
    a_s = -kk * jnp.exp(cl - lw - mid)
    r_s = r_ref[...] * jnp.exp(cl - mid)
    b_s = kka * e_neg
    k_s = k * e_neg
    b_t = kka * e_tail
    k_t = k * e_tail
    v = v_ref[...]
    zero = jnp.zeros((ch, ch), BF16)

    pairs = range(n_pairs)
    cols = [slice(p * LANES, (p + 1) * LANES) for p in pairs]
    mm = lambda a, b: jnp.dot(a, b, preferred_element_type=F32)
    lane_cat = lambda a, b: jnp.concatenate([a, b], axis=1)
    a_p = [a_s[:, c] for c in cols]
    r_p = [r_s[:, c] for c in cols]
    v_p = [v[:, c] for c in cols]
    gram = []
    for p in pairs:
        lhs = jnp.concatenate([jnp.where(head1, 0.0, a_p[p]), jnp.where(head1, 0.0, r_p[p]),
                               jnp.where(head1, a_p[p], 0.0), jnp.where(head1, r_p[p], 0.0)], axis=0)
        rhs = lane_cat(b_s[:, cols[p]].T, k_s[:, cols[p]].T)
        gram.append(mm(lhs.astype(BF16), rhs.astype(BF16)))
    ab0 = [jnp.where(strict2, g_[0:ch], 0.0) for g_ in gram]
    rb0 = [jnp.where(incl2, g_[ch:2 * ch], 0.0) for g_ in gram]
    ab1 = [jnp.where(strict2, g_[2 * ch:3 * ch], 0.0) for g_ in gram]
    rb1 = [jnp.where(incl2, g_[3 * ch:], 0.0) for g_ in gram]
    v_heads = [by_head(v_p[p], head1) for p in pairs]
    x = [lane_cat(a_p[p], mm(lane_cat(ab0[p][:, ch:], ab1[p][:, ch:]).astype(BF16), v_heads[p])) for p in pairs]
    pw = [lane_cat(ab0[p][:, :ch], ab1[p][:, :ch]).astype(BF16) for p in pairs]
    n_sq = int(math.log2(ch))
    for it in range(n_sq):
        x = [x[p] + mm(pw[p], by_head(x[p], head1_wide)) for p in pairs]
        if it < n_sq - 1:
            pw = [mm(pw[p], jnp.concatenate([lane_cat(pw[p][:, :ch], zero), lane_cat(zero, pw[p][:, ch:])],
                                            axis=0)).astype(BF16) for p in pairs]
    x = [lane_cat(x[p][:, :ch] * e_mid[:, cols[p]], x[p][:, ch:]) for p in pairs]
    qo = [mm(lane_cat(rb0[p][:, :ch], rb1[p][:, :ch]).astype(BF16), by_head(x[p], head1_wide)) for p in pairs]
    o_intra = [qo[p][:, ch:] + mm(lane_cat(rb0[p][:, ch:], rb1[p][:, ch:]).astype(BF16), v_heads[p])
               for p in pairs]
    q = [r_p[p] * e_mid[:, cols[p]] + qo[p][:, :ch] for p in pairs]
    gh = [mm(b_t[:, cols[p]].T.astype(BF16), x[p].astype(BF16)) for p in pairs]
    g = [jnp.where(same_head, gh[p][:, :ch], 0.0) + jnp.where(eye, p_last[:, cols[p]], 0.0) for p in pairs]
    h = [jnp.where(same_head, gh[p][:, ch:] + _bdot(k_t[:, cols[p]].T, v_p[p]), 0.0) for p in pairs]
    for p in pairs:
        t_old = t_scr[p].astype(BF16)
        o_ref[:, cols[p]] = mm(q[p].astype(BF16), t_old) + o_intra[p]
        t_scr[p] = mm(g[p].astype(BF16), t_old) + h[p]

    @pl.when(c == pl.num_programs(1) - 1)
    def _():
        for p in range(n_pairs):
            s_pair = t_scr[p].T
            st_ref[0, p] = s_pair[:hd, :] + s_pair[hd:, :]


def _rwkv_chunked(r, lw, k, v, kk, kka, *, batch, seq):
    n, d = r.shape
    ch = RWKV_CHUNK
    assert ch == LANES and seq % ch == 0
    n_chunks = seq // ch
    n_pairs = d // LANES
    blk = pl.BlockSpec((ch, d), lambda b, c: (b * n_chunks + c, 0))
    return pl.pallas_call(
        _rwkv_chunk_kernel,
        out_shape=[jax.ShapeDtypeStruct((n, d), F32),
                   jax.ShapeDtypeStruct((batch, n_pairs, RWKV_HEAD_DIM, LANES), F32)],
        grid=(batch, n_chunks), in_specs=[blk] * 6,
        out_specs=[blk, pl.BlockSpec((1, n_pairs, RWKV_HEAD_DIM, LANES), lambda b, c: (b, 0, 0, 0))],
        scratch_shapes=[pltpu.VMEM((n_pairs, LANES, LANES), F32)],
        compiler_params=_cparams("parallel", "arbitrary"), name="rwkv_chunked",
    )(r, lw, k, v, kk, kka)


def _rwkv_step_kernel(r_ref, lw_ref, k_ref, v_ref, kk_ref, kka_ref, s0_ref, o_ref, st_ref):
    hd = s0_ref.shape[1]
    w = jnp.exp(lw_ref[...])
    kk, kka, k, r = kk_ref[...], kka_ref[...], k_ref[...], r_ref[...]

    def value_row(i, carry):
        s = s0_ref[0, i]
        s_kk = jnp.sum(s * kk, axis=0, keepdims=True)
        s = s * w - s_kk * kka + v_ref[pl.ds(i, 1), :] * k
        st_ref[0, i] = s
        o_ref[pl.ds(i, 1), :] = jnp.sum(s * r, axis=0, keepdims=True)
        return carry

    lax.fori_loop(0, hd, value_row, 0, unroll=8)


def _rwkv_step(r, lw, k, v, kk, kka, s0):
    d, n = r.shape
    heads, hd = s0.shape[0], s0.shape[1]
    vec = pl.BlockSpec((hd, n), lambda h: (h, 0))
    st = pl.BlockSpec((1, hd, hd, n), lambda h: (h, 0, 0, 0))
    return pl.pallas_call(
        _rwkv_step_kernel,
        out_shape=[jax.ShapeDtypeStruct((d, n), F32), jax.ShapeDtypeStruct(s0.shape, F32)],
        grid=(heads,), in_specs=[vec] * 6 + [st], out_specs=[vec, st],
        compiler_params=_cparams("parallel"), name="rwkv_step",
    )(r, lw, k, v, kk, kka, s0)


def _rwkv_out_kernel(o_ref, r_ref, k_ref, v_ref, g_ref, y_ref, vec_ref, wo_ref, yout_ref):
    hd = RWKV_HEAD_DIM
    o = o_ref[...]
    mu = _seg_sum(o, hd) * (1.0 / hd)
    dlt = o - mu
    var = _seg_sum(dlt * dlt, hd) * (1.0 / hd)
    o = dlt * lax.rsqrt(var + RWKV_LNX_EPS) * vec_ref[1:2, :] + vec_ref[2:3, :]
    o = o + _seg_sum(r_ref[...] * k_ref[...] * vec_ref[0:1, :], hd) * v_ref[...]
    yout_ref[...] = y_ref[...] + _bdot(o * g_ref[...], wo_ref[...])


def _rwkv_out(o, r, k, v, g, y, vec, wo, *, tm, name):
    n, d = y.shape
    return pl.pallas_call(
        _rwkv_out_kernel, out_shape=jax.ShapeDtypeStruct((n, d), F32), grid=(n // tm,),
        in_specs=[_rows(tm, d)] * 6 + [_whole(vec.shape), _whole(wo.shape)], out_specs=_rows(tm, d),
        compiler_params=_cparams("parallel"), name=name,
    )(o, r, k, v, g, y, vec, wo)


FFN_CHUNK = 256


def _ffn_act(gate, g1, g2, up, cw, cb):
    return _gelu(cw[0:1] * g2 + cw[1:2] * g1 + cw[2:3] * gate + cb) * up


def _ffn_prompt_kernel(y_ref, ng_ref, win_ref, cw_ref, cb_ref, wout_ref, *rest, tps, final):
    if final:
        fg_ref, yout_ref, st_ref, yfin_ref, h_scr, gs_scr, act_scr, carry_scr = rest
    else:
        yout_ref, st_ref, h_scr, gs_scr, act_scr, carry_scr = rest
    i = pl.program_id(0)
    y = y_ref[...]
    tm = y.shape[0]
    hid = wout_ref.shape[0]
    tf = FFN_CHUNK
    h_scr[...] = _rms(y, ng_ref[...]).astype(BF16)

    @pl.when(i % tps == 0)
    def _():
        carry_scr[...] = jnp.zeros(carry_scr.shape, F32)

    for j in range(hid // tf):
        cols = slice(j * tf, (j + 1) * tf)
        hb = h_scr[...]
        gs_scr[0:SUBLANES, :] = carry_scr[:, cols]
        gs_scr[SUBLANES:, :] = jnp.dot(hb, win_ref[:, cols], preferred_element_type=F32)
        up = jnp.dot(hb, win_ref[:, hid + j * tf:hid + (j + 1) * tf], preferred_element_type=F32)
        gate = gs_scr[SUBLANES:, :]
        carry_scr[:, cols] = gate[tm - SUBLANES:, :]
        st_ref[0, :, cols] = gate[tm - (FFN_CONV - 1):, :]
        act = _ffn_act(gate, gs_scr[SUBLANES - 1:tm + SUBLANES - 1, :], gs_scr[SUBLANES - 2:tm + SUBLANES - 2, :],
                       up, cw_ref[:, cols], cb_ref[:, cols])
        act_scr[:, cols] = act.astype(BF16)
    y_new = y + jnp.dot(act_scr[...], wout_ref[...], preferred_element_type=F32)
    yout_ref[...] = y_new
    if final:
        yfin_ref[...] = _rms(y_new, fg_ref[...])


def _ffn_prompt(y, ng, win, cw, cb, wout, layer, fg, *, tm, seq):
    n, d = y.shape
    hid = wout.shape[1]
    tps = seq // tm
    final = fg is not None
    of_layer = lambda a: pl.BlockSpec((None,) + a.shape[1:], lambda i: (layer, 0, 0), pipeline_mode=pl.Buffered(1))
    ins = [y, ng, win, cw, cb, wout]
    in_specs = [_rows(tm, d), _whole(ng.shape), of_layer(win), _whole(cw.shape), _whole(cb.shape), of_layer(wout)]
    outs = [jax.ShapeDtypeStruct((n, d), F32), jax.ShapeDtypeStruct((n // seq, FFN_CONV - 1, hid), F32)]
    out_specs = [_rows(tm, d), pl.BlockSpec((1, FFN_CONV - 1, hid), lambda i: (i // tps, 0, 0))]
    if final:
        ins.append(fg)
        in_specs.append(_whole(fg.shape))
        outs.append(jax.ShapeDtypeStruct((n, d), F32))
        out_specs.append(_rows(tm, d))
    return pl.pallas_call(
        functools.partial(_ffn_prompt_kernel, tps=tps, final=final),
        out_shape=outs, grid=(n // tm,), in_specs=in_specs, out_specs=out_specs,
        scratch_shapes=[pltpu.VMEM((tm, d), BF16), pltpu.VMEM((tm + SUBLANES, FFN_CHUNK), F32),
                        pltpu.VMEM((tm, hid), BF16), pltpu.VMEM((SUBLANES, hid), F32)],
        compiler_params=_cparams("arbitrary"), name="ffn_prompt",
    )(*ins)


def _ffn_sample_kernel(y_ref, ng_ref, wg_ref, wu_ref, cw_ref, cb_ref, p2_ref, p1_ref, wout_ref, *rest, final):
    if final:
        fg_ref, yout_ref, gate_ref, yfin_ref, h_scr, acc_scr = rest
    else:
        yout_ref, gate_ref, h_scr, acc_scr = rest
    j = pl.program_id(0)

    @pl.when(j == 0)
    def _():
        h_scr[...] = _rms(y_ref[...], ng_ref[...]).astype(BF16)
        acc_scr[...] = jnp.zeros(acc_scr.shape, F32)

    hb = h_scr[...]
    gate = jnp.dot(hb, wg_ref[...], preferred_element_type=F32)
    up = jnp.dot(hb, wu_ref[...], preferred_element_type=F32)
    gate_ref[...] = gate
    act = _ffn_act(gate, p1_ref[...], p2_ref[...], up, cw_ref[...], cb_ref[...])
    acc_scr[...] += _bdot(act, wout_ref[...])

    @pl.when(j == pl.num_programs(0) - 1)
    def _():
        y_new = y_ref[...] + acc_scr[...]
        yout_ref[...] = y_new
        if final:
            yfin_ref[...] = _rms(y_new, fg_ref[...])


def _ffn_sample(y, ng, win, cw, cb, p2, p1, wout, layer, fg):
    n, d = y.shape
    hid = wout.shape[1]
    tf = hid // 2 if (hid // 2) % LANES == 0 else FFN_CHUNK
    nf = hid // tf
    final = fg is not None
    keep = lambda shape: pl.BlockSpec(shape, lambda j: (0,) * len(shape))
    chunk = lambda rows: pl.BlockSpec((rows, tf), lambda j: (0, j))
    ins = [y, ng, win, win, cw, cb, p2, p1, wout]
    in_specs = [keep((n, d)), keep(ng.shape), pl.BlockSpec((None, d, tf), lambda j: (layer, 0, j)),
                pl.BlockSpec((None, d, tf), lambda j: (layer, 0, nf + j)),
                chunk(FFN_CONV), chunk(1), chunk(n), chunk(n), pl.BlockSpec((None, tf, d), lambda j: (layer, j, 0))]
    outs = [jax.ShapeDtypeStruct((n, d), F32), jax.ShapeDtypeStruct((n, hid), F32)]
    out_specs = [keep((n, d)), chunk(n)]
    if final:
        ins.append(fg)
        in_specs.append(keep(fg.shape))
        outs.append(jax.ShapeDtypeStruct((n, d), F32))
        out_specs.append(keep((n, d)))
    return pl.pallas_call(
        functools.partial(_ffn_sample_kernel, final=final),
        out_shape=outs, grid=(nf,), in_specs=in_specs, out_specs=out_specs,
        scratch_shapes=[pltpu.VMEM((n, d), BF16), pltpu.VMEM((n, d), F32)],
        compiler_params=_cparams("arbitrary"), name="ffn_sample",
    )(*ins)


def kernel(x_prompt, x_sample, cache_moba_k, cache_moba_v, state_pool, state_rwkv_wkv, state_rwkv_shift, state_ffn_conv, page_table, norm_mix_g, norm_ffn_g, norm_final_g, rel_bias, gm_w_in, gm_ln_g, gm_ln_b, gm_w_s, gm_b_s, gm_w_out, moba_w_qkv, moba_w_o, pool_w, pool_scale, rwkv_mu, rwkv_w_r, rwkv_w_k, rwkv_w_v, rwkv_w_o, rwkv_w0, rwkv_w1, rwkv_w2, rwkv_a0, rwkv_a1, rwkv_a2, rwkv_g1, rwkv_g2, rwkv_k_k, rwkv_k_a, rwkv_r_k, rwkv_lnx_g, rwkv_lnx_b, ffn_w_in, ffn_conv_w, ffn_conv_b, ffn_w_out):
    bp, seq, d = x_prompt.shape
    bs = x_sample.shape[0]
    depth = norm_mix_g.shape[0]
    assert x_sample.shape[1] == 1 and depth == 4
    past_len = page_table.shape[1] * PAGE_SIZE
    assert seq % MOBA_BLOCK == 0 and past_len % MOBA_BLOCK == 0 and past_len % GM_CHUNK == 0
    row = lambda vct: vct.reshape(1, -1)
    bf = lambda m: m.astype(BF16)
    yp = x_prompt.reshape(bp * seq, d)
    ys = x_sample.reshape(bs, d)
    tm = TOKEN_TILE
    assert seq % tm == 0 and seq % RWKV_PROJ_TILE == 0 and seq % RWKV_CHUNK == 0
    conv_p, conv_s = [], []
    ffn_in, ffn_out = bf(ffn_w_in), bf(ffn_w_out)

    def ffn(i, yp, ys):
        last = i == depth - 1
        cw, cb, ng = ffn_conv_w[i], row(ffn_conv_b[i]), row(norm_ffn_g[i])
        fg = row(norm_final_g) if last else None
        res_p = _ffn_prompt(yp, ng, ffn_in, cw, cb, ffn_out, i, fg, tm=tm if last else 2 * tm, seq=seq)
        st = state_ffn_conv[i]
        res_s = _ffn_sample(ys, ng, ffn_in, cw, cb, st[:, 0], st[:, 1], ffn_out, i, fg)
        conv_p.append(res_p[1])
        conv_s.append(jnp.stack([st[:, 1], res_s[1]], axis=1))
        if last:
            return res_p[2], res_s[2]
        return res_p[0], res_s[0]

    ng = row(norm_mix_g[0])
    width = gm_w_out.shape[1]
    gd = width // GM_GROUPS
    gm_in, gm_out = bf(gm_w_in[0]), bf(gm_w_out[0])
    lng, lnb = row(gm_ln_g[0]), row(gm_ln_b[0])
    sb_prompt = jnp.repeat(gm_b_s[0].T, gd, axis=1)
    (yp,) = _gmlp(yp, ng, gm_in, lng, lnb, gm_w_s[0], sb_prompt, gm_out, tm=tm, sample=False)
    sa_first = row(jnp.repeat(gm_w_s[0][:, 0, 0], gd))
    sb_first = row(jnp.repeat(gm_b_s[0][:, 0], gd))
    ys, gm_v = _gmlp(ys, ng, gm_in, lng, lnb, sa_first, sb_first, gm_out, tm=bs, sample=True)
    gm_v_sample = gm_v.reshape(1, bs, 1, width)
    yp, ys = ffn(0, yp, ys)

    ng = row(norm_mix_g[1])
    w_qkv, w_o = bf(moba_w_qkv[0]), bf(moba_w_o[0])
    heads = MOBA_HEADS
    hd = d // heads
    qp, kp_t, vp_t = _qkv_prompt(yp, ng, w_qkv[:, :d], w_qkv[:, d:2 * d].T, w_qkv[:, 2 * d:].T,
                                 tm=2 * tm, batch=bp, seq=seq)
    qs, ks, vs = _norm_linear(ys, ng, w_qkv, 3, tm=bs, name="moba_qkv_sample")
    blk = MOBA_BLOCK
    qi = jnp.arange(blk, dtype=I32)[:, None]
    ki = jnp.arange(blk, dtype=I32)[None, :]
    bkt_tiles = _t5_bucket_table(jnp.stack([qi - ki, blk + qi - ki]))
    bias_tiles = _bias_tiles(rel_bias, bkt_tiles)
    op = _moba_prompt(rel_bias, qp, kp_t, vp_t, bias_tiles, batch=bp, seq=seq, head_dim=hd)
    bkt_rows = _t5_bucket_table(past_len - jnp.arange(past_len, dtype=I32))
    rb_pad = jnp.pad(rel_bias, ((0, 0), (0, LANES - heads)))
    bias_rows = _bias_rows(rb_pad, jnp.broadcast_to(bkt_rows[:, None], (past_len, LANES)))
    os_ = _moba_sample(page_table, qs, ks, vs, cache_moba_k, cache_moba_v, 0, bias_rows, rel_bias, head_dim=hd)
    yp = _linear_res(op, w_o, yp, tm=tm, name="moba_out_prompt")
    ys = _linear_res(os_, w_o, ys, tm=bs, name="moba_out_sample")
    moba_k_prompt = kp_t.reshape(1, bp, heads, hd, seq).transpose(0, 1, 4, 2, 3)
    moba_v_prompt = vp_t.reshape(1, bp, heads, hd, seq).transpose(0, 1, 4, 2, 3)
    moba_k_sample = ks.reshape(1, bs, 1, heads, hd)
    moba_v_sample = vs.reshape(1, bs, 1, heads, hd)
    yp, ys = ffn(1, yp, ys)

    ng = row(norm_mix_g[2])
    pw, psc = bf(pool_w[0]), row(pool_scale[0])
    yp, pool_p = _pool_prompt(yp, ng, pw, psc, tm=tm, seq=seq)
    ys, hs = _pool_sample(ys, ng, jnp.swapaxes(state_pool[0], 0, 1), pw, psc)
    pool_prompt = pool_p[None]
    pool_sample = jnp.concatenate([state_pool[0][:, 1:], hs[:, None]], axis=1)[None]
    yp, ys = ffn(2, yp, ys)

    ng = row(norm_mix_g[3])
    mats = [bf(m[0]) for m in (rwkv_w_r, rwkv_w_k, rwkv_w_v, rwkv_w1, rwkv_w2, rwkv_a1, rwkv_a2,
                               rwkv_g1, rwkv_g2)]
    vec_in = jnp.stack([rwkv_w0[0], rwkv_a0[0], rwkv_k_k[0], rwkv_k_a[0]])
    vec_out = jnp.stack([rwkv_r_k[0].reshape(-1), rwkv_lnx_g[0], rwkv_lnx_b[0]])
    rh = d // RWKV_HEAD_DIM
    *seqs_p, gp, shp = _rwkv_proj(yp, ng, rwkv_mu[0], mats, vec_in, None, tm=RWKV_PROJ_TILE, seq=seq,
                                  sample=False)
    *seqs_s, gs, shs = _rwkv_proj(ys, ng, rwkv_mu[0], mats, vec_in, state_rwkv_shift[0], tm=bs, seq=1,
                                  sample=True)
    o_p, st_p = _rwkv_chunked(*seqs_p, batch=bp, seq=seq)
    hd_r = RWKV_HEAD_DIM
    wkv_p = st_p.reshape(bp, rh // 2, hd_r, 2, hd_r).transpose(0, 1, 3, 2, 4).reshape(bp, rh, hd_r, hd_r)
    o_s_t, wkv_s_t = _rwkv_step(*[a.T for a in seqs_s], jnp.transpose(state_rwkv_wkv[0], (1, 2, 3, 0)))
    o_s, wkv_s = o_s_t.T, jnp.transpose(wkv_s_t, (3, 0, 1, 2))
    w_o = bf(rwkv_w_o[0])
    r_p, _, k_p, v_p = seqs_p[:4]
    r_s, _, k_s, v_s = seqs_s[:4]
    yp = _rwkv_out(o_p, r_p, k_p, v_p, gp, yp, vec_out, w_o, tm=tm, name="rwkv_out_prompt")
    ys = _rwkv_out(o_s.reshape(bs, d), r_s, k_s, v_s, gs, ys, vec_out, w_o, tm=bs, name="rwkv_out_sample")
    yp, ys = ffn(3, yp, ys)

    return (yp.reshape(bp, seq, d), ys.reshape(bs, 1, d), gm_v_sample, moba_k_prompt, moba_v_prompt,
            moba_k_sample, moba_v_sample, pool_prompt, pool_sample, wkv_p[None], wkv_s[None],
            shp.reshape(1, bp, d), shs[None], jnp.stack(conv_p), jnp.stack(conv_s))
```

```python
import functools
import math

import jax
import jax.numpy as jnp
from jax import lax
from jax.experimental import pallas as pl
from jax.experimental.pallas import tpu as pltpu

F32 = jnp.float32
BF16 = jnp.bfloat16
I32 = jnp.int32

LANES = 128
SUBLANES = 8
VMEM_LIMIT_BYTES = 56 * 2**20

TOKEN_TILE = 512
RWKV_PROJ_TILE = 512

RMS_EPS = 1e-6
GM_LN_EPS = 1e-5
GM_CHUNK = 128
GM_GROUPS = 8
MOBA_HEADS = 16
MOBA_BLOCK = 256
MOBA_TOPK = 3
REL_BUCKETS = 32
REL_MAX_DIST = 128
PAGE_SIZE = 128
POOL_WINDOWS = (2, 4, 8, 16)
POOL_CTX = max(POOL_WINDOWS) - 1
RWKV_HEAD_DIM = 64
RWKV_LNX_EPS = 64e-5
FFN_CONV = 3
NEG_INF = float("-inf")

assert MOBA_BLOCK >= REL_MAX_DIST


def _cparams(*sem):
    return pltpu.CompilerParams(dimension_semantics=sem, vmem_limit_bytes=VMEM_LIMIT_BYTES)


def _whole(shape):
    nd = len(shape)
    return pl.BlockSpec(shape, lambda *_: (0,) * nd, pipeline_mode=pl.Buffered(1))


def _rows(tm, width):
    return pl.BlockSpec((tm, width), lambda i: (i, 0))


def _rms(x, g):
    return x * lax.rsqrt(jnp.mean(x * x, axis=-1, keepdims=True) + RMS_EPS) * g


def _bdot(a, b):
    return jnp.dot(a.astype(BF16), b.astype(BF16), preferred_element_type=F32)


def _nt_dot(a, b):
    return lax.dot_general(a.astype(BF16), b.astype(BF16), (((1,), (1,)), ((), ())),
                           preferred_element_type=F32)


def _split(x):
    hi = x.astype(BF16)
    lo = (x - hi.astype(F32)).astype(BF16)
    return hi, lo


def _split_dot(x, m):
    hi, lo = _split(x)
    return (jnp.dot(hi, m, preferred_element_type=F32)
            + jnp.dot(lo, m, preferred_element_type=F32))


def _same_head(n, head_dim):
    shift = int(math.log2(head_dim))
    r = lax.broadcasted_iota(I32, (n, n), 0) >> shift
    c = lax.broadcasted_iota(I32, (n, n), 1) >> shift
    return jnp.where(r == c, 1.0, 0.0).astype(BF16)


def _seg_sum(x, head_dim):
    g = _same_head(LANES, head_dim)
    parts = [_split_dot(x[:, i:i + LANES], g) for i in range(0, x.shape[1], LANES)]
    return jnp.concatenate(parts, axis=1)


def _gelu(x):
    return 0.5 * x * (1.0 + jnp.tanh(0.7978845608028654 * (x + 0.044715 * x * x * x)))


def _sigmoid(x):
    return 1.0 / (1.0 + jnp.exp(-x))


def _top_mask(gate, idx, axis, n_valid):
    cur = gate
    sel = jnp.zeros(gate.shape, F32)
    for _ in range(MOBA_TOPK):
        m = jnp.max(cur, axis=axis, keepdims=True)
        first = jnp.min(jnp.where(cur == m, idx, n_valid), axis=axis, keepdims=True)
        pick = (idx == first) & (m > NEG_INF)
        sel = jnp.where(pick, 1.0, sel)
        cur = jnp.where(pick, NEG_INF, cur)
    return sel


def _gmlp_kernel(x_ref, ng_ref, win_ref, lng_ref, lnb_ref, sa_ref, sb_ref, wout_ref, y_ref, aux_ref,
                 *, sample):
    x = x_ref[...]
    tm, width = x.shape[0], wout_ref.shape[0]
    h = _rms(x, ng_ref[...])
    z = _gelu(_bdot(h, win_ref[...]))
    u, v = z[:, :width], z[:, width:]
    mu = jnp.mean(v, axis=-1, keepdims=True)
    d = v - mu
    var = jnp.mean(d * d, axis=-1, keepdims=True)
    v = d * lax.rsqrt(var + GM_LN_EPS) * lng_ref[...] + lnb_ref[...]
    if sample:
        aux_ref[...] = v
        s = v * sa_ref[...] + sb_ref[...]
    else:
        gd = width // GM_GROUPS
        causal = (lax.broadcasted_iota(I32, (GM_CHUNK, GM_CHUNK), 0)
                  >= lax.broadcasted_iota(I32, (GM_CHUNK, GM_CHUNK), 1))
        for g in range(GM_GROUPS):
            wg = jnp.where(causal, sa_ref[g], 0.0).astype(BF16)
            cols = slice(g * gd, (g + 1) * gd)
            for c in range(tm // GM_CHUNK):
                rows = slice(c * GM_CHUNK, (c + 1) * GM_CHUNK)
                aux_ref[rows, cols] = (jnp.dot(wg, v[rows, cols].astype(BF16), preferred_element_type=F32)
                                       + sb_ref[:, cols])
        s = aux_ref[...]
    y_ref[...] = x + _bdot(u * s, wout_ref[...])


def _gmlp(x, ng, win, lng, lnb, sa, sb, wout, *, tm, sample):
    n, d = x.shape
    width = wout.shape[0]
    outs = [jax.ShapeDtypeStruct((n, d), F32)]
    out_specs = [_rows(tm, d)]
    scratch = []
    if sample:
        outs.append(jax.ShapeDtypeStruct((n, width), F32))
        out_specs.append(_rows(tm, width))
    else:
        scratch.append(pltpu.VMEM((tm, width), F32))
    return pl.pallas_call(
        functools.partial(_gmlp_kernel, sample=sample),
        out_shape=outs, grid=(n // tm,),
        in_specs=[_rows(tm, d), _whole(ng.shape), _whole(win.shape), _whole(lng.shape), _whole(lnb.shape),
                  _whole(sa.shape), _whole(sb.shape), _whole(wout.shape)],
        out_specs=out_specs, scratch_shapes=scratch,
        compiler_params=_cparams("parallel"), name="gmlp_sample" if sample else "gmlp_prompt",
    )(x, ng, win, lng, lnb, sa, sb, wout)


def _norm_linear_kernel(x_ref, ng_ref, w_ref, *o_refs):
    z = _bdot(_rms(x_ref[...], ng_ref[...]), w_ref[...])
    wd = z.shape[1] // len(o_refs)
    for i, o_ref in enumerate(o_refs):
        o_ref[...] = z[:, i * wd:(i + 1) * wd]


def _norm_linear(x, ng, w, n_out, *, tm, name):
    n, d = x.shape
    wd = w.shape[1] // n_out
    return pl.pallas_call(
        _norm_linear_kernel,
        out_shape=[jax.ShapeDtypeStruct((n, wd), F32)] * n_out, grid=(n // tm,),
        in_specs=[_rows(tm, d), _whole(ng.shape), _whole(w.shape)],
        out_specs=[_rows(tm, wd)] * n_out,
        compiler_params=_cparams("parallel"), name=name,
    )(x, ng, w)


def _qkv_prompt_kernel(x_ref, ng_ref, wq_ref, wkt_ref, wvt_ref, q_ref, kt_ref, vt_ref):
    h = _rms(x_ref[...], ng_ref[...]).astype(BF16)
    q_ref[...] = jnp.dot(h, wq_ref[...], preferred_element_type=F32)
    kt_ref[0] = _nt_dot(wkt_ref[...], h)
    vt_ref[0] = _nt_dot(wvt_ref[...], h)


def _qkv_prompt(x, ng, wq, wkt, wvt, *, tm, batch, seq):
    n, d = x.shape
    tps = seq // tm
    t_spec = pl.BlockSpec((1, d, tm), lambda i: (i // tps, 0, i % tps))
    return pl.pallas_call(
        _qkv_prompt_kernel,
        out_shape=[jax.ShapeDtypeStruct((n, d), F32)] + [jax.ShapeDtypeStruct((batch, d, seq), F32)] * 2,
        grid=(n // tm,),
        in_specs=[_rows(tm, d), _whole(ng.shape), _whole(wq.shape), _whole(wkt.shape), _whole(wvt.shape)],
        out_specs=[_rows(tm, d), t_spec, t_spec],
        compiler_params=_cparams("parallel"), name="moba_qkv_prompt",
    )(x, ng, wq, wkt, wvt)


def _linear_res_kernel(a_ref, w_ref, y_ref, o_ref):
    o_ref[...] = y_ref[...] + _bdot(a_ref[...], w_ref[...])


def _linear_res(a, w, y, *, tm, name):
    n, d = y.shape
    return pl.pallas_call(
        _linear_res_kernel, out_shape=jax.ShapeDtypeStruct((n, d), F32), grid=(n // tm,),
        in_specs=[_rows(tm, a.shape[1]), _whole(w.shape), _rows(tm, d)], out_specs=_rows(tm, d),
        compiler_params=_cparams("parallel"), name=name,
    )(a, w, y)


def _t5_bucket_table(rel):
    n = jnp.maximum(rel, 0)
    exact = REL_BUCKETS // 2
    nf = jnp.maximum(n, 1).astype(F32)
    large = exact + (jnp.log(nf / exact) / math.log(REL_MAX_DIST / exact)
                     * (REL_BUCKETS - exact)).astype(I32)
    return jnp.where(n < exact, n, jnp.minimum(large, REL_BUCKETS - 1)).astype(I32)


def _bias_tiles_kernel(rb_ref, bkt_ref, o_ref):
    h = pl.program_id(0)
    bkt = bkt_ref[...]
    acc = jnp.zeros(bkt.shape, F32)
    for b in range(REL_BUCKETS):
        acc = jnp.where(bkt == b, rb_ref[b, h], acc)
    future = (lax.broadcasted_iota(I32, bkt.shape, 2) > lax.broadcasted_iota(I32, bkt.shape, 1))
    own_block = lax.broadcasted_iota(I32, bkt.shape, 0) == 0
    o_ref[0] = jnp.where(future & own_block, NEG_INF, acc)


def _bias_tiles(rel_bias, bkt):
    heads = rel_bias.shape[1]
    return pl.pallas_call(
        _bias_tiles_kernel, out_shape=jax.ShapeDtypeStruct((heads,) + bkt.shape, F32), grid=(heads,),
        in_specs=[pl.BlockSpec(memory_space=pltpu.SMEM), _whole(bkt.shape)],
        out_specs=pl.BlockSpec((1,) + bkt.shape, lambda h: (h, 0, 0, 0)),
        compiler_params=_cparams("parallel"), name="moba_bias_tiles",
    )(rel_bias, bkt)


def _bias_rows_kernel(rb_ref, bkt_ref, o_ref):
    bkt = bkt_ref[...]
    acc = jnp.zeros(bkt.shape, F32)
    for b in range(REL_BUCKETS):
        acc = jnp.where(bkt == b, rb_ref[b:b + 1, :], acc)
    o_ref[...] = acc


def _bias_rows(rb_pad, bkt):
    return pl.pallas_call(
        _bias_rows_kernel, out_shape=jax.ShapeDtypeStruct(bkt.shape, F32), grid=(1,),
        in_specs=[_whole(rb_pad.shape), _whole(bkt.shape)], out_specs=_whole(bkt.shape),
        compiler_params=_cparams("arbitrary"), name="moba_bias_rows",
    )(rb_pad, bkt)


def _moba_prompt_kernel(rb_ref, q_ref, k_ref, v_ref, bias_ref, o_ref, kb_scr, vh_scr, kmean_scr, *, head_dim):
    pair = pl.program_id(1)
    blk = MOBA_BLOCK
    seq = k_ref.shape[2]
    n_blocks = seq // blk
    assert LANES == 2 * head_dim
    shift = int(math.log2(head_dim))
    scale = head_dim ** -0.5
    assert shift % 2 == 0 and n_blocks <= kmean_scr.shape[0]

    def prepare():
        kt, vt = k_ref[0], v_ref[0]
        second = (lax.broadcasted_iota(I32, (LANES, seq), 0) >> shift) == 1
        kb_scr[...] = kt.astype(BF16)
        vh_scr[0] = jnp.where(second, 0.0, vt).astype(BF16)
        vh_scr[1] = jnp.where(second, vt, 0.0).astype(BF16)
        rows = kmean_scr.shape[0]
        in_block = jnp.where((lax.broadcasted_iota(I32, (rows, seq), 1) >> int(math.log2(blk)))
                             == lax.broadcasted_iota(I32, (rows, seq), 0), 1.0, 0.0).astype(BF16)
        k_hi, k_lo = _split(kt)
        nt = lambda a, b: lax.dot_general(a, b, (((1,), (1,)), ((), ())), preferred_element_type=F32)
        kmean_scr[...] = (nt(in_block, k_hi) + nt(in_block, k_lo)) * (1.0 / blk)

    prepare()
    second = (lax.broadcasted_iota(I32, (blk, LANES), 1) >> shift) == 1
    km_hi, km_lo = _split(kmean_scr[...])
    nt = lambda a, b: lax.dot_general(a, b, (((1,), (1,)), ((), ())), preferred_element_type=F32)
    block_id = lax.broadcasted_iota(I32, (kmean_scr.shape[0], 2 * blk), 0)
    far_bias = [rb_ref[REL_BUCKETS - 1, pair * 2 + hh] for hh in range(2)]

    def select(own):
        q = q_ref[own * blk:(own + 1) * blk, :]
        q2 = jnp.concatenate([jnp.where(second, 0.0, q), jnp.where(second, q, 0.0)], axis=0)
        q_hi, q_lo = _split(q2)
        gate = nt(km_hi, q_hi) + nt(km_lo, q_hi) + nt(km_hi, q_lo)
        sel = _top_mask(jnp.where(block_id < own, gate, NEG_INF), block_id, 0, gate.shape[0]).T
        return (q2 * scale).astype(BF16), sel

    def scores(own, q_scaled):
        return jnp.dot(q_scaled, kb_scr[:, 0:(own + 1) * blk], preferred_element_type=F32)

    def mask(own, s, sel):
        head_rows = []
        for hh in range(2):
            rows = slice(hh * blk, (hh + 1) * blk)
            tiles = []
            for n in range(own + 1):
                t = s[rows, n * blk:(n + 1) * blk]
                chosen = sel[rows, n:n + 1] > 0.0
                if n == own:
                    t = t + bias_ref[hh, 0]
                elif n == own - 1:
                    t = jnp.where(chosen, t + bias_ref[hh, 1], NEG_INF)
                else:
                    t = t + jnp.where(chosen, far_bias[hh], NEG_INF)
                tiles.append(t)
            head_rows.append(jnp.concatenate(tiles, axis=1))
        return jnp.concatenate(head_rows, axis=0)

    def weights(s):
        p = jnp.exp(s - jnp.max(s, axis=-1, keepdims=True))
        return p, jnp.sum(p, axis=-1, keepdims=True)

    def output(own, p, l):
        keys = (own + 1) * blk
        p_pair = jnp.concatenate([p[:blk], p[blk:]], axis=1).astype(BF16)
        v_pair = jnp.concatenate([vh_scr[0, :, 0:keys], vh_scr[1, :, 0:keys]], axis=1)
        o = nt(p_pair, v_pair) / jnp.where(second, l[blk:], l[:blk])
        o_ref[own * blk:(own + 1) * blk, :] = o.astype(o_ref.dtype)

    order = list(range(n_blocks))
    groups = [(order[i], order[-1 - i]) for i in range(n_blocks // 2)] + ([(order[n_blocks // 2],)] * (n_blocks % 2))
    groups = [sum(groups[i:i + 2], ()) for i in range(0, len(groups), 2)]
    for group in groups:
        picked = [select(own) for own in group]
        raw = [scores(own, qs) for own, (qs, _) in zip(group, picked)]
        masked = [mask(own, s, sel) for own, s, (_, sel) in zip(group, raw, picked)]
        probs = [weights(s) for s in masked]
        for own, (p, l) in zip(group, probs):
            output(own, p, l)


def _moba_prompt(rel_bias, q, k, v, bias_tiles, *, batch, seq, head_dim):
    n, d = q.shape
    blk = MOBA_BLOCK
    hpt = LANES // head_dim
    return pl.pallas_call(
        functools.partial(_moba_prompt_kernel, head_dim=head_dim),
        out_shape=jax.ShapeDtypeStruct((n, d), BF16), grid=(batch, d // LANES),
        in_specs=[pl.BlockSpec(memory_space=pltpu.SMEM),
                  pl.BlockSpec((seq, LANES), lambda b, p: (b, p)),
                  pl.BlockSpec((1, LANES, seq), lambda b, p: (b, p, 0)),
                  pl.BlockSpec((1, LANES, seq), lambda b, p: (b, p, 0)),
                  pl.BlockSpec((hpt, 2, blk, blk), lambda b, p: (p, 0, 0, 0))],
        out_specs=pl.BlockSpec((seq, LANES), lambda b, p: (b, p)),
        scratch_shapes=[pltpu.VMEM((LANES, seq), BF16), pltpu.VMEM((hpt, LANES, seq), BF16),
                        pltpu.VMEM((2 * SUBLANES, LANES), F32)],
        compiler_params=_cparams("parallel", "parallel"), name="moba_prompt_attn",
    )(rel_bias, q, k, v, bias_tiles)


def _moba_sample_kernel(pt_ref, q_ref, kn_ref, vn_ref, *rest, n_pages, head_dim):
    del pt_ref
    k_refs, v_refs = rest[:n_pages], rest[n_pages:2 * n_pages]
    bias_ref, rb0_ref, o_ref = rest[2 * n_pages:]
    d = q_ref.shape[2]
    heads = d // head_dim
    shift = int(math.log2(head_dim))
    per_block = MOBA_BLOCK // PAGE_SIZE
    blocks = range(n_pages // per_block)
    scale = head_dim ** -0.5
    head_lanes = jnp.where((lax.broadcasted_iota(I32, (heads, d), 1) >> shift)
                           == lax.broadcasted_iota(I32, (heads, d), 0), 1.0, 0.0)
    q = head_lanes * q_ref[0]
    q_hi, q_lo = _split(q)
    q2 = jnp.concatenate([q_hi, q_lo], axis=0)
    block_of = lambda refs, n: jnp.concatenate(
        [refs[per_block * n + j][0, 0].reshape(d, PAGE_SIZE) for j in range(per_block)], axis=1).astype(BF16)
    raw = [jnp.dot(q2, block_of(k_refs, n), preferred_element_type=F32) for n in blocks]
    raw = [r[:heads] + r[heads:] for r in raw]
    block_id = lax.broadcasted_iota(I32, (heads, LANES), 1)
    gate = jnp.full((heads, LANES), NEG_INF, F32)
    for n in blocks:
        gate = jnp.where(block_id == n, jnp.sum(raw[n], axis=-1, keepdims=True), gate)
    sel = _top_mask(gate, block_id, -1, LANES)
    s_new = jnp.sum(q * kn_ref[0], axis=-1, keepdims=True) * scale + rb0_ref[:, 0:1]
    s = [raw[n] * scale + bias_ref[n] for n in blocks]
    m = s_new
    for n in blocks:
        m = jnp.maximum(m, jnp.where(sel[:, n:n + 1] > 0.0, jnp.max(s[n], axis=-1, keepdims=True), NEG_INF))
    p = [jnp.exp(jnp.where(sel[:, n:n + 1] > 0.0, s[n] - m, NEG_INF)) for n in blocks]
    p_new = jnp.exp(s_new - m)
    den = p_new
    for n in blocks:
        den = den + jnp.sum(p[n], axis=-1, keepdims=True)
    p_all = jnp.concatenate(p, axis=1).astype(BF16)
    v_all = jnp.concatenate([block_of(v_refs, n) for n in blocks], axis=1)
    acc = lax.dot_general(p_all, v_all, (((1,), (1,)), ((), ())), preferred_element_type=F32)
    out = (acc + p_new * vn_ref[0]) * head_lanes / den
    o_ref[0] = jnp.sum(out, axis=0, keepdims=True)


def _moba_sample(page_table, q, k_new, v_new, cache_k, cache_v, layer, bias_rows, rel_bias, *, head_dim):
    nb, d = q.shape
    n_pages = page_table.shape[1]
    n_blocks = n_pages * PAGE_SIZE // MOBA_BLOCK
    heads = d // head_dim
    cache_kt = jnp.transpose(cache_k, (0, 1, 3, 4, 2))
    cache_vt = jnp.transpose(cache_v, (0, 1, 3, 4, 2))
    row = lambda x: x.reshape(nb, 1, d)
    vec = pl.BlockSpec((1, 1, d), lambda s, pt: (s, 0, 0))
    page = lambda j: pl.BlockSpec((1, 1, heads, head_dim, PAGE_SIZE), lambda s, pt: (layer, pt[s, j], 0, 0, 0))
    const = lambda shape: pl.BlockSpec(shape, lambda s, pt: (0,) * len(shape))
    bias = bias_rows[:, :heads].reshape(n_blocks, MOBA_BLOCK, heads).transpose(0, 2, 1)
    rb0 = jnp.broadcast_to(rel_bias[0][:, None], (heads, LANES))
    pages = [page(j) for j in range(n_pages)]
    out = pl.pallas_call(
        functools.partial(_moba_sample_kernel, n_pages=n_pages, head_dim=head_dim),
        out_shape=jax.ShapeDtypeStruct((nb, 1, d), F32),
        grid_spec=pltpu.PrefetchScalarGridSpec(
            num_scalar_prefetch=1, grid=(nb,),
            in_specs=[vec, vec, vec, *pages, *pages, const(bias.shape), const(rb0.shape)],
            out_specs=vec),
        compiler_params=_cparams("parallel"), name="moba_sample_attn",
    )(page_table, row(q), row(k_new), row(v_new), *([cache_kt] * n_pages), *([cache_vt] * n_pages), bias, rb0)
    return out.reshape(nb, d)


POOL_CARRY = 16
assert all(b == 2 * a for a, b in zip((1,) + POOL_WINDOWS, POOL_WINDOWS)) and POOL_CARRY > POOL_CTX


def _pool_prompt_kernel(y_ref, ng_ref, w_ref, sc_ref, yout_ref, st_ref, carry_scr, *, tps):
    i = pl.program_id(0)
    y = y_ref[...]
    tm, d = y.shape
    gd = d // len(POOL_WINDOWS)
    h = _rms(y, ng_ref[...])

    @pl.when(i % tps == 0)
    def _():
        carry_scr[...] = jnp.zeros(carry_scr.shape, F32)

    rows = jnp.concatenate([carry_scr[...], h], axis=0)
    carry_scr[...] = h[tm - POOL_CARRY:, :]
    st_ref[0] = rows[tm + POOL_CARRY - POOL_CTX:, :]
    pos = (i % tps) * tm + lax.broadcasted_iota(I32, (tm, 1), 0) + 1
    sums = rows
    for g, win in enumerate(POOL_WINDOWS):
        cols = slice(g * gd, (g + 1) * gd)
        sums = sums[:, (gd if g else 0):]
        sums = sums + pltpu.roll(sums, win // 2, axis=0)
        cnt = jnp.minimum(pos, win).astype(F32)
        mixed = _bdot(sums[POOL_CARRY:, 0:gd] / cnt - h[:, cols], w_ref[g])
        yout_ref[:, cols] = y[:, cols] + mixed * sc_ref[:, cols]


def _pool_prompt(y, ng, w, sc, *, tm, seq):
    n, d = y.shape
    tps = seq // tm
    return pl.pallas_call(
        functools.partial(_pool_prompt_kernel, tps=tps),
        out_shape=[jax.ShapeDtypeStruct((n, d), F32), jax.ShapeDtypeStruct((n // seq, POOL_CTX, d), F32)],
        grid=(n // tm,),
        in_specs=[_rows(tm, d), _whole(ng.shape), _whole(w.shape), _whole(sc.shape)],
        out_specs=[_rows(tm, d), pl.BlockSpec((1, POOL_CTX, d), lambda i: (i // tps, 0, 0))],
        scratch_shapes=[pltpu.VMEM((POOL_CARRY, d), F32)],
        compiler_params=_cparams("arbitrary"), name="pool_prompt",
    )(y, ng, w, sc)


def _pool_sample_kernel(y_ref, ng_ref, prev_ref, w_ref, sc_ref, yout_ref, h_ref):
    y = y_ref[...]
    d = y.shape[1]
    gd = d // len(POOL_WINDOWS)
    h = _rms(y, ng_ref[...])
    h_ref[...] = h
    for g, win in enumerate(POOL_WINDOWS):
        cols = slice(g * gd, (g + 1) * gd)
        wsum = h[:, cols]
        for back in range(1, win):
            wsum = wsum + prev_ref[POOL_CTX - back, :, cols]
        mixed = _bdot(wsum / float(win) - h[:, cols], w_ref[g])
        yout_ref[:, cols] = y[:, cols] + mixed * sc_ref[:, cols]


def _pool_sample(y, ng, prev_t, w, sc):
    n, d = y.shape
    return pl.pallas_call(
        _pool_sample_kernel, out_shape=[jax.ShapeDtypeStruct((n, d), F32)] * 2, grid=(1,),
        in_specs=[_whole(y.shape), _whole(ng.shape), _whole(prev_t.shape), _whole(w.shape), _whole(sc.shape)],
        out_specs=[_whole(y.shape)] * 2,
        compiler_params=_cparams("arbitrary"), name="pool_sample",
    )(y, ng, prev_t, w, sc)


def _rwkv_proj_kernel(y_ref, ng_ref, mu_ref, wr_ref, wk_ref, wv_ref, w1_ref, w2_ref, a1_ref, a2_ref,
                      g1_ref, g2_ref, vec_ref, *rest, tps, sample):
    if sample:
        prev_ref, r_o, w_o, k_o, v_o, kk_o, kka_o, g_o, sh_o = rest
    else:
        r_o, w_o, k_o, v_o, kk_o, kka_o, g_o, sh_o, hs_scr = rest
    y = y_ref[...]
    tm = y.shape[0]
    h = _rms(y, ng_ref[...])
    if sample:
        h_prev = prev_ref[...]
        sh_o[...] = h
    else:
        i = pl.program_id(0)

        @pl.when(i % tps == 0)
        def _():
            hs_scr[...] = jnp.zeros(hs_scr.shape, F32)

        h_prev = pltpu.roll(jnp.concatenate([hs_scr[...], h], axis=0), 1, axis=0)[SUBLANES:]
        hs_scr[...] = h[tm - SUBLANES:, :]
        sh_o[0] = h[tm - 1:tm, :]
    xx = h_prev - h
    mix = lambda m: h + xx * mu_ref[m:m + 1, :]
    r = _bdot(mix(0), wr_ref[...])
    k = _bdot(mix(2), wk_ref[...])
    v = _bdot(mix(3), wv_ref[...])
    z = vec_ref[0:1, :] + _bdot(jnp.tanh(_bdot(mix(1), w1_ref[...])), w2_ref[...])
    w_log = jnp.minimum(z, 0.0) - jnp.log(1.0 + jnp.exp(-jnp.abs(z))) - 0.5
    a = _sigmoid(vec_ref[1:2, :] + _bdot(_bdot(mix(4), a1_ref[...]), a2_ref[...]))
    g_o[...] = _bdot(_sigmoid(_bdot(mix(5), g1_ref[...])), g2_ref[...])
    kk = k * vec_ref[2:3, :]
    kk = kk / jnp.maximum(jnp.sqrt(_seg_sum(kk * kk, RWKV_HEAD_DIM)), 1e-12)
    r_o[...] = r
    w_o[...] = -jnp.exp(w_log)
    k_o[...] = k * (1.0 + (a - 1.0) * vec_ref[3:4, :])
    v_o[...] = v
    kk_o[...] = kk
    kka_o[...] = kk * a


def _rwkv_proj(y, ng, mu, mats, vec, prev, *, tm, seq, sample):
    n, d = y.shape
    tps = max(seq // tm, 1)
    ins = [y, ng, mu, *mats, vec]
    in_specs = [_rows(tm, d), _whole(ng.shape), _whole(mu.shape), *[_whole(m.shape) for m in mats],
                _whole(vec.shape)]
    outs = [jax.ShapeDtypeStruct((n, d), F32)] * 7
    out_specs = [_rows(tm, d)] * 7
    scratch = []
    if sample:
        ins.append(prev)
        in_specs.append(_rows(tm, d))
        outs.append(jax.ShapeDtypeStruct((n, d), F32))
        out_specs.append(_rows(tm, d))
    else:
        outs.append(jax.ShapeDtypeStruct((n // seq, 1, d), F32))
        out_specs.append(pl.BlockSpec((1, 1, d), lambda i: (i // tps, 0, 0)))
        scratch.append(pltpu.VMEM((SUBLANES, d), F32))
    return pl.pallas_call(
        functools.partial(_rwkv_proj_kernel, tps=tps, sample=sample),
        out_shape=outs, grid=(n // tm,), in_specs=in_specs, out_specs=out_specs, scratch_shapes=scratch,
        compiler_params=_cparams("arbitrary"), name="rwkv_proj_sample" if sample else "rwkv_proj_prompt",
    )(*ins)


RWKV_CHUNK = 128


def _rwkv_chunk_kernel(r_ref, lw_ref, k_ref, v_ref, kk_ref, kka_ref, o_ref, st_ref, t_scr):
    c = pl.program_id(1)
    ch, d = r_ref.shape
    hd = RWKV_HEAD_DIM
    shift = int(math.log2(hd))
    n_pairs = d // LANES

    @pl.when(c == 0)
    def _():
        t_scr[...] = jnp.zeros(t_scr.shape, F32)

    ri = lax.broadcasted_iota(I32, (ch, ch), 0)
    ci = lax.broadcasted_iota(I32, (ch, ch), 1)
    incl = ci <= ri
    strict = ci < ri
    incl2 = jnp.concatenate([incl, incl], axis=1)
    strict2 = jnp.concatenate([strict, strict], axis=1)
    same_head = (ri >> shift) == (ci >> shift)
    eye = ri == ci
    head1 = (lax.broadcasted_iota(I32, (ch, LANES), 1) >> shift) == 1
    head1_wide = jnp.concatenate([head1, head1], axis=1)

    def by_head(x, mask):
        return jnp.concatenate([jnp.where(mask, 0.0, x), jnp.where(mask, x, 0.0)], axis=0).astype(BF16)

    lw = lw_ref[...]
    tri = jnp.where(incl, 1.0, 0.0).astype(BF16)
    p1 = lw.astype(BF16)
    rem = lw - p1.astype(F32)
    p2 = rem.astype(BF16)
    p3 = (rem - p2.astype(F32)).astype(BF16)
    cl = (jnp.dot(tri, p1, preferred_element_type=F32) + jnp.dot(tri, p2, preferred_element_type=F32)
          + jnp.dot(tri, p3, preferred_element_type=F32))
    mid = cl[ch // 2 - 1:ch // 2, :]
    last = cl[ch - 1:ch, :]
    e_mid = jnp.exp(mid)
    e_neg = jnp.exp(mid - cl)
    e_tail = jnp.exp(last - cl)
    p_last = jnp.exp(last)
    kk, kka, k = kk_ref[...], kka_ref[...], k_ref[...]
    a_s = -kk * jnp.exp(cl - lw - mid)
    r_s = r_ref[...] * jnp.exp(cl - mid)
    b_s = kka * e_neg
    k_s = k * e_neg
    b_t = kka * e_tail
    k_t = k * e_tail
    v = v_ref[...]
    zero = jnp.zeros((ch, ch), BF16)

    pairs = range(n_pairs)
    cols = [slice(p * LANES, (p + 1) * LANES) for p in pairs]
    mm = lambda a, b: jnp.dot(a, b, preferred_element_type=F32)
    lane_cat = lambda a, b: jnp.concatenate([a, b], axis=1)
    a_p = [a_s[:, c] for c in cols]
    r_p = [r_s[:, c] for c in cols]
    v_p = [v[:, c] for c in cols]
    gram = []
    for p in pairs:
        lhs = jnp.concatenate([jnp.where(head1, 0.0, a_p[p]), jnp.where(head1, 0.0, r_p[p]),
                               jnp.where(head1, a_p[p], 0.0), jnp.where(head1, r_p[p], 0.0)], axis=0)
        rhs = lane_cat(b_s[:, cols[p]].T, k_s[:, cols[p]].T)
        gram.append(mm(lhs.astype(BF16), rhs.astype(BF16)))
    ab0 = [jnp.where(strict2, g_[0:ch], 0.0) for g_ in gram]
    rb0 = [jnp.where(incl2, g_[ch:2 * ch], 0.0) for g_ in gram]
    ab1 = [jnp.where(strict2, g_[2 * ch:3 * ch], 0.0) for g_ in gram]
    rb1 = [jnp.where(incl2, g_[3 * ch:], 0.0) for g_ in gram]
    v_heads = [by_head(v_p[p], head1) for p in pairs]
    x = [lane_cat(a_p[p], mm(lane_cat(ab0[p][:, ch:], ab1[p][:, ch:]).astype(BF16), v_heads[p])) for p in pairs]
    pw = [lane_cat(ab0[p][:, :ch], ab1[p][:, :ch]).astype(BF16) for p in pairs]
    n_sq = int(math.log2(ch))
    for it in range(n_sq):
        x = [x[p] + mm(pw[p], by_head(x[p], head1_wide)) for p in pairs]
        if it < n_sq - 1:
            pw = [mm(pw[p], jnp.concatenate([lane_cat(pw[p][:, :ch], zero), lane_cat(zero, pw[p][:, ch:])],
                                            axis=0)).astype(BF16) for p in pairs]
    x = [lane_cat(x[p][:, :ch] * e_mid[:, cols[p]], x[p][:, ch:]) for p in pairs]
    qo = [mm(lane_cat(rb0[p][:, :ch], rb1[p][:, :ch]).astype(BF16), by_head(x[p], head1_wide)) for p in pairs]
    o_intra = [qo[p][:, ch:] + mm(lane_cat(rb0[p][:, ch:], rb1[p][:, ch:]).astype(BF16), v_heads[p])
               for p in pairs]
    q = [r_p[p] * e_mid[:, cols[p]] + qo[p][:, :ch] for p in pairs]
    gh = [mm(b_t[:, cols[p]].T.astype(BF16), x[p].astype(BF16)) for p in pairs]
    g = [jnp.where(same_head, gh[p][:, :ch], 0.0) + jnp.where(eye, p_last[:, cols[p]], 0.0) for p in pairs]
    h = [jnp.where(same_head, gh[p][:, ch:] + _bdot(k_t[:, cols[p]].T, v_p[p]), 0.0) for p in pairs]
    for p in pairs:
        t_old = t_scr[p].astype(BF16)
        o_ref[:, cols[p]] = mm(q[p].astype(BF16), t_old) + o_intra[p]
        t_scr[p] = mm(g[p].astype(BF16), t_old) + h[p]

    @pl.when(c == pl.num_programs(1) - 1)
    def _():
        for p in range(n_pairs):
            s_pair = t_scr[p].T
            st_ref[0, p] = s_pair[:hd, :] + s_pair[hd:, :]


def _rwkv_chunked(r, lw, k, v, kk, kka, *, batch, seq):
    n, d = r.shape
    ch = RWKV_CHUNK
    assert ch == LANES and seq % ch == 0
    n_chunks = seq // ch
    n_pairs = d // LANES
    blk = pl.BlockSpec((ch, d), lambda b, c: (b * n_chunks + c, 0))
    return pl.pallas_call(
        _rwkv_chunk_kernel,
        out_shape=[jax.ShapeDtypeStruct((n, d), F32),
                   jax.ShapeDtypeStruct((batch, n_pairs, RWKV_HEAD_DIM, LANES), F32)],
        grid=(batch, n_chunks), in_specs=[blk] * 6,
        out_specs=[blk, pl.BlockSpec((1, n_pairs, RWKV_HEAD_DIM, LANES), lambda b, c: (b, 0, 0, 0))],
        scratch_shapes=[pltpu.VMEM((n_pairs, LANES, LANES), F32)],
        compiler_params=_cparams("parallel", "arbitrary"), name="rwkv_chunked",
    )(r, lw, k, v, kk, kka)


def _rwkv_step_kernel(r_ref, lw_ref, k_ref, v_ref, kk_ref, kka_ref, s0_ref, o_ref, st_ref):
    hd = s0_ref.shape[1]
    w = jnp.exp(lw_ref[...])
    kk, kka, k, r = kk_ref[...], kka_ref[...], k_ref[...], r_ref[...]

    def value_row(i, carry):
        s = s0_ref[0, i]
        s_kk = jnp.sum(s * kk, axis=0, keepdims=True)
        s = s * w - s_kk * kka + v_ref[pl.ds(i, 1), :] * k
        st_ref[0, i] = s
        o_ref[pl.ds(i, 1), :] = jnp.sum(s * r, axis=0, keepdims=True)
        return carry

    lax.fori_loop(0, hd, value_row, 0, unroll=8)


def _rwkv_step(r, lw, k, v, kk, kka, s0):
    d, n = r.shape
    heads, hd = s0.shape[0], s0.shape[1]
    vec = pl.BlockSpec((hd, n), lambda h: (h, 0))
    st = pl.BlockSpec((1, hd, hd, n), lambda h: (h, 0, 0, 0))
    return pl.pallas_call(
        _rwkv_step_kernel,
        out_shape=[jax.ShapeDtypeStruct((d, n), F32), jax.ShapeDtypeStruct(s0.shape, F32)],
        grid=(heads,), in_specs=[vec] * 6 + [st], out_specs=[vec, st],
        compiler_params=_cparams("parallel"), name="rwkv_step",
    )(r, lw, k, v, kk, kka, s0)


def _rwkv_out_kernel(o_ref, r_ref, k_ref, v_ref, g_ref, y_ref, vec_ref, wo_ref, yout_ref):
    hd = RWKV_HEAD_DIM
    o = o_ref[...]
    mu = _seg_sum(o, hd) * (1.0 / hd)
    dlt = o - mu
    var = _seg_sum(dlt * dlt, hd) * (1.0 / hd)
    o = dlt * lax.rsqrt(var + RWKV_LNX_EPS) * vec_ref[1:2, :] + vec_ref[2:3, :]
    o = o + _seg_sum(r_ref[...] * k_ref[...] * vec_ref[0:1, :], hd) * v_ref[...]
    yout_ref[...] = y_ref[...] + _bdot(o * g_ref[...], wo_ref[...])


def _rwkv_out(o, r, k, v, g, y, vec, wo, *, tm, name):
    n, d = y.shape
    return pl.pallas_call(
        _rwkv_out_kernel, out_shape=jax.ShapeDtypeStruct((n, d), F32), grid=(n // tm,),
        in_specs=[_rows(tm, d)] * 6 + [_whole(vec.shape), _whole(wo.shape)], out_specs=_rows(tm, d),
        compiler_params=_cparams("parallel"), name=name,
    )(o, r, k, v, g, y, vec, wo)


FFN_CHUNK = 256


def _ffn_act(gate, g1, g2, up, cw, cb):
    return _gelu(cw[0:1] * g2 + cw[1:2] * g1 + cw[2:3] * gate + cb) * up


def _ffn_prompt_kernel(y_ref, ng_ref, win_ref, cw_ref, cb_ref, wout_ref, *rest, tps, final):
    if final:
        fg_ref, yout_ref, st_ref, yfin_ref, h_scr, gs_scr, act_scr, carry_scr = rest
    else:
        yout_ref, st_ref, h_scr, gs_scr, act_scr, carry_scr = rest
    i = pl.program_id(0)
    y = y_ref[...]
    tm = y.shape[0]
    hid = wout_ref.shape[0]
    tf = FFN_CHUNK
    h_scr[...] = _rms(y, ng_ref[...]).astype(BF16)

    @pl.when(i % tps == 0)
    def _():
        carry_scr[...] = jnp.zeros(carry_scr.shape, F32)

    for j in range(hid // tf):
        cols = slice(j * tf, (j + 1) * tf)
        hb = h_scr[...]
        gs_scr[0:SUBLANES, :] = carry_scr[:, cols]
        gs_scr[SUBLANES:, :] = jnp.dot(hb, win_ref[:, cols], preferred_element_type=F32)
        up = jnp.dot(hb, win_ref[:, hid + j * tf:hid + (j + 1) * tf], preferred_element_type=F32)
        gate = gs_scr[SUBLANES:, :]
        carry_scr[:, cols] = gate[tm - SUBLANES:, :]
        st_ref[0, :, cols] = gate[tm - (FFN_CONV - 1):, :]
        act = _ffn_act(gate, gs_scr[SUBLANES - 1:tm + SUBLANES - 1, :], gs_scr[SUBLANES - 2:tm + SUBLANES - 2, :],
                       up, cw_ref[:, cols], cb_ref[:, cols])
        act_scr[:, cols] = act.astype(BF16)
    y_new = y + jnp.dot(act_scr[...], wout_ref[...], preferred_element_type=F32)
    yout_ref[...] = y_new
    if final:
        yfin_ref[...] = _rms(y_new, fg_ref[...])


def _ffn_prompt(y, ng, win, cw, cb, wout, layer, fg, *, tm, seq):
    n, d = y.shape
    hid = wout.shape[1]
    tps = seq // tm
    final = fg is not None
    of_layer = lambda a: pl.BlockSpec((None,) + a.shape[1:], lambda i: (layer, 0, 0), pipeline_mode=pl.Buffered(1))
    ins = [y, ng, win, cw, cb, wout]
    in_specs = [_rows(tm, d), _whole(ng.shape), of_layer(win), _whole(cw.shape), _whole(cb.shape), of_layer(wout)]
    outs = [jax.ShapeDtypeStruct((n, d), F32), jax.ShapeDtypeStruct((n // seq, FFN_CONV - 1, hid), F32)]
    out_specs = [_rows(tm, d), pl.BlockSpec((1, FFN_CONV - 1, hid), lambda i: (i // tps, 0, 0))]
    if final:
        ins.append(fg)
        in_specs.append(_whole(fg.shape))
        outs.append(jax.ShapeDtypeStruct((n, d), F32))
        out_specs.append(_rows(tm, d))
    return pl.pallas_call(
        functools.partial(_ffn_prompt_kernel, tps=tps, final=final),
        out_shape=outs, grid=(n // tm,), in_specs=in_specs, out_specs=out_specs,
        scratch_shapes=[pltpu.VMEM((tm, d), BF16), pltpu.VMEM((tm + SUBLANES, FFN_CHUNK), F32),
                        pltpu.VMEM((tm, hid), BF16), pltpu.VMEM((SUBLANES, hid), F32)],
        compiler_params=_cparams("arbitrary"), name="ffn_prompt",
    )(*ins)


def _ffn_sample_kernel(y_ref, ng_ref, wg_ref, wu_ref, cw_ref, cb_ref, p2_ref, p1_ref, wout_ref, *rest, final):
    if final:
        fg_ref, yout_ref, gate_ref, yfin_ref, h_scr, acc_scr = rest
    else:
        yout_ref, gate_ref, h_scr, acc_scr = rest
    j = pl.program_id(0)

    @pl.when(j == 0)
    def _():
        h_scr[...] = _rms(y_ref[...], ng_ref[...]).astype(BF16)
        acc_scr[...] = jnp.zeros(acc_scr.shape, F32)

    hb = h_scr[...]
    gate = jnp.dot(hb, wg_ref[...], preferred_element_type=F32)
    up = jnp.dot(hb, wu_ref[...], preferred_element_type=F32)
    gate_ref[...] = gate
    act = _ffn_act(gate, p1_ref[...], p2_ref[...], up, cw_ref[...], cb_ref[...])
    acc_scr[...] += _bdot(act, wout_ref[...])

    @pl.when(j == pl.num_programs(0) - 1)
    def _():
        y_new = y_ref[...] + acc_scr[...]
        yout_ref[...] = y_new
        if final:
            yfin_ref[...] = _rms(y_new, fg_ref[...])


def _ffn_sample(y, ng, win, cw, cb, p2, p1, wout, layer, fg):
    n, d = y.shape
    hid = wout.shape[1]
    tf = hid // 2 if (hid // 2) % LANES == 0 else FFN_CHUNK
    nf = hid // tf
    final = fg is not None
    keep = lambda shape: pl.BlockSpec(shape, lambda j: (0,) * len(shape))
    chunk = lambda rows: pl.BlockSpec((rows, tf), lambda j: (0, j))
    ins = [y, ng, win, win, cw, cb, p2, p1, wout]
    in_specs = [keep((n, d)), keep(ng.shape), pl.BlockSpec((None, d, tf), lambda j: (layer, 0, j)),
                pl.BlockSpec((None, d, tf), lambda j: (layer, 0, nf + j)),
                chunk(FFN_CONV), chunk(1), chunk(n), chunk(n), pl.BlockSpec((None, tf, d), lambda j: (layer, j, 0))]
    outs = [jax.ShapeDtypeStruct((n, d), F32), jax.ShapeDtypeStruct((n, hid), F32)]
    out_specs = [keep((n, d)), chunk(n)]
    if final:
        ins.append(fg)
        in_specs.append(keep(fg.shape))
        outs.append(jax.ShapeDtypeStruct((n, d), F32))
        out_specs.append(keep((n, d)))
    return pl.pallas_call(
        functools.partial(_ffn_sample_kernel, final=final),
        out_shape=outs, grid=(nf,), in_specs=in_specs, out_specs=out_specs,
        scratch_shapes=[pltpu.VMEM((n, d), BF16), pltpu.VMEM((n, d), F32)],
        compiler_params=_cparams("arbitrary"), name="ffn_sample",
    )(*ins)


def kernel(x_prompt, x_sample, cache_moba_k, cache_moba_v, state_pool, state_rwkv_wkv, state_rwkv_shift, state_ffn_conv, page_table, norm_mix_g, norm_ffn_g, norm_final_g, rel_bias, gm_w_in, gm_ln_g, gm_ln_b, gm_w_s, gm_b_s, gm_w_out, moba_w_qkv, moba_w_o, pool_w, pool_scale, rwkv_mu, rwkv_w_r, rwkv_w_k, rwkv_w_v, rwkv_w_o, rwkv_w0, rwkv_w1, rwkv_w2, rwkv_a0, rwkv_a1, rwkv_a2, rwkv_g1, rwkv_g2, rwkv_k_k, rwkv_k_a, rwkv_r_k, rwkv_lnx_g, rwkv_lnx_b, ffn_w_in, ffn_conv_w, ffn_conv_b, ffn_w_out):
    bp, seq, d = x_prompt.shape
    bs = x_sample.shape[0]
    depth = norm_mix_g.shape[0]
    assert x_sample.shape[1] == 1 and depth == 4
    past_len = page_table.shape[1] * PAGE_SIZE
    assert seq % MOBA_BLOCK == 0 and past_len % MOBA_BLOCK == 0 and past_len % GM_CHUNK == 0
    row = lambda vct: vct.reshape(1, -1)
    bf = lambda m: m.astype(BF16)
    yp = x_prompt.reshape(bp * seq, d)
    ys = x_sample.reshape(bs, d)
    tm = TOKEN_TILE
    assert seq % tm == 0 and seq % RWKV_PROJ_TILE == 0 and seq % RWKV_CHUNK == 0
    conv_p, conv_s = [], []
    ffn_in, ffn_out = bf(ffn_w_in), bf(ffn_w_out)

    def ffn(i, yp, ys):
        last = i == depth - 1
        cw, cb, ng = ffn_conv_w[i], row(ffn_conv_b[i]), row(norm_ffn_g[i])
        fg = row(norm_final_g) if last else None
        res_p = _ffn_prompt(yp, ng, ffn_in, cw, cb, ffn_out, i, fg, tm=tm if last else 2 * tm, seq=seq)
        st = state_ffn_conv[i]
        res_s = _ffn_sample(ys, ng, ffn_in, cw, cb, st[:, 0], st[:, 1], ffn_out, i, fg)
        conv_p.append(res_p[1])
        conv_s.append(jnp.stack([st[:, 1], res_s[1]], axis=1))
        if last:
            return res_p[2], res_s[2]
        return res_p[0], res_s[0]

    ng = row(norm_mix_g[0])
    width = gm_w_out.shape[1]
    gd = width // GM_GROUPS
    gm_in, gm_out = bf(gm_w_in[0]), bf(gm_w_out[0])
    lng, lnb = row(gm_ln_g[0]), row(gm_ln_b[0])
    sb_prompt = jnp.repeat(gm_b_s[0].T, gd, axis=1)
    (yp,) = _gmlp(yp, ng, gm_in, lng, lnb, gm_w_s[0], sb_prompt, gm_out, tm=tm, sample=False)
    sa_first = row(jnp.repeat(gm_w_s[0][:, 0, 0], gd))
    sb_first = row(jnp.repeat(gm_b_s[0][:, 0], gd))
    ys, gm_v = _gmlp(ys, ng, gm_in, lng, lnb, sa_first, sb_first, gm_out, tm=bs, sample=True)
    gm_v_sample = gm_v.reshape(1, bs, 1, width)
    yp, ys = ffn(0, yp, ys)

    ng = row(norm_mix_g[1])
    w_qkv, w_o = bf(moba_w_qkv[0]), bf(moba_w_o[0])
    heads = MOBA_HEADS
    hd = d // heads
    qp, kp_t, vp_t = _qkv_prompt(yp, ng, w_qkv[:, :d], w_qkv[:, d:2 * d].T, w_qkv[:, 2 * d:].T,
                                 tm=2 * tm, batch=bp, seq=seq)
    qs, ks, vs = _norm_linear(ys, ng, w_qkv, 3, tm=bs, name="moba_qkv_sample")
    blk = MOBA_BLOCK
    qi = jnp.arange(blk, dtype=I32)[:, None]
    ki = jnp.arange(blk, dtype=I32)[None, :]
    bkt_tiles = _t5_bucket_table(jnp.stack([qi - ki, blk + qi - ki]))
    bias_tiles = _bias_tiles(rel_bias, bkt_tiles)
    op = _moba_prompt(rel_bias, qp, kp_t, vp_t, bias_tiles, batch=bp, seq=seq, head_dim=hd)
    bkt_rows = _t5_bucket_table(past_len - jnp.arange(past_len, dtype=I32))
    rb_pad = jnp.pad(rel_bias, ((0, 0), (0, LANES - heads)))
    bias_rows = _bias_rows(rb_pad, jnp.broadcast_to(bkt_rows[:, None], (past_len, LANES)))
    os_ = _moba_sample(page_table, qs, ks, vs, cache_moba_k, cache_moba_v, 0, bias_rows, rel_bias, head_dim=hd)
    yp = _linear_res(op, w_o, yp, tm=tm, name="moba_out_prompt")
    ys = _linear_res(os_, w_o, ys, tm=bs, name="moba_out_sample")
    moba_k_prompt = kp_t.reshape(1, bp, heads, hd, seq).transpose(0, 1, 4, 2, 3)
    moba_v_prompt = vp_t.reshape(1, bp, heads, hd, seq).transpose(0, 1, 4, 2, 3)
    moba_k_sample = ks.reshape(1, bs, 1, heads, hd)
    moba_v_sample = vs.reshape(1, bs, 1, heads, hd)
    yp, ys = ffn(1, yp, ys)

    ng = row(norm_mix_g[2])
    pw, psc = bf(pool_w[0]), row(pool_scale[0])
    yp, pool_p = _pool_prompt(yp, ng, pw, psc, tm=tm, seq=seq)
    ys, hs = _pool_sample(ys, ng, jnp.swapaxes(state_pool[0], 0, 1), pw, psc)
    pool_prompt = pool_p[None]
    pool_sample = jnp.concatenate([state_pool[0][:, 1:], hs[:, None]], axis=1)[None]
    yp, ys = ffn(2, yp, ys)

    ng = row(norm_mix_g[3])
    mats = [bf(m[0]) for m in (rwkv_w_r, rwkv_w_k, rwkv_w_v, rwkv_w1, rwkv_w2, rwkv_a1, rwkv_a2,
                               rwkv_g1, rwkv_g2)]
    vec_in = jnp.stack([rwkv_w0[0], rwkv_a0[0], rwkv_k_k[0], rwkv_k_a[0]])
    vec_out = jnp.stack([rwkv_r_k[0].reshape(-1), rwkv_lnx_g[0], rwkv_lnx_b[0]])
    rh = d // RWKV_HEAD_DIM
    *seqs_p, gp, shp = _rwkv_proj(yp, ng, rwkv_mu[0], mats, vec_in, None, tm=RWKV_PROJ_TILE, seq=seq,
                                  sample=False)
    *seqs_s, gs, shs = _rwkv_proj(ys, ng, rwkv_mu[0], mats, vec_in, state_rwkv_shift[0], tm=bs, seq=1,
                                  sample=True)
    o_p, st_p = _rwkv_chunked(*seqs_p, batch=bp, seq=seq)
    hd_r = RWKV_HEAD_DIM
    wkv_p = st_p.reshape(bp, rh // 2, hd_r, 2, hd_r).transpose(0, 1, 3, 2, 4).reshape(bp, rh, hd_r, hd_r)
    o_s_t, wkv_s_t = _rwkv_step(*[a.T for a in seqs_s], jnp.transpose(state_rwkv_wkv[0], (1, 2, 3, 0)))
    o_s, wkv_s = o_s_t.T, jnp.transpose(wkv_s_t, (3, 0, 1, 2))
    w_o = bf(rwkv_w_o[0])
    r_p, _, k_p, v_p = seqs_p[:4]
    r_s, _, k_s, v_s = seqs_s[:4]
    yp = _rwkv_out(o_p, r_p, k_p, v_p, gp, yp, vec_out, w_o, tm=tm, name="rwkv_out_prompt")
    ys = _rwkv_out(o_s.reshape(bs, d), r_s, k_s, v_s, gs, ys, vec_out, w_o, tm=bs, name="rwkv_out_sample")
    yp, ys = ffn(3, yp, ys)

    return (yp.reshape(bp, seq, d), ys.reshape(bs, 1, d), gm_v_sample, moba_k_prompt, moba_v_prompt,
            moba_k_sample, moba_v_sample, pool_prompt, pool_sample, wkv_p[None], wkv_s[None],
            shp.reshape(1, bp, d), shs[None], jnp.stack(conv_p), jnp.stack(conv_s))
```
